```python
import math
import jax, jax.numpy as jnp
from jax import lax
import numpy as np

D_MODEL = 1024
BATCH = 8
SEQ = 16384
DEPTH = 4

N_MIXERS = 4
D_FF = 2816
EPS = 1e-6
S5_GROUP = 16
S5_GROUPS = D_MODEL // S5_GROUP
S5_STATE = 64
DT_MIN = 1e-3
DT_MAX = 1e-1
CONV_W = 31
GM_CHUNK = 128
GM_E = 2 * D_MODEL
GM_HEADS = 8
HEAD_DIM = 64
AT_HEADS = D_MODEL // HEAD_DIM
PATTERNS = ((128, 1), (512, 4), (2048, 16))
N_PATTERNS = len(PATTERNS)
BLOCK = 128
NUM_BUCKETS = 32
MAX_DISTANCE = 2048

kernel_name = "hybrid_interleaved_s5_conv_gmlp_dilated_attn"


def rmsnorm(x, g):
    xf = x.astype(jnp.float32)
    y = xf * lax.rsqrt(jnp.mean(xf * xf, axis=-1, keepdims=True) + EPS)
    return (y * g.astype(jnp.float32)).astype(x.dtype)


def layernorm(x, g, b):
    xf = x.astype(jnp.float32)
    mu = jnp.mean(xf, axis=-1, keepdims=True)
    var = jnp.mean(jnp.square(xf - mu), axis=-1, keepdims=True)
    y = (xf - mu) * lax.rsqrt(var + EPS)
    return (y * g.astype(jnp.float32) + b.astype(jnp.float32)).astype(x.dtype)


def swiglu(h, w1, w3, w2):
    return (jax.nn.silu(h @ w1) * (h @ w3)) @ w2


def _complex_scan_op(left, right):
    a1r, a1i, b1r, b1i = left
    a2r, a2i, b2r, b2i = right
    return (a2r * a1r - a2i * a1i,
            a2r * a1i + a2i * a1r,
            a2r * b1r - a2i * b1i + b2r,
            a2r * b1i + a2i * b1r + b2i)


def s5_mixer(h, w_in, a_re, a_im, log_dt, b_re, b_im, c_re, c_im, d_skip, w_glu, b_glu, w_out):
    bsz, seq, _ = h.shape
    u = (h @ w_in).astype(jnp.float32).reshape(bsz, seq, S5_GROUPS, S5_GROUP)
    ar = a_re.astype(jnp.float32)
    ai = a_im.astype(jnp.float32)
    dt = jnp.exp(log_dt.astype(jnp.float32))[:, None]
    mag = jnp.exp(dt * ar)
    abar_re = mag * jnp.cos(dt * ai)
    abar_im = mag * jnp.sin(dt * ai)
    den = ar * ar + ai * ai
    nr = abar_re - 1.0
    f_re = (nr * ar + abar_im * ai) / den
    f_im = (abar_im * ar - nr * ai) / den
    br = b_re.astype(jnp.float32)
    bi = b_im.astype(jnp.float32)
    bb_re = f_re[..., None] * br - f_im[..., None] * bi
    bb_im = f_re[..., None] * bi + f_im[..., None] * br
    bu_re = jnp.einsum('blgh,gph->blgp', u, bb_re)
    bu_im = jnp.einsum('blgh,gph->blgp', u, bb_im)
    elems = (jnp.broadcast_to(abar_re, bu_re.shape), jnp.broadcast_to(abar_im, bu_re.shape), bu_re, bu_im)
    _, _, st_re, st_im = lax.associative_scan(_complex_scan_op, elems, axis=1)
    y = (jnp.einsum('blgp,ghp->blgh', st_re, c_re.astype(jnp.float32))
         - jnp.einsum('blgp,ghp->blgh', st_im, c_im.astype(jnp.float32))
         + d_skip.astype(jnp.float32).reshape(S5_GROUPS, S5_GROUP) * u)
    y = jax.nn.gelu(y.reshape(bsz, seq, D_MODEL)).astype(h.dtype)
    z = y * jax.nn.sigmoid(y @ w_glu + b_glu)
    return z @ w_out


def conv_mixer(h, w_in, b_in, dw, dw_b, ln_g, ln_b, w_out, b_out):
    z = h @ w_in + b_in
    a, g = jnp.split(z, 2, axis=-1)
    z = a * jax.nn.sigmoid(g)
    z = lax.conv_general_dilated(z, dw[:, None, :].astype(z.dtype), window_strides=(1,),
                                 padding=[(CONV_W - 1, 0)],
                                 dimension_numbers=('NWC', 'WIO', 'NWC'),
                                 feature_group_count=D_MODEL) + dw_b
    z = jax.nn.silu(layernorm(z, ln_g, ln_b))
    return z @ w_out + b_out


def gmlp_mixer(h, w_in, b_in, ln_g, ln_b, w_s, b_s, w_out, b_out):
    bsz, seq, _ = h.shape
    z = jax.nn.gelu(h @ w_in + b_in)
    u, v = jnp.split(z, 2, axis=-1)
    v = layernorm(v, ln_g, ln_b)
    vc = v.reshape(bsz, seq // GM_CHUNK, GM_CHUNK, GM_HEADS, GM_E // GM_HEADS)
    causal = jnp.tril(jnp.ones((GM_CHUNK, GM_CHUNK), jnp.float32))
    s = jnp.einsum('hts,bnshc->bnthc', w_s * causal, vc) + b_s.T[None, None, :, :, None]
    s = s.reshape(bsz, seq, GM_E).astype(u.dtype)
    return (u * s) @ w_out + b_out


def t5_bucket(dist):
    max_exact = NUM_BUCKETS // 2
    distf = jnp.maximum(dist, 1).astype(jnp.float32)
    large = max_exact + (jnp.log(distf / max_exact) / math.log(MAX_DISTANCE / max_exact)
                         * (NUM_BUCKETS - max_exact)).astype(jnp.int32)
    large = jnp.minimum(large, NUM_BUCKETS - 1)
    return jnp.where(dist < max_exact, dist, large)


def _band_delta():
    return (jnp.arange(BLOCK)[:, None] + BLOCK) - jnp.arange(2 * BLOCK)[None, :]


def rel_bias_block(table, dilation):
    dist = jnp.maximum(_band_delta(), 0) * dilation
    return table.astype(jnp.float32)[t5_bucket(dist)].transpose(2, 0, 1)


def dilated_window_attention(q, k, v, bias, window, dilation):
    bsz, seq, heads, hd = q.shape
    span = BLOCK * dilation
    seq_p = -(-seq // span) * span
    n_sub = seq_p // dilation
    n_blk = n_sub // BLOCK

    def to_blocks(t):
        t = jnp.pad(t.astype(jnp.float32), ((0, 0), (0, seq_p - seq), (0, 0), (0, 0)))
        t = t.reshape(bsz, n_sub, dilation, heads, hd).transpose(0, 2, 1, 3, 4)
        return t.reshape(bsz, dilation, n_blk, BLOCK, heads, hd)

    def with_prev(t):
        prev = jnp.pad(t, ((0, 0), (0, 0), (1, 0), (0, 0), (0, 0), (0, 0)))[:, :, :-1]
        return jnp.concatenate([prev, t], axis=3)

    qb = to_blocks(q)
    kk = with_prev(to_blocks(k))
    vv = with_prev(to_blocks(v))
    logits = jnp.einsum('brnqhd,brnkhd->brnhqk', qb, kk) * (hd ** -0.5) + bias[None, None, None]
    delta = _band_delta()
    in_band = (delta >= 0) & (delta <= window // dilation)
    not_before_start = (jnp.arange(n_blk)[:, None, None] > 0) | (jnp.arange(2 * BLOCK) >= BLOCK)[None, None, :]
    mask = in_band[None] & not_before_start
    logits = jnp.where(mask[None, None, :, None], logits, -jnp.inf)
    m = jnp.max(logits, axis=-1, keepdims=True)
    p = jnp.exp(logits - m)
    den = jnp.sum(p, axis=-1)
    o = jnp.einsum('brnhqk,brnkhd->brnqhd', p, vv) / jnp.swapaxes(den, -1, -2)[..., None]
    lse = jnp.swapaxes(m[..., 0] + jnp.log(den), -1, -2)
    o = o.reshape(bsz, dilation, n_sub, heads, hd).transpose(0, 2, 1, 3, 4).reshape(bsz, seq_p, heads, hd)[:, :seq]
    lse = lse.reshape(bsz, dilation, n_sub, heads).transpose(0, 2, 1, 3).reshape(bsz, seq_p, heads)[:, :seq]
    return o, lse


def attention_mixer(h, w_qkv, w_out, rel_bias):
    bsz, seq, _ = h.shape
    qkv = (h @ w_qkv).reshape(bsz, seq, N_PATTERNS, 3, AT_HEADS, HEAD_DIM)
    outs, lses = [], []
    for g, (window, dilation) in enumerate(PATTERNS):
        bias = rel_bias_block(rel_bias[:, g * AT_HEADS:(g + 1) * AT_HEADS], dilation)
        o, lse = dilated_window_attention(qkv[:, :, g, 0], qkv[:, :, g, 1], qkv[:, :, g, 2], bias, window, dilation)
        outs.append(o)
        lses.append(lse)
    wts = jax.nn.softmax(jnp.stack(lses, axis=0), axis=0)
    o = jnp.sum(wts[..., None] * jnp.stack(outs, axis=0), axis=0)
    return o.reshape(bsz, seq, AT_HEADS * HEAD_DIM).astype(h.dtype) @ w_out


def _count(kind):
    return len(range(kind, DEPTH, N_MIXERS))


def _fwd_setup_inputs(seed: int = 0) -> dict:
    key = jax.random.key(seed)
    ks = iter(jax.random.split(key, 48))

    def nrm(shape, scale):
        return scale * jax.random.normal(next(ks), shape, jnp.float32)

    na, nb, nc, nd = _count(0), _count(1), _count(2), _count(3)
    D, F, G, P, HG, E = D_MODEL, D_FF, S5_GROUPS, S5_STATE, S5_GROUP, GM_E
    return {
        "x": nrm((BATCH, SEQ, D), 1.0),
        "norm_pre": 1.0 + nrm((DEPTH, 3, D), 0.05),
        "norm_post": 1.0 + nrm((DEPTH, 3, D), 0.05),
        "ffn_w1": nrm((DEPTH, 2, D, F), D ** -0.5),
        "ffn_w3": nrm((DEPTH, 2, D, F), D ** -0.5),
        "ffn_w2": nrm((DEPTH, 2, F, D), F ** -0.5),
        "rel_bias": nrm((NUM_BUCKETS, N_PATTERNS * AT_HEADS), 0.5),
        "s5_w_in": nrm((na, D, D), D ** -0.5),
        "s5_a_re": -0.5 + nrm((na, G, P), 0.01),
        "s5_a_im": jnp.pi * jnp.arange(P, dtype=jnp.float32) + nrm((na, G, P), 0.01),
        "s5_log_dt": jax.random.uniform(next(ks), (na, G), jnp.float32, math.log(DT_MIN), math.log(DT_MAX)),
        "s5_b_re": nrm((na, G, P, HG), (2 * HG) ** -0.5),
        "s5_b_im": nrm((na, G, P, HG), (2 * HG) ** -0.5),
        "s5_c_re": nrm((na, G, HG, P), (2 * P) ** -0.5),
        "s5_c_im": nrm((na, G, HG, P), (2 * P) ** -0.5),
        "s5_d": nrm((na, D), 1.0),
        "s5_w_glu": nrm((na, D, D), D ** -0.5),
        "s5_b_glu": nrm((na, D), 0.01),
        "s5_w_out": nrm((na, D, D), D ** -0.5),
        "cv_w_in": nrm((nb, D, 2 * D), D ** -0.5),
        "cv_b_in": nrm((nb, 2 * D), 0.01),
        "cv_dw": nrm((nb, CONV_W, D), CONV_W ** -0.5),
        "cv_dw_b": nrm((nb, D), 0.01),
        "cv_ln_g": 1.0 + nrm((nb, D), 0.05),
        "cv_ln_b": nrm((nb, D), 0.01),
        "cv_w_out": nrm((nb, D, D), D ** -0.5),
        "cv_b_out": nrm((nb, D), 0.01),
        "gm_w_in": nrm((nc, D, 2 * E), D ** -0.5),
        "gm_b_in": nrm((nc, 2 * E), 0.01),
        "gm_ln_g": 1.0 + nrm((nc, E), 0.05),
        "gm_ln_b": nrm((nc, E), 0.01),
        "gm_w_s": nrm((nc, GM_HEADS, GM_CHUNK, GM_CHUNK), GM_CHUNK ** -0.5),
        "gm_b_s": 1.0 + nrm((nc, GM_HEADS, GM_CHUNK), 0.01),
        "gm_w_out": nrm((nc, E, D), E ** -0.5),
        "gm_b_out": nrm((nc, D), 0.01),
        "at_w_qkv": nrm((nd, D, N_PATTERNS * 3 * AT_HEADS * HEAD_DIM), D ** -0.5),
        "at_w_out": nrm((nd, AT_HEADS * HEAD_DIM, D), (AT_HEADS * HEAD_DIM) ** -0.5),
    }


def _fwd_reference(x, norm_pre, norm_post, ffn_w1, ffn_w3, ffn_w2, rel_bias,
              s5_w_in, s5_a_re, s5_a_im, s5_log_dt, s5_b_re, s5_b_im, s5_c_re, s5_c_im,
              s5_d, s5_w_glu, s5_b_glu, s5_w_out,
              cv_w_in, cv_b_in, cv_dw, cv_dw_b, cv_ln_g, cv_ln_b, cv_w_out, cv_b_out,
              gm_w_in, gm_b_in, gm_ln_g, gm_ln_b, gm_w_s, gm_b_s, gm_w_out, gm_b_out,
              at_w_qkv, at_w_out):
    for i in range(DEPTH):
        kind, j = i % N_MIXERS, i // N_MIXERS
        h = swiglu(rmsnorm(x, norm_pre[i, 0]), ffn_w1[i, 0], ffn_w3[i, 0], ffn_w2[i, 0])
        x = x + 0.5 * rmsnorm(h, norm_post[i, 0])
        h = rmsnorm(x, norm_pre[i, 1])
        if kind == 0:
            h = s5_mixer(h, s5_w_in[j], s5_a_re[j], s5_a_im[j], s5_log_dt[j], s5_b_re[j], s5_b_im[j],
                         s5_c_re[j], s5_c_im[j], s5_d[j], s5_w_glu[j], s5_b_glu[j], s5_w_out[j])
        elif kind == 1:
            h = conv_mixer(h, cv_w_in[j], cv_b_in[j], cv_dw[j], cv_dw_b[j], cv_ln_g[j], cv_ln_b[j],
                           cv_w_out[j], cv_b_out[j])
        elif kind == 2:
            h = gmlp_mixer(h, gm_w_in[j], gm_b_in[j], gm_ln_g[j], gm_ln_b[j], gm_w_s[j], gm_b_s[j],
                           gm_w_out[j], gm_b_out[j])
        else:
            h = attention_mixer(h, at_w_qkv[j], at_w_out[j], rel_bias)
        x = x + rmsnorm(h, norm_post[i, 1])
        h = swiglu(rmsnorm(x, norm_pre[i, 2]), ffn_w1[i, 1], ffn_w3[i, 1], ffn_w2[i, 1])
        x = x + 0.5 * rmsnorm(h, norm_post[i, 2])
    return x


import jax as _jax
import jax.numpy as _jnp

TWIN_FORMAT = 'train_step'
FWD_PARAMS = ['x', 'norm_pre', 'norm_post', 'ffn_w1', 'ffn_w3', 'ffn_w2', 'rel_bias', 's5_w_in', 's5_a_re', 's5_a_im', 's5_log_dt', 's5_b_re', 's5_b_im', 's5_c_re', 's5_c_im', 's5_d', 's5_w_glu', 's5_b_glu', 's5_w_out', 'cv_w_in', 'cv_b_in', 'cv_dw', 'cv_dw_b', 'cv_ln_g', 'cv_ln_b', 'cv_w_out', 'cv_b_out', 'gm_w_in', 'gm_b_in', 'gm_ln_g', 'gm_ln_b', 'gm_w_s', 'gm_b_s', 'gm_w_out', 'gm_b_out', 'at_w_qkv', 'at_w_out']
TWIN_WEIGHTS = ['norm_pre', 'norm_post', 'ffn_w1', 'ffn_w3', 'ffn_w2', 'rel_bias', 's5_w_in', 's5_a_re', 's5_a_im', 's5_log_dt', 's5_b_re', 's5_b_im', 's5_c_re', 's5_c_im', 's5_d', 's5_w_glu', 's5_b_glu', 's5_w_out', 'cv_w_in', 'cv_b_in', 'cv_dw', 'cv_dw_b', 'cv_ln_g', 'cv_ln_b', 'cv_w_out', 'cv_b_out', 'gm_w_in', 'gm_b_in', 'gm_ln_g', 'gm_ln_b', 'gm_w_s', 'gm_b_s', 'gm_w_out', 'gm_b_out', 'at_w_qkv', 'at_w_out']
TWIN_DIFF_INPUT = 'x'
TWIN_INPUTS = ['x', 'norm_pre', 'norm_post', 'ffn_w1', 'ffn_w3', 'ffn_w2', 'rel_bias', 's5_w_in', 's5_a_re', 's5_a_im', 's5_log_dt', 's5_b_re', 's5_b_im', 's5_c_re', 's5_c_im', 's5_d', 's5_w_glu', 's5_b_glu', 's5_w_out', 'cv_w_in', 'cv_b_in', 'cv_dw', 'cv_dw_b', 'cv_ln_g', 'cv_ln_b', 'cv_w_out', 'cv_b_out', 'gm_w_in', 'gm_b_in', 'gm_ln_g', 'gm_ln_b', 'gm_w_s', 'gm_b_s', 'gm_w_out', 'gm_b_out', 'at_w_qkv', 'at_w_out', 'loss_target', 'm_norm_pre', 'm_norm_post', 'm_ffn_w1', 'm_ffn_w3', 'm_ffn_w2', 'm_rel_bias', 'm_s5_w_in', 'm_s5_a_re', 'm_s5_a_im', 'm_s5_log_dt', 'm_s5_b_re', 'm_s5_b_im', 'm_s5_c_re', 'm_s5_c_im', 'm_s5_d', 'm_s5_w_glu', 'm_s5_b_glu', 'm_s5_w_out', 'm_cv_w_in', 'm_cv_b_in', 'm_cv_dw', 'm_cv_dw_b', 'm_cv_ln_g', 'm_cv_ln_b', 'm_cv_w_out', 'm_cv_b_out', 'm_gm_w_in', 'm_gm_b_in', 'm_gm_ln_g', 'm_gm_ln_b', 'm_gm_w_s', 'm_gm_b_s', 'm_gm_w_out', 'm_gm_b_out', 'm_at_w_qkv', 'm_at_w_out', 'v_norm_pre', 'v_norm_post', 'v_ffn_w1', 'v_ffn_w3', 'v_ffn_w2', 'v_rel_bias', 'v_s5_w_in', 'v_s5_a_re', 'v_s5_a_im', 'v_s5_log_dt', 'v_s5_b_re', 'v_s5_b_im', 'v_s5_c_re', 'v_s5_c_im', 'v_s5_d', 'v_s5_w_glu', 'v_s5_b_glu', 'v_s5_w_out', 'v_cv_w_in', 'v_cv_b_in', 'v_cv_dw', 'v_cv_dw_b', 'v_cv_ln_g', 'v_cv_ln_b', 'v_cv_w_out', 'v_cv_b_out', 'v_gm_w_in', 'v_gm_b_in', 'v_gm_ln_g', 'v_gm_ln_b', 'v_gm_w_s', 'v_gm_b_s', 'v_gm_w_out', 'v_gm_b_out', 'v_at_w_qkv', 'v_at_w_out']
TWIN_OUTPUTS = ['loss', 'grad_x', 'grad_norm_pre', 'grad_norm_post', 'grad_ffn_w1', 'grad_ffn_w3', 'grad_ffn_w2', 'grad_rel_bias', 'grad_s5_w_in', 'grad_s5_a_re', 'grad_s5_a_im', 'grad_s5_log_dt', 'grad_s5_b_re', 'grad_s5_b_im', 'grad_s5_c_re', 'grad_s5_c_im', 'grad_s5_d', 'grad_s5_w_glu', 'grad_s5_b_glu', 'grad_s5_w_out', 'grad_cv_w_in', 'grad_cv_b_in', 'grad_cv_dw', 'grad_cv_dw_b', 'grad_cv_ln_g', 'grad_cv_ln_b', 'grad_cv_w_out', 'grad_cv_b_out', 'grad_gm_w_in', 'grad_gm_b_in', 'grad_gm_ln_g', 'grad_gm_ln_b', 'grad_gm_w_s', 'grad_gm_b_s', 'grad_gm_w_out', 'grad_gm_b_out', 'grad_at_w_qkv', 'grad_at_w_out', 'delta_norm_pre', 'delta_norm_post', 'delta_ffn_w1', 'delta_ffn_w3', 'delta_ffn_w2', 'delta_rel_bias', 'delta_s5_w_in', 'delta_s5_a_re', 'delta_s5_a_im', 'delta_s5_log_dt', 'delta_s5_b_re', 'delta_s5_b_im', 'delta_s5_c_re', 'delta_s5_c_im', 'delta_s5_d', 'delta_s5_w_glu', 'delta_s5_b_glu', 'delta_s5_w_out', 'delta_cv_w_in', 'delta_cv_b_in', 'delta_cv_dw', 'delta_cv_dw_b', 'delta_cv_ln_g', 'delta_cv_ln_b', 'delta_cv_w_out', 'delta_cv_b_out', 'delta_gm_w_in', 'delta_gm_b_in', 'delta_gm_ln_g', 'delta_gm_ln_b', 'delta_gm_w_s', 'delta_gm_b_s', 'delta_gm_w_out', 'delta_gm_b_out', 'delta_at_w_qkv', 'delta_at_w_out', 'new_m_norm_pre', 'new_m_norm_post', 'new_m_ffn_w1', 'new_m_ffn_w3', 'new_m_ffn_w2', 'new_m_rel_bias', 'new_m_s5_w_in', 'new_m_s5_a_re', 'new_m_s5_a_im', 'new_m_s5_log_dt', 'new_m_s5_b_re', 'new_m_s5_b_im', 'new_m_s5_c_re', 'new_m_s5_c_im', 'new_m_s5_d', 'new_m_s5_w_glu', 'new_m_s5_b_glu', 'new_m_s5_w_out', 'new_m_cv_w_in', 'new_m_cv_b_in', 'new_m_cv_dw', 'new_m_cv_dw_b', 'new_m_cv_ln_g', 'new_m_cv_ln_b', 'new_m_cv_w_out', 'new_m_cv_b_out', 'new_m_gm_w_in', 'new_m_gm_b_in', 'new_m_gm_ln_g', 'new_m_gm_ln_b', 'new_m_gm_w_s', 'new_m_gm_b_s', 'new_m_gm_w_out', 'new_m_gm_b_out', 'new_m_at_w_qkv', 'new_m_at_w_out', 'new_v_norm_pre', 'new_v_norm_post', 'new_v_ffn_w1', 'new_v_ffn_w3', 'new_v_ffn_w2', 'new_v_rel_bias', 'new_v_s5_w_in', 'new_v_s5_a_re', 'new_v_s5_a_im', 'new_v_s5_log_dt', 'new_v_s5_b_re', 'new_v_s5_b_im', 'new_v_s5_c_re', 'new_v_s5_c_im', 'new_v_s5_d', 'new_v_s5_w_glu', 'new_v_s5_b_glu', 'new_v_s5_w_out', 'new_v_cv_w_in', 'new_v_cv_b_in', 'new_v_cv_dw', 'new_v_cv_dw_b', 'new_v_cv_ln_g', 'new_v_cv_ln_b', 'new_v_cv_w_out', 'new_v_cv_b_out', 'new_v_gm_w_in', 'new_v_gm_b_in', 'new_v_gm_ln_g', 'new_v_gm_ln_b', 'new_v_gm_w_s', 'new_v_gm_b_s', 'new_v_gm_w_out', 'new_v_gm_b_out', 'new_v_at_w_qkv', 'new_v_at_w_out']
TWIN_LEAF_KINDS = {'loss': 'loss', 'grad_x': 'grad_x', 'grad_norm_pre': 'grad_w', 'grad_norm_post': 'grad_w', 'grad_ffn_w1': 'grad_w', 'grad_ffn_w3': 'grad_w', 'grad_ffn_w2': 'grad_w', 'grad_rel_bias': 'grad_w', 'grad_s5_w_in': 'grad_w', 'grad_s5_a_re': 'grad_w', 'grad_s5_a_im': 'grad_w', 'grad_s5_log_dt': 'grad_w', 'grad_s5_b_re': 'grad_w', 'grad_s5_b_im': 'grad_w', 'grad_s5_c_re': 'grad_w', 'grad_s5_c_im': 'grad_w', 'grad_s5_d': 'grad_w', 'grad_s5_w_glu': 'grad_w', 'grad_s5_b_glu': 'grad_w', 'grad_s5_w_out': 'grad_w', 'grad_cv_w_in': 'grad_w', 'grad_cv_b_in': 'grad_w', 'grad_cv_dw': 'grad_w', 'grad_cv_dw_b': 'grad_w', 'grad_cv_ln_g': 'grad_w', 'grad_cv_ln_b': 'grad_w', 'grad_cv_w_out': 'grad_w', 'grad_cv_b_out': 'grad_w', 'grad_gm_w_in': 'grad_w', 'grad_gm_b_in': 'grad_w', 'grad_gm_ln_g': 'grad_w', 'grad_gm_ln_b': 'grad_w', 'grad_gm_w_s': 'grad_w', 'grad_gm_b_s': 'grad_w', 'grad_gm_w_out': 'grad_w', 'grad_gm_b_out': 'grad_w', 'grad_at_w_qkv': 'grad_w', 'grad_at_w_out': 'grad_w', 'delta_norm_pre': 'delta_w', 'delta_norm_post': 'delta_w', 'delta_ffn_w1': 'delta_w', 'delta_ffn_w3': 'delta_w', 'delta_ffn_w2': 'delta_w', 'delta_rel_bias': 'delta_w', 'delta_s5_w_in': 'delta_w', 'delta_s5_a_re': 'delta_w', 'delta_s5_a_im': 'delta_w', 'delta_s5_log_dt': 'delta_w', 'delta_s5_b_re': 'delta_w', 'delta_s5_b_im': 'delta_w', 'delta_s5_c_re': 'delta_w', 'delta_s5_c_im': 'delta_w', 'delta_s5_d': 'delta_w', 'delta_s5_w_glu': 'delta_w', 'delta_s5_b_glu': 'delta_w', 'delta_s5_w_out': 'delta_w', 'delta_cv_w_in': 'delta_w', 'delta_cv_b_in': 'delta_w', 'delta_cv_dw': 'delta_w', 'delta_cv_dw_b': 'delta_w', 'delta_cv_ln_g': 'delta_w', 'delta_cv_ln_b': 'delta_w', 'delta_cv_w_out': 'delta_w', 'delta_cv_b_out': 'delta_w', 'delta_gm_w_in': 'delta_w', 'delta_gm_b_in': 'delta_w', 'delta_gm_ln_g': 'delta_w', 'delta_gm_ln_b': 'delta_w', 'delta_gm_w_s': 'delta_w', 'delta_gm_b_s': 'delta_w', 'delta_gm_w_out': 'delta_w', 'delta_gm_b_out': 'delta_w', 'delta_at_w_qkv': 'delta_w', 'delta_at_w_out': 'delta_w', 'new_m_norm_pre': 'new_m', 'new_m_norm_post': 'new_m', 'new_m_ffn_w1': 'new_m', 'new_m_ffn_w3': 'new_m', 'new_m_ffn_w2': 'new_m', 'new_m_rel_bias': 'new_m', 'new_m_s5_w_in': 'new_m', 'new_m_s5_a_re': 'new_m', 'new_m_s5_a_im': 'new_m', 'new_m_s5_log_dt': 'new_m', 'new_m_s5_b_re': 'new_m', 'new_m_s5_b_im': 'new_m', 'new_m_s5_c_re': 'new_m', 'new_m_s5_c_im': 'new_m', 'new_m_s5_d': 'new_m', 'new_m_s5_w_glu': 'new_m', 'new_m_s5_b_glu': 'new_m', 'new_m_s5_w_out': 'new_m', 'new_m_cv_w_in': 'new_m', 'new_m_cv_b_in': 'new_m', 'new_m_cv_dw': 'new_m', 'new_m_cv_dw_b': 'new_m', 'new_m_cv_ln_g': 'new_m', 'new_m_cv_ln_b': 'new_m', 'new_m_cv_w_out': 'new_m', 'new_m_cv_b_out': 'new_m', 'new_m_gm_w_in': 'new_m', 'new_m_gm_b_in': 'new_m', 'new_m_gm_ln_g': 'new_m', 'new_m_gm_ln_b': 'new_m', 'new_m_gm_w_s': 'new_m', 'new_m_gm_b_s': 'new_m', 'new_m_gm_w_out': 'new_m', 'new_m_gm_b_out': 'new_m', 'new_m_at_w_qkv': 'new_m', 'new_m_at_w_out': 'new_m', 'new_v_norm_pre': 'new_v', 'new_v_norm_post': 'new_v', 'new_v_ffn_w1': 'new_v', 'new_v_ffn_w3': 'new_v', 'new_v_ffn_w2': 'new_v', 'new_v_rel_bias': 'new_v', 'new_v_s5_w_in': 'new_v', 'new_v_s5_a_re': 'new_v', 'new_v_s5_a_im': 'new_v', 'new_v_s5_log_dt': 'new_v', 'new_v_s5_b_re': 'new_v', 'new_v_s5_b_im': 'new_v', 'new_v_s5_c_re': 'new_v', 'new_v_s5_c_im': 'new_v', 'new_v_s5_d': 'new_v', 'new_v_s5_w_glu': 'new_v', 'new_v_s5_b_glu': 'new_v', 'new_v_s5_w_out': 'new_v', 'new_v_cv_w_in': 'new_v', 'new_v_cv_b_in': 'new_v', 'new_v_cv_dw': 'new_v', 'new_v_cv_dw_b': 'new_v', 'new_v_cv_ln_g': 'new_v', 'new_v_cv_ln_b': 'new_v', 'new_v_cv_w_out': 'new_v', 'new_v_cv_b_out': 'new_v', 'new_v_gm_w_in': 'new_v', 'new_v_gm_b_in': 'new_v', 'new_v_gm_ln_g': 'new_v', 'new_v_gm_ln_b': 'new_v', 'new_v_gm_w_s': 'new_v', 'new_v_gm_b_s': 'new_v', 'new_v_gm_w_out': 'new_v', 'new_v_gm_b_out': 'new_v', 'new_v_at_w_qkv': 'new_v', 'new_v_at_w_out': 'new_v'}


def _forward(args):
    return _fwd_reference(*[args[k] for k in FWD_PARAMS])


def _output_shape():
    def fwd():
        inp = _fwd_setup_inputs(0)
        return _fwd_reference(*[inp[k] for k in FWD_PARAMS])
    out = _jax.eval_shape(fwd)
    return out.shape, out.dtype

N_MICROBATCH = 1
ADAM_LR = 0.001
ADAM_B1 = 0.9
ADAM_B2 = 0.999
ADAM_EPS = 1e-08
ADAM_WD = 0.01
ADAM_STEP = 10
PER_EXAMPLE_BATCH_AXIS = {'x': 0, 'loss_target': 0}
SHARED_INPUTS = []
_WEIGHT_DTYPES = {'norm_pre': _jnp.float32, 'norm_post': _jnp.float32, 'ffn_w1': _jnp.float32, 'ffn_w3': _jnp.float32, 'ffn_w2': _jnp.float32, 'rel_bias': _jnp.float32, 's5_w_in': _jnp.float32, 's5_a_re': _jnp.float32, 's5_a_im': _jnp.float32, 's5_log_dt': _jnp.float32, 's5_b_re': _jnp.float32, 's5_b_im': _jnp.float32, 's5_c_re': _jnp.float32, 's5_c_im': _jnp.float32, 's5_d': _jnp.float32, 's5_w_glu': _jnp.float32, 's5_b_glu': _jnp.float32, 's5_w_out': _jnp.float32, 'cv_w_in': _jnp.float32, 'cv_b_in': _jnp.float32, 'cv_dw': _jnp.float32, 'cv_dw_b': _jnp.float32, 'cv_ln_g': _jnp.float32, 'cv_ln_b': _jnp.float32, 'cv_w_out': _jnp.float32, 'cv_b_out': _jnp.float32, 'gm_w_in': _jnp.float32, 'gm_b_in': _jnp.float32, 'gm_ln_g': _jnp.float32, 'gm_ln_b': _jnp.float32, 'gm_w_s': _jnp.float32, 'gm_b_s': _jnp.float32, 'gm_w_out': _jnp.float32, 'gm_b_out': _jnp.float32, 'at_w_qkv': _jnp.float32, 'at_w_out': _jnp.float32}
MOMENT_SCALE = {'norm_pre': 3.181656e+01, 'norm_post': 1.014304e+02, 'ffn_w1': 4.592609e+00, 'ffn_w3': 7.681918e+00, 'ffn_w2': 1.301682e+01, 'rel_bias': 9.741015e+00, 's5_w_in': 4.452705e+00, 's5_a_re': 3.379409e-01, 's5_a_im': 2.046220e-01, 's5_log_dt': 4.725558e+01, 's5_b_re': 1.673114e-01, 's5_b_im': 1.758097e-01, 's5_c_re': 3.063469e-01, 's5_c_im': 3.265841e-01, 's5_d': 1.111071e+02, 's5_w_glu': 1.579700e+01, 's5_b_glu': 4.757498e+01, 's5_w_out': 1.025863e+02, 'cv_w_in': 3.162413e+01, 'cv_b_in': 1.031722e+02, 'cv_dw': 4.924449e+01, 'cv_dw_b': 2.835804e+02, 'cv_ln_g': 1.184296e+02, 'cv_ln_b': 1.626710e+02, 'cv_w_out': 7.846999e+01, 'cv_b_out': 3.190470e+02, 'gm_w_in': 1.885011e+01, 'gm_b_in': 5.959163e+01, 'gm_ln_g': 6.713707e-01, 'gm_ln_b': 1.174887e+00, 'gm_w_s': 8.018491e-01, 'gm_b_s': 3.253376e+00, 'gm_w_out': 7.236643e+01, 'gm_b_out': 2.191259e+02, 'at_w_qkv': 2.641764e+01, 'at_w_out': 8.086093e+01}


def _to_microbatches(a, axis):
    t = _jnp.moveaxis(a, axis, 0)
    t = t.reshape((N_MICROBATCH, t.shape[0] // N_MICROBATCH) + t.shape[1:])
    return _jnp.moveaxis(t, 1, axis + 1)


def setup_inputs(seed: int = 0) -> dict:
    inp = _fwd_setup_inputs(seed)
    key = _jax.random.fold_in(_jax.random.key(seed), 7919)
    shape, _ = _output_shape()
    out = dict(inp)
    out["loss_target"] = _jax.random.normal(_jax.random.fold_in(key, 0), shape, _jnp.float32)
    for i, name in enumerate(TWIN_WEIGHTS):
        w = inp[name].astype(_jnp.float32)
        if MOMENT_SCALE is None:
            s = _jnp.sqrt(_jnp.mean(_jnp.square(w)) + 1e-30)
        else:
            s = MOMENT_SCALE[name]
        km, kv = _jax.random.split(_jax.random.fold_in(key, i + 1))
        out[name] = w
        out["m_" + name] = s * _jax.random.normal(km, w.shape, _jnp.float32)
        out["v_" + name] = (s * s) * _jax.random.uniform(kv, w.shape, _jnp.float32, 0.5, 1.5)
    if N_MICROBATCH > 1:
        for name, axis in PER_EXAMPLE_BATCH_AXIS.items():
            out[name] = _to_microbatches(out[name], axis)
    return {'x': out['x'], 'norm_pre': out['norm_pre'], 'norm_post': out['norm_post'], 'ffn_w1': out['ffn_w1'], 'ffn_w3': out['ffn_w3'], 'ffn_w2': out['ffn_w2'], 'rel_bias': out['rel_bias'], 's5_w_in': out['s5_w_in'], 's5_a_re': out['s5_a_re'], 's5_a_im': out['s5_a_im'], 's5_log_dt': out['s5_log_dt'], 's5_b_re': out['s5_b_re'], 's5_b_im': out['s5_b_im'], 's5_c_re': out['s5_c_re'], 's5_c_im': out['s5_c_im'], 's5_d': out['s5_d'], 's5_w_glu': out['s5_w_glu'], 's5_b_glu': out['s5_b_glu'], 's5_w_out': out['s5_w_out'], 'cv_w_in': out['cv_w_in'], 'cv_b_in': out['cv_b_in'], 'cv_dw': out['cv_dw'], 'cv_dw_b': out['cv_dw_b'], 'cv_ln_g': out['cv_ln_g'], 'cv_ln_b': out['cv_ln_b'], 'cv_w_out': out['cv_w_out'], 'cv_b_out': out['cv_b_out'], 'gm_w_in': out['gm_w_in'], 'gm_b_in': out['gm_b_in'], 'gm_ln_g': out['gm_ln_g'], 'gm_ln_b': out['gm_ln_b'], 'gm_w_s': out['gm_w_s'], 'gm_b_s': out['gm_b_s'], 'gm_w_out': out['gm_w_out'], 'gm_b_out': out['gm_b_out'], 'at_w_qkv': out['at_w_qkv'], 'at_w_out': out['at_w_out'], 'loss_target': out['loss_target'], 'm_norm_pre': out['m_norm_pre'], 'm_norm_post': out['m_norm_post'], 'm_ffn_w1': out['m_ffn_w1'], 'm_ffn_w3': out['m_ffn_w3'], 'm_ffn_w2': out['m_ffn_w2'], 'm_rel_bias': out['m_rel_bias'], 'm_s5_w_in': out['m_s5_w_in'], 'm_s5_a_re': out['m_s5_a_re'], 'm_s5_a_im': out['m_s5_a_im'], 'm_s5_log_dt': out['m_s5_log_dt'], 'm_s5_b_re': out['m_s5_b_re'], 'm_s5_b_im': out['m_s5_b_im'], 'm_s5_c_re': out['m_s5_c_re'], 'm_s5_c_im': out['m_s5_c_im'], 'm_s5_d': out['m_s5_d'], 'm_s5_w_glu': out['m_s5_w_glu'], 'm_s5_b_glu': out['m_s5_b_glu'], 'm_s5_w_out': out['m_s5_w_out'], 'm_cv_w_in': out['m_cv_w_in'], 'm_cv_b_in': out['m_cv_b_in'], 'm_cv_dw': out['m_cv_dw'], 'm_cv_dw_b': out['m_cv_dw_b'], 'm_cv_ln_g': out['m_cv_ln_g'], 'm_cv_ln_b': out['m_cv_ln_b'], 'm_cv_w_out': out['m_cv_w_out'], 'm_cv_b_out': out['m_cv_b_out'], 'm_gm_w_in': out['m_gm_w_in'], 'm_gm_b_in': out['m_gm_b_in'], 'm_gm_ln_g': out['m_gm_ln_g'], 'm_gm_ln_b': out['m_gm_ln_b'], 'm_gm_w_s': out['m_gm_w_s'], 'm_gm_b_s': out['m_gm_b_s'], 'm_gm_w_out': out['m_gm_w_out'], 'm_gm_b_out': out['m_gm_b_out'], 'm_at_w_qkv': out['m_at_w_qkv'], 'm_at_w_out': out['m_at_w_out'], 'v_norm_pre': out['v_norm_pre'], 'v_norm_post': out['v_norm_post'], 'v_ffn_w1': out['v_ffn_w1'], 'v_ffn_w3': out['v_ffn_w3'], 'v_ffn_w2': out['v_ffn_w2'], 'v_rel_bias': out['v_rel_bias'], 'v_s5_w_in': out['v_s5_w_in'], 'v_s5_a_re': out['v_s5_a_re'], 'v_s5_a_im': out['v_s5_a_im'], 'v_s5_log_dt': out['v_s5_log_dt'], 'v_s5_b_re': out['v_s5_b_re'], 'v_s5_b_im': out['v_s5_b_im'], 'v_s5_c_re': out['v_s5_c_re'], 'v_s5_c_im': out['v_s5_c_im'], 'v_s5_d': out['v_s5_d'], 'v_s5_w_glu': out['v_s5_w_glu'], 'v_s5_b_glu': out['v_s5_b_glu'], 'v_s5_w_out': out['v_s5_w_out'], 'v_cv_w_in': out['v_cv_w_in'], 'v_cv_b_in': out['v_cv_b_in'], 'v_cv_dw': out['v_cv_dw'], 'v_cv_dw_b': out['v_cv_dw_b'], 'v_cv_ln_g': out['v_cv_ln_g'], 'v_cv_ln_b': out['v_cv_ln_b'], 'v_cv_w_out': out['v_cv_w_out'], 'v_cv_b_out': out['v_cv_b_out'], 'v_gm_w_in': out['v_gm_w_in'], 'v_gm_b_in': out['v_gm_b_in'], 'v_gm_ln_g': out['v_gm_ln_g'], 'v_gm_ln_b': out['v_gm_ln_b'], 'v_gm_w_s': out['v_gm_w_s'], 'v_gm_b_s': out['v_gm_b_s'], 'v_gm_w_out': out['v_gm_w_out'], 'v_gm_b_out': out['v_gm_b_out'], 'v_at_w_qkv': out['v_at_w_qkv'], 'v_at_w_out': out['v_at_w_out']}


def _loss(weights, diff, rest, loss_target):
    with _jax.named_scope("forward"):
        args = {**rest, TWIN_DIFF_INPUT: diff, **{k: w.astype(_WEIGHT_DTYPES[k]) for k, w in weights.items()}}
        y = _forward(args)
    with _jax.named_scope("loss_head"):
        err = _jnp.square(y.astype(_jnp.float32) - loss_target)
        return 0.5 * _jnp.sum(_jnp.mean(err, axis=-1)) if err.ndim else 0.5 * err


def _adamw(w, g, m, v):
    m = ADAM_B1 * m + (1.0 - ADAM_B1) * g
    v = ADAM_B2 * v + (1.0 - ADAM_B2) * _jnp.square(g)
    m_hat = m / (1.0 - ADAM_B1 ** ADAM_STEP)
    v_hat = v / (1.0 - ADAM_B2 ** ADAM_STEP)
    delta = -ADAM_LR * (m_hat / (_jnp.sqrt(v_hat) + ADAM_EPS) + ADAM_WD * w)
    return delta, m, v


def reference(x, norm_pre, norm_post, ffn_w1, ffn_w3, ffn_w2, rel_bias, s5_w_in, s5_a_re, s5_a_im, s5_log_dt, s5_b_re, s5_b_im, s5_c_re, s5_c_im, s5_d, s5_w_glu, s5_b_glu, s5_w_out, cv_w_in, cv_b_in, cv_dw, cv_dw_b, cv_ln_g, cv_ln_b, cv_w_out, cv_b_out, gm_w_in, gm_b_in, gm_ln_g, gm_ln_b, gm_w_s, gm_b_s, gm_w_out, gm_b_out, at_w_qkv, at_w_out, loss_target, m_norm_pre, m_norm_post, m_ffn_w1, m_ffn_w3, m_ffn_w2, m_rel_bias, m_s5_w_in, m_s5_a_re, m_s5_a_im, m_s5_log_dt, m_s5_b_re, m_s5_b_im, m_s5_c_re, m_s5_c_im, m_s5_d, m_s5_w_glu, m_s5_b_glu, m_s5_w_out, m_cv_w_in, m_cv_b_in, m_cv_dw, m_cv_dw_b, m_cv_ln_g, m_cv_ln_b, m_cv_w_out, m_cv_b_out, m_gm_w_in, m_gm_b_in, m_gm_ln_g, m_gm_ln_b, m_gm_w_s, m_gm_b_s, m_gm_w_out, m_gm_b_out, m_at_w_qkv, m_at_w_out, v_norm_pre, v_norm_post, v_ffn_w1, v_ffn_w3, v_ffn_w2, v_rel_bias, v_s5_w_in, v_s5_a_re, v_s5_a_im, v_s5_log_dt, v_s5_b_re, v_s5_b_im, v_s5_c_re, v_s5_c_im, v_s5_d, v_s5_w_glu, v_s5_b_glu, v_s5_w_out, v_cv_w_in, v_cv_b_in, v_cv_dw, v_cv_dw_b, v_cv_ln_g, v_cv_ln_b, v_cv_w_out, v_cv_b_out, v_gm_w_in, v_gm_b_in, v_gm_ln_g, v_gm_ln_b, v_gm_w_s, v_gm_b_s, v_gm_w_out, v_gm_b_out, v_at_w_qkv, v_at_w_out):
    given = dict(x=x, norm_pre=norm_pre, norm_post=norm_post, ffn_w1=ffn_w1, ffn_w3=ffn_w3, ffn_w2=ffn_w2, rel_bias=rel_bias, s5_w_in=s5_w_in, s5_a_re=s5_a_re, s5_a_im=s5_a_im, s5_log_dt=s5_log_dt, s5_b_re=s5_b_re, s5_b_im=s5_b_im, s5_c_re=s5_c_re, s5_c_im=s5_c_im, s5_d=s5_d, s5_w_glu=s5_w_glu, s5_b_glu=s5_b_glu, s5_w_out=s5_w_out, cv_w_in=cv_w_in, cv_b_in=cv_b_in, cv_dw=cv_dw, cv_dw_b=cv_dw_b, cv_ln_g=cv_ln_g, cv_ln_b=cv_ln_b, cv_w_out=cv_w_out, cv_b_out=cv_b_out, gm_w_in=gm_w_in, gm_b_in=gm_b_in, gm_ln_g=gm_ln_g, gm_ln_b=gm_ln_b, gm_w_s=gm_w_s, gm_b_s=gm_b_s, gm_w_out=gm_w_out, gm_b_out=gm_b_out, at_w_qkv=at_w_qkv, at_w_out=at_w_out, loss_target=loss_target, m_norm_pre=m_norm_pre, m_norm_post=m_norm_post, m_ffn_w1=m_ffn_w1, m_ffn_w3=m_ffn_w3, m_ffn_w2=m_ffn_w2, m_rel_bias=m_rel_bias, m_s5_w_in=m_s5_w_in, m_s5_a_re=m_s5_a_re, m_s5_a_im=m_s5_a_im, m_s5_log_dt=m_s5_log_dt, m_s5_b_re=m_s5_b_re, m_s5_b_im=m_s5_b_im, m_s5_c_re=m_s5_c_re, m_s5_c_im=m_s5_c_im, m_s5_d=m_s5_d, m_s5_w_glu=m_s5_w_glu, m_s5_b_glu=m_s5_b_glu, m_s5_w_out=m_s5_w_out, m_cv_w_in=m_cv_w_in, m_cv_b_in=m_cv_b_in, m_cv_dw=m_cv_dw, m_cv_dw_b=m_cv_dw_b, m_cv_ln_g=m_cv_ln_g, m_cv_ln_b=m_cv_ln_b, m_cv_w_out=m_cv_w_out, m_cv_b_out=m_cv_b_out, m_gm_w_in=m_gm_w_in, m_gm_b_in=m_gm_b_in, m_gm_ln_g=m_gm_ln_g, m_gm_ln_b=m_gm_ln_b, m_gm_w_s=m_gm_w_s, m_gm_b_s=m_gm_b_s, m_gm_w_out=m_gm_w_out, m_gm_b_out=m_gm_b_out, m_at_w_qkv=m_at_w_qkv, m_at_w_out=m_at_w_out, v_norm_pre=v_norm_pre, v_norm_post=v_norm_post, v_ffn_w1=v_ffn_w1, v_ffn_w3=v_ffn_w3, v_ffn_w2=v_ffn_w2, v_rel_bias=v_rel_bias, v_s5_w_in=v_s5_w_in, v_s5_a_re=v_s5_a_re, v_s5_a_im=v_s5_a_im, v_s5_log_dt=v_s5_log_dt, v_s5_b_re=v_s5_b_re, v_s5_b_im=v_s5_b_im, v_s5_c_re=v_s5_c_re, v_s5_c_im=v_s5_c_im, v_s5_d=v_s5_d, v_s5_w_glu=v_s5_w_glu, v_s5_b_glu=v_s5_b_glu, v_s5_w_out=v_s5_w_out, v_cv_w_in=v_cv_w_in, v_cv_b_in=v_cv_b_in, v_cv_dw=v_cv_dw, v_cv_dw_b=v_cv_dw_b, v_cv_ln_g=v_cv_ln_g, v_cv_ln_b=v_cv_ln_b, v_cv_w_out=v_cv_w_out, v_cv_b_out=v_cv_b_out, v_gm_w_in=v_gm_w_in, v_gm_b_in=v_gm_b_in, v_gm_ln_g=v_gm_ln_g, v_gm_ln_b=v_gm_ln_b, v_gm_w_s=v_gm_w_s, v_gm_b_s=v_gm_b_s, v_gm_w_out=v_gm_w_out, v_gm_b_out=v_gm_b_out, v_at_w_qkv=v_at_w_qkv, v_at_w_out=v_at_w_out)
    weights = {n: given[n] for n in TWIN_WEIGHTS}
    shared = {n: given[n] for n in SHARED_INPUTS}
    per_example = {n: given[n] for n in ['x']}
    grad_fn = _jax.value_and_grad(_loss, argnums=(0, 1))

    def one_microbatch(ex, loss_target):
        ex = dict(ex)
        diff = ex.pop(TWIN_DIFF_INPUT)
        return grad_fn(weights, diff, {**shared, **ex}, loss_target)

    if N_MICROBATCH == 1:
        loss, (grad_w, grad_x) = one_microbatch(per_example, given["loss_target"])
    else:
        def body(carry, xs):
            loss_sum, grad_sum = carry
            l_k, (gw_k, gx_k) = one_microbatch(xs[0], xs[1])
            with _jax.named_scope("update"):
                return (loss_sum + l_k, _jax.tree.map(_jnp.add, grad_sum, gw_k)), gx_k

        init = (_jnp.zeros((), _jnp.float32), _jax.tree.map(_jnp.zeros_like, weights))
        (loss, grad_w), grad_x = _jax.lax.scan(body, init, (per_example, given["loss_target"]))
    with _jax.named_scope("update"):
        delta_w, new_m, new_v = {}, {}, {}
        for n in TWIN_WEIGHTS:
            delta_w[n], new_m[n], new_v[n] = _adamw(weights[n], grad_w[n], given["m_" + n], given["v_" + n])
    return (loss, grad_x, *[grad_w[n] for n in TWIN_WEIGHTS], *[delta_w[n] for n in TWIN_WEIGHTS],
            *[new_m[n] for n in TWIN_WEIGHTS], *[new_v[n] for n in TWIN_WEIGHTS])
```

```python
import functools
import math

import numpy as np

import jax
import jax.numpy as jnp
from jax import lax
from jax.experimental import pallas as pl
from jax.experimental.pallas import tpu as pltpu

F32 = jnp.float32
BF16 = jnp.bfloat16

D_MODEL = 1024
DEPTH = 4
D_FF = 2816
EPS = 1e-6
S5_GROUP = 16
S5_STATE = 64
CONV_W = 31
GM_CHUNK = 128
GM_HEADS = 8
HEAD_DIM = 64
PATTERNS = ((128, 1), (512, 4), (2048, 16))
BLOCK = 128
NUM_BUCKETS = 32
MAX_DISTANCE = 2048
ADAM_LR = 0.001
ADAM_B1 = 0.9
ADAM_B2 = 0.999
ADAM_EPS = 1e-08
ADAM_WD = 0.01
ADAM_STEP = 10

N_DEV = 8
AXES = ("x", "y", "c")
LANES = 128
GM_E = 2 * D_MODEL
S5_GROUPS = D_MODEL // S5_GROUP
S5_GB = LANES // S5_GROUP
S5_NB = D_MODEL // LANES
S5_BW = S5_GB * S5_STATE
S5_NS = S5_GROUPS * S5_STATE
AT_HEADS = D_MODEL // HEAD_DIM
VMEM_LIMIT = 56 * 1024 * 1024
PACK_ALIGN = 16 * 1024


def _cparams(sem):
    return pltpu.CompilerParams(dimension_semantics=sem, vmem_limit_bytes=VMEM_LIMIT)


def _pick(n, cap):
    if n <= cap:
        return n
    best = None
    for t in range(LANES, cap + 1, LANES):
        if n % t == 0:
            best = t
    assert best is not None, (n, cap)
    return best


def _pick_rows(n, cap):
    best = None
    for t in range(8, min(n, cap) + 1, 8):
        if n % t == 0:
            best = t
    assert best is not None, (n, cap)
    return best


def _mm(a, b, *, ta=False, tb=False, out_dtype=F32, name):
    if ta:
        K, M = a.shape
    else:
        M, K = a.shape
    if tb:
        N, K2 = b.shape
    else:
        K2, N = b.shape
    assert K == K2, (a.shape, b.shape, ta, tb)
    if ta:
        tm, tn, tk = _pick(M, 1408), _pick(N, 1408), _pick(K, 512)
    else:
        tm, tn, tk = _pick(M, 512), _pick(N, 1408), _pick(K, 2816)
    nk = K // tk
    a_spec = pl.BlockSpec((tk, tm), lambda i, j, k: (k, i)) if ta else pl.BlockSpec((tm, tk), lambda i, j, k: (i, k))
    b_spec = pl.BlockSpec((tn, tk), lambda i, j, k: (j, k)) if tb else pl.BlockSpec((tk, tn), lambda i, j, k: (k, j))
    dims = (((0 if ta else 1,), (1 if tb else 0,)), ((), ()))

    def body(a_ref, b_ref, o_ref, *scratch):
        p = lax.dot_general(a_ref[...].astype(BF16), b_ref[...].astype(BF16), dims, preferred_element_type=F32)
        if nk == 1:
            o_ref[...] = p.astype(o_ref.dtype)
        else:
            acc = scratch[0]
            k = pl.program_id(2)

            @pl.when(k == 0)
            def _():
                acc[...] = p

            @pl.when(k > 0)
            def _():
                acc[...] += p

            @pl.when(k == nk - 1)
            def _():
                o_ref[...] = acc[...].astype(o_ref.dtype)

    return pl.pallas_call(
        body, name=name, grid=(M // tm, N // tn, nk), in_specs=[a_spec, b_spec],
        out_specs=pl.BlockSpec((tm, tn), lambda i, j, k: (i, j)),
        out_shape=jax.ShapeDtypeStruct((M, N), out_dtype),
        scratch_shapes=[] if nk == 1 else [pltpu.VMEM((tm, tn), F32)],
        compiler_params=_cparams(("parallel", "parallel", "arbitrary")),
    )(a, b)


ROW_TILE_BYTES = 8 * 1024 * 1024


def _row_tile(arrays):
    row_bytes = sum(w * jnp.dtype(dt).itemsize for w, dt in arrays)
    for tile in (256, 128, 64, 32):
        if tile * row_bytes <= ROW_TILE_BYTES:
            return tile
    return 16


def _rows(fn, rows, pars, outs, *, name):
    T = rows[0].shape[0]
    tile = _row_tile([(r.shape[1], r.dtype) for r in rows] + list(outs))
    nr, npar = len(rows), len(pars)

    def body(*refs):
        r = [refs[i][...] for i in range(nr)]
        p = [refs[nr + i][...] for i in range(npar)]
        res = fn(*r, *p)
        for o_ref, o in zip(refs[nr + npar:], res):
            o_ref[...] = o.astype(o_ref.dtype)

    in_specs = [pl.BlockSpec((tile, r.shape[1]), lambda i: (i, 0)) for r in rows]
    in_specs += [pl.BlockSpec(p.shape, lambda i, nd=p.ndim: (0,) * nd) for p in pars]
    return pl.pallas_call(
        body, name=name, grid=(T // tile,), in_specs=in_specs,
        out_specs=[pl.BlockSpec((tile, w), lambda i: (i, 0)) for w, _ in outs],
        out_shape=[jax.ShapeDtypeStruct((T, w), dt) for w, dt in outs],
        compiler_params=_cparams(("parallel",)),
    )(*rows, *pars)


def _rows_vjp(fn, rows, pars, cts, *, dtypes, adds=None, name):
    adds = adds or {}
    cts = [c if isinstance(c, (tuple, list)) else (c,) for c in cts]
    flat_cts = [a for c in cts for a in c]
    add_keys = sorted(adds)
    add_arrs = [adds[k] for k in add_keys]
    want = [i for i, d in enumerate(dtypes) if d is not None]
    T = rows[0].shape[0]
    tile = _row_tile([(a.shape[1], a.dtype) for a in list(rows) + flat_cts + add_arrs]
                     + [(rows[i].shape[1], dtypes[i]) for i in want])
    nr, npar, nc, na = len(rows), len(pars), len(flat_cts), len(add_arrs)

    def body(*refs):
        r = [refs[i][...].astype(F32) for i in range(nr)]
        p = [refs[nr + i][...] for i in range(npar)]
        cvals = [refs[nr + npar + i][...].astype(F32) for i in range(nc)]
        avals = [refs[nr + npar + nc + i][...].astype(F32) for i in range(na)]
        outs = refs[nr + npar + nc + na:]
        ct, pos = [], 0
        for c in cts:
            s = cvals[pos]
            for extra in cvals[pos + 1:pos + len(c)]:
                s = s + extra
            pos += len(c)
            ct.append(s)
        _, vjp = jax.vjp(lambda *a: tuple(fn(*a)), *r, *p)
        g = vjp(tuple(ct))
        for o_ref, i in zip(outs[:len(want)], want):
            gi = g[i]
            if i in adds:
                gi = gi + avals[add_keys.index(i)]
            o_ref[...] = gi.astype(o_ref.dtype)
        first = pl.program_id(0) == 0
        for o_ref, gp in zip(outs[len(want):], g[nr:]):
            @pl.when(first)
            def _(o_ref=o_ref, gp=gp):
                o_ref[...] = gp

            @pl.when(jnp.logical_not(first))
            def _(o_ref=o_ref, gp=gp):
                o_ref[...] += gp

    row_spec = lambda a: pl.BlockSpec((tile, a.shape[1]), lambda i: (i, 0))
    par_spec = lambda a: pl.BlockSpec(a.shape, lambda i, nd=a.ndim: (0,) * nd)
    res = pl.pallas_call(
        body, name=name, grid=(T // tile,),
        in_specs=[row_spec(a) for a in rows] + [par_spec(a) for a in pars] + [row_spec(a) for a in flat_cts + add_arrs],
        out_specs=[row_spec(rows[i]) for i in want] + [par_spec(a) for a in pars],
        out_shape=[jax.ShapeDtypeStruct(rows[i].shape, dtypes[i]) for i in want]
        + [jax.ShapeDtypeStruct(a.shape, F32) for a in pars],
        compiler_params=_cparams(("arbitrary",)),
    )(*rows, *pars, *flat_cts, *add_arrs)
    return res[:len(want)], res[len(want):]


def _small(fn, args, outs, *, name):
    n = len(args)

    def body(*refs):
        res = fn(*[r[...] for r in refs[:n]])
        for o_ref, o in zip(refs[n:], res):
            o_ref[...] = o

    return pl.pallas_call(body, name=name, out_shape=[jax.ShapeDtypeStruct(s, F32) for s in outs],
                          compiler_params=pltpu.CompilerParams(vmem_limit_bytes=VMEM_LIMIT))(*args)


def _small_vjp(fn, args, cts, *, name):
    n, nc = len(args), len(cts)

    def body(*refs):
        _, vjp = jax.vjp(lambda *a: tuple(fn(*a)), *[r[...] for r in refs[:n]])
        g = vjp(tuple(r[...] for r in refs[n:n + nc]))
        for o_ref, gi in zip(refs[n + nc:], g):
            o_ref[...] = gi

    return pl.pallas_call(body, name=name, out_shape=[jax.ShapeDtypeStruct(a.shape, F32) for a in args],
                          compiler_params=pltpu.CompilerParams(vmem_limit_bytes=VMEM_LIMIT))(*args, *cts)


def _rms(x, g):
    return x * lax.rsqrt(jnp.mean(x * x, axis=-1, keepdims=True) + EPS) * g


def _layernorm(x, g, b):
    mu = jnp.mean(x, axis=-1, keepdims=True)
    var = jnp.mean(jnp.square(x - mu), axis=-1, keepdims=True)
    return (x - mu) * lax.rsqrt(var + EPS) * g + b


def _f_pre(x, g):
    return (_rms(x.astype(F32), g),)


def _f_post_term(scale, has_bias):
    def fn(o, g, *b):
        o = o.astype(F32)
        if has_bias:
            o = o + b[0]
        return (scale * _rms(o, g),)
    return fn


def _f_post(scale, has_bias):
    term = _f_post_term(scale, has_bias)

    def fn(x, o, g, *b):
        return (x + term(o, g, *b)[0],)
    return fn


def _f_swiglu(ab):
    ab = ab.astype(F32)
    return (jax.nn.silu(ab[:, :D_FF]) * ab[:, D_FF:],)


def _f_s5_gelu(ylin, u, d):
    return (jax.nn.gelu(ylin.astype(F32) + d * u.astype(F32)),)


def _f_s5_glu(y, gl, b):
    return (y.astype(F32) * jax.nn.sigmoid(gl.astype(F32) + b),)


def _f_cv_glu(z0, b):
    z = z0.astype(F32) + b
    return (z[:, :D_MODEL] * jax.nn.sigmoid(z[:, D_MODEL:]),)


def _f_cv_ln(zc, g, b):
    return (jax.nn.silu(_layernorm(zc.astype(F32), g, b)),)


def _f_gm_in(z0, b, g, bl):
    z = jax.nn.gelu(z0.astype(F32) + b)
    return z[:, :GM_E], _layernorm(z[:, GM_E:], g, bl)


def _f_at_combine(o0, o1, o2, l0, l1, l2):
    m = jnp.maximum(jnp.maximum(l0, l1), l2)
    e0, e1, e2 = jnp.exp(l0 - m), jnp.exp(l1 - m), jnp.exp(l2 - m)
    return ((e0 * o0 + e1 * o1 + e2 * o2) / (e0 + e1 + e2),)


def _f_s5_disc(ar, ai, ldt, br, bi):
    dt = jnp.exp(ldt)
    mag = jnp.exp(dt * ar)
    abr = mag * jnp.cos(dt * ai)
    abi = mag * jnp.sin(dt * ai)
    den = ar * ar + ai * ai
    nr = abr - 1.0
    f_re = (nr * ar + abi * ai) / den
    f_im = (abi * ar - nr * ai) / den
    return abr, abi, f_re * br - f_im * bi, f_re * bi + f_im * br


def _loss_call(y, tgt):
    T, D = y.shape
    tile = 256

    def body(y_ref, t_ref, dy_ref, l_ref):
        err = y_ref[...] - t_ref[...]
        dy_ref[...] = err * (1.0 / D)
        part = 0.5 * jnp.sum(jnp.mean(err * err, axis=-1, keepdims=True), axis=0, keepdims=True)
        part = jnp.broadcast_to(part, (1, LANES))
        first = pl.program_id(0) == 0

        @pl.when(first)
        def _():
            l_ref[...] = part

        @pl.when(jnp.logical_not(first))
        def _():
            l_ref[...] += part

    return pl.pallas_call(
        body, name="loss", grid=(T // tile,),
        in_specs=[pl.BlockSpec((tile, D), lambda i: (i, 0))] * 2,
        out_specs=[pl.BlockSpec((tile, D), lambda i: (i, 0)), pl.BlockSpec((1, LANES), lambda i: (0, 0))],
        out_shape=[jax.ShapeDtypeStruct((T, D), F32), jax.ShapeDtypeStruct((1, LANES), F32)],
        compiler_params=_cparams(("arbitrary",)),
    )(y, tgt)


def _bd(xs, ws, *, add=None, out_dtype=F32, name):
    T = xs[0].shape[0]
    nb, kw, nw = ws[0].shape
    tm = 512
    n = len(xs)

    def body(*refs):
        acc = None
        for x_ref, w_ref in zip(refs[:n], refs[n:2 * n]):
            p = jnp.dot(x_ref[...].astype(BF16), w_ref[...].astype(BF16), preferred_element_type=F32)
            acc = p if acc is None else acc + p
        if add is not None:
            acc = acc + refs[2 * n][...].astype(F32)
        refs[-1][...] = acc.astype(refs[-1].dtype)

    in_specs = [pl.BlockSpec((tm, kw), lambda i, j: (i, j)) for _ in xs]
    in_specs += [pl.BlockSpec((None, kw, nw), lambda i, j: (j, 0, 0)) for _ in ws]
    args = list(xs) + list(ws)
    if add is not None:
        in_specs.append(pl.BlockSpec((tm, nw), lambda i, j: (i, j)))
        args.append(add)
    return pl.pallas_call(
        body, name=name, grid=(T // tm, nb), in_specs=in_specs,
        out_specs=pl.BlockSpec((tm, nw), lambda i, j: (i, j)),
        out_shape=jax.ShapeDtypeStruct((T, nb * nw), out_dtype),
        compiler_params=_cparams(("parallel", "parallel")),
    )(*args)


def _bd_wgrad(x, dy, kw, nw, *, name):
    T = x.shape[0]
    nb = x.shape[1] // kw
    tk = 512
    nk = T // tk

    def body(x_ref, dy_ref, o_ref):
        p = lax.dot_general(x_ref[...].astype(BF16), dy_ref[...].astype(BF16), (((0,), (0,)), ((), ())),
                            preferred_element_type=F32)
        k = pl.program_id(1)

        @pl.when(k == 0)
        def _():
            o_ref[...] = p

        @pl.when(k > 0)
        def _():
            o_ref[...] += p

    return pl.pallas_call(
        body, name=name, grid=(nb, nk),
        in_specs=[pl.BlockSpec((tk, kw), lambda j, k: (k, j)), pl.BlockSpec((tk, nw), lambda j, k: (k, j))],
        out_specs=pl.BlockSpec((None, kw, nw), lambda j, k: (j, 0, 0)),
        out_shape=jax.ShapeDtypeStruct((nb, kw, nw), F32),
        compiler_params=_cparams(("parallel", "arbitrary")),
    )(x, dy)


SCAN_COLS = 512
SCAN_ROWS = 256


def _scan_fwd(bur, bui, ar, ai):
    T, NS = bur.shape
    cw, tc = SCAN_COLS, SCAN_ROWS

    def body(bur_ref, bui_ref, ar_ref, ai_ref, sr_ref, si_ref, cr, ci):
        @pl.when(pl.program_id(1) == 0)
        def _():
            cr[...] = jnp.zeros_like(cr)
            ci[...] = jnp.zeros_like(ci)

        a_r, a_i = ar_ref[...], ai_ref[...]

        def step8(t8, carry):
            sr, si = carry
            base = pl.multiple_of(t8 * 8, 8)
            for r in range(8):
                br = bur_ref[pl.ds(base + r, 1), :]
                bi = bui_ref[pl.ds(base + r, 1), :]
                sr, si = a_r * sr - a_i * si + br, a_r * si + a_i * sr + bi
                sr_ref[pl.ds(base + r, 1), :] = sr
                si_ref[pl.ds(base + r, 1), :] = si
            return sr, si

        sr, si = lax.fori_loop(0, tc // 8, step8, (cr[...], ci[...]))
        cr[...] = sr
        ci[...] = si

    blk = pl.BlockSpec((tc, cw), lambda c, t: (t, c))
    vec = pl.BlockSpec((1, cw), lambda c, t: (0, c))
    return pl.pallas_call(
        body, name="s5_scan_fwd", grid=(NS // cw, T // tc), in_specs=[blk, blk, vec, vec], out_specs=[blk, blk],
        out_shape=[jax.ShapeDtypeStruct((T, NS), F32)] * 2,
        scratch_shapes=[pltpu.VMEM((1, cw), F32)] * 2,
        compiler_params=_cparams(("parallel", "arbitrary")),
    )(bur, bui, ar, ai)


def _scan_bwd(gr, gi, sr, si, ar, ai):
    T, NS = gr.shape
    cw, tc = SCAN_COLS, SCAN_ROWS
    nt = T // tc

    def body(gr_ref, gi_ref, sr_ref, si_ref, ar_ref, ai_ref, lr_ref, li_ref, dar_ref, dai_ref, cr, ci):
        @pl.when(pl.program_id(1) == 0)
        def _():
            cr[...] = jnp.zeros_like(cr)
            ci[...] = jnp.zeros_like(ci)
            dar_ref[...] = jnp.zeros_like(dar_ref)
            dai_ref[...] = jnp.zeros_like(dai_ref)

        a_r, a_i = ar_ref[...], ai_ref[...]

        def step8(k, carry):
            lr, li, dar, dai = carry
            base = pl.multiple_of((tc // 8 - 1 - k) * 8, 8)
            for r in range(7, -1, -1):
                s_r = sr_ref[pl.ds(base + r, 1), :]
                s_i = si_ref[pl.ds(base + r, 1), :]
                dar = dar + lr * s_r + li * s_i
                dai = dai + li * s_r - lr * s_i
                g_r = gr_ref[pl.ds(base + r, 1), :]
                g_i = gi_ref[pl.ds(base + r, 1), :]
                lr, li = g_r + a_r * lr + a_i * li, g_i + a_r * li - a_i * lr
                lr_ref[pl.ds(base + r, 1), :] = lr
                li_ref[pl.ds(base + r, 1), :] = li
            return lr, li, dar, dai

        lr, li, dar, dai = lax.fori_loop(0, tc // 8, step8, (cr[...], ci[...], dar_ref[...], dai_ref[...]))
        cr[...] = lr
        ci[...] = li
        dar_ref[...] = dar
        dai_ref[...] = dai

    blk = pl.BlockSpec((tc, cw), lambda c, t: (nt - 1 - t, c))
    vec = pl.BlockSpec((1, cw), lambda c, t: (0, c))
    return pl.pallas_call(
        body, name="s5_scan_bwd", grid=(NS // cw, nt), in_specs=[blk, blk, blk, blk, vec, vec],
        out_specs=[blk, blk, vec, vec],
        out_shape=[jax.ShapeDtypeStruct((T, NS), F32)] * 2 + [jax.ShapeDtypeStruct((1, NS), F32)] * 2,
        scratch_shapes=[pltpu.VMEM((1, cw), F32)] * 2,
        compiler_params=_cparams(("parallel", "arbitrary")),
    )(gr, gi, sr, si, ar, ai)


CONV_ROWS = 256
CONV_HALO = 32
CONV_PAD = CONV_HALO - (CONV_W - 1)


def _conv_fwd(z, dw, dwb):
    T, D = z.shape
    tc, hl = CONV_ROWS, CONV_HALO
    per = tc // hl

    def body(z_ref, zp_ref, dw_ref, b_ref, o_ref, ext):
        i = pl.program_id(0)
        ext[pl.ds(0, hl), :] = jnp.where(i > 0, zp_ref[...], 0.0)
        ext[pl.ds(hl, tc), :] = z_ref[...]
        acc = jnp.zeros((tc, D), F32) + b_ref[...]
        for k in range(CONV_W):
            acc = acc + dw_ref[pl.ds(k, 1), :] * ext[pl.ds(CONV_PAD + k, tc), :]
        o_ref[...] = acc

    return pl.pallas_call(
        body, name="conv_fwd", grid=(T // tc,),
        in_specs=[pl.BlockSpec((tc, D), lambda i: (i, 0)),
                  pl.BlockSpec((hl, D), lambda i: (jnp.maximum(i * per - 1, 0), 0)),
                  pl.BlockSpec((hl, D), lambda i: (0, 0)), pl.BlockSpec((1, D), lambda i: (0, 0))],
        out_specs=pl.BlockSpec((tc, D), lambda i: (i, 0)),
        out_shape=jax.ShapeDtypeStruct((T, D), F32),
        scratch_shapes=[pltpu.VMEM((tc + hl, D), F32)],
        compiler_params=_cparams(("parallel",)),
    )(z, z, dw, dwb)


def _conv_bwd(dout, z, dw):
    T, D = z.shape
    tc, hl = CONV_ROWS, CONV_HALO
    per = tc // hl
    nblk = T // tc

    def body(g_ref, gn_ref, z_ref, zp_ref, dw_ref, dz_ref, ddw_ref, db_ref, gext, zext):
        i = pl.program_id(0)
        g = g_ref[...]
        gext[pl.ds(0, tc), :] = g
        gext[pl.ds(tc, hl), :] = jnp.where(i < nblk - 1, gn_ref[...], 0.0)
        zext[pl.ds(0, hl), :] = jnp.where(i > 0, zp_ref[...], 0.0)
        zext[pl.ds(hl, tc), :] = z_ref[...]
        acc = jnp.zeros((tc, D), F32)
        for k in range(CONV_W):
            acc = acc + dw_ref[pl.ds(k, 1), :] * gext[pl.ds(CONV_W - 1 - k, tc), :]
        dz_ref[...] = acc

        @pl.when(i == 0)
        def _():
            ddw_ref[...] = jnp.zeros_like(ddw_ref)
            db_ref[...] = jnp.zeros_like(db_ref)

        db_ref[...] += jnp.sum(g, axis=0, keepdims=True)
        for k in range(CONV_W):
            ddw_ref[pl.ds(k, 1), :] += jnp.sum(g * zext[pl.ds(CONV_PAD + k, tc), :], axis=0, keepdims=True)

    return pl.pallas_call(
        body, name="conv_bwd", grid=(nblk,),
        in_specs=[pl.BlockSpec((tc, D), lambda i: (i, 0)),
                  pl.BlockSpec((hl, D), lambda i: (jnp.minimum((i + 1) * per, nblk * per - 1), 0)),
                  pl.BlockSpec((tc, D), lambda i: (i, 0)),
                  pl.BlockSpec((hl, D), lambda i: (jnp.maximum(i * per - 1, 0), 0)),
                  pl.BlockSpec((hl, D), lambda i: (0, 0))],
        out_specs=[pl.BlockSpec((tc, D), lambda i: (i, 0)), pl.BlockSpec((hl, D), lambda i: (0, 0)),
                   pl.BlockSpec((1, D), lambda i: (0, 0))],
        out_shape=[jax.ShapeDtypeStruct((T, D), F32), jax.ShapeDtypeStruct((hl, D), F32),
                   jax.ShapeDtypeStruct((1, D), F32)],
        scratch_shapes=[pltpu.VMEM((tc + hl, D), F32)] * 2,
        compiler_params=_cparams(("arbitrary",)),
    )(dout, dout, z, z, dw)


def _gm_causal():
    r = lax.broadcasted_iota(jnp.int32, (GM_CHUNK, GM_CHUNK), 0)
    c = lax.broadcasted_iota(jnp.int32, (GM_CHUNK, GM_CHUNK), 1)
    return r >= c


def _gm_sg_fwd(u, v, ws, bs_col):
    T, E = u.shape
    hw = E // GM_HEADS

    def body(u_ref, v_ref, w_ref, b_ref, o_ref):
        causal = _gm_causal()
        for h in range(GM_HEADS):
            cols = slice(h * hw, (h + 1) * hw)
            w = jnp.where(causal, w_ref[h], 0.0).astype(BF16)
            s = jnp.dot(w, v_ref[:, cols], preferred_element_type=F32) + b_ref[h]
            o_ref[:, cols] = (u_ref[:, cols] * s).astype(o_ref.dtype)

    return pl.pallas_call(
        body, name="gm_sg_fwd", grid=(T // GM_CHUNK,),
        in_specs=[pl.BlockSpec((GM_CHUNK, E), lambda i: (i, 0)), pl.BlockSpec((GM_CHUNK, E), lambda i: (i, 0)),
                  pl.BlockSpec(ws.shape, lambda i: (0, 0, 0)), pl.BlockSpec(bs_col.shape, lambda i: (0, 0, 0))],
        out_specs=pl.BlockSpec((GM_CHUNK, E), lambda i: (i, 0)),
        out_shape=jax.ShapeDtypeStruct((T, E), BF16),
        compiler_params=_cparams(("parallel",)),
    )(u, v, ws, bs_col)


def _gm_sg_bwd(dus, u, v, ws, bs_col):
    T, E = u.shape
    hw = E // GM_HEADS

    def body(g_ref, u_ref, v_ref, w_ref, b_ref, du_ref, dv_ref, dw_ref, db_ref):
        causal = _gm_causal()

        @pl.when(pl.program_id(0) == 0)
        def _():
            dw_ref[...] = jnp.zeros_like(dw_ref)
            db_ref[...] = jnp.zeros_like(db_ref)

        for h in range(GM_HEADS):
            cols = slice(h * hw, (h + 1) * hw)
            w = jnp.where(causal, w_ref[h], 0.0).astype(BF16)
            vh = v_ref[:, cols]
            s = jnp.dot(w, vh, preferred_element_type=F32) + b_ref[h]
            g = g_ref[:, cols]
            du_ref[:, cols] = g * s
            ds = g * u_ref[:, cols]
            dsb = ds.astype(BF16)
            dv_ref[:, cols] = lax.dot_general(w, dsb, (((0,), (0,)), ((), ())), preferred_element_type=F32)
            dwh = lax.dot_general(dsb, vh, (((1,), (1,)), ((), ())), preferred_element_type=F32)
            dw_ref[h] += jnp.where(causal, dwh, 0.0)
            db_ref[h] += jnp.broadcast_to(jnp.sum(ds, axis=1, keepdims=True), (GM_CHUNK, LANES))

    blk = pl.BlockSpec((GM_CHUNK, E), lambda i: (i, 0))
    return pl.pallas_call(
        body, name="gm_sg_bwd", grid=(T // GM_CHUNK,),
        in_specs=[blk, blk, blk, pl.BlockSpec(ws.shape, lambda i: (0, 0, 0)),
                  pl.BlockSpec(bs_col.shape, lambda i: (0, 0, 0))],
        out_specs=[blk, blk, pl.BlockSpec(ws.shape, lambda i: (0, 0, 0)),
                   pl.BlockSpec((GM_HEADS, GM_CHUNK, LANES), lambda i: (0, 0, 0))],
        out_shape=[jax.ShapeDtypeStruct((T, E), F32), jax.ShapeDtypeStruct((T, E), F32),
                   jax.ShapeDtypeStruct(ws.shape, F32), jax.ShapeDtypeStruct((GM_HEADS, GM_CHUNK, LANES), F32)],
        compiler_params=_cparams(("arbitrary",)),
    )(dus, u, v, ws, bs_col)


def _t5_bucket_steps(dilation):
    max_exact = NUM_BUCKETS // 2
    delta = np.arange(BLOCK + 1)
    dist = delta * dilation
    distf = np.maximum(dist, 1).astype(np.float32)
    large = max_exact + (np.log(distf / np.float32(max_exact)) / np.float32(math.log(MAX_DISTANCE / max_exact))
                         * np.float32(NUM_BUCKETS - max_exact)).astype(np.int32)
    large = np.minimum(large, NUM_BUCKETS - 1)
    bucket = np.where(dist < max_exact, dist, large)
    steps = []
    for d in range(1, BLOCK + 1):
        inc = int(bucket[d] - bucket[d - 1])
        assert inc >= 0
        if inc:
            steps.append((d, inc))
    assert int(bucket[0]) == 0
    return steps


def _bucket_map(dilation):
    qi = lax.broadcasted_iota(jnp.int32, (BLOCK, 2 * BLOCK), 0)
    ki = lax.broadcasted_iota(jnp.int32, (BLOCK, 2 * BLOCK), 1)
    delta = qi + BLOCK - ki
    bm = jnp.zeros((BLOCK, 2 * BLOCK), jnp.int32)
    for thr, inc in _t5_bucket_steps(dilation):
        bm = bm + jnp.where(delta >= thr, inc, 0)
    return bm


def _at_bias(table, g, dilation):
    H = AT_HEADS

    def body(t_ref, o_ref):
        bm = _bucket_map(dilation)
        for h in range(H):
            acc = jnp.zeros((BLOCK, 2 * BLOCK), F32)
            for b in range(NUM_BUCKETS):
                acc = jnp.where(bm == b, t_ref[b, g * H + h], acc)
            o_ref[h] = acc

    return pl.pallas_call(body, name="at_bias", in_specs=[pl.BlockSpec(memory_space=pltpu.SMEM)],
                          out_shape=jax.ShapeDtypeStruct((H, BLOCK, 2 * BLOCK), F32))(table)


def _at_bias_bwd(dbias, dilation):
    H = AT_HEADS

    def body(d_ref, o_ref):
        bm = _bucket_map(dilation)
        for h in range(H):
            d = d_ref[h]
            for b in range(NUM_BUCKETS):
                o_ref[b, h] = jnp.sum(jnp.where(bm == b, d, 0.0))

    return pl.pallas_call(body, name="at_bias_bwd", out_specs=pl.BlockSpec(memory_space=pltpu.SMEM),
                          out_shape=jax.ShapeDtypeStruct((NUM_BUCKETS, H), F32))(dbias)


def _at_mask(i, nbs):
    qi = lax.broadcasted_iota(jnp.int32, (BLOCK, 2 * BLOCK), 0)
    ki = lax.broadcasted_iota(jnp.int32, (BLOCK, 2 * BLOCK), 1)
    no_prev = jnp.where(i % nbs == 0, 4 * BLOCK, 0)
    return ((ki < BLOCK) & (ki >= qi + no_prev)) | ((ki >= BLOCK) & (ki - BLOCK <= qi))


def _head_lanes():
    lane = lax.broadcasted_iota(jnp.int32, (BLOCK, LANES), 1)
    return [lane < HEAD_DIM, lane >= HEAD_DIM]


def _at_fwd(qkv, bias, nbs):
    T = qkv.shape[0]
    D = D_MODEL
    npair = D // LANES
    scale = HEAD_DIM ** -0.5

    def body(q_ref, kc_ref, kp_ref, vc_ref, vp_ref, b_ref, o_ref, l_ref):
        i = pl.program_id(0)
        mask = _at_mask(i, nbs)
        sel = _head_lanes()
        for j in range(npair):
            cols = slice(j * LANES, (j + 1) * LANES)
            q = q_ref[:, cols]
            kk = jnp.concatenate([kp_ref[:, cols], kc_ref[:, cols]], axis=0)
            vv = jnp.concatenate([vp_ref[:, cols], vc_ref[:, cols]], axis=0)
            o_pair = jnp.zeros((BLOCK, LANES), F32)
            l_pair = jnp.zeros((BLOCK, LANES), F32)
            for e in range(2):
                qh = jnp.where(sel[e], q, jnp.zeros_like(q))
                s = lax.dot_general(qh, kk, (((1,), (1,)), ((), ())), preferred_element_type=F32) * scale
                s = jnp.where(mask, s + b_ref[2 * j + e], -1e30)
                m = jnp.max(s, axis=1, keepdims=True)
                p = jnp.exp(s - m)
                den = jnp.sum(p, axis=1, keepdims=True)
                o = jnp.dot(p.astype(BF16), vv, preferred_element_type=F32) / den
                o_pair = jnp.where(sel[e], o, o_pair)
                l_pair = jnp.where(sel[e], m + jnp.log(den), l_pair)
            o_ref[:, cols] = o_pair
            l_ref[:, cols] = l_pair

    blk = lambda c, prev: pl.BlockSpec((BLOCK, D), (lambda i: (jnp.maximum(i - 1, 0), c)) if prev else (lambda i: (i, c)))
    out = pl.BlockSpec((BLOCK, D), lambda i: (i, 0))
    return pl.pallas_call(
        body, name="at_fwd", grid=(T // BLOCK,),
        in_specs=[blk(0, False), blk(1, False), blk(1, True), blk(2, False), blk(2, True),
                  pl.BlockSpec(bias.shape, lambda i: (0, 0, 0))],
        out_specs=[out, out], out_shape=[jax.ShapeDtypeStruct((T, D), F32)] * 2,
        compiler_params=_cparams(("parallel",)),
    )(qkv, qkv, qkv, qkv, qkv, bias)


def _at_bwd(qkv, bias, o, lse, do, dlse, nbs):
    T = qkv.shape[0]
    D = D_MODEL
    nblk = T // BLOCK
    npair = D // LANES
    scale = HEAD_DIM ** -0.5

    def body(q_ref, kc_ref, kp_ref, vc_ref, vp_ref, b_ref, o_ref, l_ref, do_ref, dl_ref, dqkv_ref, db_ref, carry):
        i = pl.program_id(0)

        @pl.when(i == 0)
        def _():
            carry[...] = jnp.zeros_like(carry)
            db_ref[...] = jnp.zeros_like(db_ref)

        @pl.when(i == nblk)
        def _():
            dqkv_ref[...] = carry[...].astype(dqkv_ref.dtype)

        @pl.when(i < nblk)
        def _():
            mask = _at_mask(i, nbs)
            sel = _head_lanes()
            for j in range(npair):
                cols = slice(j * LANES, (j + 1) * LANES)
                kcols = slice(D + j * LANES, D + (j + 1) * LANES)
                vcols = slice(2 * D + j * LANES, 2 * D + (j + 1) * LANES)
                q = q_ref[:, cols]
                kk = jnp.concatenate([kp_ref[:, cols], kc_ref[:, cols]], axis=0)
                vv = jnp.concatenate([vp_ref[:, cols], vc_ref[:, cols]], axis=0)
                dov = do_ref[:, cols]
                dob = dov.astype(BF16)
                oo = dov * o_ref[:, cols]
                lv = l_ref[:, cols]
                dlv = dl_ref[:, cols]
                dq_pair = jnp.zeros((BLOCK, LANES), F32)
                dk_pair = jnp.zeros((2 * BLOCK, LANES), F32)
                dv_pair = jnp.zeros((2 * BLOCK, LANES), F32)
                sel2 = [jnp.concatenate([s_, s_], axis=0) for s_ in sel]
                for e in range(2):
                    qh = jnp.where(sel[e], q, jnp.zeros_like(q))
                    s = lax.dot_general(qh, kk, (((1,), (1,)), ((), ())), preferred_element_type=F32) * scale
                    s = jnp.where(mask, s + b_ref[2 * j + e], -1e30)
                    lse_h = jnp.max(jnp.where(sel[e], lv, -jnp.inf), axis=1, keepdims=True)
                    p = jnp.exp(s - lse_h)
                    doh = jnp.where(sel[e], dob, jnp.zeros_like(dob))
                    dp = lax.dot_general(doh, vv, (((1,), (1,)), ((), ())), preferred_element_type=F32)
                    delta = jnp.sum(jnp.where(sel[e], oo, 0.0), axis=1, keepdims=True)
                    dlse_h = jnp.sum(jnp.where(sel[e], dlv, 0.0), axis=1, keepdims=True)
                    ds = p * (dp - delta + dlse_h)
                    db_ref[2 * j + e] += ds
                    dsb = (ds * scale).astype(BF16)
                    dq_pair = jnp.where(sel[e], jnp.dot(dsb, kk, preferred_element_type=F32), dq_pair)
                    dk = lax.dot_general(dsb, q, (((0,), (0,)), ((), ())), preferred_element_type=F32)
                    dk_pair = jnp.where(sel2[e], dk, dk_pair)
                    dv = lax.dot_general(p.astype(BF16), dob, (((0,), (0,)), ((), ())), preferred_element_type=F32)
                    dv_pair = jnp.where(sel2[e], dv, dv_pair)
                dqkv_ref[:, cols] = carry[:, cols].astype(dqkv_ref.dtype)
                dqkv_ref[:, kcols] = (carry[:, kcols] + dk_pair[:BLOCK]).astype(dqkv_ref.dtype)
                dqkv_ref[:, vcols] = (carry[:, vcols] + dv_pair[:BLOCK]).astype(dqkv_ref.dtype)
                carry[:, cols] = dq_pair
                carry[:, kcols] = dk_pair[BLOCK:]
                carry[:, vcols] = dv_pair[BLOCK:]

    cur = lambda i: jnp.minimum(i, nblk - 1)
    prev = lambda i: jnp.maximum(jnp.minimum(i, nblk - 1) - 1, 0)
    blk = lambda c, pv: pl.BlockSpec((BLOCK, D), (lambda i: (prev(i), c)) if pv else (lambda i: (cur(i), c)))
    row = pl.BlockSpec((BLOCK, D), lambda i: (cur(i), 0))
    return pl.pallas_call(
        body, name="at_bwd", grid=(nblk + 1,),
        in_specs=[blk(0, False), blk(1, False), blk(1, True), blk(2, False), blk(2, True),
                  pl.BlockSpec(bias.shape, lambda i: (0, 0, 0)), row, row, row, row],
        out_specs=[pl.BlockSpec((BLOCK, 3 * D), lambda i: (jnp.maximum(i - 1, 0), 0)),
                   pl.BlockSpec(bias.shape, lambda i: (0, 0, 0))],
        out_shape=[jax.ShapeDtypeStruct((T, 3 * D), BF16), jax.ShapeDtypeStruct(bias.shape, F32)],
        scratch_shapes=[pltpu.VMEM((BLOCK, 3 * D), F32)],
        compiler_params=_cparams(("arbitrary",)),
    )(qkv, qkv, qkv, qkv, qkv, bias, o, lse, do, dlse)


def _to_residue_major(a, d):
    if d == 1:
        return a
    T, C = a.shape
    return a.reshape(T // d, d, C).transpose(1, 0, 2).reshape(T, C)


def _from_residue_major(a, d):
    if d == 1:
        return a
    T, C = a.shape
    return a.reshape(d, T // d, C).transpose(1, 0, 2).reshape(T, C)


def _ffn_fwd(x, p):
    h, = _rows(_f_pre, [x], [p["g_pre"]], [(D_MODEL, BF16)], name="ffn_pre")
    ab = _mm(h, p["w13"], name="ffn_up")
    u, = _rows(_f_swiglu, [ab], [], [(D_FF, BF16)], name="ffn_glu")
    o = _mm(u, p["w2"], name="ffn_down")
    xo, = _rows(_f_post(0.5, False), [x, o], [p["g_post"]], [(D_MODEL, F32)], name="ffn_post")
    return xo, (x, h, ab, u, o)


def _ffn_bwd(saved, p, dxo):
    x, h, ab, u, o = saved
    (do,), (dg_post,) = _rows_vjp(_f_post_term(0.5, False), [o], [p["g_post"]], [dxo], dtypes=[BF16], name="ffn_post_bwd")
    du = _mm(do, p["w2"], tb=True, name="ffn_down_dx")
    dw2 = _mm(u, do, ta=True, name="ffn_down_dw")
    (dab,), _ = _rows_vjp(_f_swiglu, [ab], [], [du], dtypes=[BF16], name="ffn_glu_bwd")
    dh = _mm(dab, p["w13"], tb=True, name="ffn_up_dx")
    dw13 = _mm(h, dab, ta=True, name="ffn_up_dw")
    (dx,), (dg_pre,) = _rows_vjp(_f_pre, [x], [p["g_pre"]], [dh], dtypes=[F32], adds={0: dxo}, name="ffn_pre_bwd")
    return dx, {"w1": dw13[:, :D_FF], "w3": dw13[:, D_FF:], "w2": dw2, "g_pre": dg_pre, "g_post": dg_post}


def _expand_blocks(w, rows_first):
    w = w.reshape(S5_NB, S5_GB, S5_GROUP, S5_STATE)
    eye = jnp.eye(S5_GB, dtype=F32)
    if rows_first:
        e = w[:, :, :, None, :] * eye[None, :, None, :, None]
        return e.reshape(S5_NB, S5_GB * S5_GROUP, S5_BW)
    e = jnp.transpose(w, (0, 1, 3, 2))[:, :, :, None, :] * eye[None, :, None, :, None]
    return e.reshape(S5_NB, S5_BW, S5_GB * S5_GROUP)


def _extract_blocks(e, rows_first):
    eye = jnp.eye(S5_GB, dtype=F32)
    if rows_first:
        e = e.reshape(S5_NB, S5_GB, S5_GROUP, S5_GB, S5_STATE)
        w = jnp.sum(e * eye[None, :, None, :, None], axis=3)
    else:
        e = e.reshape(S5_NB, S5_GB, S5_STATE, S5_GB, S5_GROUP)
        w = jnp.transpose(jnp.sum(e * eye[None, :, None, :, None], axis=3), (0, 1, 3, 2))
    return w.reshape(S5_GROUPS, S5_GROUP, S5_STATE)


def _s5_prep(p):
    G, P, HG = S5_GROUPS, S5_STATE, S5_GROUP
    args = [p["a_re"].reshape(G, 1, P), p["a_im"].reshape(G, 1, P), p["log_dt"].reshape(G, 1, 1),
            jnp.transpose(p["b_re"], (0, 2, 1)), jnp.transpose(p["b_im"], (0, 2, 1))]
    abr, abi, bbr, bbi = _small(_f_s5_disc, args, [(G, 1, P)] * 2 + [(G, HG, P)] * 2, name="s5_disc")
    return args, abr.reshape(1, G * P), abi.reshape(1, G * P), bbr, bbi


def _s5_fwd(h, p):
    disc_args, abr, abi, bbr, bbi = _s5_prep(p)
    c_re, c_im = p["c_re"], p["c_im"]
    u = _mm(h, p["w_in"], name="s5_in")
    bur = _bd([u], [_expand_blocks(bbr, True)], name="s5_bu")
    bui = _bd([u], [_expand_blocks(bbi, True)], name="s5_bu")
    sr, si = _scan_fwd(bur, bui, abr, abi)
    ylin = _bd([sr, si], [_expand_blocks(c_re, False), _expand_blocks(-c_im, False)], name="s5_y")
    y, = _rows(_f_s5_gelu, [ylin, u], [p["d"]], [(D_MODEL, F32)], name="s5_gelu")
    gl = _mm(y, p["w_glu"], name="s5_glu_mm")
    z, = _rows(_f_s5_glu, [y, gl], [p["b_glu"]], [(D_MODEL, BF16)], name="s5_glu")
    m = _mm(z, p["w_out"], name="s5_out")
    return m, None, (h, disc_args, abr, abi, bbr, bbi, u, sr, si, ylin, y, gl, z)


def _s5_bwd(saved, p, dm):
    h, disc_args, abr, abi, bbr, bbi, u, sr, si, ylin, y, gl, z = saved
    c_re, c_im = p["c_re"], p["c_im"]
    dz = _mm(dm, p["w_out"], tb=True, name="s5_out_dx")
    dw_out = _mm(z, dm, ta=True, name="s5_out_dw")
    (dy1, dgl), (db_glu,) = _rows_vjp(_f_s5_glu, [y, gl], [p["b_glu"]], [dz], dtypes=[F32, BF16], name="s5_glu_bwd")
    dy2 = _mm(dgl, p["w_glu"], tb=True, name="s5_glu_dx")
    dw_glu = _mm(y, dgl, ta=True, name="s5_glu_dw")
    (dylin, du1), (dd,) = _rows_vjp(_f_s5_gelu, [ylin, u], [p["d"]], [(dy1, dy2)], dtypes=[F32, F32], name="s5_gelu_bwd")
    gr = _bd([dylin], [jnp.transpose(_expand_blocks(c_re, False), (0, 2, 1))], name="s5_y_dx")
    gi = _bd([dylin], [jnp.transpose(_expand_blocks(-c_im, False), (0, 2, 1))], name="s5_y_dx")
    dc_re = _extract_blocks(_bd_wgrad(sr, dylin, S5_BW, LANES, name="s5_y_dw"), False)
    dc_im = -_extract_blocks(_bd_wgrad(si, dylin, S5_BW, LANES, name="s5_y_dw"), False)
    lr, li, dabr, dabi = _scan_bwd(gr, gi, sr, si, abr, abi)
    du = _bd([lr, li], [jnp.transpose(_expand_blocks(bbr, True), (0, 2, 1)),
                        jnp.transpose(_expand_blocks(bbi, True), (0, 2, 1))], add=du1, out_dtype=BF16, name="s5_bu_dx")
    dbbr = _extract_blocks(_bd_wgrad(u, lr, LANES, S5_BW, name="s5_bu_dw"), True)
    dbbi = _extract_blocks(_bd_wgrad(u, li, LANES, S5_BW, name="s5_bu_dw"), True)
    G, P = S5_GROUPS, S5_STATE
    dar, dai, dldt, dbr, dbi = _small_vjp(_f_s5_disc, disc_args,
                                          [dabr.reshape(G, 1, P), dabi.reshape(G, 1, P), dbbr, dbbi], name="s5_disc_bwd")
    dh = _mm(du, p["w_in"], tb=True, name="s5_in_dx")
    dw_in = _mm(h, du, ta=True, name="s5_in_dw")
    grads = {"w_in": dw_in, "w_glu": dw_glu, "w_out": dw_out, "b_glu": db_glu, "d": dd,
             "a_re": dar.reshape(G, P), "a_im": dai.reshape(G, P), "log_dt": dldt.reshape(G),
             "b_re": jnp.transpose(dbr, (0, 2, 1)), "b_im": jnp.transpose(dbi, (0, 2, 1)),
             "c_re": dc_re, "c_im": dc_im}
    return dh, grads


def _cv_fwd(h, p):
    z0 = _mm(h, p["w_in"], name="cv_in")
    zg, = _rows(_f_cv_glu, [z0], [p["b_in"]], [(D_MODEL, F32)], name="cv_glu")
    zc = _conv_fwd(zg, p["dw"], p["dw_b"])
    zl, = _rows(_f_cv_ln, [zc], [p["ln_g"], p["ln_b"]], [(D_MODEL, BF16)], name="cv_ln")
    m = _mm(zl, p["w_out"], name="cv_out")
    return m, p["b_out"], (h, z0, zg, zc, zl)


def _cv_bwd(saved, p, dm):
    h, z0, zg, zc, zl = saved
    dzl = _mm(dm, p["w_out"], tb=True, name="cv_out_dx")
    dw_out = _mm(zl, dm, ta=True, name="cv_out_dw")
    (dzc,), (dln_g, dln_b) = _rows_vjp(_f_cv_ln, [zc], [p["ln_g"], p["ln_b"]], [dzl], dtypes=[F32], name="cv_ln_bwd")
    dzg, ddw, ddw_b = _conv_bwd(dzc, zg, p["dw"])
    (dz0,), (db_in,) = _rows_vjp(_f_cv_glu, [z0], [p["b_in"]], [dzg], dtypes=[BF16], name="cv_glu_bwd")
    dh = _mm(dz0, p["w_in"], tb=True, name="cv_in_dx")
    dw_in = _mm(h, dz0, ta=True, name="cv_in_dw")
    return dh, {"w_in": dw_in, "b_in": db_in, "dw": ddw, "dw_b": ddw_b, "ln_g": dln_g, "ln_b": dln_b, "w_out": dw_out}


def _gm_fwd(h, p):
    z0 = _mm(h, p["w_in"], name="gm_in")
    u, v = _rows(_f_gm_in, [z0], [p["b_in"], p["ln_g"], p["ln_b"]], [(GM_E, F32), (GM_E, BF16)], name="gm_act")
    bs_col = p["b_s"].reshape(GM_HEADS, GM_CHUNK, 1)
    us = _gm_sg_fwd(u, v, p["w_s"], bs_col)
    m = _mm(us, p["w_out"], name="gm_out")
    return m, p["b_out"], (h, z0, u, v, us, bs_col)


def _gm_bwd(saved, p, dm):
    h, z0, u, v, us, bs_col = saved
    dus = _mm(dm, p["w_out"], tb=True, name="gm_out_dx")
    dw_out = _mm(us, dm, ta=True, name="gm_out_dw")
    du, dv, dw_s, db_s = _gm_sg_bwd(dus, u, v, p["w_s"], bs_col)
    (dz0,), (db_in, dln_g, dln_b) = _rows_vjp(_f_gm_in, [z0], [p["b_in"], p["ln_g"], p["ln_b"]], [du, dv],
                                              dtypes=[BF16], name="gm_act_bwd")
    dh = _mm(dz0, p["w_in"], tb=True, name="gm_in_dx")
    dw_in = _mm(h, dz0, ta=True, name="gm_in_dw")
    return dh, {"w_in": dw_in, "b_in": db_in, "ln_g": dln_g, "ln_b": dln_b, "w_s": dw_s, "b_s": db_s[:, :, 0],
                "w_out": dw_out}


def _at_fwd_mixer(h, p):
    T = h.shape[0]
    D = D_MODEL
    qkv = _mm(h, p["w_qkv"], out_dtype=BF16, name="at_qkv")
    res, outs, lses, biases = [], [], [], []
    for g, (window, d) in enumerate(PATTERNS):
        assert window // d == BLOCK and T % (BLOCK * d) == 0
        bias = _at_bias(p["rel_bias"], g, d)
        r = _to_residue_major(qkv[:, g * 3 * D:(g + 1) * 3 * D], d)
        o, lse = _at_fwd(r, bias, T // d // BLOCK)
        res.append(r)
        biases.append(bias)
        outs.append(_from_residue_major(o, d))
        lses.append(_from_residue_major(lse, d))
    oc, = _rows(_f_at_combine, outs + lses, [], [(D, BF16)], name="at_combine")
    m = _mm(oc, p["w_out"], name="at_out")
    return m, None, (h, res, biases, outs, lses, oc)


def _at_bwd_mixer(saved, p, dm):
    h, res, biases, outs, lses, oc = saved
    T = h.shape[0]
    doc = _mm(dm, p["w_out"], tb=True, name="at_out_dx")
    dw_out = _mm(oc, dm, ta=True, name="at_out_dw")
    dol, _ = _rows_vjp(_f_at_combine, outs + lses, [], [doc], dtypes=[F32] * 6, name="at_combine_bwd")
    dqkv, dtab = [], []
    for g, (window, d) in enumerate(PATTERNS):
        dq, dbias = _at_bwd(res[g], biases[g], _to_residue_major(outs[g], d), _to_residue_major(lses[g], d),
                            _to_residue_major(dol[g], d), _to_residue_major(dol[3 + g], d), T // d // BLOCK)
        dqkv.append(_from_residue_major(dq, d))
        dtab.append(_at_bias_bwd(dbias, d))
    dqkv = jnp.concatenate(dqkv, axis=1)
    dh = _mm(dqkv, p["w_qkv"], tb=True, name="at_qkv_dx")
    dw_qkv = _mm(h, dqkv, ta=True, name="at_qkv_dw")
    return dh, {"w_qkv": dw_qkv, "w_out": dw_out, "rel_bias": jnp.concatenate(dtab, axis=1)}


_MIXERS = ((_s5_fwd, _s5_bwd), (_cv_fwd, _cv_bwd), (_gm_fwd, _gm_bwd), (_at_fwd_mixer, _at_bwd_mixer))


def _mixer_fwd(x, p, kind):
    h, = _rows(_f_pre, [x], [p["g_pre"]], [(D_MODEL, BF16)], name="mix_pre")
    m, bias, saved = _MIXERS[kind][0](h, p)
    extra = [] if bias is None else [bias]
    xo, = _rows(_f_post(1.0, bias is not None), [x, m], [p["g_post"]] + extra, [(D_MODEL, F32)], name="mix_post")
    return xo, (x, m, bias, saved)


def _mixer_bwd(saved_all, p, kind, dxo):
    x, m, bias, saved = saved_all
    extra = [] if bias is None else [bias]
    (dm,), dpars = _rows_vjp(_f_post_term(1.0, bias is not None), [m], [p["g_post"]] + extra, [dxo], dtypes=[BF16],
                             name="mix_post_bwd")
    dh, grads = _MIXERS[kind][1](saved, p, dm)
    (dx,), (dg_pre,) = _rows_vjp(_f_pre, [x], [p["g_pre"]], [dh], dtypes=[F32], adds={0: dxo}, name="mix_pre_bwd")
    grads["g_pre"] = dg_pre
    grads["g_post"] = dpars[0]
    if bias is not None:
        grads["b_out"] = dpars[1]
    return dx, grads


def _exchange(arrays, kinds, *, name):
    n = len(arrays)
    outs_shape = [jax.ShapeDtypeStruct((N_DEV,) + (a.shape[1:] if k == "a2a" else a.shape), a.dtype)
                  for a, k in zip(arrays, kinds)]

    def body(*refs):
        ins, outs = refs[:n], refs[n:2 * n]
        send_sems, recv_sems, local_sems = refs[2 * n:]
        x, y, c = lax.axis_index("x"), lax.axis_index("y"), lax.axis_index("c")
        me = 4 * x + 2 * y + c
        sends, recvs, locals_ = [], [], []
        for a in range(n):
            a2a = kinds[a] == "a2a"
            lc = pltpu.make_async_copy(ins[a].at[me] if a2a else ins[a], outs[a].at[me], local_sems.at[a])
            lc.start()
            locals_.append(lc)
            for k in range(1, N_DEV):
                px = 1 - x if k & 4 else x
                py = 1 - y if k & 2 else y
                pc = 1 - c if k & 1 else c
                peer = 4 * px + 2 * py + pc
                idx = a * (N_DEV - 1) + k - 1
                src = ins[a].at[peer] if a2a else ins[a]
                cp = pltpu.make_async_remote_copy(src_ref=src, dst_ref=outs[a].at[me], send_sem=send_sems.at[idx],
                                                  recv_sem=recv_sems.at[idx], device_id=(px, py, pc),
                                                  device_id_type=pl.DeviceIdType.MESH)
                cp.start()
                sends.append(cp)
                recvs.append(pltpu.make_async_remote_copy(src_ref=src, dst_ref=outs[a].at[peer],
                                                          send_sem=send_sems.at[idx], recv_sem=recv_sems.at[idx],
                                                          device_id=(px, py, pc), device_id_type=pl.DeviceIdType.MESH))
        for cp in sends:
            cp.wait_send()
        for cp in recvs:
            cp.wait_recv()
        for lc in locals_:
            lc.wait()

    hbm = pl.BlockSpec(memory_space=pl.ANY)
    return pl.pallas_call(
        body, name=name, in_specs=[hbm] * n, out_specs=[hbm] * n, out_shape=outs_shape,
        scratch_shapes=[pltpu.SemaphoreType.DMA((n * (N_DEV - 1),)), pltpu.SemaphoreType.DMA((n * (N_DEV - 1),)),
                        pltpu.SemaphoreType.DMA((n,))],
    )(*arrays)


def _adam(recv, w, m, v, *, name):
    R, C = w.shape
    tr = _pick_rows(R, 128)
    c1 = 1.0 - ADAM_B1 ** ADAM_STEP
    c2 = 1.0 - ADAM_B2 ** ADAM_STEP

    def body(r_ref, w_ref, m_ref, v_ref, g_ref, d_ref, nm_ref, nv_ref):
        g = r_ref[0]
        for q in range(1, N_DEV):
            g = g + r_ref[q]
        mm = ADAM_B1 * m_ref[...] + (1.0 - ADAM_B1) * g
        vv = ADAM_B2 * v_ref[...] + (1.0 - ADAM_B2) * jnp.square(g)
        m_hat = mm / c1
        v_hat = vv / c2
        g_ref[...] = g
        d_ref[...] = -ADAM_LR * (m_hat / (jnp.sqrt(v_hat) + ADAM_EPS) + ADAM_WD * w_ref[...])
        nm_ref[...] = mm
        nv_ref[...] = vv

    blk = pl.BlockSpec((tr, C), lambda i: (i, 0))
    return pl.pallas_call(
        body, name=name, grid=(R // tr,),
        in_specs=[pl.BlockSpec((N_DEV, tr, C), lambda i: (0, i, 0)), blk, blk, blk], out_specs=[blk] * 4,
        out_shape=[jax.ShapeDtypeStruct((R, C), F32)] * 4,
        compiler_params=_cparams(("parallel",)),
    )(recv, w, m, v)


PACK_COLS = 1024


def _padded(n):
    return -(-n // PACK_ALIGN) * PACK_ALIGN


def _pack_flat(pieces):
    flat = jnp.concatenate([p.reshape(-1) for p in pieces])
    n = flat.shape[0]
    return jnp.pad(flat, (0, _padded(n) - n)).reshape(-1, PACK_COLS)


def _shard_shape(shape, axis):
    s = list(shape)
    assert s[axis] % N_DEV == 0
    s[axis] //= N_DEV
    return tuple(s)


def _split_full(full, axis):
    s = full.shape
    r = full.reshape(s[:axis] + (N_DEV, s[axis] // N_DEV) + s[axis + 1:])
    return jnp.moveaxis(r, axis, 0)


def _merge_full(parts, axis):
    r = jnp.moveaxis(parts, 0, axis)
    s = r.shape
    return r.reshape(s[:axis] + (s[axis] * s[axis + 1],) + s[axis + 2:])


def _pack_full(entries, grads):
    flat = jnp.concatenate([_split_full(grads[k].reshape(shape), axis).reshape(N_DEV, -1)
                            for k, shape, axis in entries], axis=1)
    n = flat.shape[1]
    return jnp.pad(flat, ((0, 0), (0, _padded(n) - n))).reshape(N_DEV, -1, PACK_COLS)


def _unpack_gathered(entries, buf):
    flat = buf.reshape(N_DEV, -1)
    out, pos = {}, 0
    for k, shape, axis in entries:
        ss = _shard_shape(shape, axis)
        n = int(np.prod(ss))
        out[k] = _merge_full(flat[:, pos:pos + n].reshape((N_DEV,) + ss), axis)
        pos += n
    return out


def _unpack_shard(entries, buf):
    flat = buf.reshape(-1)
    out, pos = {}, 0
    for k, shape, axis in entries:
        ss = _shard_shape(shape, axis)
        n = int(np.prod(ss))
        out[k] = flat[pos:pos + n].reshape(ss)
        pos += n
    return out


def _unpack_flat(entries, buf):
    flat = buf.reshape(-1)
    out, pos = {}, 0
    for k, shape in entries:
        n = int(np.prod(shape))
        out[k] = flat[pos:pos + n].reshape(shape)
        pos += n
    return out


D, FF = D_MODEL, D_FF
_FFN_MATS = (("w1", (D, FF), 1), ("w3", (D, FF), 1), ("w2", (FF, D), 0))
_NORM_VECS = (("g_pre", (D,), 0), ("g_post", (D,), 0))
_MIX_MATS = (
    (("w_in", (D, D), 0), ("w_glu", (D, D), 0), ("w_out", (D, D), 0)),
    (("w_in", (D, 2 * D), 1), ("w_out", (D, D), 0)),
    (("w_in", (D, 2 * GM_E), 1), ("w_out", (GM_E, D), 0)),
    (("w_qkv", (D, 9 * D), 1), ("w_out", (D, D), 0)),
)
_MIX_VECS = (
    (),
    (("b_in", (2 * D,), 0), ("dw", (CONV_W, D), 1), ("dw_b", (D,), 0), ("ln_g", (D,), 0), ("ln_b", (D,), 0),
     ("b_out", (D,), 0)),
    (("b_in", (2 * GM_E,), 0), ("ln_g", (GM_E,), 0), ("ln_b", (GM_E,), 0), ("b_out", (D,), 0)),
    (),
)
_REPLICATED = (
    ("rel_bias", 3, "rel_bias", (NUM_BUCKETS, 3 * AT_HEADS)),
    ("s5_a_re", 0, "a_re", (S5_GROUPS, S5_STATE)), ("s5_a_im", 0, "a_im", (S5_GROUPS, S5_STATE)),
    ("s5_log_dt", 0, "log_dt", (S5_GROUPS,)),
    ("s5_b_re", 0, "b_re", (S5_GROUPS, S5_STATE, S5_GROUP)), ("s5_b_im", 0, "b_im", (S5_GROUPS, S5_STATE, S5_GROUP)),
    ("s5_c_re", 0, "c_re", (S5_GROUPS, S5_GROUP, S5_STATE)), ("s5_c_im", 0, "c_im", (S5_GROUPS, S5_GROUP, S5_STATE)),
    ("s5_d", 0, "d", (D,)), ("s5_b_glu", 0, "b_glu", (D,)),
    ("gm_w_s", 2, "w_s", (GM_HEADS, GM_CHUNK, GM_CHUNK)), ("gm_b_s", 2, "b_s", (GM_HEADS, GM_CHUNK)),
)
_MIX_PREFIX = ("s5_", "cv_", "gm_", "at_")
_TWIN_WEIGHTS = ('norm_pre', 'norm_post', 'ffn_w1', 'ffn_w3', 'ffn_w2', 'rel_bias', 's5_w_in', 's5_a_re', 's5_a_im',
                 's5_log_dt', 's5_b_re', 's5_b_im', 's5_c_re', 's5_c_im', 's5_d', 's5_w_glu', 's5_b_glu', 's5_w_out',
                 'cv_w_in', 'cv_b_in', 'cv_dw', 'cv_dw_b', 'cv_ln_g', 'cv_ln_b', 'cv_w_out', 'cv_b_out', 'gm_w_in',
                 'gm_b_in', 'gm_ln_g', 'gm_ln_b', 'gm_w_s', 'gm_b_s', 'gm_w_out', 'gm_b_out', 'at_w_qkv', 'at_w_out')


def _part_entries(part):
    if part[0] == "ffn":
        return _FFN_MATS, _NORM_VECS
    kind = part[1] % 4
    return _MIX_MATS[kind], _NORM_VECS + _MIX_VECS[kind]


def _part_shards(part, get):
    if part[0] == "ffn":
        _, i, j = part
        n = 0 if j == 0 else 2
        return {"w1": get("ffn_w1")[i, j], "w3": get("ffn_w3")[i, j], "w2": get("ffn_w2")[i, j],
                "g_pre": get("norm_pre")[i, n], "g_post": get("norm_post")[i, n]}
    _, i = part
    kind, j = i % 4, i // 4
    out = {"g_pre": get("norm_pre")[i, 1], "g_post": get("norm_post")[i, 1]}
    for k, _, _ in _MIX_MATS[kind] + _MIX_VECS[kind]:
        out[k] = get(_MIX_PREFIX[kind] + k)[j]
    return out


def _parts():
    parts = []
    for i in range(DEPTH):
        parts += [("ffn", i, 0), ("mix", i), ("ffn", i, 1)]
    return parts


def _as_par(v):
    return v.reshape(1, -1)


def _prepare_part(part, full, rep):
    if part[0] == "ffn":
        return {"w13": jnp.concatenate([full["w1"], full["w3"]], axis=1), "w2": full["w2"],
                "g_pre": _as_par(full["g_pre"]), "g_post": _as_par(full["g_post"])}
    kind = part[1] % 4
    p = {"g_pre": _as_par(full["g_pre"]), "g_post": _as_par(full["g_post"])}
    for k, _, _ in _MIX_MATS[kind]:
        p[k] = full[k]
    for k, _, _ in _MIX_VECS[kind]:
        p[k] = _as_par(full[k]) if k != "dw" else jnp.pad(full[k], ((0, CONV_HALO - CONV_W), (0, 0)))
    if kind == 0:
        for k in ("a_re", "a_im", "log_dt", "b_re", "b_im"):
            p[k] = rep[k]
        p["c_re"], p["c_im"] = rep["c_re"], rep["c_im"]
        p["d"], p["b_glu"] = _as_par(rep["d"]), _as_par(rep["b_glu"])
    elif kind == 2:
        p["w_s"], p["b_s"] = rep["w_s"], rep["b_s"]
    elif kind == 3:
        p["rel_bias"] = rep["rel_bias"]
    return p


def _finish_grads(part, grads):
    out = dict(grads)
    for k in ("g_pre", "g_post", "b_in", "dw_b", "ln_g", "ln_b", "b_out", "d", "b_glu"):
        if k in out:
            out[k] = out[k].reshape(-1)
    if "dw" in out:
        out["dw"] = out["dw"][:CONV_W]
    return out


def _step(x, tgt, inputs, moments_m, moments_v):
    parts = _parts()
    rep = {}
    for name, kind, key, shape in _REPLICATED:
        rep[key] = inputs[name][0] if name != "rel_bias" else inputs[name]

    params, packed_w = [], []
    for part in parts:
        mats, vecs = _part_entries(part)
        sh = _part_shards(part, lambda n: inputs[n])
        wm = _pack_flat([sh[k] for k, _, _ in mats])
        wv = _pack_flat([sh[k] for k, _, _ in vecs])
        gm_, gv_ = _exchange([wm.astype(BF16), wv], ["bcast", "bcast"], name="gather_" + part[0])
        full = _unpack_gathered(mats, gm_)
        full.update(_unpack_gathered(vecs, gv_))
        params.append(_prepare_part(part, full, rep))
        packed_w.append((wm, wv))

    saved = []
    h = x
    for part, p in zip(parts, params):
        if part[0] == "ffn":
            h, s = _ffn_fwd(h, p)
        else:
            h, s = _mixer_fwd(h, p, part[1] % 4)
        saved.append(s)
    dh, loss_vec = _loss_call(h, tgt)
    loss_local = loss_vec[0, 0]

    results = {}
    rep_grads = {}
    for idx in range(len(parts) - 1, -1, -1):
        part, p = parts[idx], params[idx]
        if part[0] == "ffn":
            dh, grads = _ffn_bwd(saved[idx], p, dh)
        else:
            dh, grads = _mixer_bwd(saved[idx], p, part[1] % 4, dh)
        grads = _finish_grads(part, grads)
        for name, kind, key, shape in _REPLICATED:
            if part[0] == "mix" and kind == part[1] % 4:
                rep_grads[name] = grads[key]
        mats, vecs = _part_entries(part)
        rm, rv = _exchange([_pack_full(mats, grads), _pack_full(vecs, grads)], ["a2a", "a2a"], name="scatter_" + part[0])
        wm, wv = packed_w[idx]
        mm_ = _part_shards(part, lambda n: moments_m[n])
        vv_ = _part_shards(part, lambda n: moments_v[n])
        om = _adam(rm, wm, _pack_flat([mm_[k] for k, _, _ in mats]), _pack_flat([vv_[k] for k, _, _ in mats]),
                   name="adam_" + part[0] + "_mats")
        ov = _adam(rv, wv, _pack_flat([mm_[k] for k, _, _ in vecs]), _pack_flat([vv_[k] for k, _, _ in vecs]),
                   name="adam_" + part[0] + "_vecs")
        res = [_unpack_shard(mats, o) for o in om]
        for r, o in zip(res, ov):
            r.update(_unpack_shard(vecs, o))
        results[part] = res

    rep_entries = [(name, shape) for name, _, _, shape in _REPLICATED]
    get_rep = lambda d: _pack_flat([(d[name][0] if name != "rel_bias" else d[name]) for name, _ in rep_entries])
    rg, = _exchange([_pack_flat([rep_grads[name] for name, _ in rep_entries])], ["bcast"], name="allgather_rep")
    orep = _adam(rg, get_rep(inputs), get_rep(moments_m), get_rep(moments_v), name="adam_rep")
    rep_out = [_unpack_flat(rep_entries, o) for o in orep]
    return loss_local, dh, results, rep_out


def _assemble(name, results, rep_out, which):
    for rname, _, _, _ in _REPLICATED:
        if rname == name:
            a = rep_out[which][name]
            return a if name == "rel_bias" else a[None]
    if name in ("norm_pre", "norm_post"):
        key = "g_pre" if name == "norm_pre" else "g_post"
        rows = []
        for i in range(DEPTH):
            rows.append(jnp.stack([results[("ffn", i, 0)][which][key], results[("mix", i)][which][key],
                                   results[("ffn", i, 1)][which][key]]))
        return jnp.stack(rows)
    if name.startswith("ffn_"):
        key = name[4:]
        return jnp.stack([jnp.stack([results[("ffn", i, j)][which][key] for j in range(2)]) for i in range(DEPTH)])
    kind = _MIX_PREFIX.index(name[:3])
    layers = [i for i in range(DEPTH) if i % 4 == kind]
    return jnp.stack([results[("mix", i)][which][name[3:]] for i in layers])


def kernel(x, norm_pre, norm_post, ffn_w1, ffn_w3, ffn_w2, rel_bias, s5_w_in, s5_a_re, s5_a_im, s5_log_dt, s5_b_re, s5_b_im, s5_c_re, s5_c_im, s5_d, s5_w_glu, s5_b_glu, s5_w_out, cv_w_in, cv_b_in, cv_dw, cv_dw_b, cv_ln_g, cv_ln_b, cv_w_out, cv_b_out, gm_w_in, gm_b_in, gm_ln_g, gm_ln_b, gm_w_s, gm_b_s, gm_w_out, gm_b_out, at_w_qkv, at_w_out, loss_target, m_norm_pre, m_norm_post, m_ffn_w1, m_ffn_w3, m_ffn_w2, m_rel_bias, m_s5_w_in, m_s5_a_re, m_s5_a_im, m_s5_log_dt, m_s5_b_re, m_s5_b_im, m_s5_c_re, m_s5_c_im, m_s5_d, m_s5_w_glu, m_s5_b_glu, m_s5_w_out, m_cv_w_in, m_cv_b_in, m_cv_dw, m_cv_dw_b, m_cv_ln_g, m_cv_ln_b, m_cv_w_out, m_cv_b_out, m_gm_w_in, m_gm_b_in, m_gm_ln_g, m_gm_ln_b, m_gm_w_s, m_gm_b_s, m_gm_w_out, m_gm_b_out, m_at_w_qkv, m_at_w_out, v_norm_pre, v_norm_post, v_ffn_w1, v_ffn_w3, v_ffn_w2, v_rel_bias, v_s5_w_in, v_s5_a_re, v_s5_a_im, v_s5_log_dt, v_s5_b_re, v_s5_b_im, v_s5_c_re, v_s5_c_im, v_s5_d, v_s5_w_glu, v_s5_b_glu, v_s5_w_out, v_cv_w_in, v_cv_b_in, v_cv_dw, v_cv_dw_b, v_cv_ln_g, v_cv_ln_b, v_cv_w_out, v_cv_b_out, v_gm_w_in, v_gm_b_in, v_gm_ln_g, v_gm_ln_b, v_gm_w_s, v_gm_b_s, v_gm_w_out, v_gm_b_out, v_at_w_qkv, v_at_w_out):
    args = locals()
    inputs = {n: args[n] for n in _TWIN_WEIGHTS}
    moments_m = {n: args["m_" + n] for n in _TWIN_WEIGHTS}
    moments_v = {n: args["v_" + n] for n in _TWIN_WEIGHTS}
    loss_local, dx, results, rep_out = _step(x[0], loss_target[0], inputs, moments_m, moments_v)
    loss = lax.psum(loss_local, AXES)
    out = [loss, dx[None]]
    for which in range(4):
        out += [_assemble(n, results, rep_out, which) for n in _TWIN_WEIGHTS]
    return tuple(out)
```

```python
import functools
import math

import numpy as np

import jax
import jax.numpy as jnp
from jax import lax
from jax.experimental import pallas as pl
from jax.experimental.pallas import tpu as pltpu

F32 = jnp.float32
BF16 = jnp.bfloat16

D_MODEL = 1024
DEPTH = 4
D_FF = 2816
EPS = 1e-6
S5_GROUP = 16
S5_STATE = 64
CONV_W = 31
GM_CHUNK = 128
GM_HEADS = 8
HEAD_DIM = 64
PATTERNS = ((128, 1), (512, 4), (2048, 16))
BLOCK = 128
NUM_BUCKETS = 32
MAX_DISTANCE = 2048
ADAM_LR = 0.001
ADAM_B1 = 0.9
ADAM_B2 = 0.999
ADAM_EPS = 1e-08
ADAM_WD = 0.01
ADAM_STEP = 10

N_DEV = 8
AXES = ("x", "y", "c")
LANES = 128
GM_E = 2 * D_MODEL
S5_GROUPS = D_MODEL // S5_GROUP
S5_GB = LANES // S5_GROUP
S5_NB = D_MODEL // LANES
S5_BW = S5_GB * S5_STATE
S5_NS = S5_GROUPS * S5_STATE
AT_HEADS = D_MODEL // HEAD_DIM
VMEM_LIMIT = 56 * 1024 * 1024
PACK_ALIGN = 16 * 1024


def _cparams(sem):
    return pltpu.CompilerParams(dimension_semantics=sem, vmem_limit_bytes=VMEM_LIMIT)


def _pick(n, cap):
    if n <= cap:
        return n
    best = None
    for t in range(LANES, cap + 1, LANES):
        if n % t == 0:
            best = t
    assert best is not None, (n, cap)
    return best


def _pick_rows(n, cap):
    best = None
    for t in range(16, min(n, cap) + 1, 16):
        if n % t == 0:
            best = t
    assert best is not None, (n, cap)
    return best


MM_VMEM_BUDGET = 40 * 1024 * 1024


def _mm(a, b, *, ta=False, tb=False, out_dtype=F32, name):
    a_list = list(a) if isinstance(a, (tuple, list)) else [a]
    b_list = list(b) if isinstance(b, (tuple, list)) else [b]
    n_op = len(a_list)
    assert n_op == len(b_list)
    K, M = a_list[0].shape if ta else a_list[0].shape[::-1]
    N, K2 = b_list[0].shape if tb else b_list[0].shape[::-1]
    assert K == K2, (a_list[0].shape, b_list[0].shape, ta, tb)
    a_bytes = sum(x.dtype.itemsize for x in a_list)
    b_bytes = sum(x.dtype.itemsize for x in b_list)
    o_bytes = jnp.dtype(out_dtype).itemsize

    def vmem(tm, tn, tk, nk):
        acc = tm * tn * 4 if (nk > 1 and out_dtype != F32) else 0
        return 2 * (tm * tk * a_bytes + tk * tn * b_bytes + tm * tn * o_bytes) + acc

    if ta:
        tm, tn = _pick(M, 1408), _pick(N, 1408)
        tk = next(t for t in (2048, 1024, 512, 256) if K % t == 0 and vmem(tm, tn, t, 2) <= MM_VMEM_BUDGET)
    else:
        tm, tn = _pick(M, 512), _pick(N, 1408)
        tk = next(t for t in (K, _pick(K, 4608), _pick(K, 2816), _pick(K, 1024))
                  if vmem(tm, tn, t, K // t) <= MM_VMEM_BUDGET)
    nk = K // tk
    a_spec = pl.BlockSpec((tk, tm), lambda j, i, k: (k, i)) if ta else pl.BlockSpec((tm, tk), lambda j, i, k: (i, k))
    b_spec = pl.BlockSpec((tn, tk), lambda j, i, k: (j, k)) if tb else pl.BlockSpec((tk, tn), lambda j, i, k: (k, j))
    dims = (((0 if ta else 1,), (1 if tb else 0,)), ((), ()))
    use_scratch = nk > 1 and out_dtype != F32

    def body(*refs):
        o_ref = refs[2 * n_op]
        p = None
        for a_ref, b_ref in zip(refs[:n_op], refs[n_op:2 * n_op]):
            d = lax.dot_general(a_ref[...].astype(BF16), b_ref[...].astype(BF16), dims, preferred_element_type=F32)
            p = d if p is None else p + d
        if nk == 1:
            o_ref[...] = p.astype(o_ref.dtype)
        else:
            acc = refs[2 * n_op + 1] if use_scratch else o_ref
            k = pl.program_id(2)

            @pl.when(k == 0)
            def _():
                acc[...] = p

            @pl.when(k > 0)
            def _():
                acc[...] += p

            if use_scratch:
                @pl.when(k == nk - 1)
                def _():
                    o_ref[...] = acc[...].astype(o_ref.dtype)

    return pl.pallas_call(
        body, name=name, grid=(N // tn, M // tm, nk), in_specs=[a_spec] * n_op + [b_spec] * n_op,
        out_specs=pl.BlockSpec((tm, tn), lambda j, i, k: (i, j)),
        out_shape=jax.ShapeDtypeStruct((M, N), out_dtype),
        scratch_shapes=[pltpu.VMEM((tm, tn), F32)] if use_scratch else [],
        compiler_params=_cparams(("parallel", "parallel", "arbitrary")),
    )(*a_list, *b_list)


ROW_TILE_BYTES = 8 * 1024 * 1024


def _row_tile(arrays):
    row_bytes = sum(w * jnp.dtype(dt).itemsize for w, dt in arrays)
    for tile in (256, 128, 64, 32):
        if tile * row_bytes <= ROW_TILE_BYTES:
            return tile
    return 16


def _rows(fn, rows, pars, outs, *, name):
    T = rows[0].shape[0]
    tile = _row_tile([(r.shape[1], r.dtype) for r in rows] + list(outs))
    nr, npar = len(rows), len(pars)

    def body(*refs):
        r = [refs[i][...] for i in range(nr)]
        p = [refs[nr + i][...] for i in range(npar)]
        res = fn(*r, *p)
        for o_ref, o in zip(refs[nr + npar:], res):
            o_ref[...] = o.astype(o_ref.dtype)

    in_specs = [pl.BlockSpec((tile, r.shape[1]), lambda i: (i, 0)) for r in rows]
    in_specs += [pl.BlockSpec(p.shape, lambda i, nd=p.ndim: (0,) * nd) for p in pars]
    return pl.pallas_call(
        body, name=name, grid=(T // tile,), in_specs=in_specs,
        out_specs=[pl.BlockSpec((tile, w), lambda i: (i, 0)) for w, _ in outs],
        out_shape=[jax.ShapeDtypeStruct((T, w), dt) for w, dt in outs],
        compiler_params=_cparams(("parallel",)),
    )(*rows, *pars)


def _rows_vjp(fn, rows, pars, cts, *, dtypes, adds=None, name):
    adds = adds or {}
    cts = [c if isinstance(c, (tuple, list)) else (c,) for c in cts]
    flat_cts = [a for c in cts for a in c]
    add_keys = sorted(adds)
    add_arrs = [adds[k] for k in add_keys]
    want = [i for i, d in enumerate(dtypes) if d is not None]
    T = rows[0].shape[0]
    tile = _row_tile([(a.shape[1], a.dtype) for a in list(rows) + flat_cts + add_arrs]
                     + [(rows[i].shape[1], dtypes[i]) for i in want])
    nr, npar, nc, na = len(rows), len(pars), len(flat_cts), len(add_arrs)

    def body(*refs):
        r = [refs[i][...].astype(F32) for i in range(nr)]
        p = [refs[nr + i][...] for i in range(npar)]
        cvals = [refs[nr + npar + i][...].astype(F32) for i in range(nc)]
        avals = [refs[nr + npar + nc + i][...].astype(F32) for i in range(na)]
        outs = refs[nr + npar + nc + na:]
        ct, pos = [], 0
        for c in cts:
            s = cvals[pos]
            for extra in cvals[pos + 1:pos + len(c)]:
                s = s + extra
            pos += len(c)
            ct.append(s)
        _, vjp = jax.vjp(lambda *a: tuple(fn(*a)), *r, *p)
        g = vjp(tuple(ct))
        for o_ref, i in zip(outs[:len(want)], want):
            gi = g[i]
            if i in adds:
                gi = gi + avals[add_keys.index(i)]
            o_ref[...] = gi.astype(o_ref.dtype)
        first = pl.program_id(0) == 0
        for o_ref, gp in zip(outs[len(want):], g[nr:]):
            @pl.when(first)
            def _(o_ref=o_ref, gp=gp):
                o_ref[...] = gp

            @pl.when(jnp.logical_not(first))
            def _(o_ref=o_ref, gp=gp):
                o_ref[...] += gp

    row_spec = lambda a: pl.BlockSpec((tile, a.shape[1]), lambda i: (i, 0))
    par_spec = lambda a: pl.BlockSpec(a.shape, lambda i, nd=a.ndim: (0,) * nd)
    res = pl.pallas_call(
        body, name=name, grid=(T // tile,),
        in_specs=[row_spec(a) for a in rows] + [par_spec(a) for a in pars] + [row_spec(a) for a in flat_cts + add_arrs],
        out_specs=[row_spec(rows[i]) for i in want] + [par_spec(a) for a in pars],
        out_shape=[jax.ShapeDtypeStruct(rows[i].shape, dtypes[i]) for i in want]
        + [jax.ShapeDtypeStruct(a.shape, F32) for a in pars],
        compiler_params=_cparams(("arbitrary",)),
    )(*rows, *pars, *flat_cts, *add_arrs)
    return res[:len(want)], res[len(want):]


def _small(fn, args, outs, *, name):
    n = len(args)

    def body(*refs):
        res = fn(*[r[...] for r in refs[:n]])
        for o_ref, o in zip(refs[n:], res):
            o_ref[...] = o

    return pl.pallas_call(body, name=name, out_shape=[jax.ShapeDtypeStruct(s, F32) for s in outs],
                          compiler_params=pltpu.CompilerParams(vmem_limit_bytes=VMEM_LIMIT))(*args)


def _small_vjp(fn, args, cts, *, name):
    n, nc = len(args), len(cts)

    def body(*refs):
        _, vjp = jax.vjp(lambda *a: tuple(fn(*a)), *[r[...] for r in refs[:n]])
        g = vjp(tuple(r[...] for r in refs[n:n + nc]))
        for o_ref, gi in zip(refs[n + nc:], g):
            o_ref[...] = gi

    return pl.pallas_call(body, name=name, out_shape=[jax.ShapeDtypeStruct(a.shape, F32) for a in args],
                          compiler_params=pltpu.CompilerParams(vmem_limit_bytes=VMEM_LIMIT))(*args, *cts)


def _rms(x, g):
    return x * lax.rsqrt(jnp.mean(x * x, axis=-1, keepdims=True) + EPS) * g


def _layernorm(x, g, b):
    mu = jnp.mean(x, axis=-1, keepdims=True)
    var = jnp.mean(jnp.square(x - mu), axis=-1, keepdims=True)
    return (x - mu) * lax.rsqrt(var + EPS) * g + b


def _f_pre(x, g):
    return (_rms(x.astype(F32), g),)


def _f_post_term(scale, has_bias):
    def fn(o, g, *b):
        o = o.astype(F32)
        if has_bias:
            o = o + b[0]
        return (scale * _rms(o, g),)
    return fn


def _f_post(scale, has_bias):
    term = _f_post_term(scale, has_bias)

    def fn(x, o, g, *b):
        return (x + term(o, g, *b)[0],)
    return fn


def _f_s5_gelu(ylin, u, d):
    return (jax.nn.gelu(ylin.astype(F32) + d * u.astype(F32)),)


def _f_s5_glu(y, gl, b):
    return (y.astype(F32) * jax.nn.sigmoid(gl.astype(F32) + b),)


def _f_cv_glu(z0, b):
    z = z0.astype(F32) + b
    return (z[:, :D_MODEL] * jax.nn.sigmoid(z[:, D_MODEL:]),)


def _f_cv_ln(zc, g, b):
    return (jax.nn.silu(_layernorm(zc.astype(F32), g, b)),)


def _f_gm_in(z0, b, g, bl):
    z = jax.nn.gelu(z0.astype(F32) + b)
    return z[:, :GM_E], _layernorm(z[:, GM_E:], g, bl)


def _f_at_combine(o0, o1, o2, l0, l1, l2):
    m = jnp.maximum(jnp.maximum(l0, l1), l2)
    e0, e1, e2 = jnp.exp(l0 - m), jnp.exp(l1 - m), jnp.exp(l2 - m)
    return ((e0 * o0 + e1 * o1 + e2 * o2) / (e0 + e1 + e2),)


def _f_s5_disc(ar, ai, ldt, br, bi):
    dt = jnp.exp(ldt)
    mag = jnp.exp(dt * ar)
    abr = mag * jnp.cos(dt * ai)
    abi = mag * jnp.sin(dt * ai)
    den = ar * ar + ai * ai
    nr = abr - 1.0
    f_re = (nr * ar + abi * ai) / den
    f_im = (abi * ar - nr * ai) / den
    return abr, abi, f_re * br - f_im * bi, f_re * bi + f_im * br


def _loss_call(y, tgt):
    T, D = y.shape
    tile = 256

    def body(y_ref, t_ref, dy_ref, l_ref):
        err = y_ref[...] - t_ref[...]
        dy_ref[...] = err * (1.0 / D)
        part = 0.5 * jnp.sum(jnp.mean(err * err, axis=-1, keepdims=True), axis=0, keepdims=True)
        part = jnp.broadcast_to(part, (1, LANES))
        first = pl.program_id(0) == 0

        @pl.when(first)
        def _():
            l_ref[...] = part

        @pl.when(jnp.logical_not(first))
        def _():
            l_ref[...] += part

    return pl.pallas_call(
        body, name="loss", grid=(T // tile,),
        in_specs=[pl.BlockSpec((tile, D), lambda i: (i, 0))] * 2,
        out_specs=[pl.BlockSpec((tile, D), lambda i: (i, 0)), pl.BlockSpec((1, LANES), lambda i: (0, 0))],
        out_shape=[jax.ShapeDtypeStruct((T, D), F32), jax.ShapeDtypeStruct((1, LANES), F32)],
        compiler_params=_cparams(("arbitrary",)),
    )(y, tgt)


def _bd(xs, ws, *, add=None, out_dtype=F32, name):
    T = xs[0].shape[0]
    nb, kw, nw = ws[0].shape
    tm = 256
    n = len(xs)

    def body(*refs):
        o_ref = refs[-1]
        for j in range(nb):
            acc = None
            for x_ref, w_ref in zip(refs[:n], refs[n:2 * n]):
                p = jnp.dot(x_ref[:, j * kw:(j + 1) * kw].astype(BF16), w_ref[j].astype(BF16),
                            preferred_element_type=F32)
                acc = p if acc is None else acc + p
            if add is not None:
                acc = acc + refs[2 * n][:, j * nw:(j + 1) * nw].astype(F32)
            o_ref[:, j * nw:(j + 1) * nw] = acc.astype(o_ref.dtype)

    in_specs = [pl.BlockSpec((tm, nb * kw), lambda i: (i, 0)) for _ in xs]
    in_specs += [pl.BlockSpec((nb, kw, nw), lambda i: (0, 0, 0)) for _ in ws]
    args = list(xs) + list(ws)
    if add is not None:
        in_specs.append(pl.BlockSpec((tm, nb * nw), lambda i: (i, 0)))
        args.append(add)
    return pl.pallas_call(
        body, name=name, grid=(T // tm,), in_specs=in_specs,
        out_specs=pl.BlockSpec((tm, nb * nw), lambda i: (i, 0)),
        out_shape=jax.ShapeDtypeStruct((T, nb * nw), out_dtype),
        compiler_params=_cparams(("parallel",)),
    )(*args)


def _bd_wgrad(x, dy, kw, nw, *, name):
    T = x.shape[0]
    nb = x.shape[1] // kw
    tk = 512

    def body(x_ref, dy_ref, o_ref):
        first = pl.program_id(0) == 0
        for j in range(nb):
            p = lax.dot_general(x_ref[:, j * kw:(j + 1) * kw].astype(BF16), dy_ref[:, j * nw:(j + 1) * nw].astype(BF16),
                                (((0,), (0,)), ((), ())), preferred_element_type=F32)

            @pl.when(first)
            def _(j=j, p=p):
                o_ref[j] = p

            @pl.when(jnp.logical_not(first))
            def _(j=j, p=p):
                o_ref[j] += p

    return pl.pallas_call(
        body, name=name, grid=(T // tk,),
        in_specs=[pl.BlockSpec((tk, nb * kw), lambda k: (k, 0)), pl.BlockSpec((tk, nb * nw), lambda k: (k, 0))],
        out_specs=pl.BlockSpec((nb, kw, nw), lambda k: (0, 0, 0)),
        out_shape=jax.ShapeDtypeStruct((nb, kw, nw), F32),
        compiler_params=_cparams(("arbitrary",)),
    )(x, dy)


SCAN_COLS = 512
SCAN_ROWS = 256


def _scan_fwd(bur, bui, ar, ai):
    T, NS = bur.shape
    cw, tc = SCAN_COLS, SCAN_ROWS

    def body(bur_ref, bui_ref, ar_ref, ai_ref, sr_ref, si_ref, cr, ci):
        @pl.when(pl.program_id(1) == 0)
        def _():
            cr[...] = jnp.zeros_like(cr)
            ci[...] = jnp.zeros_like(ci)

        a_r, a_i = ar_ref[...], ai_ref[...]

        def step8(t8, carry):
            sr, si = carry
            base = pl.multiple_of(t8 * 8, 8)
            for r in range(8):
                br = bur_ref[pl.ds(base + r, 1), :]
                bi = bui_ref[pl.ds(base + r, 1), :]
                sr, si = a_r * sr - a_i * si + br, a_r * si + a_i * sr + bi
                sr_ref[pl.ds(base + r, 1), :] = sr
                si_ref[pl.ds(base + r, 1), :] = si
            return sr, si

        sr, si = lax.fori_loop(0, tc // 8, step8, (cr[...], ci[...]))
        cr[...] = sr
        ci[...] = si

    blk = pl.BlockSpec((tc, cw), lambda c, t: (t, c))
    vec = pl.BlockSpec((1, cw), lambda c, t: (0, c))
    return pl.pallas_call(
        body, name="s5_scan_fwd", grid=(NS // cw, T // tc), in_specs=[blk, blk, vec, vec], out_specs=[blk, blk],
        out_shape=[jax.ShapeDtypeStruct((T, NS), F32)] * 2,
        scratch_shapes=[pltpu.VMEM((1, cw), F32)] * 2,
        compiler_params=_cparams(("parallel", "arbitrary")),
    )(bur, bui, ar, ai)


def _scan_bwd(gr, gi, sr, si, ar, ai):
    T, NS = gr.shape
    cw, tc = SCAN_COLS, SCAN_ROWS
    nt = T // tc

    def body(gr_ref, gi_ref, sr_ref, si_ref, ar_ref, ai_ref, lr_ref, li_ref, dar_ref, dai_ref, cr, ci):
        @pl.when(pl.program_id(1) == 0)
        def _():
            cr[...] = jnp.zeros_like(cr)
            ci[...] = jnp.zeros_like(ci)
            dar_ref[...] = jnp.zeros_like(dar_ref)
            dai_ref[...] = jnp.zeros_like(dai_ref)

        a_r, a_i = ar_ref[...], ai_ref[...]

        def step8(k, carry):
            lr, li, dar, dai = carry
            base = pl.multiple_of((tc // 8 - 1 - k) * 8, 8)
            for r in range(7, -1, -1):
                s_r = sr_ref[pl.ds(base + r, 1), :]
                s_i = si_ref[pl.ds(base + r, 1), :]
                dar = dar + lr * s_r + li * s_i
                dai = dai + li * s_r - lr * s_i
                g_r = gr_ref[pl.ds(base + r, 1), :]
                g_i = gi_ref[pl.ds(base + r, 1), :]
                lr, li = g_r + a_r * lr + a_i * li, g_i + a_r * li - a_i * lr
                lr_ref[pl.ds(base + r, 1), :] = lr
                li_ref[pl.ds(base + r, 1), :] = li
            return lr, li, dar, dai

        lr, li, dar, dai = lax.fori_loop(0, tc // 8, step8, (cr[...], ci[...], dar_ref[...], dai_ref[...]))
        cr[...] = lr
        ci[...] = li
        dar_ref[...] = dar
        dai_ref[...] = dai

    blk = pl.BlockSpec((tc, cw), lambda c, t: (nt - 1 - t, c))
    vec = pl.BlockSpec((1, cw), lambda c, t: (0, c))
    return pl.pallas_call(
        body, name="s5_scan_bwd", grid=(NS // cw, nt), in_specs=[blk, blk, blk, blk, vec, vec],
        out_specs=[blk, blk, vec, vec],
        out_shape=[jax.ShapeDtypeStruct((T, NS), F32)] * 2 + [jax.ShapeDtypeStruct((1, NS), F32)] * 2,
        scratch_shapes=[pltpu.VMEM((1, cw), F32)] * 2,
        compiler_params=_cparams(("parallel", "arbitrary")),
    )(gr, gi, sr, si, ar, ai)


CONV_ROWS = 256
CONV_HALO = 32
CONV_PAD = CONV_HALO - (CONV_W - 1)


def _conv_fwd(z, dw, dwb):
    T, D = z.shape
    tc, hl = CONV_ROWS, CONV_HALO
    per = tc // hl

    def body(z_ref, zp_ref, dw_ref, b_ref, o_ref, ext):
        i = pl.program_id(0)
        ext[pl.ds(0, hl), :] = jnp.where(i > 0, zp_ref[...], 0.0)
        ext[pl.ds(hl, tc), :] = z_ref[...]
        acc = jnp.zeros((tc, D), F32) + b_ref[...]
        for k in range(CONV_W):
            acc = acc + dw_ref[pl.ds(k, 1), :] * ext[pl.ds(CONV_PAD + k, tc), :]
        o_ref[...] = acc

    return pl.pallas_call(
        body, name="conv_fwd", grid=(T // tc,),
        in_specs=[pl.BlockSpec((tc, D), lambda i: (i, 0)),
                  pl.BlockSpec((hl, D), lambda i: (jnp.maximum(i * per - 1, 0), 0)),
                  pl.BlockSpec((hl, D), lambda i: (0, 0)), pl.BlockSpec((1, D), lambda i: (0, 0))],
        out_specs=pl.BlockSpec((tc, D), lambda i: (i, 0)),
        out_shape=jax.ShapeDtypeStruct((T, D), F32),
        scratch_shapes=[pltpu.VMEM((tc + hl, D), F32)],
        compiler_params=_cparams(("parallel",)),
    )(z, z, dw, dwb)


def _conv_bwd(dout, z, dw):
    T, D = z.shape
    tc, hl = CONV_ROWS, CONV_HALO
    per = tc // hl
    nblk = T // tc

    def body(g_ref, gn_ref, z_ref, zp_ref, dw_ref, dz_ref, ddw_ref, db_ref, gext, zext):
        i = pl.program_id(0)
        g = g_ref[...]
        gext[pl.ds(0, tc), :] = g
        gext[pl.ds(tc, hl), :] = jnp.where(i < nblk - 1, gn_ref[...], 0.0)
        zext[pl.ds(0, hl), :] = jnp.where(i > 0, zp_ref[...], 0.0)
        zext[pl.ds(hl, tc), :] = z_ref[...]
        acc = jnp.zeros((tc, D), F32)
        for k in range(CONV_W):
            acc = acc + dw_ref[pl.ds(k, 1), :] * gext[pl.ds(CONV_W - 1 - k, tc), :]
        dz_ref[...] = acc

        @pl.when(i == 0)
        def _():
            ddw_ref[...] = jnp.zeros_like(ddw_ref)
            db_ref[...] = jnp.zeros_like(db_ref)

        db_ref[...] += jnp.sum(g, axis=0, keepdims=True)
        for k in range(CONV_W):
            ddw_ref[pl.ds(k, 1), :] += jnp.sum(g * zext[pl.ds(CONV_PAD + k, tc), :], axis=0, keepdims=True)

    return pl.pallas_call(
        body, name="conv_bwd", grid=(nblk,),
        in_specs=[pl.BlockSpec((tc, D), lambda i: (i, 0)),
                  pl.BlockSpec((hl, D), lambda i: (jnp.minimum((i + 1) * per, nblk * per - 1), 0)),
                  pl.BlockSpec((tc, D), lambda i: (i, 0)),
                  pl.BlockSpec((hl, D), lambda i: (jnp.maximum(i * per - 1, 0), 0)),
                  pl.BlockSpec((hl, D), lambda i: (0, 0))],
        out_specs=[pl.BlockSpec((tc, D), lambda i: (i, 0)), pl.BlockSpec((hl, D), lambda i: (0, 0)),
                   pl.BlockSpec((1, D), lambda i: (0, 0))],
        out_shape=[jax.ShapeDtypeStruct((T, D), F32), jax.ShapeDtypeStruct((hl, D), F32),
                   jax.ShapeDtypeStruct((1, D), F32)],
        scratch_shapes=[pltpu.VMEM((tc + hl, D), F32)] * 2,
        compiler_params=_cparams(("arbitrary",)),
    )(dout, dout, z, z, dw)


def _gm_causal():
    r = lax.broadcasted_iota(jnp.int32, (GM_CHUNK, GM_CHUNK), 0)
    c = lax.broadcasted_iota(jnp.int32, (GM_CHUNK, GM_CHUNK), 1)
    return r >= c


def _gm_sg_fwd(u, v, ws, bs_col):
    T, E = u.shape
    hw = E // GM_HEADS

    def body(u_ref, v_ref, w_ref, b_ref, o_ref):
        causal = _gm_causal()
        for h in range(GM_HEADS):
            cols = slice(h * hw, (h + 1) * hw)
            w = jnp.where(causal, w_ref[h], 0.0).astype(BF16)
            s = jnp.dot(w, v_ref[:, cols], preferred_element_type=F32) + b_ref[h]
            o_ref[:, cols] = (u_ref[:, cols] * s).astype(o_ref.dtype)

    return pl.pallas_call(
        body, name="gm_sg_fwd", grid=(T // GM_CHUNK,),
        in_specs=[pl.BlockSpec((GM_CHUNK, E), lambda i: (i, 0)), pl.BlockSpec((GM_CHUNK, E), lambda i: (i, 0)),
                  pl.BlockSpec(ws.shape, lambda i: (0, 0, 0)), pl.BlockSpec(bs_col.shape, lambda i: (0, 0, 0))],
        out_specs=pl.BlockSpec((GM_CHUNK, E), lambda i: (i, 0)),
        out_shape=jax.ShapeDtypeStruct((T, E), BF16),
        compiler_params=_cparams(("parallel",)),
    )(u, v, ws, bs_col)


def _gm_sg_bwd(dus, u, v, ws, bs_col):
    T, E = u.shape
    hw = E // GM_HEADS

    def body(g_ref, u_ref, v_ref, w_ref, b_ref, du_ref, dv_ref, dw_ref, db_ref):
        causal = _gm_causal()

        @pl.when(pl.program_id(0) == 0)
        def _():
            dw_ref[...] = jnp.zeros_like(dw_ref)
            db_ref[...] = jnp.zeros_like(db_ref)

        for h in range(GM_HEADS):
            cols = slice(h * hw, (h + 1) * hw)
            w = jnp.where(causal, w_ref[h], 0.0).astype(BF16)
            vh = v_ref[:, cols]
            s = jnp.dot(w, vh, preferred_element_type=F32) + b_ref[h]
            g = g_ref[:, cols]
            du_ref[:, cols] = g * s
            ds = g * u_ref[:, cols]
            dsb = ds.astype(BF16)
            dv_ref[:, cols] = lax.dot_general(w, dsb, (((0,), (0,)), ((), ())), preferred_element_type=F32)
            dwh = lax.dot_general(dsb, vh, (((1,), (1,)), ((), ())), preferred_element_type=F32)
            dw_ref[h] += jnp.where(causal, dwh, 0.0)
            db_ref[h] += jnp.broadcast_to(jnp.sum(ds, axis=1, keepdims=True), (GM_CHUNK, LANES))

    blk = pl.BlockSpec((GM_CHUNK, E), lambda i: (i, 0))
    return pl.pallas_call(
        body, name="gm_sg_bwd", grid=(T // GM_CHUNK,),
        in_specs=[blk, blk, blk, pl.BlockSpec(ws.shape, lambda i: (0, 0, 0)),
                  pl.BlockSpec(bs_col.shape, lambda i: (0, 0, 0))],
        out_specs=[blk, blk, pl.BlockSpec(ws.shape, lambda i: (0, 0, 0)),
                   pl.BlockSpec((GM_HEADS, GM_CHUNK, LANES), lambda i: (0, 0, 0))],
        out_shape=[jax.ShapeDtypeStruct((T, E), F32), jax.ShapeDtypeStruct((T, E), F32),
                   jax.ShapeDtypeStruct(ws.shape, F32), jax.ShapeDtypeStruct((GM_HEADS, GM_CHUNK, LANES), F32)],
        compiler_params=_cparams(("arbitrary",)),
    )(dus, u, v, ws, bs_col)


def _t5_bucket_steps(dilation):
    max_exact = NUM_BUCKETS // 2
    delta = np.arange(BLOCK + 1)
    dist = delta * dilation
    distf = np.maximum(dist, 1).astype(np.float32)
    large = max_exact + (np.log(distf / np.float32(max_exact)) / np.float32(math.log(MAX_DISTANCE / max_exact))
                         * np.float32(NUM_BUCKETS - max_exact)).astype(np.int32)
    large = np.minimum(large, NUM_BUCKETS - 1)
    bucket = np.where(dist < max_exact, dist, large)
    steps = []
    for d in range(1, BLOCK + 1):
        inc = int(bucket[d] - bucket[d - 1])
        assert inc >= 0
        if inc:
            steps.append((d, inc))
    assert int(bucket[0]) == 0
    return steps


def _bucket_map(dilation):
    qi = lax.broadcasted_iota(jnp.int32, (BLOCK, 2 * BLOCK), 0)
    ki = lax.broadcasted_iota(jnp.int32, (BLOCK, 2 * BLOCK), 1)
    delta = qi + BLOCK - ki
    bm = jnp.zeros((BLOCK, 2 * BLOCK), jnp.int32)
    for thr, inc in _t5_bucket_steps(dilation):
        bm = bm + jnp.where(delta >= thr, inc, 0)
    return bm


def _at_bias(table, g, dilation):
    H = AT_HEADS

    def body(t_ref, o_ref):
        bm = _bucket_map(dilation)
        for h in range(H):
            acc = jnp.zeros((BLOCK, 2 * BLOCK), F32)
            for b in range(NUM_BUCKETS):
                acc = jnp.where(bm == b, t_ref[b, g * H + h], acc)
            o_ref[h] = acc

    return pl.pallas_call(body, name="at_bias", in_specs=[pl.BlockSpec(memory_space=pltpu.SMEM)],
                          out_shape=jax.ShapeDtypeStruct((H, BLOCK, 2 * BLOCK), F32))(table)


def _at_bias_bwd(dbias, dilation):
    H = AT_HEADS

    def body(d_ref, o_ref):
        bm = _bucket_map(dilation)
        for h in range(H):
            d = d_ref[h]
            for b in range(NUM_BUCKETS):
                o_ref[b, h] = jnp.sum(jnp.where(bm == b, d, 0.0))

    return pl.pallas_call(body, name="at_bias_bwd", out_specs=pl.BlockSpec(memory_space=pltpu.SMEM),
                          out_shape=jax.ShapeDtypeStruct((NUM_BUCKETS, H), F32))(dbias)


def _at_mask(i, nbs):
    qi = lax.broadcasted_iota(jnp.int32, (BLOCK, 2 * BLOCK), 0)
    ki = lax.broadcasted_iota(jnp.int32, (BLOCK, 2 * BLOCK), 1)
    no_prev = jnp.where(i % nbs == 0, 4 * BLOCK, 0)
    return ((ki < BLOCK) & (ki >= qi + no_prev)) | ((ki >= BLOCK) & (ki - BLOCK <= qi))


def _head_lanes():
    lane = lax.broadcasted_iota(jnp.int32, (BLOCK, LANES), 1)
    return [lane < HEAD_DIM, lane >= HEAD_DIM]


def _at_fwd(qkv, bias, nbs, cb):
    T = qkv.shape[0]
    D = D_MODEL
    npair = D // LANES
    scale = HEAD_DIM ** -0.5

    def body(q_ref, kc_ref, kp_ref, vc_ref, vp_ref, b_ref, o_ref, l_ref):
        i = pl.program_id(0)
        mask = _at_mask(i, nbs)
        sel = _head_lanes()
        for j in range(npair):
            cols = slice(j * LANES, (j + 1) * LANES)
            q = q_ref[:, cols]
            kk = jnp.concatenate([kp_ref[:, cols], kc_ref[:, cols]], axis=0)
            vv = jnp.concatenate([vp_ref[:, cols], vc_ref[:, cols]], axis=0)
            o_pair = jnp.zeros((BLOCK, LANES), F32)
            l_pair = jnp.zeros((BLOCK, LANES), F32)
            for e in range(2):
                qh = jnp.where(sel[e], q, jnp.zeros_like(q))
                s = lax.dot_general(qh, kk, (((1,), (1,)), ((), ())), preferred_element_type=F32) * scale
                s = jnp.where(mask, s + b_ref[2 * j + e], -1e30)
                m = jnp.max(s, axis=1, keepdims=True)
                p = jnp.exp(s - m)
                den = jnp.sum(p, axis=1, keepdims=True)
                o = jnp.dot(p.astype(BF16), vv, preferred_element_type=F32) / den
                o_pair = jnp.where(sel[e], o, o_pair)
                l_pair = jnp.where(sel[e], m + jnp.log(den), l_pair)
            o_ref[:, cols] = o_pair
            l_ref[:, cols] = l_pair

    blk = lambda c, prev: pl.BlockSpec((BLOCK, D), (lambda i: (jnp.maximum(i - 1, 0), cb + c)) if prev
                                       else (lambda i: (i, cb + c)))
    out = pl.BlockSpec((BLOCK, D), lambda i: (i, 0))
    return pl.pallas_call(
        body, name="at_fwd", grid=(T // BLOCK,),
        in_specs=[blk(0, False), blk(1, False), blk(1, True), blk(2, False), blk(2, True),
                  pl.BlockSpec(bias.shape, lambda i: (0, 0, 0))],
        out_specs=[out, out], out_shape=[jax.ShapeDtypeStruct((T, D), F32)] * 2,
        compiler_params=_cparams(("parallel",)),
    )(qkv, qkv, qkv, qkv, qkv, bias)


def _at_bwd(qkv, bias, o, lse, do, dlse, nbs, cb):
    T = qkv.shape[0]
    D = D_MODEL
    nblk = T // BLOCK
    npair = D // LANES
    scale = HEAD_DIM ** -0.5

    def body(q_ref, kc_ref, kp_ref, vc_ref, vp_ref, b_ref, o_ref, l_ref, do_ref, dl_ref, dqkv_ref, db_ref, carry):
        i = pl.program_id(0)

        @pl.when(i == 0)
        def _():
            carry[...] = jnp.zeros_like(carry)
            db_ref[...] = jnp.zeros_like(db_ref)

        @pl.when(i == nblk)
        def _():
            dqkv_ref[...] = carry[...].astype(dqkv_ref.dtype)

        @pl.when(i < nblk)
        def _():
            mask = _at_mask(i, nbs)
            sel = _head_lanes()
            for j in range(npair):
                cols = slice(j * LANES, (j + 1) * LANES)
                kcols = slice(D + j * LANES, D + (j + 1) * LANES)
                vcols = slice(2 * D + j * LANES, 2 * D + (j + 1) * LANES)
                q = q_ref[:, cols]
                kk = jnp.concatenate([kp_ref[:, cols], kc_ref[:, cols]], axis=0)
                vv = jnp.concatenate([vp_ref[:, cols], vc_ref[:, cols]], axis=0)
                dov = do_ref[:, cols]
                dob = dov.astype(BF16)
                oo = dov * o_ref[:, cols]
                lv = l_ref[:, cols]
                dlv = dl_ref[:, cols]
                dq_pair = jnp.zeros((BLOCK, LANES), F32)
                dk_pair = jnp.zeros((2 * BLOCK, LANES), F32)
                dv_pair = jnp.zeros((2 * BLOCK, LANES), F32)
                sel2 = [jnp.concatenate([s_, s_], axis=0) for s_ in sel]
                for e in range(2):
                    qh = jnp.where(sel[e], q, jnp.zeros_like(q))
                    s = lax.dot_general(qh, kk, (((1,), (1,)), ((), ())), preferred_element_type=F32) * scale
                    s = jnp.where(mask, s + b_ref[2 * j + e], -1e30)
                    lse_h = jnp.max(jnp.where(sel[e], lv, -jnp.inf), axis=1, keepdims=True)
                    p = jnp.exp(s - lse_h)
                    doh = jnp.where(sel[e], dob, jnp.zeros_like(dob))
                    dp = lax.dot_general(doh, vv, (((1,), (1,)), ((), ())), preferred_element_type=F32)
                    delta = jnp.sum(jnp.where(sel[e], oo, 0.0), axis=1, keepdims=True)
                    dlse_h = jnp.sum(jnp.where(sel[e], dlv, 0.0), axis=1, keepdims=True)
                    ds = p * (dp - delta + dlse_h)
                    db_ref[2 * j + e] += ds
                    dsb = (ds * scale).astype(BF16)
                    dq_pair = jnp.where(sel[e], jnp.dot(dsb, kk, preferred_element_type=F32), dq_pair)
                    dk = lax.dot_general(dsb, q, (((0,), (0,)), ((), ())), preferred_element_type=F32)
                    dk_pair = jnp.where(sel2[e], dk, dk_pair)
                    dv = lax.dot_general(p.astype(BF16), dob, (((0,), (0,)), ((), ())), preferred_element_type=F32)
                    dv_pair = jnp.where(sel2[e], dv, dv_pair)
                dqkv_ref[:, cols] = carry[:, cols].astype(dqkv_ref.dtype)
                dqkv_ref[:, kcols] = (carry[:, kcols] + dk_pair[:BLOCK]).astype(dqkv_ref.dtype)
                dqkv_ref[:, vcols] = (carry[:, vcols] + dv_pair[:BLOCK]).astype(dqkv_ref.dtype)
                carry[:, cols] = dq_pair
                carry[:, kcols] = dk_pair[BLOCK:]
                carry[:, vcols] = dv_pair[BLOCK:]

    cur = lambda i: jnp.minimum(i, nblk - 1)
    prev = lambda i: jnp.maximum(jnp.minimum(i, nblk - 1) - 1, 0)
    blk = lambda c, pv: pl.BlockSpec((BLOCK, D), (lambda i: (prev(i), cb + c)) if pv else (lambda i: (cur(i), cb + c)))
    row = pl.BlockSpec((BLOCK, D), lambda i: (cur(i), 0))
    return pl.pallas_call(
        body, name="at_bwd", grid=(nblk + 1,),
        in_specs=[blk(0, False), blk(1, False), blk(1, True), blk(2, False), blk(2, True),
                  pl.BlockSpec(bias.shape, lambda i: (0, 0, 0)), row, row, row, row],
        out_specs=[pl.BlockSpec((BLOCK, 3 * D), lambda i: (jnp.maximum(i - 1, 0), 0)),
                   pl.BlockSpec(bias.shape, lambda i: (0, 0, 0))],
        out_shape=[jax.ShapeDtypeStruct((T, 3 * D), BF16), jax.ShapeDtypeStruct(bias.shape, F32)],
        scratch_shapes=[pltpu.VMEM((BLOCK, 3 * D), F32)],
        compiler_params=_cparams(("arbitrary",)),
    )(qkv, qkv, qkv, qkv, qkv, bias, o, lse, do, dlse)


def _to_residue_major(a, d):
    if d == 1:
        return a
    T, C = a.shape
    return a.reshape(T // d, d, C).transpose(1, 0, 2).reshape(T, C)


def _from_residue_major(a, d):
    if d == 1:
        return a
    T, C = a.shape
    return a.reshape(d, T // d, C).transpose(1, 0, 2).reshape(T, C)


FFN_TM = 512


def _ffn_up(h, w1, w3):
    T, Dm = h.shape
    Fw = w1.shape[1]
    tm, tn = FFN_TM, _pick(Fw, 1408)

    def body(h_ref, w1_ref, w3_ref, a_ref, b_ref, u_ref):
        hv = h_ref[...]
        a = jnp.dot(hv, w1_ref[...], preferred_element_type=F32)
        b = jnp.dot(hv, w3_ref[...], preferred_element_type=F32)
        a_ref[...] = a.astype(a_ref.dtype)
        b_ref[...] = b.astype(b_ref.dtype)
        u_ref[...] = (jax.nn.silu(a) * b).astype(u_ref.dtype)

    wspec = pl.BlockSpec((Dm, tn), lambda j, i: (0, j))
    ospec = pl.BlockSpec((tm, tn), lambda j, i: (i, j))
    return pl.pallas_call(
        body, name="ffn_up", grid=(Fw // tn, T // tm),
        in_specs=[pl.BlockSpec((tm, Dm), lambda j, i: (i, 0)), wspec, wspec], out_specs=[ospec] * 3,
        out_shape=[jax.ShapeDtypeStruct((T, Fw), BF16)] * 3,
        compiler_params=_cparams(("parallel", "parallel")),
    )(h, w1, w3)


def _ffn_down_dx(do, w2, a, b):
    T, Dm = do.shape
    Fw = w2.shape[0]
    tm, tn = FFN_TM, _pick(Fw, 1408)

    def body(do_ref, w2_ref, a_ref, b_ref, da_ref, db_ref):
        du = lax.dot_general(do_ref[...], w2_ref[...], (((1,), (1,)), ((), ())), preferred_element_type=F32)
        av = a_ref[...].astype(F32)
        bv = b_ref[...].astype(F32)
        sg = jax.nn.sigmoid(av)
        silu = av * sg
        da_ref[...] = (du * bv * (sg + silu * (1.0 - sg))).astype(da_ref.dtype)
        db_ref[...] = (du * silu).astype(db_ref.dtype)

    ospec = pl.BlockSpec((tm, tn), lambda j, i: (i, j))
    return pl.pallas_call(
        body, name="ffn_down_dx", grid=(Fw // tn, T // tm),
        in_specs=[pl.BlockSpec((tm, Dm), lambda j, i: (i, 0)), pl.BlockSpec((tn, Dm), lambda j, i: (j, 0)), ospec, ospec],
        out_specs=[ospec] * 2, out_shape=[jax.ShapeDtypeStruct((T, Fw), BF16)] * 2,
        compiler_params=_cparams(("parallel", "parallel")),
    )(do, w2, a, b)


def _ffn_fwd(x, p):
    h, = _rows(_f_pre, [x], [p["g_pre"]], [(D_MODEL, BF16)], name="ffn_pre")
    a, b, u = _ffn_up(h, p["w1"], p["w3"])
    o = _mm(u, p["w2"], name="ffn_down")
    xo, = _rows(_f_post(0.5, False), [x, o], [p["g_post"]], [(D_MODEL, F32)], name="ffn_post")
    return xo, (x, h, a, b, u, o)


def _ffn_bwd(saved, p, dxo):
    x, h, a, b, u, o = saved
    (do,), (dg_post,) = _rows_vjp(_f_post_term(0.5, False), [o], [p["g_post"]], [dxo], dtypes=[BF16], name="ffn_post_bwd")
    da, db = _ffn_down_dx(do, p["w2"], a, b)
    dw2 = _mm(u, do, ta=True, name="ffn_down_dw")
    dh = _mm((da, db), (p["w1"], p["w3"]), tb=True, name="ffn_up_dx")
    dw1 = _mm(h, da, ta=True, name="ffn_up_dw")
    dw3 = _mm(h, db, ta=True, name="ffn_up_dw")
    (dx,), (dg_pre,) = _rows_vjp(_f_pre, [x], [p["g_pre"]], [dh], dtypes=[F32], adds={0: dxo}, name="ffn_pre_bwd")
    return dx, {"w1": dw1, "w3": dw3, "w2": dw2, "g_pre": dg_pre, "g_post": dg_post}


def _expand_blocks(w, rows_first):
    w = w.reshape(S5_NB, S5_GB, S5_GROUP, S5_STATE)
    eye = jnp.eye(S5_GB, dtype=F32)
    if rows_first:
        e = w[:, :, :, None, :] * eye[None, :, None, :, None]
        return e.reshape(S5_NB, S5_GB * S5_GROUP, S5_BW)
    e = jnp.transpose(w, (0, 1, 3, 2))[:, :, :, None, :] * eye[None, :, None, :, None]
    return e.reshape(S5_NB, S5_BW, S5_GB * S5_GROUP)


def _extract_blocks(e, rows_first):
    eye = jnp.eye(S5_GB, dtype=F32)
    if rows_first:
        e = e.reshape(S5_NB, S5_GB, S5_GROUP, S5_GB, S5_STATE)
        w = jnp.sum(e * eye[None, :, None, :, None], axis=3)
    else:
        e = e.reshape(S5_NB, S5_GB, S5_STATE, S5_GB, S5_GROUP)
        w = jnp.transpose(jnp.sum(e * eye[None, :, None, :, None], axis=3), (0, 1, 3, 2))
    return w.reshape(S5_GROUPS, S5_GROUP, S5_STATE)


def _s5_prep(p):
    G, P, HG = S5_GROUPS, S5_STATE, S5_GROUP
    args = [p["a_re"].reshape(G, 1, P), p["a_im"].reshape(G, 1, P), p["log_dt"].reshape(G, 1, 1),
            jnp.transpose(p["b_re"], (0, 2, 1)), jnp.transpose(p["b_im"], (0, 2, 1))]
    abr, abi, bbr, bbi = _small(_f_s5_disc, args, [(G, 1, P)] * 2 + [(G, HG, P)] * 2, name="s5_disc")
    return args, abr.reshape(1, G * P), abi.reshape(1, G * P), bbr, bbi


def _s5_fwd(h, p):
    disc_args, abr, abi, bbr, bbi = _s5_prep(p)
    c_re, c_im = p["c_re"], p["c_im"]
    u = _mm(h, p["w_in"], name="s5_in")
    bur = _bd([u], [_expand_blocks(bbr, True)], name="s5_bu")
    bui = _bd([u], [_expand_blocks(bbi, True)], name="s5_bu")
    sr, si = _scan_fwd(bur, bui, abr, abi)
    ylin = _bd([sr, si], [_expand_blocks(c_re, False), _expand_blocks(-c_im, False)], name="s5_y")
    y, = _rows(_f_s5_gelu, [ylin, u], [p["d"]], [(D_MODEL, F32)], name="s5_gelu")
    gl = _mm(y, p["w_glu"], name="s5_glu_mm")
    z, = _rows(_f_s5_glu, [y, gl], [p["b_glu"]], [(D_MODEL, BF16)], name="s5_glu")
    m = _mm(z, p["w_out"], name="s5_out")
    return m, None, (h, disc_args, abr, abi, bbr, bbi, u, sr, si, ylin, y, gl, z)


def _s5_bwd(saved, p, dm):
    h, disc_args, abr, abi, bbr, bbi, u, sr, si, ylin, y, gl, z = saved
    c_re, c_im = p["c_re"], p["c_im"]
    dz = _mm(dm, p["w_out"], tb=True, name="s5_out_dx")
    dw_out = _mm(z, dm, ta=True, name="s5_out_dw")
    (dy1, dgl), (db_glu,) = _rows_vjp(_f_s5_glu, [y, gl], [p["b_glu"]], [dz], dtypes=[F32, BF16], name="s5_glu_bwd")
    dy2 = _mm(dgl, p["w_glu"], tb=True, name="s5_glu_dx")
    dw_glu = _mm(y, dgl, ta=True, name="s5_glu_dw")
    (dylin, du1), (dd,) = _rows_vjp(_f_s5_gelu, [ylin, u], [p["d"]], [(dy1, dy2)], dtypes=[F32, F32], name="s5_gelu_bwd")
    gr = _bd([dylin], [jnp.transpose(_expand_blocks(c_re, False), (0, 2, 1))], name="s5_y_dx")
    gi = _bd([dylin], [jnp.transpose(_expand_blocks(-c_im, False), (0, 2, 1))], name="s5_y_dx")
    dc_re = _extract_blocks(_bd_wgrad(sr, dylin, S5_BW, LANES, name="s5_y_dw"), False)
    dc_im = -_extract_blocks(_bd_wgrad(si, dylin, S5_BW, LANES, name="s5_y_dw"), False)
    lr, li, dabr, dabi = _scan_bwd(gr, gi, sr, si, abr, abi)
    du = _bd([lr, li], [jnp.transpose(_expand_blocks(bbr, True), (0, 2, 1)),
                        jnp.transpose(_expand_blocks(bbi, True), (0, 2, 1))], add=du1, out_dtype=BF16, name="s5_bu_dx")
    dbbr = _extract_blocks(_bd_wgrad(u, lr, LANES, S5_BW, name="s5_bu_dw"), True)
    dbbi = _extract_blocks(_bd_wgrad(u, li, LANES, S5_BW, name="s5_bu_dw"), True)
    G, P = S5_GROUPS, S5_STATE
    dar, dai, dldt, dbr, dbi = _small_vjp(_f_s5_disc, disc_args,
                                          [dabr.reshape(G, 1, P), dabi.reshape(G, 1, P), dbbr, dbbi], name="s5_disc_bwd")
    dh = _mm(du, p["w_in"], tb=True, name="s5_in_dx")
    dw_in = _mm(h, du, ta=True, name="s5_in_dw")
    grads = {"w_in": dw_in, "w_glu": dw_glu, "w_out": dw_out, "b_glu": db_glu, "d": dd,
             "a_re": dar.reshape(G, P), "a_im": dai.reshape(G, P), "log_dt": dldt.reshape(G),
             "b_re": jnp.transpose(dbr, (0, 2, 1)), "b_im": jnp.transpose(dbi, (0, 2, 1)),
             "c_re": dc_re, "c_im": dc_im}
    return dh, grads


def _cv_fwd(h, p):
    z0 = _mm(h, p["w_in"], name="cv_in")
    zg, = _rows(_f_cv_glu, [z0], [p["b_in"]], [(D_MODEL, F32)], name="cv_glu")
    zc = _conv_fwd(zg, p["dw"], p["dw_b"])
    zl, = _rows(_f_cv_ln, [zc], [p["ln_g"], p["ln_b"]], [(D_MODEL, BF16)], name="cv_ln")
    m = _mm(zl, p["w_out"], name="cv_out")
    return m, p["b_out"], (h, z0, zg, zc, zl)


def _cv_bwd(saved, p, dm):
    h, z0, zg, zc, zl = saved
    dzl = _mm(dm, p["w_out"], tb=True, name="cv_out_dx")
    dw_out = _mm(zl, dm, ta=True, name="cv_out_dw")
    (dzc,), (dln_g, dln_b) = _rows_vjp(_f_cv_ln, [zc], [p["ln_g"], p["ln_b"]], [dzl], dtypes=[F32], name="cv_ln_bwd")
    dzg, ddw, ddw_b = _conv_bwd(dzc, zg, p["dw"])
    (dz0,), (db_in,) = _rows_vjp(_f_cv_glu, [z0], [p["b_in"]], [dzg], dtypes=[BF16], name="cv_glu_bwd")
    dh = _mm(dz0, p["w_in"], tb=True, name="cv_in_dx")
    dw_in = _mm(h, dz0, ta=True, name="cv_in_dw")
    return dh, {"w_in": dw_in, "b_in": db_in, "dw": ddw, "dw_b": ddw_b, "ln_g": dln_g, "ln_b": dln_b, "w_out": dw_out}


def _gm_fwd(h, p):
    z0 = _mm(h, p["w_in"], name="gm_in")
    u, v = _rows(_f_gm_in, [z0], [p["b_in"], p["ln_g"], p["ln_b"]], [(GM_E, F32), (GM_E, BF16)], name="gm_act")
    bs_col = p["b_s"].reshape(GM_HEADS, GM_CHUNK, 1)
    us = _gm_sg_fwd(u, v, p["w_s"], bs_col)
    m = _mm(us, p["w_out"], name="gm_out")
    return m, p["b_out"], (h, z0, u, v, us, bs_col)


def _gm_bwd(saved, p, dm):
    h, z0, u, v, us, bs_col = saved
    dus = _mm(dm, p["w_out"], tb=True, name="gm_out_dx")
    dw_out = _mm(us, dm, ta=True, name="gm_out_dw")
    du, dv, dw_s, db_s = _gm_sg_bwd(dus, u, v, p["w_s"], bs_col)
    (dz0,), (db_in, dln_g, dln_b) = _rows_vjp(_f_gm_in, [z0], [p["b_in"], p["ln_g"], p["ln_b"]], [du, dv],
                                              dtypes=[BF16], name="gm_act_bwd")
    dh = _mm(dz0, p["w_in"], tb=True, name="gm_in_dx")
    dw_in = _mm(h, dz0, ta=True, name="gm_in_dw")
    return dh, {"w_in": dw_in, "b_in": db_in, "ln_g": dln_g, "ln_b": dln_b, "w_s": dw_s, "b_s": db_s[:, :, 0],
                "w_out": dw_out}


def _at_fwd_mixer(h, p):
    T = h.shape[0]
    D = D_MODEL
    qkv = _mm(h, p["w_qkv"], out_dtype=BF16, name="at_qkv")
    res, outs, lses, biases = [], [], [], []
    for g, (window, d) in enumerate(PATTERNS):
        assert window // d == BLOCK and T % (BLOCK * d) == 0
        bias = _at_bias(p["rel_bias"], g, d)
        if d == 1:
            r, cb = qkv, 3 * g
        else:
            r, cb = _to_residue_major(qkv[:, g * 3 * D:(g + 1) * 3 * D], d), 0
        o, lse = _at_fwd(r, bias, T // d // BLOCK, cb)
        res.append((r, cb, o, lse))
        biases.append(bias)
        outs.append(_from_residue_major(o, d))
        lses.append(_from_residue_major(lse, d))
    oc, = _rows(_f_at_combine, outs + lses, [], [(D, BF16)], name="at_combine")
    m = _mm(oc, p["w_out"], name="at_out")
    return m, None, (h, res, biases, outs, lses, oc)


def _at_bwd_mixer(saved, p, dm):
    h, res, biases, outs, lses, oc = saved
    T = h.shape[0]
    doc = _mm(dm, p["w_out"], tb=True, name="at_out_dx")
    dw_out = _mm(oc, dm, ta=True, name="at_out_dw")
    dol, _ = _rows_vjp(_f_at_combine, outs + lses, [], [doc], dtypes=[F32] * 6, name="at_combine_bwd")
    dqkv, dtab = [], []
    for g, (window, d) in enumerate(PATTERNS):
        r, cb, o_res, lse_res = res[g]
        dq, dbias = _at_bwd(r, biases[g], o_res, lse_res, _to_residue_major(dol[g], d),
                            _to_residue_major(dol[3 + g], d), T // d // BLOCK, cb)
        dqkv.append(_from_residue_major(dq, d))
        dtab.append(_at_bias_bwd(dbias, d))
    dqkv = jnp.concatenate(dqkv, axis=1)
    dh = _mm(dqkv, p["w_qkv"], tb=True, name="at_qkv_dx")
    dw_qkv = _mm(h, dqkv, ta=True, name="at_qkv_dw")
    return dh, {"w_qkv": dw_qkv, "w_out": dw_out, "rel_bias": jnp.concatenate(dtab, axis=1)}


_MIXERS = ((_s5_fwd, _s5_bwd), (_cv_fwd, _cv_bwd), (_gm_fwd, _gm_bwd), (_at_fwd_mixer, _at_bwd_mixer))


def _mixer_fwd(x, p, kind):
    h, = _rows(_f_pre, [x], [p["g_pre"]], [(D_MODEL, BF16)], name="mix_pre")
    m, bias, saved = _MIXERS[kind][0](h, p)
    extra = [] if bias is None else [bias]
    xo, = _rows(_f_post(1.0, bias is not None), [x, m], [p["g_post"]] + extra, [(D_MODEL, F32)], name="mix_post")
    return xo, (x, m, bias, saved)


def _mixer_bwd(saved_all, p, kind, dxo):
    x, m, bias, saved = saved_all
    extra = [] if bias is None else [bias]
    (dm,), dpars = _rows_vjp(_f_post_term(1.0, bias is not None), [m], [p["g_post"]] + extra, [dxo], dtypes=[BF16],
                             name="mix_post_bwd")
    dh, grads = _MIXERS[kind][1](saved, p, dm)
    (dx,), (dg_pre,) = _rows_vjp(_f_pre, [x], [p["g_pre"]], [dh], dtypes=[F32], adds={0: dxo}, name="mix_pre_bwd")
    grads["g_pre"] = dg_pre
    grads["g_post"] = dpars[0]
    if bias is not None:
        grads["b_out"] = dpars[1]
    return dx, grads


def _exchange(arrays, kinds, *, name):
    n = len(arrays)
    outs_shape = [jax.ShapeDtypeStruct((N_DEV,) + (a.shape[1:] if k == "a2a" else a.shape), a.dtype)
                  for a, k in zip(arrays, kinds)]

    def body(*refs):
        ins, outs = refs[:n], refs[n:2 * n]
        send_sems, recv_sems, local_sems = refs[2 * n:]
        x, y, c = lax.axis_index("x"), lax.axis_index("y"), lax.axis_index("c")
        me = 4 * x + 2 * y + c
        sends, recvs, locals_ = [], [], []
        for a in range(n):
            a2a = kinds[a] == "a2a"
            lc = pltpu.make_async_copy(ins[a].at[me] if a2a else ins[a], outs[a].at[me], local_sems.at[a])
            lc.start()
            locals_.append(lc)
            for k in range(1, N_DEV):
                px = 1 - x if k & 4 else x
                py = 1 - y if k & 2 else y
                pc = 1 - c if k & 1 else c
                peer = 4 * px + 2 * py + pc
                idx = a * (N_DEV - 1) + k - 1
                src = ins[a].at[peer] if a2a else ins[a]
                cp = pltpu.make_async_remote_copy(src_ref=src, dst_ref=outs[a].at[me], send_sem=send_sems.at[idx],
                                                  recv_sem=recv_sems.at[idx], device_id=(px, py, pc),
                                                  device_id_type=pl.DeviceIdType.MESH)
                cp.start()
                sends.append(cp)
                recvs.append(pltpu.make_async_remote_copy(src_ref=src, dst_ref=outs[a].at[peer],
                                                          send_sem=send_sems.at[idx], recv_sem=recv_sems.at[idx],
                                                          device_id=(px, py, pc), device_id_type=pl.DeviceIdType.MESH))
        for cp in sends:
            cp.wait_send()
        for cp in recvs:
            cp.wait_recv()
        for lc in locals_:
            lc.wait()

    hbm = pl.BlockSpec(memory_space=pl.ANY)
    return pl.pallas_call(
        body, name=name, in_specs=[hbm] * n, out_specs=[hbm] * n, out_shape=outs_shape,
        scratch_shapes=[pltpu.SemaphoreType.DMA((n * (N_DEV - 1),)), pltpu.SemaphoreType.DMA((n * (N_DEV - 1),)),
                        pltpu.SemaphoreType.DMA((n,))],
    )(*arrays)


def _adam(recv, w, m, v, *, name):
    R, C = w.shape
    tr = _pick_rows(R, 128)
    c1 = 1.0 - ADAM_B1 ** ADAM_STEP
    c2 = 1.0 - ADAM_B2 ** ADAM_STEP

    def body(r_ref, w_ref, m_ref, v_ref, g_ref, d_ref, nm_ref, nv_ref):
        g = r_ref[0].astype(F32)
        for q in range(1, N_DEV):
            g = g + r_ref[q].astype(F32)
        mm = ADAM_B1 * m_ref[...] + (1.0 - ADAM_B1) * g
        vv = ADAM_B2 * v_ref[...] + (1.0 - ADAM_B2) * jnp.square(g)
        m_hat = mm / c1
        v_hat = vv / c2
        g_ref[...] = g
        d_ref[...] = -ADAM_LR * (m_hat / (jnp.sqrt(v_hat) + ADAM_EPS) + ADAM_WD * w_ref[...])
        nm_ref[...] = mm
        nv_ref[...] = vv

    blk = pl.BlockSpec((tr, C), lambda i: (i, 0))
    return pl.pallas_call(
        body, name=name, grid=(R // tr,),
        in_specs=[pl.BlockSpec((N_DEV, tr, C), lambda i: (0, i, 0)), blk, blk, blk], out_specs=[blk] * 4,
        out_shape=[jax.ShapeDtypeStruct((R, C), F32)] * 4,
        compiler_params=_cparams(("parallel",)),
    )(recv, w, m, v)


PACK_COLS = 1024


def _padded(n):
    return -(-n // PACK_ALIGN) * PACK_ALIGN


def _pack_flat(pieces):
    flat = jnp.concatenate([p.reshape(-1) for p in pieces])
    n = flat.shape[0]
    return jnp.pad(flat, (0, _padded(n) - n)).reshape(-1, PACK_COLS)


def _shard_shape(shape, axis):
    s = list(shape)
    assert s[axis] % N_DEV == 0
    s[axis] //= N_DEV
    return tuple(s)


def _split_full(full, axis):
    s = full.shape
    r = full.reshape(s[:axis] + (N_DEV, s[axis] // N_DEV) + s[axis + 1:])
    return jnp.moveaxis(r, axis, 0)


def _merge_full(parts, axis):
    r = jnp.moveaxis(parts, 0, axis)
    s = r.shape
    return r.reshape(s[:axis] + (s[axis] * s[axis + 1],) + s[axis + 2:])


def _pack_full(entries, grads):
    flat = jnp.concatenate([_split_full(grads[k].reshape(shape), axis).reshape(N_DEV, -1)
                            for k, shape, axis in entries], axis=1)
    n = flat.shape[1]
    return jnp.pad(flat, ((0, 0), (0, _padded(n) - n))).reshape(N_DEV, -1, PACK_COLS)


def _unpack_gathered(entries, buf):
    flat = buf.reshape(N_DEV, -1)
    out, pos = {}, 0
    for k, shape, axis in entries:
        ss = _shard_shape(shape, axis)
        n = int(np.prod(ss))
        out[k] = _merge_full(flat[:, pos:pos + n].reshape((N_DEV,) + ss), axis)
        pos += n
    return out


def _unpack_shard(entries, buf):
    flat = buf.reshape(-1)
    out, pos = {}, 0
    for k, shape, axis in entries:
        ss = _shard_shape(shape, axis)
        n = int(np.prod(ss))
        out[k] = flat[pos:pos + n].reshape(ss)
        pos += n
    return out


def _unpack_flat(entries, buf):
    flat = buf.reshape(-1)
    out, pos = {}, 0
    for k, shape in entries:
        n = int(np.prod(shape))
        out[k] = flat[pos:pos + n].reshape(shape)
        pos += n
    return out


D, FF = D_MODEL, D_FF
_FFN_MATS = (("w1", (D, FF), 1), ("w3", (D, FF), 1), ("w2", (FF, D), 0))
_NORM_VECS = (("g_pre", (D,), 0), ("g_post", (D,), 0))
_MIX_MATS = (
    (("w_in", (D, D), 0), ("w_glu", (D, D), 0), ("w_out", (D, D), 0)),
    (("w_in", (D, 2 * D), 1), ("w_out", (D, D), 0)),
    (("w_in", (D, 2 * GM_E), 1), ("w_out", (GM_E, D), 0)),
    (("w_qkv", (D, 9 * D), 1), ("w_out", (D, D), 0)),
)
_MIX_VECS = (
    (),
    (("b_in", (2 * D,), 0), ("dw", (CONV_W, D), 1), ("dw_b", (D,), 0), ("ln_g", (D,), 0), ("ln_b", (D,), 0),
     ("b_out", (D,), 0)),
    (("b_in", (2 * GM_E,), 0), ("ln_g", (GM_E,), 0), ("ln_b", (GM_E,), 0), ("b_out", (D,), 0)),
    (),
)
_REPLICATED = (
    ("rel_bias", 3, "rel_bias", (NUM_BUCKETS, 3 * AT_HEADS)),
    ("s5_a_re", 0, "a_re", (S5_GROUPS, S5_STATE)), ("s5_a_im", 0, "a_im", (S5_GROUPS, S5_STATE)),
    ("s5_log_dt", 0, "log_dt", (S5_GROUPS,)),
    ("s5_b_re", 0, "b_re", (S5_GROUPS, S5_STATE, S5_GROUP)), ("s5_b_im", 0, "b_im", (S5_GROUPS, S5_STATE, S5_GROUP)),
    ("s5_c_re", 0, "c_re", (S5_GROUPS, S5_GROUP, S5_STATE)), ("s5_c_im", 0, "c_im", (S5_GROUPS, S5_GROUP, S5_STATE)),
    ("s5_d", 0, "d", (D,)), ("s5_b_glu", 0, "b_glu", (D,)),
    ("gm_w_s", 2, "w_s", (GM_HEADS, GM_CHUNK, GM_CHUNK)), ("gm_b_s", 2, "b_s", (GM_HEADS, GM_CHUNK)),
)
_MIX_PREFIX = ("s5_", "cv_", "gm_", "at_")
_TWIN_WEIGHTS = ('norm_pre', 'norm_post', 'ffn_w1', 'ffn_w3', 'ffn_w2', 'rel_bias', 's5_w_in', 's5_a_re', 's5_a_im',
                 's5_log_dt', 's5_b_re', 's5_b_im', 's5_c_re', 's5_c_im', 's5_d', 's5_w_glu', 's5_b_glu', 's5_w_out',
                 'cv_w_in', 'cv_b_in', 'cv_dw', 'cv_dw_b', 'cv_ln_g', 'cv_ln_b', 'cv_w_out', 'cv_b_out', 'gm_w_in',
                 'gm_b_in', 'gm_ln_g', 'gm_ln_b', 'gm_w_s', 'gm_b_s', 'gm_w_out', 'gm_b_out', 'at_w_qkv', 'at_w_out')


def _part_entries(part):
    if part[0] == "ffn":
        return _FFN_MATS, _NORM_VECS
    kind = part[1] % 4
    return _MIX_MATS[kind], _NORM_VECS + _MIX_VECS[kind]


def _part_shards(part, get):
    if part[0] == "ffn":
        _, i, j = part
        n = 0 if j == 0 else 2
        return {"w1": get("ffn_w1")[i, j], "w3": get("ffn_w3")[i, j], "w2": get("ffn_w2")[i, j],
                "g_pre": get("norm_pre")[i, n], "g_post": get("norm_post")[i, n]}
    _, i = part
    kind, j = i % 4, i // 4
    out = {"g_pre": get("norm_pre")[i, 1], "g_post": get("norm_post")[i, 1]}
    for k, _, _ in _MIX_MATS[kind] + _MIX_VECS[kind]:
        out[k] = get(_MIX_PREFIX[kind] + k)[j]
    return out


def _parts():
    parts = []
    for i in range(DEPTH):
        parts += [("ffn", i, 0), ("mix", i), ("ffn", i, 1)]
    return parts


def _as_par(v):
    return v.reshape(1, -1)


def _prepare_part(part, full, rep):
    if part[0] == "ffn":
        return {"w1": full["w1"], "w3": full["w3"], "w2": full["w2"],
                "g_pre": _as_par(full["g_pre"]), "g_post": _as_par(full["g_post"])}
    kind = part[1] % 4
    p = {"g_pre": _as_par(full["g_pre"]), "g_post": _as_par(full["g_post"])}
    for k, _, _ in _MIX_MATS[kind]:
        p[k] = full[k]
    for k, _, _ in _MIX_VECS[kind]:
        p[k] = _as_par(full[k]) if k != "dw" else jnp.pad(full[k], ((0, CONV_HALO - CONV_W), (0, 0)))
    if kind == 0:
        for k in ("a_re", "a_im", "log_dt", "b_re", "b_im"):
            p[k] = rep[k]
        p["c_re"], p["c_im"] = rep["c_re"], rep["c_im"]
        p["d"], p["b_glu"] = _as_par(rep["d"]), _as_par(rep["b_glu"])
    elif kind == 2:
        p["w_s"], p["b_s"] = rep["w_s"], rep["b_s"]
    elif kind == 3:
        p["rel_bias"] = rep["rel_bias"]
    return p


def _finish_grads(part, grads):
    out = dict(grads)
    for k in ("g_pre", "g_post", "b_in", "dw_b", "ln_g", "ln_b", "b_out", "d", "b_glu"):
        if k in out:
            out[k] = out[k].reshape(-1)
    if "dw" in out:
        out["dw"] = out["dw"][:CONV_W]
    return out


def _step(x, tgt, inputs, moments_m, moments_v):
    parts = _parts()
    rep = {}
    for name, kind, key, shape in _REPLICATED:
        rep[key] = inputs[name][0] if name != "rel_bias" else inputs[name]

    params, packed_w = [], []
    for part in parts:
        mats, vecs = _part_entries(part)
        sh = _part_shards(part, lambda n: inputs[n])
        wm = _pack_flat([sh[k] for k, _, _ in mats])
        wv = _pack_flat([sh[k] for k, _, _ in vecs])
        gm_, gv_ = _exchange([wm.astype(BF16), wv], ["bcast", "bcast"], name="gather_" + part[0])
        full = _unpack_gathered(mats, gm_)
        full.update(_unpack_gathered(vecs, gv_))
        params.append(_prepare_part(part, full, rep))
        packed_w.append((wm, wv))

    saved = []
    h = x
    for part, p in zip(parts, params):
        if part[0] == "ffn":
            h, s = _ffn_fwd(h, p)
        else:
            h, s = _mixer_fwd(h, p, part[1] % 4)
        saved.append(s)
    dh, loss_vec = _loss_call(h, tgt)
    loss_local = loss_vec[0, 0]

    results = {}
    rep_grads = {}
    for idx in range(len(parts) - 1, -1, -1):
        part, p = parts[idx], params[idx]
        if part[0] == "ffn":
            dh, grads = _ffn_bwd(saved[idx], p, dh)
        else:
            dh, grads = _mixer_bwd(saved[idx], p, part[1] % 4, dh)
        grads = _finish_grads(part, grads)
        for name, kind, key, shape in _REPLICATED:
            if part[0] == "mix" and kind == part[1] % 4:
                rep_grads[name] = grads[key]
        mats, vecs = _part_entries(part)
        rm, rv = _exchange([_pack_full(mats, grads).astype(BF16), _pack_full(vecs, grads)], ["a2a", "a2a"],
                           name="scatter_" + part[0])
        wm, wv = packed_w[idx]
        mm_ = _part_shards(part, lambda n: moments_m[n])
        vv_ = _part_shards(part, lambda n: moments_v[n])
        om = _adam(rm, wm, _pack_flat([mm_[k] for k, _, _ in mats]), _pack_flat([vv_[k] for k, _, _ in mats]),
                   name="adam_" + part[0] + "_mats")
        ov = _adam(rv, wv, _pack_flat([mm_[k] for k, _, _ in vecs]), _pack_flat([vv_[k] for k, _, _ in vecs]),
                   name="adam_" + part[0] + "_vecs")
        res = [_unpack_shard(mats, o) for o in om]
        for r, o in zip(res, ov):
            r.update(_unpack_shard(vecs, o))
        results[part] = res

    rep_entries = [(name, shape) for name, _, _, shape in _REPLICATED]
    get_rep = lambda d: _pack_flat([(d[name][0] if name != "rel_bias" else d[name]) for name, _ in rep_entries])
    rg, = _exchange([_pack_flat([rep_grads[name] for name, _ in rep_entries])], ["bcast"], name="allgather_rep")
    orep = _adam(rg, get_rep(inputs), get_rep(moments_m), get_rep(moments_v), name="adam_rep")
    rep_out = [_unpack_flat(rep_entries, o) for o in orep]
    return loss_local, dh, results, rep_out


def _assemble(name, results, rep_out, which):
    for rname, _, _, _ in _REPLICATED:
        if rname == name:
            a = rep_out[which][name]
            return a if name == "rel_bias" else a[None]
    if name in ("norm_pre", "norm_post"):
        key = "g_pre" if name == "norm_pre" else "g_post"
        rows = []
        for i in range(DEPTH):
            rows.append(jnp.stack([results[("ffn", i, 0)][which][key], results[("mix", i)][which][key],
                                   results[("ffn", i, 1)][which][key]]))
        return jnp.stack(rows)
    if name.startswith("ffn_"):
        key = name[4:]
        return jnp.stack([jnp.stack([results[("ffn", i, j)][which][key] for j in range(2)]) for i in range(DEPTH)])
    kind = _MIX_PREFIX.index(name[:3])
    layers = [i for i in range(DEPTH) if i % 4 == kind]
    return jnp.stack([results[("mix", i)][which][name[3:]] for i in layers])


def kernel(x, norm_pre, norm_post, ffn_w1, ffn_w3, ffn_w2, rel_bias, s5_w_in, s5_a_re, s5_a_im, s5_log_dt, s5_b_re, s5_b_im, s5_c_re, s5_c_im, s5_d, s5_w_glu, s5_b_glu, s5_w_out, cv_w_in, cv_b_in, cv_dw, cv_dw_b, cv_ln_g, cv_ln_b, cv_w_out, cv_b_out, gm_w_in, gm_b_in, gm_ln_g, gm_ln_b, gm_w_s, gm_b_s, gm_w_out, gm_b_out, at_w_qkv, at_w_out, loss_target, m_norm_pre, m_norm_post, m_ffn_w1, m_ffn_w3, m_ffn_w2, m_rel_bias, m_s5_w_in, m_s5_a_re, m_s5_a_im, m_s5_log_dt, m_s5_b_re, m_s5_b_im, m_s5_c_re, m_s5_c_im, m_s5_d, m_s5_w_glu, m_s5_b_glu, m_s5_w_out, m_cv_w_in, m_cv_b_in, m_cv_dw, m_cv_dw_b, m_cv_ln_g, m_cv_ln_b, m_cv_w_out, m_cv_b_out, m_gm_w_in, m_gm_b_in, m_gm_ln_g, m_gm_ln_b, m_gm_w_s, m_gm_b_s, m_gm_w_out, m_gm_b_out, m_at_w_qkv, m_at_w_out, v_norm_pre, v_norm_post, v_ffn_w1, v_ffn_w3, v_ffn_w2, v_rel_bias, v_s5_w_in, v_s5_a_re, v_s5_a_im, v_s5_log_dt, v_s5_b_re, v_s5_b_im, v_s5_c_re, v_s5_c_im, v_s5_d, v_s5_w_glu, v_s5_b_glu, v_s5_w_out, v_cv_w_in, v_cv_b_in, v_cv_dw, v_cv_dw_b, v_cv_ln_g, v_cv_ln_b, v_cv_w_out, v_cv_b_out, v_gm_w_in, v_gm_b_in, v_gm_ln_g, v_gm_ln_b, v_gm_w_s, v_gm_b_s, v_gm_w_out, v_gm_b_out, v_at_w_qkv, v_at_w_out):
    args = locals()
    inputs = {n: args[n] for n in _TWIN_WEIGHTS}
    moments_m = {n: args["m_" + n] for n in _TWIN_WEIGHTS}
    moments_v = {n: args["v_" + n] for n in _TWIN_WEIGHTS}
    loss_local, dx, results, rep_out = _step(x[0], loss_target[0], inputs, moments_m, moments_v)
    loss = lax.psum(loss_local, AXES)
    out = [loss, dx[None]]
    for which in range(4):
        out += [_assemble(n, results, rep_out, which) for n in _TWIN_WEIGHTS]
    return tuple(out)
```

```python
import functools
import math

import numpy as np

import jax
import jax.numpy as jnp
from jax import lax
from jax.experimental import pallas as pl
from jax.experimental.pallas import tpu as pltpu

F32 = jnp.float32
BF16 = jnp.bfloat16

D_MODEL = 1024
DEPTH = 4
D_FF = 2816
EPS = 1e-6
S5_GROUP = 16
S5_STATE = 64
CONV_W = 31
GM_CHUNK = 128
GM_HEADS = 8
HEAD_DIM = 64
PATTERNS = ((128, 1), (512, 4), (2048, 16))
BLOCK = 128
NUM_BUCKETS = 32
MAX_DISTANCE = 2048
ADAM_LR = 0.001
ADAM_B1 = 0.9
ADAM_B2 = 0.999
ADAM_EPS = 1e-08
ADAM_WD = 0.01
ADAM_STEP = 10

N_DEV = 8
AXES = ("x", "y", "c")
LANES = 128
GM_E = 2 * D_MODEL
S5_GROUPS = D_MODEL // S5_GROUP
S5_GB = LANES // S5_GROUP
S5_NB = D_MODEL // LANES
S5_BW = S5_GB * S5_STATE
S5_NS = S5_GROUPS * S5_STATE
AT_HEADS = D_MODEL // HEAD_DIM
VMEM_LIMIT = 56 * 1024 * 1024
PACK_ALIGN = 16 * 1024


def _cparams(sem):
    return pltpu.CompilerParams(dimension_semantics=sem, vmem_limit_bytes=VMEM_LIMIT)


def _pick(n, cap):
    if n <= cap:
        return n
    best = None
    for t in range(LANES, cap + 1, LANES):
        if n % t == 0:
            best = t
    assert best is not None, (n, cap)
    return best


def _pick_rows(n, cap):
    best = None
    for t in range(16, min(n, cap) + 1, 16):
        if n % t == 0:
            best = t
    assert best is not None, (n, cap)
    return best


MM_VMEM_BUDGET = 40 * 1024 * 1024


def _mm(a, b, *, ta=False, tb=False, out_dtype=F32, name, carry=None):
    a_list = list(a) if isinstance(a, (tuple, list)) else [a]
    b_list = list(b) if isinstance(b, (tuple, list)) else [b]
    n_op = len(a_list)
    assert n_op == len(b_list)
    K, M = a_list[0].shape if ta else a_list[0].shape[::-1]
    N, K2 = b_list[0].shape if tb else b_list[0].shape[::-1]
    assert K == K2, (a_list[0].shape, b_list[0].shape, ta, tb)
    a_bytes = sum(x.dtype.itemsize for x in a_list)
    b_bytes = sum(x.dtype.itemsize for x in b_list)
    o_bytes = jnp.dtype(out_dtype).itemsize

    def vmem(tm, tn, tk, nk):
        acc = tm * tn * 4 if (nk > 1 and out_dtype != F32) else 0
        return 2 * (tm * tk * a_bytes + tk * tn * b_bytes + tm * tn * o_bytes) + acc

    if ta:
        tm, tn = _pick(M, 1408), _pick(N, 1408)
        tk = next(t for t in (2048, 1024, 512, 256) if K % t == 0 and vmem(tm, tn, t, 2) <= MM_VMEM_BUDGET)
    else:
        tm, tn = _pick(M, 512), _pick(N, 1408)
        tk = next(t for t in (K, _pick(K, 4608), _pick(K, 2816), _pick(K, 1024))
                  if vmem(tm, tn, t, K // t) <= MM_VMEM_BUDGET)
    nk = K // tk
    a_spec = pl.BlockSpec((tk, tm), lambda j, i, k: (k, i)) if ta else pl.BlockSpec((tm, tk), lambda j, i, k: (i, k))
    b_spec = pl.BlockSpec((tn, tk), lambda j, i, k: (j, k)) if tb else pl.BlockSpec((tk, tn), lambda j, i, k: (k, j))
    dims = (((0 if ta else 1,), (1 if tb else 0,)), ((), ()))
    use_scratch = nk > 1 and out_dtype != F32
    grid = (N // tn, M // tm, nk)

    def body(*refs):
        ins, (o_ref,), scratch, begin, end = _carry_hooks(carry, refs, 2 * n_op, 1, 3, grid)
        begin()
        p = None
        for a_ref, b_ref in zip(ins[:n_op], ins[n_op:]):
            d = lax.dot_general(a_ref[...].astype(BF16), b_ref[...].astype(BF16), dims, preferred_element_type=F32)
            p = d if p is None else p + d
        if nk == 1:
            o_ref[...] = p.astype(o_ref.dtype)
        else:
            acc = scratch[0] if use_scratch else o_ref
            k = pl.program_id(2)

            @pl.when(k == 0)
            def _():
                acc[...] = p

            @pl.when(k > 0)
            def _():
                acc[...] += p

            if use_scratch:
                @pl.when(k == nk - 1)
                def _():
                    o_ref[...] = acc[...].astype(o_ref.dtype)
        end()

    extra = carry if carry is not None else _NO_CARRY
    res = pl.pallas_call(
        body, name=name, grid=grid, in_specs=[a_spec] * n_op + [b_spec] * n_op + extra.in_specs,
        out_specs=[pl.BlockSpec((tm, tn), lambda j, i, k: (i, j))] + extra.out_specs,
        out_shape=[jax.ShapeDtypeStruct((M, N), out_dtype)] + extra.out_shape,
        scratch_shapes=([pltpu.VMEM((tm, tn), F32)] if use_scratch else []) + extra.scratch,
        compiler_params=_cparams(("arbitrary",) * 3 if carry is not None else ("parallel", "parallel", "arbitrary")),
    )(*a_list, *b_list, *extra.arrays)
    return res[0] if carry is None else (res[0], res[1:])


ROW_TILE_BYTES = 8 * 1024 * 1024


def _row_tile(arrays):
    row_bytes = sum(w * jnp.dtype(dt).itemsize for w, dt in arrays)
    for tile in (256, 128, 64, 32):
        if tile * row_bytes <= ROW_TILE_BYTES:
            return tile
    return 16


def _rows(fn, rows, pars, outs, *, name):
    T = rows[0].shape[0]
    tile = _row_tile([(r.shape[1], r.dtype) for r in rows] + list(outs))
    nr, npar = len(rows), len(pars)

    def body(*refs):
        r = [refs[i][...] for i in range(nr)]
        p = [refs[nr + i][...] for i in range(npar)]
        res = fn(*r, *p)
        for o_ref, o in zip(refs[nr + npar:], res):
            o_ref[...] = o.astype(o_ref.dtype)

    in_specs = [pl.BlockSpec((tile, r.shape[1]), lambda i: (i, 0)) for r in rows]
    in_specs += [pl.BlockSpec(p.shape, lambda i, nd=p.ndim: (0,) * nd) for p in pars]
    return pl.pallas_call(
        body, name=name, grid=(T // tile,), in_specs=in_specs,
        out_specs=[pl.BlockSpec((tile, w), lambda i: (i, 0)) for w, _ in outs],
        out_shape=[jax.ShapeDtypeStruct((T, w), dt) for w, dt in outs],
        compiler_params=_cparams(("parallel",)),
    )(*rows, *pars)


def _rows_vjp(fn, rows, pars, cts, *, dtypes, adds=None, name):
    adds = adds or {}
    cts = [c if isinstance(c, (tuple, list)) else (c,) for c in cts]
    flat_cts = [a for c in cts for a in c]
    add_keys = sorted(adds)
    add_arrs = [adds[k] for k in add_keys]
    want = [i for i, d in enumerate(dtypes) if d is not None]
    T = rows[0].shape[0]
    tile = _row_tile([(a.shape[1], a.dtype) for a in list(rows) + flat_cts + add_arrs]
                     + [(rows[i].shape[1], dtypes[i]) for i in want])
    nr, npar, nc, na = len(rows), len(pars), len(flat_cts), len(add_arrs)

    def body(*refs):
        r = [refs[i][...].astype(F32) for i in range(nr)]
        p = [refs[nr + i][...] for i in range(npar)]
        cvals = [refs[nr + npar + i][...].astype(F32) for i in range(nc)]
        avals = [refs[nr + npar + nc + i][...].astype(F32) for i in range(na)]
        outs = refs[nr + npar + nc + na:]
        ct, pos = [], 0
        for c in cts:
            s = cvals[pos]
            for extra in cvals[pos + 1:pos + len(c)]:
                s = s + extra
            pos += len(c)
            ct.append(s)
        _, vjp = jax.vjp(lambda *a: tuple(fn(*a)), *r, *p)
        g = vjp(tuple(ct))
        for o_ref, i in zip(outs[:len(want)], want):
            gi = g[i]
            if i in adds:
                gi = gi + avals[add_keys.index(i)]
            o_ref[...] = gi.astype(o_ref.dtype)
        first = pl.program_id(0) == 0
        for o_ref, gp in zip(outs[len(want):], g[nr:]):
            @pl.when(first)
            def _(o_ref=o_ref, gp=gp):
                o_ref[...] = gp

            @pl.when(jnp.logical_not(first))
            def _(o_ref=o_ref, gp=gp):
                o_ref[...] += gp

    row_spec = lambda a: pl.BlockSpec((tile, a.shape[1]), lambda i: (i, 0))
    par_spec = lambda a: pl.BlockSpec(a.shape, lambda i, nd=a.ndim: (0,) * nd)
    res = pl.pallas_call(
        body, name=name, grid=(T // tile,),
        in_specs=[row_spec(a) for a in rows] + [par_spec(a) for a in pars] + [row_spec(a) for a in flat_cts + add_arrs],
        out_specs=[row_spec(rows[i]) for i in want] + [par_spec(a) for a in pars],
        out_shape=[jax.ShapeDtypeStruct(rows[i].shape, dtypes[i]) for i in want]
        + [jax.ShapeDtypeStruct(a.shape, F32) for a in pars],
        compiler_params=_cparams(("arbitrary",)),
    )(*rows, *pars, *flat_cts, *add_arrs)
    return res[:len(want)], res[len(want):]


def _small(fn, args, outs, *, name):
    n = len(args)

    def body(*refs):
        res = fn(*[r[...] for r in refs[:n]])
        for o_ref, o in zip(refs[n:], res):
            o_ref[...] = o

    return pl.pallas_call(body, name=name, out_shape=[jax.ShapeDtypeStruct(s, F32) for s in outs],
                          compiler_params=pltpu.CompilerParams(vmem_limit_bytes=VMEM_LIMIT))(*args)


def _small_vjp(fn, args, cts, *, name):
    n, nc = len(args), len(cts)

    def body(*refs):
        _, vjp = jax.vjp(lambda *a: tuple(fn(*a)), *[r[...] for r in refs[:n]])
        g = vjp(tuple(r[...] for r in refs[n:n + nc]))
        for o_ref, gi in zip(refs[n + nc:], g):
            o_ref[...] = gi

    return pl.pallas_call(body, name=name, out_shape=[jax.ShapeDtypeStruct(a.shape, F32) for a in args],
                          compiler_params=pltpu.CompilerParams(vmem_limit_bytes=VMEM_LIMIT))(*args, *cts)


def _rms(x, g):
    return x * lax.rsqrt(jnp.mean(x * x, axis=-1, keepdims=True) + EPS) * g


def _layernorm(x, g, b):
    mu = jnp.mean(x, axis=-1, keepdims=True)
    var = jnp.mean(jnp.square(x - mu), axis=-1, keepdims=True)
    return (x - mu) * lax.rsqrt(var + EPS) * g + b


def _f_pre(x, g):
    return (_rms(x.astype(F32), g),)


def _f_post_term(scale, has_bias):
    def fn(o, g, *b):
        o = o.astype(F32)
        if has_bias:
            o = o + b[0]
        return (scale * _rms(o, g),)
    return fn


def _f_post(scale, has_bias):
    term = _f_post_term(scale, has_bias)

    def fn(x, o, g, *b):
        return (x + term(o, g, *b)[0],)
    return fn


def _f_s5_gelu(ylin, u, d):
    return (jax.nn.gelu(ylin.astype(F32) + d * u.astype(F32)),)


def _f_s5_glu(y, gl, b):
    return (y.astype(F32) * jax.nn.sigmoid(gl.astype(F32) + b),)


def _f_cv_glu(z0, b):
    z = z0.astype(F32) + b
    return (z[:, :D_MODEL] * jax.nn.sigmoid(z[:, D_MODEL:]),)


def _f_cv_ln(zc, g, b):
    return (jax.nn.silu(_layernorm(zc.astype(F32), g, b)),)


def _f_gm_in(z0, b, g, bl):
    z = jax.nn.gelu(z0.astype(F32) + b)
    return z[:, :GM_E], _layernorm(z[:, GM_E:], g, bl)


def _f_at_combine(o0, o1, o2, l0, l1, l2):
    m = jnp.maximum(jnp.maximum(l0, l1), l2)
    e0, e1, e2 = jnp.exp(l0 - m), jnp.exp(l1 - m), jnp.exp(l2 - m)
    return ((e0 * o0 + e1 * o1 + e2 * o2) / (e0 + e1 + e2),)


def _f_s5_disc(ar, ai, ldt, br, bi):
    dt = jnp.exp(ldt)
    mag = jnp.exp(dt * ar)
    abr = mag * jnp.cos(dt * ai)
    abi = mag * jnp.sin(dt * ai)
    den = ar * ar + ai * ai
    nr = abr - 1.0
    f_re = (nr * ar + abi * ai) / den
    f_im = (abi * ar - nr * ai) / den
    return abr, abi, f_re * br - f_im * bi, f_re * bi + f_im * br


def _loss_call(y, tgt):
    T, D = y.shape
    tile = 256

    def body(y_ref, t_ref, dy_ref, l_ref):
        err = y_ref[...] - t_ref[...]
        dy_ref[...] = err * (1.0 / D)
        part = 0.5 * jnp.sum(jnp.mean(err * err, axis=-1, keepdims=True), axis=0, keepdims=True)
        part = jnp.broadcast_to(part, (1, LANES))
        first = pl.program_id(0) == 0

        @pl.when(first)
        def _():
            l_ref[...] = part

        @pl.when(jnp.logical_not(first))
        def _():
            l_ref[...] += part

    return pl.pallas_call(
        body, name="loss", grid=(T // tile,),
        in_specs=[pl.BlockSpec((tile, D), lambda i: (i, 0))] * 2,
        out_specs=[pl.BlockSpec((tile, D), lambda i: (i, 0)), pl.BlockSpec((1, LANES), lambda i: (0, 0))],
        out_shape=[jax.ShapeDtypeStruct((T, D), F32), jax.ShapeDtypeStruct((1, LANES), F32)],
        compiler_params=_cparams(("arbitrary",)),
    )(y, tgt)


def _bd(xs, ws, *, add=None, out_dtype=F32, name):
    T = xs[0].shape[0]
    nb, kw, nw = ws[0].shape
    tm = 256
    n = len(xs)

    def body(*refs):
        o_ref = refs[-1]
        for j in range(nb):
            acc = None
            for x_ref, w_ref in zip(refs[:n], refs[n:2 * n]):
                p = jnp.dot(x_ref[:, j * kw:(j + 1) * kw].astype(BF16), w_ref[j].astype(BF16),
                            preferred_element_type=F32)
                acc = p if acc is None else acc + p
            if add is not None:
                acc = acc + refs[2 * n][:, j * nw:(j + 1) * nw].astype(F32)
            o_ref[:, j * nw:(j + 1) * nw] = acc.astype(o_ref.dtype)

    in_specs = [pl.BlockSpec((tm, nb * kw), lambda i: (i, 0)) for _ in xs]
    in_specs += [pl.BlockSpec((nb, kw, nw), lambda i: (0, 0, 0)) for _ in ws]
    args = list(xs) + list(ws)
    if add is not None:
        in_specs.append(pl.BlockSpec((tm, nb * nw), lambda i: (i, 0)))
        args.append(add)
    return pl.pallas_call(
        body, name=name, grid=(T // tm,), in_specs=in_specs,
        out_specs=pl.BlockSpec((tm, nb * nw), lambda i: (i, 0)),
        out_shape=jax.ShapeDtypeStruct((T, nb * nw), out_dtype),
        compiler_params=_cparams(("parallel",)),
    )(*args)


def _bd_wgrad(x, dy, kw, nw, *, name):
    T = x.shape[0]
    nb = x.shape[1] // kw
    tk = 512

    def body(x_ref, dy_ref, o_ref):
        first = pl.program_id(0) == 0
        for j in range(nb):
            p = lax.dot_general(x_ref[:, j * kw:(j + 1) * kw].astype(BF16), dy_ref[:, j * nw:(j + 1) * nw].astype(BF16),
                                (((0,), (0,)), ((), ())), preferred_element_type=F32)

            @pl.when(first)
            def _(j=j, p=p):
                o_ref[j] = p

            @pl.when(jnp.logical_not(first))
            def _(j=j, p=p):
                o_ref[j] += p

    return pl.pallas_call(
        body, name=name, grid=(T // tk,),
        in_specs=[pl.BlockSpec((tk, nb * kw), lambda k: (k, 0)), pl.BlockSpec((tk, nb * nw), lambda k: (k, 0))],
        out_specs=pl.BlockSpec((nb, kw, nw), lambda k: (0, 0, 0)),
        out_shape=jax.ShapeDtypeStruct((nb, kw, nw), F32),
        compiler_params=_cparams(("arbitrary",)),
    )(x, dy)


SCAN_COLS = 512
SCAN_ROWS = 256


def _scan_fwd(bur, bui, ar, ai):
    T, NS = bur.shape
    cw, tc = SCAN_COLS, SCAN_ROWS

    def body(bur_ref, bui_ref, ar_ref, ai_ref, sr_ref, si_ref, cr, ci):
        @pl.when(pl.program_id(1) == 0)
        def _():
            cr[...] = jnp.zeros_like(cr)
            ci[...] = jnp.zeros_like(ci)

        a_r, a_i = ar_ref[...], ai_ref[...]

        def step8(t8, carry):
            sr, si = carry
            base = pl.multiple_of(t8 * 8, 8)
            for r in range(8):
                br = bur_ref[pl.ds(base + r, 1), :]
                bi = bui_ref[pl.ds(base + r, 1), :]
                sr, si = a_r * sr - a_i * si + br, a_r * si + a_i * sr + bi
                sr_ref[pl.ds(base + r, 1), :] = sr
                si_ref[pl.ds(base + r, 1), :] = si
            return sr, si

        sr, si = lax.fori_loop(0, tc // 8, step8, (cr[...], ci[...]))
        cr[...] = sr
        ci[...] = si

    blk = pl.BlockSpec((tc, cw), lambda c, t: (t, c))
    vec = pl.BlockSpec((1, cw), lambda c, t: (0, c))
    return pl.pallas_call(
        body, name="s5_scan_fwd", grid=(NS // cw, T // tc), in_specs=[blk, blk, vec, vec], out_specs=[blk, blk],
        out_shape=[jax.ShapeDtypeStruct((T, NS), F32)] * 2,
        scratch_shapes=[pltpu.VMEM((1, cw), F32)] * 2,
        compiler_params=_cparams(("parallel", "arbitrary")),
    )(bur, bui, ar, ai)


def _scan_bwd(gr, gi, sr, si, ar, ai):
    T, NS = gr.shape
    cw, tc = SCAN_COLS, SCAN_ROWS
    nt = T // tc

    def body(gr_ref, gi_ref, sr_ref, si_ref, ar_ref, ai_ref, lr_ref, li_ref, dar_ref, dai_ref, cr, ci):
        @pl.when(pl.program_id(1) == 0)
        def _():
            cr[...] = jnp.zeros_like(cr)
            ci[...] = jnp.zeros_like(ci)
            dar_ref[...] = jnp.zeros_like(dar_ref)
            dai_ref[...] = jnp.zeros_like(dai_ref)

        a_r, a_i = ar_ref[...], ai_ref[...]

        def step8(k, carry):
            lr, li, dar, dai = carry
            base = pl.multiple_of((tc // 8 - 1 - k) * 8, 8)
            for r in range(7, -1, -1):
                s_r = sr_ref[pl.ds(base + r, 1), :]
                s_i = si_ref[pl.ds(base + r, 1), :]
                dar = dar + lr * s_r + li * s_i
                dai = dai + li * s_r - lr * s_i
                g_r = gr_ref[pl.ds(base + r, 1), :]
                g_i = gi_ref[pl.ds(base + r, 1), :]
                lr, li = g_r + a_r * lr + a_i * li, g_i + a_r * li - a_i * lr
                lr_ref[pl.ds(base + r, 1), :] = lr
                li_ref[pl.ds(base + r, 1), :] = li
            return lr, li, dar, dai

        lr, li, dar, dai = lax.fori_loop(0, tc // 8, step8, (cr[...], ci[...], dar_ref[...], dai_ref[...]))
        cr[...] = lr
        ci[...] = li
        dar_ref[...] = dar
        dai_ref[...] = dai

    blk = pl.BlockSpec((tc, cw), lambda c, t: (nt - 1 - t, c))
    vec = pl.BlockSpec((1, cw), lambda c, t: (0, c))
    return pl.pallas_call(
        body, name="s5_scan_bwd", grid=(NS // cw, nt), in_specs=[blk, blk, blk, blk, vec, vec],
        out_specs=[blk, blk, vec, vec],
        out_shape=[jax.ShapeDtypeStruct((T, NS), F32)] * 2 + [jax.ShapeDtypeStruct((1, NS), F32)] * 2,
        scratch_shapes=[pltpu.VMEM((1, cw), F32)] * 2,
        compiler_params=_cparams(("parallel", "arbitrary")),
    )(gr, gi, sr, si, ar, ai)


CONV_ROWS = 256
CONV_HALO = 32
CONV_PAD = CONV_HALO - (CONV_W - 1)


def _conv_fwd(z, dw, dwb):
    T, D = z.shape
    tc, hl = CONV_ROWS, CONV_HALO
    per = tc // hl

    def body(z_ref, zp_ref, dw_ref, b_ref, o_ref, ext):
        i = pl.program_id(0)
        ext[pl.ds(0, hl), :] = jnp.where(i > 0, zp_ref[...], 0.0)
        ext[pl.ds(hl, tc), :] = z_ref[...]
        acc = jnp.zeros((tc, D), F32) + b_ref[...]
        for k in range(CONV_W):
            acc = acc + dw_ref[pl.ds(k, 1), :] * ext[pl.ds(CONV_PAD + k, tc), :]
        o_ref[...] = acc

    return pl.pallas_call(
        body, name="conv_fwd", grid=(T // tc,),
        in_specs=[pl.BlockSpec((tc, D), lambda i: (i, 0)),
                  pl.BlockSpec((hl, D), lambda i: (jnp.maximum(i * per - 1, 0), 0)),
                  pl.BlockSpec((hl, D), lambda i: (0, 0)), pl.BlockSpec((1, D), lambda i: (0, 0))],
        out_specs=pl.BlockSpec((tc, D), lambda i: (i, 0)),
        out_shape=jax.ShapeDtypeStruct((T, D), F32),
        scratch_shapes=[pltpu.VMEM((tc + hl, D), F32)],
        compiler_params=_cparams(("parallel",)),
    )(z, z, dw, dwb)


def _conv_bwd(dout, z, dw):
    T, D = z.shape
    tc, hl = CONV_ROWS, CONV_HALO
    per = tc // hl
    nblk = T // tc

    def body(g_ref, gn_ref, z_ref, zp_ref, dw_ref, dz_ref, ddw_ref, db_ref, gext, zext):
        i = pl.program_id(0)
        g = g_ref[...]
        gext[pl.ds(0, tc), :] = g
        gext[pl.ds(tc, hl), :] = jnp.where(i < nblk - 1, gn_ref[...], 0.0)
        zext[pl.ds(0, hl), :] = jnp.where(i > 0, zp_ref[...], 0.0)
        zext[pl.ds(hl, tc), :] = z_ref[...]
        acc = jnp.zeros((tc, D), F32)
        for k in range(CONV_W):
            acc = acc + dw_ref[pl.ds(k, 1), :] * gext[pl.ds(CONV_W - 1 - k, tc), :]
        dz_ref[...] = acc

        @pl.when(i == 0)
        def _():
            ddw_ref[...] = jnp.zeros_like(ddw_ref)
            db_ref[...] = jnp.zeros_like(db_ref)

        db_ref[...] += jnp.sum(g, axis=0, keepdims=True)
        for k in range(CONV_W):
            ddw_ref[pl.ds(k, 1), :] += jnp.sum(g * zext[pl.ds(CONV_PAD + k, tc), :], axis=0, keepdims=True)

    return pl.pallas_call(
        body, name="conv_bwd", grid=(nblk,),
        in_specs=[pl.BlockSpec((tc, D), lambda i: (i, 0)),
                  pl.BlockSpec((hl, D), lambda i: (jnp.minimum((i + 1) * per, nblk * per - 1), 0)),
                  pl.BlockSpec((tc, D), lambda i: (i, 0)),
                  pl.BlockSpec((hl, D), lambda i: (jnp.maximum(i * per - 1, 0), 0)),
                  pl.BlockSpec((hl, D), lambda i: (0, 0))],
        out_specs=[pl.BlockSpec((tc, D), lambda i: (i, 0)), pl.BlockSpec((hl, D), lambda i: (0, 0)),
                   pl.BlockSpec((1, D), lambda i: (0, 0))],
        out_shape=[jax.ShapeDtypeStruct((T, D), F32), jax.ShapeDtypeStruct((hl, D), F32),
                   jax.ShapeDtypeStruct((1, D), F32)],
        scratch_shapes=[pltpu.VMEM((tc + hl, D), F32)] * 2,
        compiler_params=_cparams(("arbitrary",)),
    )(dout, dout, z, z, dw)


def _gm_causal():
    r = lax.broadcasted_iota(jnp.int32, (GM_CHUNK, GM_CHUNK), 0)
    c = lax.broadcasted_iota(jnp.int32, (GM_CHUNK, GM_CHUNK), 1)
    return r >= c


def _gm_sg_fwd(u, v, ws, bs_col):
    T, E = u.shape
    hw = E // GM_HEADS

    def body(u_ref, v_ref, w_ref, b_ref, o_ref):
        causal = _gm_causal()
        for h in range(GM_HEADS):
            cols = slice(h * hw, (h + 1) * hw)
            w = jnp.where(causal, w_ref[h], 0.0).astype(BF16)
            s = jnp.dot(w, v_ref[:, cols], preferred_element_type=F32) + b_ref[h]
            o_ref[:, cols] = (u_ref[:, cols] * s).astype(o_ref.dtype)

    return pl.pallas_call(
        body, name="gm_sg_fwd", grid=(T // GM_CHUNK,),
        in_specs=[pl.BlockSpec((GM_CHUNK, E), lambda i: (i, 0)), pl.BlockSpec((GM_CHUNK, E), lambda i: (i, 0)),
                  pl.BlockSpec(ws.shape, lambda i: (0, 0, 0)), pl.BlockSpec(bs_col.shape, lambda i: (0, 0, 0))],
        out_specs=pl.BlockSpec((GM_CHUNK, E), lambda i: (i, 0)),
        out_shape=jax.ShapeDtypeStruct((T, E), BF16),
        compiler_params=_cparams(("parallel",)),
    )(u, v, ws, bs_col)


def _gm_sg_bwd(dus, u, v, ws, bs_col):
    T, E = u.shape
    hw = E // GM_HEADS

    def body(g_ref, u_ref, v_ref, w_ref, b_ref, du_ref, dv_ref, dw_ref, db_ref):
        causal = _gm_causal()

        @pl.when(pl.program_id(0) == 0)
        def _():
            dw_ref[...] = jnp.zeros_like(dw_ref)
            db_ref[...] = jnp.zeros_like(db_ref)

        for h in range(GM_HEADS):
            cols = slice(h * hw, (h + 1) * hw)
            w = jnp.where(causal, w_ref[h], 0.0).astype(BF16)
            vh = v_ref[:, cols]
            s = jnp.dot(w, vh, preferred_element_type=F32) + b_ref[h]
            g = g_ref[:, cols]
            du_ref[:, cols] = g * s
            ds = g * u_ref[:, cols]
            dsb = ds.astype(BF16)
            dv_ref[:, cols] = lax.dot_general(w, dsb, (((0,), (0,)), ((), ())), preferred_element_type=F32)
            dwh = lax.dot_general(dsb, vh, (((1,), (1,)), ((), ())), preferred_element_type=F32)
            dw_ref[h] += jnp.where(causal, dwh, 0.0)
            db_ref[h] += jnp.broadcast_to(jnp.sum(ds, axis=1, keepdims=True), (GM_CHUNK, LANES))

    blk = pl.BlockSpec((GM_CHUNK, E), lambda i: (i, 0))
    return pl.pallas_call(
        body, name="gm_sg_bwd", grid=(T // GM_CHUNK,),
        in_specs=[blk, blk, blk, pl.BlockSpec(ws.shape, lambda i: (0, 0, 0)),
                  pl.BlockSpec(bs_col.shape, lambda i: (0, 0, 0))],
        out_specs=[blk, blk, pl.BlockSpec(ws.shape, lambda i: (0, 0, 0)),
                   pl.BlockSpec((GM_HEADS, GM_CHUNK, LANES), lambda i: (0, 0, 0))],
        out_shape=[jax.ShapeDtypeStruct((T, E), F32), jax.ShapeDtypeStruct((T, E), F32),
                   jax.ShapeDtypeStruct(ws.shape, F32), jax.ShapeDtypeStruct((GM_HEADS, GM_CHUNK, LANES), F32)],
        compiler_params=_cparams(("arbitrary",)),
    )(dus, u, v, ws, bs_col)


def _t5_bucket_steps(dilation):
    max_exact = NUM_BUCKETS // 2
    delta = np.arange(BLOCK + 1)
    dist = delta * dilation
    distf = np.maximum(dist, 1).astype(np.float32)
    large = max_exact + (np.log(distf / np.float32(max_exact)) / np.float32(math.log(MAX_DISTANCE / max_exact))
                         * np.float32(NUM_BUCKETS - max_exact)).astype(np.int32)
    large = np.minimum(large, NUM_BUCKETS - 1)
    bucket = np.where(dist < max_exact, dist, large)
    steps = []
    for d in range(1, BLOCK + 1):
        inc = int(bucket[d] - bucket[d - 1])
        assert inc >= 0
        if inc:
            steps.append((d, inc))
    assert int(bucket[0]) == 0
    return steps


def _bucket_map(dilation):
    qi = lax.broadcasted_iota(jnp.int32, (BLOCK, 2 * BLOCK), 0)
    ki = lax.broadcasted_iota(jnp.int32, (BLOCK, 2 * BLOCK), 1)
    delta = qi + BLOCK - ki
    bm = jnp.zeros((BLOCK, 2 * BLOCK), jnp.int32)
    for thr, inc in _t5_bucket_steps(dilation):
        bm = bm + jnp.where(delta >= thr, inc, 0)
    return bm


def _at_bias(table, g, dilation):
    H = AT_HEADS

    def body(t_ref, o_ref):
        bm = _bucket_map(dilation)
        for h in range(H):
            acc = jnp.zeros((BLOCK, 2 * BLOCK), F32)
            for b in range(NUM_BUCKETS):
                acc = jnp.where(bm == b, t_ref[b, g * H + h], acc)
            o_ref[h] = acc

    return pl.pallas_call(body, name="at_bias", in_specs=[pl.BlockSpec(memory_space=pltpu.SMEM)],
                          out_shape=jax.ShapeDtypeStruct((H, BLOCK, 2 * BLOCK), F32))(table)


def _at_bias_bwd(dbias, dilation):
    H = AT_HEADS

    def body(d_ref, o_ref):
        bm = _bucket_map(dilation)
        for h in range(H):
            d = d_ref[h]
            for b in range(NUM_BUCKETS):
                o_ref[b, h] = jnp.sum(jnp.where(bm == b, d, 0.0))

    return pl.pallas_call(body, name="at_bias_bwd", out_specs=pl.BlockSpec(memory_space=pltpu.SMEM),
                          out_shape=jax.ShapeDtypeStruct((NUM_BUCKETS, H), F32))(dbias)


def _at_mask(i, nbs):
    qi = lax.broadcasted_iota(jnp.int32, (BLOCK, 2 * BLOCK), 0)
    ki = lax.broadcasted_iota(jnp.int32, (BLOCK, 2 * BLOCK), 1)
    no_prev = jnp.where(i % nbs == 0, 4 * BLOCK, 0)
    return ((ki < BLOCK) & (ki >= qi + no_prev)) | ((ki >= BLOCK) & (ki - BLOCK <= qi))


def _head_lanes():
    lane = lax.broadcasted_iota(jnp.int32, (BLOCK, LANES), 1)
    return [lane < HEAD_DIM, lane >= HEAD_DIM]


def _at_fwd(qkv, bias, nbs, cb):
    T = qkv.shape[0]
    D = D_MODEL
    npair = D // LANES
    scale = HEAD_DIM ** -0.5

    def body(q_ref, kc_ref, kp_ref, vc_ref, vp_ref, b_ref, o_ref, l_ref):
        i = pl.program_id(0)
        mask = _at_mask(i, nbs)
        sel = _head_lanes()
        for j in range(npair):
            cols = slice(j * LANES, (j + 1) * LANES)
            q = q_ref[:, cols]
            kk = jnp.concatenate([kp_ref[:, cols], kc_ref[:, cols]], axis=0)
            vv = jnp.concatenate([vp_ref[:, cols], vc_ref[:, cols]], axis=0)
            o_pair = jnp.zeros((BLOCK, LANES), F32)
            l_pair = jnp.zeros((BLOCK, LANES), F32)
            for e in range(2):
                qh = jnp.where(sel[e], q, jnp.zeros_like(q))
                s = lax.dot_general(qh, kk, (((1,), (1,)), ((), ())), preferred_element_type=F32) * scale
                s = jnp.where(mask, s + b_ref[2 * j + e], -1e30)
                m = jnp.max(s, axis=1, keepdims=True)
                p = jnp.exp(s - m)
                den = jnp.sum(p, axis=1, keepdims=True)
                o = jnp.dot(p.astype(BF16), vv, preferred_element_type=F32) / den
                o_pair = jnp.where(sel[e], o, o_pair)
                l_pair = jnp.where(sel[e], m + jnp.log(den), l_pair)
            o_ref[:, cols] = o_pair
            l_ref[:, cols] = l_pair

    blk = lambda c, prev: pl.BlockSpec((BLOCK, D), (lambda i: (jnp.maximum(i - 1, 0), cb + c)) if prev
                                       else (lambda i: (i, cb + c)))
    out = pl.BlockSpec((BLOCK, D), lambda i: (i, 0))
    return pl.pallas_call(
        body, name="at_fwd", grid=(T // BLOCK,),
        in_specs=[blk(0, False), blk(1, False), blk(1, True), blk(2, False), blk(2, True),
                  pl.BlockSpec(bias.shape, lambda i: (0, 0, 0))],
        out_specs=[out, out], out_shape=[jax.ShapeDtypeStruct((T, D), F32)] * 2,
        compiler_params=_cparams(("parallel",)),
    )(qkv, qkv, qkv, qkv, qkv, bias)


def _at_bwd(qkv, bias, o, lse, do, dlse, nbs, cb):
    T = qkv.shape[0]
    D = D_MODEL
    nblk = T // BLOCK
    npair = D // LANES
    scale = HEAD_DIM ** -0.5

    def body(q_ref, kc_ref, kp_ref, vc_ref, vp_ref, b_ref, o_ref, l_ref, do_ref, dl_ref, dqkv_ref, db_ref, carry):
        i = pl.program_id(0)

        @pl.when(i == 0)
        def _():
            carry[...] = jnp.zeros_like(carry)
            db_ref[...] = jnp.zeros_like(db_ref)

        @pl.when(i == nblk)
        def _():
            dqkv_ref[...] = carry[...].astype(dqkv_ref.dtype)

        @pl.when(i < nblk)
        def _():
            mask = _at_mask(i, nbs)
            sel = _head_lanes()
            for j in range(npair):
                cols = slice(j * LANES, (j + 1) * LANES)
                kcols = slice(D + j * LANES, D + (j + 1) * LANES)
                vcols = slice(2 * D + j * LANES, 2 * D + (j + 1) * LANES)
                q = q_ref[:, cols]
                kk = jnp.concatenate([kp_ref[:, cols], kc_ref[:, cols]], axis=0)
                vv = jnp.concatenate([vp_ref[:, cols], vc_ref[:, cols]], axis=0)
                dov = do_ref[:, cols]
                dob = dov.astype(BF16)
                oo = dov * o_ref[:, cols]
                lv = l_ref[:, cols]
                dlv = dl_ref[:, cols]
                dq_pair = jnp.zeros((BLOCK, LANES), F32)
                dk_pair = jnp.zeros((2 * BLOCK, LANES), F32)
                dv_pair = jnp.zeros((2 * BLOCK, LANES), F32)
                sel2 = [jnp.concatenate([s_, s_], axis=0) for s_ in sel]
                for e in range(2):
                    qh = jnp.where(sel[e], q, jnp.zeros_like(q))
                    s = lax.dot_general(qh, kk, (((1,), (1,)), ((), ())), preferred_element_type=F32) * scale
                    s = jnp.where(mask, s + b_ref[2 * j + e], -1e30)
                    lse_h = jnp.max(jnp.where(sel[e], lv, -jnp.inf), axis=1, keepdims=True)
                    p = jnp.exp(s - lse_h)
                    doh = jnp.where(sel[e], dob, jnp.zeros_like(dob))
                    dp = lax.dot_general(doh, vv, (((1,), (1,)), ((), ())), preferred_element_type=F32)
                    delta = jnp.sum(jnp.where(sel[e], oo, 0.0), axis=1, keepdims=True)
                    dlse_h = jnp.sum(jnp.where(sel[e], dlv, 0.0), axis=1, keepdims=True)
                    ds = p * (dp - delta + dlse_h)
                    db_ref[2 * j + e] += ds
                    dsb = (ds * scale).astype(BF16)
                    dq_pair = jnp.where(sel[e], jnp.dot(dsb, kk, preferred_element_type=F32), dq_pair)
                    dk = lax.dot_general(dsb, q, (((0,), (0,)), ((), ())), preferred_element_type=F32)
                    dk_pair = jnp.where(sel2[e], dk, dk_pair)
                    dv = lax.dot_general(p.astype(BF16), dob, (((0,), (0,)), ((), ())), preferred_element_type=F32)
                    dv_pair = jnp.where(sel2[e], dv, dv_pair)
                dqkv_ref[:, cols] = carry[:, cols].astype(dqkv_ref.dtype)
                dqkv_ref[:, kcols] = (carry[:, kcols] + dk_pair[:BLOCK]).astype(dqkv_ref.dtype)
                dqkv_ref[:, vcols] = (carry[:, vcols] + dv_pair[:BLOCK]).astype(dqkv_ref.dtype)
                carry[:, cols] = dq_pair
                carry[:, kcols] = dk_pair[BLOCK:]
                carry[:, vcols] = dv_pair[BLOCK:]

    cur = lambda i: jnp.minimum(i, nblk - 1)
    prev = lambda i: jnp.maximum(jnp.minimum(i, nblk - 1) - 1, 0)
    blk = lambda c, pv: pl.BlockSpec((BLOCK, D), (lambda i: (prev(i), cb + c)) if pv else (lambda i: (cur(i), cb + c)))
    row = pl.BlockSpec((BLOCK, D), lambda i: (cur(i), 0))
    return pl.pallas_call(
        body, name="at_bwd", grid=(nblk + 1,),
        in_specs=[blk(0, False), blk(1, False), blk(1, True), blk(2, False), blk(2, True),
                  pl.BlockSpec(bias.shape, lambda i: (0, 0, 0)), row, row, row, row],
        out_specs=[pl.BlockSpec((BLOCK, 3 * D), lambda i: (jnp.maximum(i - 1, 0), 0)),
                   pl.BlockSpec(bias.shape, lambda i: (0, 0, 0))],
        out_shape=[jax.ShapeDtypeStruct((T, 3 * D), BF16), jax.ShapeDtypeStruct(bias.shape, F32)],
        scratch_shapes=[pltpu.VMEM((BLOCK, 3 * D), F32)],
        compiler_params=_cparams(("arbitrary",)),
    )(qkv, qkv, qkv, qkv, qkv, bias, o, lse, do, dlse)


def _to_residue_major(a, d):
    if d == 1:
        return a
    T, C = a.shape
    return a.reshape(T // d, d, C).transpose(1, 0, 2).reshape(T, C)


def _from_residue_major(a, d):
    if d == 1:
        return a
    T, C = a.shape
    return a.reshape(d, T // d, C).transpose(1, 0, 2).reshape(T, C)


FFN_TM = 512


def _ffn_up(h, w1, w3, carry=None):
    T, Dm = h.shape
    Fw = w1.shape[1]
    tm, tn = FFN_TM, _pick(Fw, 1408)
    grid = (Fw // tn, T // tm)

    def body(*refs):
        (h_ref, w1_ref, w3_ref), (a_ref, b_ref, u_ref), _, begin, end = _carry_hooks(carry, refs, 3, 3, 2, grid)
        begin()
        hv = h_ref[...]
        a = jnp.dot(hv, w1_ref[...], preferred_element_type=F32)
        b = jnp.dot(hv, w3_ref[...], preferred_element_type=F32)
        a_ref[...] = a.astype(a_ref.dtype)
        b_ref[...] = b.astype(b_ref.dtype)
        u_ref[...] = (jax.nn.silu(a) * b).astype(u_ref.dtype)
        end()

    extra = carry if carry is not None else _NO_CARRY
    wspec = pl.BlockSpec((Dm, tn), lambda j, i: (0, j))
    ospec = pl.BlockSpec((tm, tn), lambda j, i: (i, j))
    res = pl.pallas_call(
        body, name="ffn_up", grid=grid,
        in_specs=[pl.BlockSpec((tm, Dm), lambda j, i: (i, 0)), wspec, wspec] + extra.in_specs,
        out_specs=[ospec] * 3 + extra.out_specs,
        out_shape=[jax.ShapeDtypeStruct((T, Fw), BF16)] * 3 + extra.out_shape, scratch_shapes=extra.scratch,
        compiler_params=_cparams(("arbitrary",) * 2 if carry is not None else ("parallel", "parallel")),
    )(h, w1, w3, *extra.arrays)
    return res[:3], res[3:]


def _ffn_down_dx(do, w2, a, b, carry=None):
    T, Dm = do.shape
    Fw = w2.shape[0]
    tm, tn = FFN_TM, _pick(Fw, 1408)
    grid = (Fw // tn, T // tm)

    def body(*refs):
        (do_ref, w2_ref, a_ref, b_ref), (da_ref, db_ref), _, begin, end = _carry_hooks(carry, refs, 4, 2, 2, grid)
        begin()
        du = lax.dot_general(do_ref[...], w2_ref[...], (((1,), (1,)), ((), ())), preferred_element_type=F32)
        av = a_ref[...].astype(F32)
        bv = b_ref[...].astype(F32)
        sg = jax.nn.sigmoid(av)
        silu = av * sg
        da_ref[...] = (du * bv * (sg + silu * (1.0 - sg))).astype(da_ref.dtype)
        db_ref[...] = (du * silu).astype(db_ref.dtype)
        end()

    extra = carry if carry is not None else _NO_CARRY
    ospec = pl.BlockSpec((tm, tn), lambda j, i: (i, j))
    res = pl.pallas_call(
        body, name="ffn_down_dx", grid=grid,
        in_specs=[pl.BlockSpec((tm, Dm), lambda j, i: (i, 0)), pl.BlockSpec((tn, Dm), lambda j, i: (j, 0)), ospec, ospec]
        + extra.in_specs,
        out_specs=[ospec] * 2 + extra.out_specs,
        out_shape=[jax.ShapeDtypeStruct((T, Fw), BF16)] * 2 + extra.out_shape, scratch_shapes=extra.scratch,
        compiler_params=_cparams(("arbitrary",) * 2 if carry is not None else ("parallel", "parallel")),
    )(do, w2, a, b, *extra.arrays)
    return res[:2], res[2:]


def _ffn_fwd(x, p, carry_up=None, carry_down=None):
    h, = _rows(_f_pre, [x], [p["g_pre"]], [(D_MODEL, BF16)], name="ffn_pre")
    (a, b, u), got_up = _ffn_up(h, p["w1"], p["w3"], carry_up)
    o = _mm(u, p["w2"], name="ffn_down", carry=carry_down)
    got_down = None
    if carry_down is not None:
        o, got_down = o
    xo, = _rows(_f_post(0.5, False), [x, o], [p["g_post"]], [(D_MODEL, F32)], name="ffn_post")
    return xo, (x, h, a, b, u, o), got_up, got_down


def _ffn_bwd(saved, p, dxo, carry_a=None, carry_b=None):
    x, h, a, b, u, o = saved
    (do,), (dg_post,) = _rows_vjp(_f_post_term(0.5, False), [o], [p["g_post"]], [dxo], dtypes=[BF16], name="ffn_post_bwd")
    (da, db), got_a = _ffn_down_dx(do, p["w2"], a, b, carry_a)
    dw2 = _mm(u, do, ta=True, name="ffn_down_dw")
    dh = _mm((da, db), (p["w1"], p["w3"]), tb=True, name="ffn_up_dx", carry=carry_b)
    got_b = None
    if carry_b is not None:
        dh, got_b = dh
    dw1 = _mm(h, da, ta=True, name="ffn_up_dw")
    dw3 = _mm(h, db, ta=True, name="ffn_up_dw")
    (dx,), (dg_pre,) = _rows_vjp(_f_pre, [x], [p["g_pre"]], [dh], dtypes=[F32], adds={0: dxo}, name="ffn_pre_bwd")
    return dx, {"w1": dw1, "w3": dw3, "w2": dw2, "g_pre": dg_pre, "g_post": dg_post}, got_a, got_b


def _expand_blocks(w, rows_first):
    w = w.reshape(S5_NB, S5_GB, S5_GROUP, S5_STATE)
    eye = jnp.eye(S5_GB, dtype=F32)
    if rows_first:
        e = w[:, :, :, None, :] * eye[None, :, None, :, None]
        return e.reshape(S5_NB, S5_GB * S5_GROUP, S5_BW)
    e = jnp.transpose(w, (0, 1, 3, 2))[:, :, :, None, :] * eye[None, :, None, :, None]
    return e.reshape(S5_NB, S5_BW, S5_GB * S5_GROUP)


def _extract_blocks(e, rows_first):
    eye = jnp.eye(S5_GB, dtype=F32)
    if rows_first:
        e = e.reshape(S5_NB, S5_GB, S5_GROUP, S5_GB, S5_STATE)
        w = jnp.sum(e * eye[None, :, None, :, None], axis=3)
    else:
        e = e.reshape(S5_NB, S5_GB, S5_STATE, S5_GB, S5_GROUP)
        w = jnp.transpose(jnp.sum(e * eye[None, :, None, :, None], axis=3), (0, 1, 3, 2))
    return w.reshape(S5_GROUPS, S5_GROUP, S5_STATE)


def _s5_prep(p):
    G, P, HG = S5_GROUPS, S5_STATE, S5_GROUP
    args = [p["a_re"].reshape(G, 1, P), p["a_im"].reshape(G, 1, P), p["log_dt"].reshape(G, 1, 1),
            jnp.transpose(p["b_re"], (0, 2, 1)), jnp.transpose(p["b_im"], (0, 2, 1))]
    abr, abi, bbr, bbi = _small(_f_s5_disc, args, [(G, 1, P)] * 2 + [(G, HG, P)] * 2, name="s5_disc")
    return args, abr.reshape(1, G * P), abi.reshape(1, G * P), bbr, bbi


def _s5_fwd(h, p):
    disc_args, abr, abi, bbr, bbi = _s5_prep(p)
    c_re, c_im = p["c_re"], p["c_im"]
    u = _mm(h, p["w_in"], name="s5_in")
    bur = _bd([u], [_expand_blocks(bbr, True)], name="s5_bu")
    bui = _bd([u], [_expand_blocks(bbi, True)], name="s5_bu")
    sr, si = _scan_fwd(bur, bui, abr, abi)
    ylin = _bd([sr, si], [_expand_blocks(c_re, False), _expand_blocks(-c_im, False)], name="s5_y")
    y, = _rows(_f_s5_gelu, [ylin, u], [p["d"]], [(D_MODEL, F32)], name="s5_gelu")
    gl = _mm(y, p["w_glu"], name="s5_glu_mm")
    z, = _rows(_f_s5_glu, [y, gl], [p["b_glu"]], [(D_MODEL, BF16)], name="s5_glu")
    m = _mm(z, p["w_out"], name="s5_out")
    return m, None, (h, disc_args, abr, abi, bbr, bbi, u, sr, si, ylin, y, gl, z)


def _s5_bwd(saved, p, dm):
    h, disc_args, abr, abi, bbr, bbi, u, sr, si, ylin, y, gl, z = saved
    c_re, c_im = p["c_re"], p["c_im"]
    dz = _mm(dm, p["w_out"], tb=True, name="s5_out_dx")
    dw_out = _mm(z, dm, ta=True, name="s5_out_dw")
    (dy1, dgl), (db_glu,) = _rows_vjp(_f_s5_glu, [y, gl], [p["b_glu"]], [dz], dtypes=[F32, BF16], name="s5_glu_bwd")
    dy2 = _mm(dgl, p["w_glu"], tb=True, name="s5_glu_dx")
    dw_glu = _mm(y, dgl, ta=True, name="s5_glu_dw")
    (dylin, du1), (dd,) = _rows_vjp(_f_s5_gelu, [ylin, u], [p["d"]], [(dy1, dy2)], dtypes=[F32, F32], name="s5_gelu_bwd")
    gr = _bd([dylin], [jnp.transpose(_expand_blocks(c_re, False), (0, 2, 1))], name="s5_y_dx")
    gi = _bd([dylin], [jnp.transpose(_expand_blocks(-c_im, False), (0, 2, 1))], name="s5_y_dx")
    dc_re = _extract_blocks(_bd_wgrad(sr, dylin, S5_BW, LANES, name="s5_y_dw"), False)
    dc_im = -_extract_blocks(_bd_wgrad(si, dylin, S5_BW, LANES, name="s5_y_dw"), False)
    lr, li, dabr, dabi = _scan_bwd(gr, gi, sr, si, abr, abi)
    du = _bd([lr, li], [jnp.transpose(_expand_blocks(bbr, True), (0, 2, 1)),
                        jnp.transpose(_expand_blocks(bbi, True), (0, 2, 1))], add=du1, out_dtype=BF16, name="s5_bu_dx")
    dbbr = _extract_blocks(_bd_wgrad(u, lr, LANES, S5_BW, name="s5_bu_dw"), True)
    dbbi = _extract_blocks(_bd_wgrad(u, li, LANES, S5_BW, name="s5_bu_dw"), True)
    G, P = S5_GROUPS, S5_STATE
    dar, dai, dldt, dbr, dbi = _small_vjp(_f_s5_disc, disc_args,
                                          [dabr.reshape(G, 1, P), dabi.reshape(G, 1, P), dbbr, dbbi], name="s5_disc_bwd")
    dh = _mm(du, p["w_in"], tb=True, name="s5_in_dx")
    dw_in = _mm(h, du, ta=True, name="s5_in_dw")
    grads = {"w_in": dw_in, "w_glu": dw_glu, "w_out": dw_out, "b_glu": db_glu, "d": dd,
             "a_re": dar.reshape(G, P), "a_im": dai.reshape(G, P), "log_dt": dldt.reshape(G),
             "b_re": jnp.transpose(dbr, (0, 2, 1)), "b_im": jnp.transpose(dbi, (0, 2, 1)),
             "c_re": dc_re, "c_im": dc_im}
    return dh, grads


def _cv_fwd(h, p):
    z0 = _mm(h, p["w_in"], name="cv_in")
    zg, = _rows(_f_cv_glu, [z0], [p["b_in"]], [(D_MODEL, F32)], name="cv_glu")
    zc = _conv_fwd(zg, p["dw"], p["dw_b"])
    zl, = _rows(_f_cv_ln, [zc], [p["ln_g"], p["ln_b"]], [(D_MODEL, BF16)], name="cv_ln")
    m = _mm(zl, p["w_out"], name="cv_out")
    return m, p["b_out"], (h, z0, zg, zc, zl)


def _cv_bwd(saved, p, dm):
    h, z0, zg, zc, zl = saved
    dzl = _mm(dm, p["w_out"], tb=True, name="cv_out_dx")
    dw_out = _mm(zl, dm, ta=True, name="cv_out_dw")
    (dzc,), (dln_g, dln_b) = _rows_vjp(_f_cv_ln, [zc], [p["ln_g"], p["ln_b"]], [dzl], dtypes=[F32], name="cv_ln_bwd")
    dzg, ddw, ddw_b = _conv_bwd(dzc, zg, p["dw"])
    (dz0,), (db_in,) = _rows_vjp(_f_cv_glu, [z0], [p["b_in"]], [dzg], dtypes=[BF16], name="cv_glu_bwd")
    dh = _mm(dz0, p["w_in"], tb=True, name="cv_in_dx")
    dw_in = _mm(h, dz0, ta=True, name="cv_in_dw")
    return dh, {"w_in": dw_in, "b_in": db_in, "dw": ddw, "dw_b": ddw_b, "ln_g": dln_g, "ln_b": dln_b, "w_out": dw_out}


def _gm_fwd(h, p):
    z0 = _mm(h, p["w_in"], name="gm_in")
    u, v = _rows(_f_gm_in, [z0], [p["b_in"], p["ln_g"], p["ln_b"]], [(GM_E, F32), (GM_E, BF16)], name="gm_act")
    bs_col = p["b_s"].reshape(GM_HEADS, GM_CHUNK, 1)
    us = _gm_sg_fwd(u, v, p["w_s"], bs_col)
    m = _mm(us, p["w_out"], name="gm_out")
    return m, p["b_out"], (h, z0, u, v, us, bs_col)


def _gm_bwd(saved, p, dm):
    h, z0, u, v, us, bs_col = saved
    dus = _mm(dm, p["w_out"], tb=True, name="gm_out_dx")
    dw_out = _mm(us, dm, ta=True, name="gm_out_dw")
    du, dv, dw_s, db_s = _gm_sg_bwd(dus, u, v, p["w_s"], bs_col)
    (dz0,), (db_in, dln_g, dln_b) = _rows_vjp(_f_gm_in, [z0], [p["b_in"], p["ln_g"], p["ln_b"]], [du, dv],
                                              dtypes=[BF16], name="gm_act_bwd")
    dh = _mm(dz0, p["w_in"], tb=True, name="gm_in_dx")
    dw_in = _mm(h, dz0, ta=True, name="gm_in_dw")
    return dh, {"w_in": dw_in, "b_in": db_in, "ln_g": dln_g, "ln_b": dln_b, "w_s": dw_s, "b_s": db_s[:, :, 0],
                "w_out": dw_out}


def _at_fwd_mixer(h, p):
    T = h.shape[0]
    D = D_MODEL
    qkv = _mm(h, p["w_qkv"], out_dtype=BF16, name="at_qkv")
    res, outs, lses, biases = [], [], [], []
    for g, (window, d) in enumerate(PATTERNS):
        assert window // d == BLOCK and T % (BLOCK * d) == 0
        bias = _at_bias(p["rel_bias"], g, d)
        if d == 1:
            r, cb = qkv, 3 * g
        else:
            r, cb = _to_residue_major(qkv[:, g * 3 * D:(g + 1) * 3 * D], d), 0
        o, lse = _at_fwd(r, bias, T // d // BLOCK, cb)
        res.append((r, cb, o, lse))
        biases.append(bias)
        outs.append(_from_residue_major(o, d))
        lses.append(_from_residue_major(lse, d))
    oc, = _rows(_f_at_combine, outs + lses, [], [(D, BF16)], name="at_combine")
    m = _mm(oc, p["w_out"], name="at_out")
    return m, None, (h, res, biases, outs, lses, oc)


def _at_bwd_mixer(saved, p, dm):
    h, res, biases, outs, lses, oc = saved
    T = h.shape[0]
    doc = _mm(dm, p["w_out"], tb=True, name="at_out_dx")
    dw_out = _mm(oc, dm, ta=True, name="at_out_dw")
    dol, _ = _rows_vjp(_f_at_combine, outs + lses, [], [doc], dtypes=[F32] * 6, name="at_combine_bwd")
    dqkv, dtab = [], []
    for g, (window, d) in enumerate(PATTERNS):
        r, cb, o_res, lse_res = res[g]
        dq, dbias = _at_bwd(r, biases[g], o_res, lse_res, _to_residue_major(dol[g], d),
                            _to_residue_major(dol[3 + g], d), T // d // BLOCK, cb)
        dqkv.append(_from_residue_major(dq, d))
        dtab.append(_at_bias_bwd(dbias, d))
    dqkv = jnp.concatenate(dqkv, axis=1)
    dh = _mm(dqkv, p["w_qkv"], tb=True, name="at_qkv_dx")
    dw_qkv = _mm(h, dqkv, ta=True, name="at_qkv_dw")
    return dh, {"w_qkv": dw_qkv, "w_out": dw_out, "rel_bias": jnp.concatenate(dtab, axis=1)}


_MIXERS = ((_s5_fwd, _s5_bwd), (_cv_fwd, _cv_bwd), (_gm_fwd, _gm_bwd), (_at_fwd_mixer, _at_bwd_mixer))


def _mixer_fwd(x, p, kind):
    h, = _rows(_f_pre, [x], [p["g_pre"]], [(D_MODEL, BF16)], name="mix_pre")
    m, bias, saved = _MIXERS[kind][0](h, p)
    extra = [] if bias is None else [bias]
    xo, = _rows(_f_post(1.0, bias is not None), [x, m], [p["g_post"]] + extra, [(D_MODEL, F32)], name="mix_post")
    return xo, (x, m, bias, saved)


def _mixer_bwd(saved_all, p, kind, dxo):
    x, m, bias, saved = saved_all
    extra = [] if bias is None else [bias]
    (dm,), dpars = _rows_vjp(_f_post_term(1.0, bias is not None), [m], [p["g_post"]] + extra, [dxo], dtypes=[BF16],
                             name="mix_post_bwd")
    dh, grads = _MIXERS[kind][1](saved, p, dm)
    (dx,), (dg_pre,) = _rows_vjp(_f_pre, [x], [p["g_pre"]], [dh], dtypes=[F32], adds={0: dxo}, name="mix_pre_bwd")
    grads["g_pre"] = dg_pre
    grads["g_post"] = dpars[0]
    if bias is not None:
        grads["b_out"] = dpars[1]
    return dx, grads


class _Carry:
    def __init__(self, arrays, kinds):
        self.arrays, self.kinds, self.n = list(arrays), list(kinds), len(arrays)
        hbm = pl.BlockSpec(memory_space=pl.ANY)
        self.in_specs = [hbm] * self.n
        self.out_specs = [hbm] * self.n
        self.out_shape = [jax.ShapeDtypeStruct((N_DEV,) + (a.shape[1:] if k == "a2a" else a.shape), a.dtype)
                          for a, k in zip(arrays, kinds)]
        self.scratch = [pltpu.SemaphoreType.DMA((self.n * (N_DEV - 1),)), pltpu.SemaphoreType.DMA((self.n * (N_DEV - 1),)),
                        pltpu.SemaphoreType.DMA((self.n,))]

    def _copies(self, ins, outs, sems, arrivals):
        send_sems, recv_sems, local_sems = sems
        x, y, c = lax.axis_index("x"), lax.axis_index("y"), lax.axis_index("c")
        me = 4 * x + 2 * y + c
        local, remote = [], []
        for a in range(self.n):
            a2a = self.kinds[a] == "a2a"
            if not arrivals:
                local.append(pltpu.make_async_copy(ins[a].at[me] if a2a else ins[a], outs[a].at[me], local_sems.at[a]))
            for k in range(1, N_DEV):
                px = 1 - x if k & 4 else x
                py = 1 - y if k & 2 else y
                pc = 1 - c if k & 1 else c
                peer = 4 * px + 2 * py + pc
                idx = a * (N_DEV - 1) + k - 1
                remote.append(pltpu.make_async_remote_copy(
                    src_ref=ins[a].at[peer] if a2a else ins[a], dst_ref=outs[a].at[peer if arrivals else me],
                    send_sem=send_sems.at[idx], recv_sem=recv_sems.at[idx], device_id=(px, py, pc),
                    device_id_type=pl.DeviceIdType.MESH))
        return local, remote

    def start(self, ins, outs, sems):
        local, sends = self._copies(ins, outs, sems, False)
        for cp in local + sends:
            cp.start()

    def wait(self, ins, outs, sems):
        local, sends = self._copies(ins, outs, sems, False)
        _, recvs = self._copies(ins, outs, sems, True)
        for cp in sends:
            cp.wait_send()
        for cp in recvs:
            cp.wait_recv()
        for cp in local:
            cp.wait()


class _NoCarry:
    n = 0
    arrays = in_specs = out_specs = out_shape = scratch = []


_NO_CARRY = _NoCarry()


def _carry_hooks(carry, refs, n_in, n_out, grid_rank, grid):
    nc = carry.n if carry is not None else 0
    ins, cin = refs[:n_in], refs[n_in:n_in + nc]
    outs, cout = refs[n_in + nc:n_in + nc + n_out], refs[n_in + nc + n_out:n_in + 2 * nc + n_out]
    rest = refs[n_in + 2 * nc + n_out:]
    scratch, sems = (rest[:len(rest) - 3], rest[len(rest) - 3:]) if nc else (rest, ())

    def at(step_of):
        cond = None
        for ax in range(grid_rank):
            c = pl.program_id(ax) == step_of(ax)
            cond = c if cond is None else cond & c
        return cond

    def begin():
        if nc:
            @pl.when(at(lambda ax: 0))
            def _():
                carry.start(cin, cout, sems)

    def end():
        if nc:
            @pl.when(at(lambda ax: grid[ax] - 1))
            def _():
                carry.wait(cin, cout, sems)

    return ins, outs, scratch, begin, end


def _exchange(arrays, kinds, *, name):
    carry = _Carry(arrays, kinds)

    def body(*refs):
        n = carry.n
        carry.start(refs[:n], refs[n:2 * n], refs[2 * n:])
        carry.wait(refs[:n], refs[n:2 * n], refs[2 * n:])

    return pl.pallas_call(body, name=name, in_specs=carry.in_specs, out_specs=carry.out_specs,
                          out_shape=carry.out_shape, scratch_shapes=carry.scratch)(*arrays)


def _adam(recv, w, m, v, *, name):
    R, C = w.shape
    tr = _pick_rows(R, 128)
    c1 = 1.0 - ADAM_B1 ** ADAM_STEP
    c2 = 1.0 - ADAM_B2 ** ADAM_STEP

    def body(r_ref, w_ref, m_ref, v_ref, g_ref, d_ref, nm_ref, nv_ref):
        g = r_ref[0].astype(F32)
        for q in range(1, N_DEV):
            g = g + r_ref[q].astype(F32)
        mm = ADAM_B1 * m_ref[...] + (1.0 - ADAM_B1) * g
        vv = ADAM_B2 * v_ref[...] + (1.0 - ADAM_B2) * jnp.square(g)
        m_hat = mm / c1
        v_hat = vv / c2
        g_ref[...] = g
        d_ref[...] = -ADAM_LR * (m_hat / (jnp.sqrt(v_hat) + ADAM_EPS) + ADAM_WD * w_ref[...])
        nm_ref[...] = mm
        nv_ref[...] = vv

    blk = pl.BlockSpec((tr, C), lambda i: (i, 0))
    return pl.pallas_call(
        body, name=name, grid=(R // tr,),
        in_specs=[pl.BlockSpec((N_DEV, tr, C), lambda i: (0, i, 0)), blk, blk, blk], out_specs=[blk] * 4,
        out_shape=[jax.ShapeDtypeStruct((R, C), F32)] * 4,
        compiler_params=_cparams(("parallel",)),
    )(recv, w, m, v)


PACK_COLS = 1024


def _padded(n):
    return -(-n // PACK_ALIGN) * PACK_ALIGN


def _pack_flat(pieces):
    flat = jnp.concatenate([p.reshape(-1) for p in pieces])
    n = flat.shape[0]
    return jnp.pad(flat, (0, _padded(n) - n)).reshape(-1, PACK_COLS)


def _shard_shape(shape, axis):
    s = list(shape)
    assert s[axis] % N_DEV == 0
    s[axis] //= N_DEV
    return tuple(s)


def _split_full(full, axis):
    s = full.shape
    r = full.reshape(s[:axis] + (N_DEV, s[axis] // N_DEV) + s[axis + 1:])
    return jnp.moveaxis(r, axis, 0)


def _merge_full(parts, axis):
    r = jnp.moveaxis(parts, 0, axis)
    s = r.shape
    return r.reshape(s[:axis] + (s[axis] * s[axis + 1],) + s[axis + 2:])


def _pack_full(entries, grads):
    flat = jnp.concatenate([_split_full(grads[k].reshape(shape), axis).reshape(N_DEV, -1)
                            for k, shape, axis in entries], axis=1)
    n = flat.shape[1]
    return jnp.pad(flat, ((0, 0), (0, _padded(n) - n))).reshape(N_DEV, -1, PACK_COLS)


def _unpack_gathered(entries, buf):
    flat = buf.reshape(N_DEV, -1)
    out, pos = {}, 0
    for k, shape, axis in entries:
        ss = _shard_shape(shape, axis)
        n = int(np.prod(ss))
        out[k] = _merge_full(flat[:, pos:pos + n].reshape((N_DEV,) + ss), axis)
        pos += n
    return out


def _unpack_shard(entries, buf):
    flat = buf.reshape(-1)
    out, pos = {}, 0
    for k, shape, axis in entries:
        ss = _shard_shape(shape, axis)
        n = int(np.prod(ss))
        out[k] = flat[pos:pos + n].reshape(ss)
        pos += n
    return out


def _unpack_flat(entries, buf):
    flat = buf.reshape(-1)
    out, pos = {}, 0
    for k, shape in entries:
        n = int(np.prod(shape))
        out[k] = flat[pos:pos + n].reshape(shape)
        pos += n
    return out


D, FF = D_MODEL, D_FF
_FFN_MATS = (("w1", (D, FF), 1), ("w3", (D, FF), 1), ("w2", (FF, D), 0))
_NORM_VECS = (("g_pre", (D,), 0), ("g_post", (D,), 0))
_MIX_MATS = (
    (("w_in", (D, D), 0), ("w_glu", (D, D), 0), ("w_out", (D, D), 0)),
    (("w_in", (D, 2 * D), 1), ("w_out", (D, D), 0)),
    (("w_in", (D, 2 * GM_E), 1), ("w_out", (GM_E, D), 0)),
    (("w_qkv", (D, 9 * D), 1), ("w_out", (D, D), 0)),
)
_MIX_VECS = (
    (),
    (("b_in", (2 * D,), 0), ("dw", (CONV_W, D), 1), ("dw_b", (D,), 0), ("ln_g", (D,), 0), ("ln_b", (D,), 0),
     ("b_out", (D,), 0)),
    (("b_in", (2 * GM_E,), 0), ("ln_g", (GM_E,), 0), ("ln_b", (GM_E,), 0), ("b_out", (D,), 0)),
    (),
)
_REPLICATED = (
    ("rel_bias", 3, "rel_bias", (NUM_BUCKETS, 3 * AT_HEADS)),
    ("s5_a_re", 0, "a_re", (S5_GROUPS, S5_STATE)), ("s5_a_im", 0, "a_im", (S5_GROUPS, S5_STATE)),
    ("s5_log_dt", 0, "log_dt", (S5_GROUPS,)),
    ("s5_b_re", 0, "b_re", (S5_GROUPS, S5_STATE, S5_GROUP)), ("s5_b_im", 0, "b_im", (S5_GROUPS, S5_STATE, S5_GROUP)),
    ("s5_c_re", 0, "c_re", (S5_GROUPS, S5_GROUP, S5_STATE)), ("s5_c_im", 0, "c_im", (S5_GROUPS, S5_GROUP, S5_STATE)),
    ("s5_d", 0, "d", (D,)), ("s5_b_glu", 0, "b_glu", (D,)),
    ("gm_w_s", 2, "w_s", (GM_HEADS, GM_CHUNK, GM_CHUNK)), ("gm_b_s", 2, "b_s", (GM_HEADS, GM_CHUNK)),
)
_MIX_PREFIX = ("s5_", "cv_", "gm_", "at_")
_TWIN_WEIGHTS = ('norm_pre', 'norm_post', 'ffn_w1', 'ffn_w3', 'ffn_w2', 'rel_bias', 's5_w_in', 's5_a_re', 's5_a_im',
                 's5_log_dt', 's5_b_re', 's5_b_im', 's5_c_re', 's5_c_im', 's5_d', 's5_w_glu', 's5_b_glu', 's5_w_out',
                 'cv_w_in', 'cv_b_in', 'cv_dw', 'cv_dw_b', 'cv_ln_g', 'cv_ln_b', 'cv_w_out', 'cv_b_out', 'gm_w_in',
                 'gm_b_in', 'gm_ln_g', 'gm_ln_b', 'gm_w_s', 'gm_b_s', 'gm_w_out', 'gm_b_out', 'at_w_qkv', 'at_w_out')


def _part_entries(part):
    if part[0] == "ffn":
        return _FFN_MATS, _NORM_VECS
    kind = part[1] % 4
    return _MIX_MATS[kind], _NORM_VECS + _MIX_VECS[kind]


def _part_shards(part, get):
    if part[0] == "ffn":
        _, i, j = part
        n = 0 if j == 0 else 2
        return {"w1": get("ffn_w1")[i, j], "w3": get("ffn_w3")[i, j], "w2": get("ffn_w2")[i, j],
                "g_pre": get("norm_pre")[i, n], "g_post": get("norm_post")[i, n]}
    _, i = part
    kind, j = i % 4, i // 4
    out = {"g_pre": get("norm_pre")[i, 1], "g_post": get("norm_post")[i, 1]}
    for k, _, _ in _MIX_MATS[kind] + _MIX_VECS[kind]:
        out[k] = get(_MIX_PREFIX[kind] + k)[j]
    return out


def _parts():
    parts = []
    for i in range(DEPTH):
        parts += [("ffn", i, 0), ("mix", i), ("ffn", i, 1)]
    return parts


def _as_par(v):
    return v.reshape(1, -1)


def _prepare_part(part, full, rep):
    if part[0] == "ffn":
        return {"w1": full["w1"], "w3": full["w3"], "w2": full["w2"],
                "g_pre": _as_par(full["g_pre"]), "g_post": _as_par(full["g_post"])}
    kind = part[1] % 4
    p = {"g_pre": _as_par(full["g_pre"]), "g_post": _as_par(full["g_post"])}
    for k, _, _ in _MIX_MATS[kind]:
        p[k] = full[k]
    for k, _, _ in _MIX_VECS[kind]:
        p[k] = _as_par(full[k]) if k != "dw" else jnp.pad(full[k], ((0, CONV_HALO - CONV_W), (0, 0)))
    if kind == 0:
        for k in ("a_re", "a_im", "log_dt", "b_re", "b_im"):
            p[k] = rep[k]
        p["c_re"], p["c_im"] = rep["c_re"], rep["c_im"]
        p["d"], p["b_glu"] = _as_par(rep["d"]), _as_par(rep["b_glu"])
    elif kind == 2:
        p["w_s"], p["b_s"] = rep["w_s"], rep["b_s"]
    elif kind == 3:
        p["rel_bias"] = rep["rel_bias"]
    return p


def _finish_grads(part, grads):
    out = dict(grads)
    for k in ("g_pre", "g_post", "b_in", "dw_b", "ln_g", "ln_b", "b_out", "d", "b_glu"):
        if k in out:
            out[k] = out[k].reshape(-1)
    if "dw" in out:
        out["dw"] = out["dw"][:CONV_W]
    return out


def _step(x, tgt, inputs, moments_m, moments_v):
    parts = _parts()
    rep = {}
    for name, kind, key, shape in _REPLICATED:
        rep[key] = inputs[name][0] if name != "rel_bias" else inputs[name]

    packed_w = []
    for part in parts:
        mats, vecs = _part_entries(part)
        sh = _part_shards(part, lambda n: inputs[n])
        packed_w.append((_pack_flat([sh[k] for k, _, _ in mats]), _pack_flat([sh[k] for k, _, _ in vecs])))

    def gather_of(idx):
        wm, wv = packed_w[idx]
        return _Carry([wm.astype(BF16), wv], ["bcast", "bcast"])

    def gathered(idx, bufs):
        mats, vecs = _part_entries(parts[idx])
        full = _unpack_gathered(mats, bufs[0])
        full.update(_unpack_gathered(vecs, bufs[1]))
        return _prepare_part(parts[idx], full, rep)

    params = [None] * len(parts)
    first = gather_of(0)
    params[0] = gathered(0, _exchange(first.arrays, first.kinds, name="gather_first"))
    saved = []
    h = x
    for idx, part in enumerate(parts):
        if part[0] == "ffn":
            ahead = [i for i in (idx + 1, idx + 2) if i < len(parts) and params[i] is None]
            c_up = gather_of(ahead[0]) if ahead else None
            c_down = gather_of(ahead[1]) if len(ahead) > 1 and part[2] == 0 else None
            h, s, got_up, got_down = _ffn_fwd(h, params[idx], c_up, c_down)
            if c_up is not None:
                params[ahead[0]] = gathered(ahead[0], got_up)
            if c_down is not None:
                params[ahead[1]] = gathered(ahead[1], got_down)
        else:
            h, s = _mixer_fwd(h, params[idx], part[1] % 4)
        saved.append(s)
    dh, loss_vec = _loss_call(h, tgt)
    loss_local = loss_vec[0, 0]

    results = {}
    rep_grads = {}

    def scatter_of(idx, grads):
        mats, vecs = _part_entries(parts[idx])
        return _Carry([_pack_full(mats, grads).astype(BF16), _pack_full(vecs, grads)], ["a2a", "a2a"])

    def update(idx, bufs):
        part = parts[idx]
        mats, vecs = _part_entries(part)
        wm, wv = packed_w[idx]
        mm_ = _part_shards(part, lambda n: moments_m[n])
        vv_ = _part_shards(part, lambda n: moments_v[n])
        om = _adam(bufs[0], wm, _pack_flat([mm_[k] for k, _, _ in mats]), _pack_flat([vv_[k] for k, _, _ in mats]),
                   name="adam_" + part[0] + "_mats")
        ov = _adam(bufs[1], wv, _pack_flat([mm_[k] for k, _, _ in vecs]), _pack_flat([vv_[k] for k, _, _ in vecs]),
                   name="adam_" + part[0] + "_vecs")
        res = [_unpack_shard(mats, o) for o in om]
        for r, o in zip(res, ov):
            r.update(_unpack_shard(vecs, o))
        results[part] = res

    pending = []
    for idx in range(len(parts) - 1, -1, -1):
        part, p = parts[idx], params[idx]
        if part[0] == "ffn":
            riders = pending[:2]
            pending = pending[2:]
            c_a = riders[0][1] if riders else None
            c_b = riders[1][1] if len(riders) > 1 else None
            dh, grads, got_a, got_b = _ffn_bwd(saved[idx], p, dh, c_a, c_b)
            for (ridx, _), got in zip(riders, (got_a, got_b)):
                update(ridx, got)
        else:
            dh, grads = _mixer_bwd(saved[idx], p, part[1] % 4, dh)
        grads = _finish_grads(part, grads)
        for name, kind, key, shape in _REPLICATED:
            if part[0] == "mix" and kind == part[1] % 4:
                rep_grads[name] = grads[key]
        pending.append((idx, scatter_of(idx, grads)))
    for ridx, c in pending:
        update(ridx, _exchange(c.arrays, c.kinds, name="scatter_last"))

    rep_entries = [(name, shape) for name, _, _, shape in _REPLICATED]
    get_rep = lambda d: _pack_flat([(d[name][0] if name != "rel_bias" else d[name]) for name, _ in rep_entries])
    rg, = _exchange([_pack_flat([rep_grads[name] for name, _ in rep_entries])], ["bcast"], name="allgather_rep")
    orep = _adam(rg, get_rep(inputs), get_rep(moments_m), get_rep(moments_v), name="adam_rep")
    rep_out = [_unpack_flat(rep_entries, o) for o in orep]
    return loss_local, dh, results, rep_out


def _assemble(name, results, rep_out, which):
    for rname, _, _, _ in _REPLICATED:
        if rname == name:
            a = rep_out[which][name]
            return a if name == "rel_bias" else a[None]
    if name in ("norm_pre", "norm_post"):
        key = "g_pre" if name == "norm_pre" else "g_post"
        rows = []
        for i in range(DEPTH):
            rows.append(jnp.stack([results[("ffn", i, 0)][which][key], results[("mix", i)][which][key],
                                   results[("ffn", i, 1)][which][key]]))
        return jnp.stack(rows)
    if name.startswith("ffn_"):
        key = name[4:]
        return jnp.stack([jnp.stack([results[("ffn", i, j)][which][key] for j in range(2)]) for i in range(DEPTH)])
    kind = _MIX_PREFIX.index(name[:3])
    layers = [i for i in range(DEPTH) if i % 4 == kind]
    return jnp.stack([results[("mix", i)][which][name[3:]] for i in layers])


def kernel(x, norm_pre, norm_post, ffn_w1, ffn_w3, ffn_w2, rel_bias, s5_w_in, s5_a_re, s5_a_im, s5_log_dt, s5_b_re, s5_b_im, s5_c_re, s5_c_im, s5_d, s5_w_glu, s5_b_glu, s5_w_out, cv_w_in, cv_b_in, cv_dw, cv_dw_b, cv_ln_g, cv_ln_b, cv_w_out, cv_b_out, gm_w_in, gm_b_in, gm_ln_g, gm_ln_b, gm_w_s, gm_b_s, gm_w_out, gm_b_out, at_w_qkv, at_w_out, loss_target, m_norm_pre, m_norm_post, m_ffn_w1, m_ffn_w3, m_ffn_w2, m_rel_bias, m_s5_w_in, m_s5_a_re, m_s5_a_im, m_s5_log_dt, m_s5_b_re, m_s5_b_im, m_s5_c_re, m_s5_c_im, m_s5_d, m_s5_w_glu, m_s5_b_glu, m_s5_w_out, m_cv_w_in, m_cv_b_in, m_cv_dw, m_cv_dw_b, m_cv_ln_g, m_cv_ln_b, m_cv_w_out, m_cv_b_out, m_gm_w_in, m_gm_b_in, m_gm_ln_g, m_gm_ln_b, m_gm_w_s, m_gm_b_s, m_gm_w_out, m_gm_b_out, m_at_w_qkv, m_at_w_out, v_norm_pre, v_norm_post, v_ffn_w1, v_ffn_w3, v_ffn_w2, v_rel_bias, v_s5_w_in, v_s5_a_re, v_s5_a_im, v_s5_log_dt, v_s5_b_re, v_s5_b_im, v_s5_c_re, v_s5_c_im, v_s5_d, v_s5_w_glu, v_s5_b_glu, v_s5_w_out, v_cv_w_in, v_cv_b_in, v_cv_dw, v_cv_dw_b, v_cv_ln_g, v_cv_ln_b, v_cv_w_out, v_cv_b_out, v_gm_w_in, v_gm_b_in, v_gm_ln_g, v_gm_ln_b, v_gm_w_s, v_gm_b_s, v_gm_w_out, v_gm_b_out, v_at_w_qkv, v_at_w_out):
    args = locals()
    inputs = {n: args[n] for n in _TWIN_WEIGHTS}
    moments_m = {n: args["m_" + n] for n in _TWIN_WEIGHTS}
    moments_v = {n: args["v_" + n] for n in _TWIN_WEIGHTS}
    loss_local, dx, results, rep_out = _step(x[0], loss_target[0], inputs, moments_m, moments_v)
    loss = lax.psum(loss_local, AXES)
    out = [loss, dx[None]]
    for which in range(4):
        out += [_assemble(n, results, rep_out, which) for n in _TWIN_WEIGHTS]
    return tuple(out)
```

```python
import functools
import math

import numpy as np

import jax
import jax.numpy as jnp
from jax import lax
from jax.experimental import pallas as pl
from jax.experimental.pallas import tpu as pltpu

F32 = jnp.float32
BF16 = jnp.bfloat16

D_MODEL = 1024
DEPTH = 4
D_FF = 2816
EPS = 1e-6
S5_GROUP = 16
S5_STATE = 64
CONV_W = 31
GM_CHUNK = 128
GM_HEADS = 8
HEAD_DIM = 64
PATTERNS = ((128, 1), (512, 4), (2048, 16))
BLOCK = 128
NUM_BUCKETS = 32
MAX_DISTANCE = 2048
ADAM_LR = 0.001
ADAM_B1 = 0.9
ADAM_B2 = 0.999
ADAM_EPS = 1e-08
ADAM_WD = 0.01
ADAM_STEP = 10

N_DEV = 8
AXES = ("x", "y", "c")
LANES = 128
GM_E = 2 * D_MODEL
S5_GROUPS = D_MODEL // S5_GROUP
S5_GB = LANES // S5_GROUP
S5_NB = D_MODEL // LANES
S5_BW = S5_GB * S5_STATE
S5_NS = S5_GROUPS * S5_STATE
AT_HEADS = D_MODEL // HEAD_DIM
VMEM_LIMIT = 56 * 1024 * 1024
PACK_ALIGN = 16 * 1024


def _cparams(sem):
    return pltpu.CompilerParams(dimension_semantics=sem, vmem_limit_bytes=VMEM_LIMIT)


def _pick(n, cap):
    if n <= cap:
        return n
    best = None
    for t in range(LANES, cap + 1, LANES):
        if n % t == 0:
            best = t
    assert best is not None, (n, cap)
    return best


def _pick_rows(n, cap):
    best = None
    for t in range(16, min(n, cap) + 1, 16):
        if n % t == 0:
            best = t
    assert best is not None, (n, cap)
    return best


MM_VMEM_BUDGET = 40 * 1024 * 1024


def _mm(a, b, *, ta=False, tb=False, out_dtype=F32, name, carry=None):
    a_list = list(a) if isinstance(a, (tuple, list)) else [a]
    b_list = list(b) if isinstance(b, (tuple, list)) else [b]
    n_op = len(a_list)
    assert n_op == len(b_list)
    K, M = a_list[0].shape if ta else a_list[0].shape[::-1]
    N, K2 = b_list[0].shape if tb else b_list[0].shape[::-1]
    assert K == K2, (a_list[0].shape, b_list[0].shape, ta, tb)
    a_bytes = sum(x.dtype.itemsize for x in a_list)
    b_bytes = sum(x.dtype.itemsize for x in b_list)
    o_bytes = jnp.dtype(out_dtype).itemsize

    def vmem(tm, tn, tk, nk):
        acc = tm * tn * 4 if (nk > 1 and out_dtype != F32) else 0
        return 2 * (tm * tk * a_bytes + tk * tn * b_bytes + tm * tn * o_bytes) + acc

    if ta:
        tm, tn = _pick(M, 1408), _pick(N, 1408)
        tk = next(t for t in (2048, 1024, 512, 256) if K % t == 0 and vmem(tm, tn, t, 2) <= MM_VMEM_BUDGET)
    else:
        tm, tn = _pick(M, 512), _pick(N, 1408)
        tk = next(t for t in (K, _pick(K, 4608), _pick(K, 2816), _pick(K, 1024))
                  if vmem(tm, tn, t, K // t) <= MM_VMEM_BUDGET)
    nk = K // tk
    a_spec = pl.BlockSpec((tk, tm), lambda j, i, k: (k, i)) if ta else pl.BlockSpec((tm, tk), lambda j, i, k: (i, k))
    b_spec = pl.BlockSpec((tn, tk), lambda j, i, k: (j, k)) if tb else pl.BlockSpec((tk, tn), lambda j, i, k: (k, j))
    dims = (((0 if ta else 1,), (1 if tb else 0,)), ((), ()))
    use_scratch = nk > 1 and out_dtype != F32
    grid = (N // tn, M // tm, nk)

    def body(*refs):
        ins, (o_ref,), scratch, begin, end = _carry_hooks(carry, refs, 2 * n_op, 1, 3, grid)
        begin()
        p = None
        for a_ref, b_ref in zip(ins[:n_op], ins[n_op:]):
            d = lax.dot_general(a_ref[...].astype(BF16), b_ref[...].astype(BF16), dims, preferred_element_type=F32)
            p = d if p is None else p + d
        if nk == 1:
            o_ref[...] = p.astype(o_ref.dtype)
        else:
            acc = scratch[0] if use_scratch else o_ref
            k = pl.program_id(2)

            @pl.when(k == 0)
            def _():
                acc[...] = p

            @pl.when(k > 0)
            def _():
                acc[...] += p

            if use_scratch:
                @pl.when(k == nk - 1)
                def _():
                    o_ref[...] = acc[...].astype(o_ref.dtype)
        end()

    extra = carry if carry is not None else _NO_CARRY
    res = pl.pallas_call(
        body, name=name, grid=grid, in_specs=[a_spec] * n_op + [b_spec] * n_op + extra.in_specs,
        out_specs=[pl.BlockSpec((tm, tn), lambda j, i, k: (i, j))] + extra.out_specs,
        out_shape=[jax.ShapeDtypeStruct((M, N), out_dtype)] + extra.out_shape,
        scratch_shapes=([pltpu.VMEM((tm, tn), F32)] if use_scratch else []) + extra.scratch,
        compiler_params=_cparams(("arbitrary",) * 3 if carry is not None else ("parallel", "parallel", "arbitrary")),
    )(*a_list, *b_list, *extra.arrays)
    return res[0] if carry is None else (res[0], res[1:])


ROW_TILE_BYTES = 8 * 1024 * 1024


def _row_tile(arrays):
    row_bytes = sum(w * jnp.dtype(dt).itemsize for w, dt in arrays)
    for tile in (256, 128, 64, 32):
        if tile * row_bytes <= ROW_TILE_BYTES:
            return tile
    return 16


def _rows(fn, rows, pars, outs, *, name):
    T = rows[0].shape[0]
    tile = _row_tile([(r.shape[1], r.dtype) for r in rows] + list(outs))
    nr, npar = len(rows), len(pars)

    def body(*refs):
        r = [refs[i][...] for i in range(nr)]
        p = [refs[nr + i][...] for i in range(npar)]
        res = fn(*r, *p)
        for o_ref, o in zip(refs[nr + npar:], res):
            o_ref[...] = o.astype(o_ref.dtype)

    in_specs = [pl.BlockSpec((tile, r.shape[1]), lambda i: (i, 0)) for r in rows]
    in_specs += [pl.BlockSpec(p.shape, lambda i, nd=p.ndim: (0,) * nd) for p in pars]
    return pl.pallas_call(
        body, name=name, grid=(T // tile,), in_specs=in_specs,
        out_specs=[pl.BlockSpec((tile, w), lambda i: (i, 0)) for w, _ in outs],
        out_shape=[jax.ShapeDtypeStruct((T, w), dt) for w, dt in outs],
        compiler_params=_cparams(("parallel",)),
    )(*rows, *pars)


def _rows_vjp(fn, rows, pars, cts, *, dtypes, adds=None, name):
    adds = adds or {}
    cts = [c if isinstance(c, (tuple, list)) else (c,) for c in cts]
    flat_cts = [a for c in cts for a in c]
    add_keys = sorted(adds)
    add_arrs = [adds[k] for k in add_keys]
    want = [i for i, d in enumerate(dtypes) if d is not None]
    T = rows[0].shape[0]
    tile = _row_tile([(a.shape[1], a.dtype) for a in list(rows) + flat_cts + add_arrs]
                     + [(rows[i].shape[1], dtypes[i]) for i in want])
    nr, npar, nc, na = len(rows), len(pars), len(flat_cts), len(add_arrs)

    def body(*refs):
        r = [refs[i][...].astype(F32) for i in range(nr)]
        p = [refs[nr + i][...] for i in range(npar)]
        cvals = [refs[nr + npar + i][...].astype(F32) for i in range(nc)]
        avals = [refs[nr + npar + nc + i][...].astype(F32) for i in range(na)]
        outs = refs[nr + npar + nc + na:]
        ct, pos = [], 0
        for c in cts:
            s = cvals[pos]
            for extra in cvals[pos + 1:pos + len(c)]:
                s = s + extra
            pos += len(c)
            ct.append(s)
        _, vjp = jax.vjp(lambda *a: tuple(fn(*a)), *r, *p)
        g = vjp(tuple(ct))
        for o_ref, i in zip(outs[:len(want)], want):
            gi = g[i]
            if i in adds:
                gi = gi + avals[add_keys.index(i)]
            o_ref[...] = gi.astype(o_ref.dtype)
        first = pl.program_id(0) == 0
        for o_ref, gp in zip(outs[len(want):], g[nr:]):
            @pl.when(first)
            def _(o_ref=o_ref, gp=gp):
                o_ref[...] = gp

            @pl.when(jnp.logical_not(first))
            def _(o_ref=o_ref, gp=gp):
                o_ref[...] += gp

    row_spec = lambda a: pl.BlockSpec((tile, a.shape[1]), lambda i: (i, 0))
    par_spec = lambda a: pl.BlockSpec(a.shape, lambda i, nd=a.ndim: (0,) * nd)
    res = pl.pallas_call(
        body, name=name, grid=(T // tile,),
        in_specs=[row_spec(a) for a in rows] + [par_spec(a) for a in pars] + [row_spec(a) for a in flat_cts + add_arrs],
        out_specs=[row_spec(rows[i]) for i in want] + [par_spec(a) for a in pars],
        out_shape=[jax.ShapeDtypeStruct(rows[i].shape, dtypes[i]) for i in want]
        + [jax.ShapeDtypeStruct(a.shape, F32) for a in pars],
        compiler_params=_cparams(("arbitrary",)),
    )(*rows, *pars, *flat_cts, *add_arrs)
    return res[:len(want)], res[len(want):]


def _small(fn, args, outs, *, name):
    n = len(args)

    def body(*refs):
        res = fn(*[r[...] for r in refs[:n]])
        for o_ref, o in zip(refs[n:], res):
            o_ref[...] = o

    return pl.pallas_call(body, name=name, out_shape=[jax.ShapeDtypeStruct(s, F32) for s in outs],
                          compiler_params=pltpu.CompilerParams(vmem_limit_bytes=VMEM_LIMIT))(*args)


def _small_vjp(fn, args, cts, *, name):
    n, nc = len(args), len(cts)

    def body(*refs):
        _, vjp = jax.vjp(lambda *a: tuple(fn(*a)), *[r[...] for r in refs[:n]])
        g = vjp(tuple(r[...] for r in refs[n:n + nc]))
        for o_ref, gi in zip(refs[n + nc:], g):
            o_ref[...] = gi

    return pl.pallas_call(body, name=name, out_shape=[jax.ShapeDtypeStruct(a.shape, F32) for a in args],
                          compiler_params=pltpu.CompilerParams(vmem_limit_bytes=VMEM_LIMIT))(*args, *cts)


def _rms(x, g):
    return x * lax.rsqrt(jnp.mean(x * x, axis=-1, keepdims=True) + EPS) * g


def _layernorm(x, g, b):
    mu = jnp.mean(x, axis=-1, keepdims=True)
    var = jnp.mean(jnp.square(x - mu), axis=-1, keepdims=True)
    return (x - mu) * lax.rsqrt(var + EPS) * g + b


def _f_pre(x, g):
    return (_rms(x.astype(F32), g),)


def _f_post_term(scale, has_bias):
    def fn(o, g, *b):
        o = o.astype(F32)
        if has_bias:
            o = o + b[0]
        return (scale * _rms(o, g),)
    return fn


def _f_post(scale, has_bias):
    term = _f_post_term(scale, has_bias)

    def fn(x, o, g, *b):
        return (x + term(o, g, *b)[0],)
    return fn


def _f_s5_gelu(ylin, u, d):
    return (jax.nn.gelu(ylin.astype(F32) + d * u.astype(F32)),)


def _f_s5_glu(y, gl, b):
    return (y.astype(F32) * jax.nn.sigmoid(gl.astype(F32) + b),)


def _f_cv_glu(z0, b):
    z = z0.astype(F32) + b
    return (z[:, :D_MODEL] * jax.nn.sigmoid(z[:, D_MODEL:]),)


def _f_cv_ln(zc, g, b):
    return (jax.nn.silu(_layernorm(zc.astype(F32), g, b)),)


def _f_gm_in(z0, b, g, bl):
    z = jax.nn.gelu(z0.astype(F32) + b)
    return z[:, :GM_E], _layernorm(z[:, GM_E:], g, bl)


def _f_at_combine(o0, o1, o2, l0, l1, l2):
    m = jnp.maximum(jnp.maximum(l0, l1), l2)
    e0, e1, e2 = jnp.exp(l0 - m), jnp.exp(l1 - m), jnp.exp(l2 - m)
    return ((e0 * o0 + e1 * o1 + e2 * o2) / (e0 + e1 + e2),)


def _f_s5_disc(ar, ai, ldt, br, bi):
    dt = jnp.exp(ldt)
    mag = jnp.exp(dt * ar)
    abr = mag * jnp.cos(dt * ai)
    abi = mag * jnp.sin(dt * ai)
    den = ar * ar + ai * ai
    nr = abr - 1.0
    f_re = (nr * ar + abi * ai) / den
    f_im = (abi * ar - nr * ai) / den
    return abr, abi, f_re * br - f_im * bi, f_re * bi + f_im * br


def _loss_call(y, tgt):
    T, D = y.shape
    tile = 256

    def body(y_ref, t_ref, dy_ref, l_ref):
        err = y_ref[...] - t_ref[...]
        dy_ref[...] = err * (1.0 / D)
        part = 0.5 * jnp.sum(jnp.mean(err * err, axis=-1, keepdims=True), axis=0, keepdims=True)
        part = jnp.broadcast_to(part, (1, LANES))
        first = pl.program_id(0) == 0

        @pl.when(first)
        def _():
            l_ref[...] = part

        @pl.when(jnp.logical_not(first))
        def _():
            l_ref[...] += part

    return pl.pallas_call(
        body, name="loss", grid=(T // tile,),
        in_specs=[pl.BlockSpec((tile, D), lambda i: (i, 0))] * 2,
        out_specs=[pl.BlockSpec((tile, D), lambda i: (i, 0)), pl.BlockSpec((1, LANES), lambda i: (0, 0))],
        out_shape=[jax.ShapeDtypeStruct((T, D), F32), jax.ShapeDtypeStruct((1, LANES), F32)],
        compiler_params=_cparams(("arbitrary",)),
    )(y, tgt)


def _bd(xs, ws, *, add=None, out_dtype=F32, name):
    T = xs[0].shape[0]
    nb, kw, nw = ws[0].shape
    tm = 256
    n = len(xs)

    def body(*refs):
        o_ref = refs[-1]
        for j in range(nb):
            acc = None
            for x_ref, w_ref in zip(refs[:n], refs[n:2 * n]):
                p = jnp.dot(x_ref[:, j * kw:(j + 1) * kw].astype(BF16), w_ref[j].astype(BF16),
                            preferred_element_type=F32)
                acc = p if acc is None else acc + p
            if add is not None:
                acc = acc + refs[2 * n][:, j * nw:(j + 1) * nw].astype(F32)
            o_ref[:, j * nw:(j + 1) * nw] = acc.astype(o_ref.dtype)

    in_specs = [pl.BlockSpec((tm, nb * kw), lambda i: (i, 0)) for _ in xs]
    in_specs += [pl.BlockSpec((nb, kw, nw), lambda i: (0, 0, 0)) for _ in ws]
    args = list(xs) + list(ws)
    if add is not None:
        in_specs.append(pl.BlockSpec((tm, nb * nw), lambda i: (i, 0)))
        args.append(add)
    return pl.pallas_call(
        body, name=name, grid=(T // tm,), in_specs=in_specs,
        out_specs=pl.BlockSpec((tm, nb * nw), lambda i: (i, 0)),
        out_shape=jax.ShapeDtypeStruct((T, nb * nw), out_dtype),
        compiler_params=_cparams(("parallel",)),
    )(*args)


def _bd_wgrad(x, dy, kw, nw, *, name):
    T = x.shape[0]
    nb = x.shape[1] // kw
    tk = 512

    def body(x_ref, dy_ref, o_ref):
        first = pl.program_id(0) == 0
        for j in range(nb):
            p = lax.dot_general(x_ref[:, j * kw:(j + 1) * kw].astype(BF16), dy_ref[:, j * nw:(j + 1) * nw].astype(BF16),
                                (((0,), (0,)), ((), ())), preferred_element_type=F32)

            @pl.when(first)
            def _(j=j, p=p):
                o_ref[j] = p

            @pl.when(jnp.logical_not(first))
            def _(j=j, p=p):
                o_ref[j] += p

    return pl.pallas_call(
        body, name=name, grid=(T // tk,),
        in_specs=[pl.BlockSpec((tk, nb * kw), lambda k: (k, 0)), pl.BlockSpec((tk, nb * nw), lambda k: (k, 0))],
        out_specs=pl.BlockSpec((nb, kw, nw), lambda k: (0, 0, 0)),
        out_shape=jax.ShapeDtypeStruct((nb, kw, nw), F32),
        compiler_params=_cparams(("arbitrary",)),
    )(x, dy)


SCAN_COLS = 1024
SCAN_ROWS = 256


def _scan_fwd(bur, bui, ar, ai):
    T, NS = bur.shape
    cw, tc = SCAN_COLS, SCAN_ROWS

    def body(bur_ref, bui_ref, ar_ref, ai_ref, sr_ref, si_ref, cr, ci):
        @pl.when(pl.program_id(1) == 0)
        def _():
            cr[...] = jnp.zeros_like(cr)
            ci[...] = jnp.zeros_like(ci)

        a_r, a_i = ar_ref[...], ai_ref[...]

        def step8(t8, carry):
            sr, si = carry
            base = pl.multiple_of(t8 * 8, 8)
            for r in range(8):
                br = bur_ref[pl.ds(base + r, 1), :]
                bi = bui_ref[pl.ds(base + r, 1), :]
                sr, si = a_r * sr - a_i * si + br, a_r * si + a_i * sr + bi
                sr_ref[pl.ds(base + r, 1), :] = sr
                si_ref[pl.ds(base + r, 1), :] = si
            return sr, si

        sr, si = lax.fori_loop(0, tc // 8, step8, (cr[...], ci[...]))
        cr[...] = sr
        ci[...] = si

    blk = pl.BlockSpec((tc, cw), lambda c, t: (t, c))
    vec = pl.BlockSpec((1, cw), lambda c, t: (0, c))
    return pl.pallas_call(
        body, name="s5_scan_fwd", grid=(NS // cw, T // tc), in_specs=[blk, blk, vec, vec], out_specs=[blk, blk],
        out_shape=[jax.ShapeDtypeStruct((T, NS), F32)] * 2,
        scratch_shapes=[pltpu.VMEM((1, cw), F32)] * 2,
        compiler_params=_cparams(("parallel", "arbitrary")),
    )(bur, bui, ar, ai)


def _scan_bwd(gr, gi, sr, si, ar, ai):
    T, NS = gr.shape
    cw, tc = SCAN_COLS, SCAN_ROWS
    nt = T // tc

    def body(gr_ref, gi_ref, sr_ref, si_ref, ar_ref, ai_ref, lr_ref, li_ref, dar_ref, dai_ref, cr, ci):
        @pl.when(pl.program_id(1) == 0)
        def _():
            cr[...] = jnp.zeros_like(cr)
            ci[...] = jnp.zeros_like(ci)
            dar_ref[...] = jnp.zeros_like(dar_ref)
            dai_ref[...] = jnp.zeros_like(dai_ref)

        a_r, a_i = ar_ref[...], ai_ref[...]

        def step8(k, carry):
            lr, li, dar, dai = carry
            base = pl.multiple_of((tc // 8 - 1 - k) * 8, 8)
            for r in range(7, -1, -1):
                s_r = sr_ref[pl.ds(base + r, 1), :]
                s_i = si_ref[pl.ds(base + r, 1), :]
                dar = dar + lr * s_r + li * s_i
                dai = dai + li * s_r - lr * s_i
                g_r = gr_ref[pl.ds(base + r, 1), :]
                g_i = gi_ref[pl.ds(base + r, 1), :]
                lr, li = g_r + a_r * lr + a_i * li, g_i + a_r * li - a_i * lr
                lr_ref[pl.ds(base + r, 1), :] = lr
                li_ref[pl.ds(base + r, 1), :] = li
            return lr, li, dar, dai

        lr, li, dar, dai = lax.fori_loop(0, tc // 8, step8, (cr[...], ci[...], dar_ref[...], dai_ref[...]))
        cr[...] = lr
        ci[...] = li
        dar_ref[...] = dar
        dai_ref[...] = dai

    blk = pl.BlockSpec((tc, cw), lambda c, t: (nt - 1 - t, c))
    vec = pl.BlockSpec((1, cw), lambda c, t: (0, c))
    return pl.pallas_call(
        body, name="s5_scan_bwd", grid=(NS // cw, nt), in_specs=[blk, blk, blk, blk, vec, vec],
        out_specs=[blk, blk, vec, vec],
        out_shape=[jax.ShapeDtypeStruct((T, NS), F32)] * 2 + [jax.ShapeDtypeStruct((1, NS), F32)] * 2,
        scratch_shapes=[pltpu.VMEM((1, cw), F32)] * 2,
        compiler_params=_cparams(("parallel", "arbitrary")),
    )(gr, gi, sr, si, ar, ai)


CONV_ROWS = 256
CONV_HALO = 32
CONV_PAD = CONV_HALO - (CONV_W - 1)


def _conv_fwd(z, dw, dwb):
    T, D = z.shape
    tc, hl = CONV_ROWS, CONV_HALO
    per = tc // hl

    def body(z_ref, zp_ref, dw_ref, b_ref, o_ref, ext):
        i = pl.program_id(0)
        ext[pl.ds(0, hl), :] = jnp.where(i > 0, zp_ref[...], 0.0)
        ext[pl.ds(hl, tc), :] = z_ref[...]
        acc = jnp.zeros((tc, D), F32) + b_ref[...]
        for k in range(CONV_W):
            acc = acc + dw_ref[pl.ds(k, 1), :] * ext[pl.ds(CONV_PAD + k, tc), :]
        o_ref[...] = acc

    return pl.pallas_call(
        body, name="conv_fwd", grid=(T // tc,),
        in_specs=[pl.BlockSpec((tc, D), lambda i: (i, 0)),
                  pl.BlockSpec((hl, D), lambda i: (jnp.maximum(i * per - 1, 0), 0)),
                  pl.BlockSpec((hl, D), lambda i: (0, 0)), pl.BlockSpec((1, D), lambda i: (0, 0))],
        out_specs=pl.BlockSpec((tc, D), lambda i: (i, 0)),
        out_shape=jax.ShapeDtypeStruct((T, D), F32),
        scratch_shapes=[pltpu.VMEM((tc + hl, D), F32)],
        compiler_params=_cparams(("parallel",)),
    )(z, z, dw, dwb)


def _conv_bwd(dout, z, dw):
    T, D = z.shape
    tc, hl = CONV_ROWS, CONV_HALO
    per = tc // hl
    nblk = T // tc

    def body(g_ref, gn_ref, z_ref, zp_ref, dw_ref, dz_ref, ddw_ref, db_ref, gext, zext):
        i = pl.program_id(0)
        g = g_ref[...]
        gext[pl.ds(0, tc), :] = g
        gext[pl.ds(tc, hl), :] = jnp.where(i < nblk - 1, gn_ref[...], 0.0)
        zext[pl.ds(0, hl), :] = jnp.where(i > 0, zp_ref[...], 0.0)
        zext[pl.ds(hl, tc), :] = z_ref[...]
        acc = jnp.zeros((tc, D), F32)
        for k in range(CONV_W):
            acc = acc + dw_ref[pl.ds(k, 1), :] * gext[pl.ds(CONV_W - 1 - k, tc), :]
        dz_ref[...] = acc

        @pl.when(i == 0)
        def _():
            ddw_ref[...] = jnp.zeros_like(ddw_ref)
            db_ref[...] = jnp.zeros_like(db_ref)

        db_ref[...] += jnp.sum(g, axis=0, keepdims=True)
        for k in range(CONV_W):
            ddw_ref[pl.ds(k, 1), :] += jnp.sum(g * zext[pl.ds(CONV_PAD + k, tc), :], axis=0, keepdims=True)

    return pl.pallas_call(
        body, name="conv_bwd", grid=(nblk,),
        in_specs=[pl.BlockSpec((tc, D), lambda i: (i, 0)),
                  pl.BlockSpec((hl, D), lambda i: (jnp.minimum((i + 1) * per, nblk * per - 1), 0)),
                  pl.BlockSpec((tc, D), lambda i: (i, 0)),
                  pl.BlockSpec((hl, D), lambda i: (jnp.maximum(i * per - 1, 0), 0)),
                  pl.BlockSpec((hl, D), lambda i: (0, 0))],
        out_specs=[pl.BlockSpec((tc, D), lambda i: (i, 0)), pl.BlockSpec((hl, D), lambda i: (0, 0)),
                   pl.BlockSpec((1, D), lambda i: (0, 0))],
        out_shape=[jax.ShapeDtypeStruct((T, D), F32), jax.ShapeDtypeStruct((hl, D), F32),
                   jax.ShapeDtypeStruct((1, D), F32)],
        scratch_shapes=[pltpu.VMEM((tc + hl, D), F32)] * 2,
        compiler_params=_cparams(("arbitrary",)),
    )(dout, dout, z, z, dw)


def _gm_causal():
    r = lax.broadcasted_iota(jnp.int32, (GM_CHUNK, GM_CHUNK), 0)
    c = lax.broadcasted_iota(jnp.int32, (GM_CHUNK, GM_CHUNK), 1)
    return r >= c


def _gm_sg_fwd(u, v, ws, bs_col):
    T, E = u.shape
    hw = E // GM_HEADS

    def body(u_ref, v_ref, w_ref, b_ref, o_ref):
        causal = _gm_causal()
        for h in range(GM_HEADS):
            cols = slice(h * hw, (h + 1) * hw)
            w = jnp.where(causal, w_ref[h], 0.0).astype(BF16)
            s = jnp.dot(w, v_ref[:, cols], preferred_element_type=F32) + b_ref[h]
            o_ref[:, cols] = (u_ref[:, cols] * s).astype(o_ref.dtype)

    return pl.pallas_call(
        body, name="gm_sg_fwd", grid=(T // GM_CHUNK,),
        in_specs=[pl.BlockSpec((GM_CHUNK, E), lambda i: (i, 0)), pl.BlockSpec((GM_CHUNK, E), lambda i: (i, 0)),
                  pl.BlockSpec(ws.shape, lambda i: (0, 0, 0)), pl.BlockSpec(bs_col.shape, lambda i: (0, 0, 0))],
        out_specs=pl.BlockSpec((GM_CHUNK, E), lambda i: (i, 0)),
        out_shape=jax.ShapeDtypeStruct((T, E), BF16),
        compiler_params=_cparams(("parallel",)),
    )(u, v, ws, bs_col)


def _gm_sg_bwd(dus, u, v, ws, bs_col):
    T, E = u.shape
    hw = E // GM_HEADS

    def body(g_ref, u_ref, v_ref, w_ref, b_ref, du_ref, dv_ref, dw_ref, db_ref):
        causal = _gm_causal()

        @pl.when(pl.program_id(0) == 0)
        def _():
            dw_ref[...] = jnp.zeros_like(dw_ref)
            db_ref[...] = jnp.zeros_like(db_ref)

        for h in range(GM_HEADS):
            cols = slice(h * hw, (h + 1) * hw)
            w = jnp.where(causal, w_ref[h], 0.0).astype(BF16)
            vh = v_ref[:, cols]
            s = jnp.dot(w, vh, preferred_element_type=F32) + b_ref[h]
            g = g_ref[:, cols]
            du_ref[:, cols] = g * s
            ds = g * u_ref[:, cols]
            dsb = ds.astype(BF16)
            dv_ref[:, cols] = lax.dot_general(w, dsb, (((0,), (0,)), ((), ())), preferred_element_type=F32)
            dwh = lax.dot_general(dsb, vh, (((1,), (1,)), ((), ())), preferred_element_type=F32)
            dw_ref[h] += jnp.where(causal, dwh, 0.0)
            db_ref[h] += jnp.broadcast_to(jnp.sum(ds, axis=1, keepdims=True), (GM_CHUNK, LANES))

    blk = pl.BlockSpec((GM_CHUNK, E), lambda i: (i, 0))
    return pl.pallas_call(
        body, name="gm_sg_bwd", grid=(T // GM_CHUNK,),
        in_specs=[blk, blk, blk, pl.BlockSpec(ws.shape, lambda i: (0, 0, 0)),
                  pl.BlockSpec(bs_col.shape, lambda i: (0, 0, 0))],
        out_specs=[blk, blk, pl.BlockSpec(ws.shape, lambda i: (0, 0, 0)),
                   pl.BlockSpec((GM_HEADS, GM_CHUNK, LANES), lambda i: (0, 0, 0))],
        out_shape=[jax.ShapeDtypeStruct((T, E), F32), jax.ShapeDtypeStruct((T, E), F32),
                   jax.ShapeDtypeStruct(ws.shape, F32), jax.ShapeDtypeStruct((GM_HEADS, GM_CHUNK, LANES), F32)],
        compiler_params=_cparams(("arbitrary",)),
    )(dus, u, v, ws, bs_col)


def _t5_bucket_steps(dilation):
    max_exact = NUM_BUCKETS // 2
    delta = np.arange(BLOCK + 1)
    dist = delta * dilation
    distf = np.maximum(dist, 1).astype(np.float32)
    large = max_exact + (np.log(distf / np.float32(max_exact)) / np.float32(math.log(MAX_DISTANCE / max_exact))
                         * np.float32(NUM_BUCKETS - max_exact)).astype(np.int32)
    large = np.minimum(large, NUM_BUCKETS - 1)
    bucket = np.where(dist < max_exact, dist, large)
    steps = []
    for d in range(1, BLOCK + 1):
        inc = int(bucket[d] - bucket[d - 1])
        assert inc >= 0
        if inc:
            steps.append((d, inc))
    assert int(bucket[0]) == 0
    return steps


def _bucket_map(dilation):
    qi = lax.broadcasted_iota(jnp.int32, (BLOCK, 2 * BLOCK), 0)
    ki = lax.broadcasted_iota(jnp.int32, (BLOCK, 2 * BLOCK), 1)
    delta = qi + BLOCK - ki
    bm = jnp.zeros((BLOCK, 2 * BLOCK), jnp.int32)
    for thr, inc in _t5_bucket_steps(dilation):
        bm = bm + jnp.where(delta >= thr, inc, 0)
    return bm


def _at_bias(table, g, dilation):
    H = AT_HEADS

    def body(t_ref, o_ref):
        bm = _bucket_map(dilation)
        for h in range(H):
            acc = jnp.zeros((BLOCK, 2 * BLOCK), F32)
            for b in range(NUM_BUCKETS):
                acc = jnp.where(bm == b, t_ref[b, g * H + h], acc)
            o_ref[h] = acc

    return pl.pallas_call(body, name="at_bias", in_specs=[pl.BlockSpec(memory_space=pltpu.SMEM)],
                          out_shape=jax.ShapeDtypeStruct((H, BLOCK, 2 * BLOCK), F32))(table)


def _at_bias_bwd(dbias, dilation):
    H = AT_HEADS

    def body(d_ref, o_ref):
        bm = _bucket_map(dilation)
        for h in range(H):
            d = d_ref[h]
            for b in range(NUM_BUCKETS):
                o_ref[b, h] = jnp.sum(jnp.where(bm == b, d, 0.0))

    return pl.pallas_call(body, name="at_bias_bwd", out_specs=pl.BlockSpec(memory_space=pltpu.SMEM),
                          out_shape=jax.ShapeDtypeStruct((NUM_BUCKETS, H), F32))(dbias)


def _at_mask(i, nbs):
    qi = lax.broadcasted_iota(jnp.int32, (BLOCK, 2 * BLOCK), 0)
    ki = lax.broadcasted_iota(jnp.int32, (BLOCK, 2 * BLOCK), 1)
    no_prev = jnp.where(i % nbs == 0, 4 * BLOCK, 0)
    return ((ki < BLOCK) & (ki >= qi + no_prev)) | ((ki >= BLOCK) & (ki - BLOCK <= qi))


def _head_lanes():
    lane = lax.broadcasted_iota(jnp.int32, (BLOCK, LANES), 1)
    return [lane < HEAD_DIM, lane >= HEAD_DIM]


def _at_fwd(qkv, bias, nbs, cb):
    T = qkv.shape[0]
    D = D_MODEL
    npair = D // LANES
    scale = HEAD_DIM ** -0.5

    def body(q_ref, kc_ref, kp_ref, vc_ref, vp_ref, b_ref, o_ref, l_ref):
        i = pl.program_id(0)
        mask = _at_mask(i, nbs)
        sel = _head_lanes()
        for j in range(npair):
            cols = slice(j * LANES, (j + 1) * LANES)
            q = q_ref[:, cols]
            kk = jnp.concatenate([kp_ref[:, cols], kc_ref[:, cols]], axis=0)
            vv = jnp.concatenate([vp_ref[:, cols], vc_ref[:, cols]], axis=0)
            o_pair = jnp.zeros((BLOCK, LANES), F32)
            l_pair = jnp.zeros((BLOCK, LANES), F32)
            for e in range(2):
                qh = jnp.where(sel[e], q, jnp.zeros_like(q))
                s = lax.dot_general(qh, kk, (((1,), (1,)), ((), ())), preferred_element_type=F32) * scale
                s = jnp.where(mask, s + b_ref[2 * j + e], -1e30)
                m = jnp.max(s, axis=1, keepdims=True)
                p = jnp.exp(s - m)
                den = jnp.sum(p, axis=1, keepdims=True)
                o = jnp.dot(p.astype(BF16), vv, preferred_element_type=F32) / den
                o_pair = jnp.where(sel[e], o, o_pair)
                l_pair = jnp.where(sel[e], m + jnp.log(den), l_pair)
            o_ref[:, cols] = o_pair
            l_ref[:, cols] = l_pair

    blk = lambda c, prev: pl.BlockSpec((BLOCK, D), (lambda i: (jnp.maximum(i - 1, 0), cb + c)) if prev
                                       else (lambda i: (i, cb + c)))
    out = pl.BlockSpec((BLOCK, D), lambda i: (i, 0))
    return pl.pallas_call(
        body, name="at_fwd", grid=(T // BLOCK,),
        in_specs=[blk(0, False), blk(1, False), blk(1, True), blk(2, False), blk(2, True),
                  pl.BlockSpec(bias.shape, lambda i: (0, 0, 0))],
        out_specs=[out, out], out_shape=[jax.ShapeDtypeStruct((T, D), F32)] * 2,
        compiler_params=_cparams(("parallel",)),
    )(qkv, qkv, qkv, qkv, qkv, bias)


def _at_bwd(qkv, bias, o, lse, do, dlse, nbs, cb):
    T = qkv.shape[0]
    D = D_MODEL
    nblk = T // BLOCK
    npair = D // LANES
    scale = HEAD_DIM ** -0.5

    def body(q_ref, kc_ref, kp_ref, vc_ref, vp_ref, b_ref, o_ref, l_ref, do_ref, dl_ref, dqkv_ref, db_ref, carry):
        i = pl.program_id(0)

        @pl.when(i == 0)
        def _():
            carry[...] = jnp.zeros_like(carry)
            db_ref[...] = jnp.zeros_like(db_ref)

        @pl.when(i == nblk)
        def _():
            dqkv_ref[...] = carry[...].astype(dqkv_ref.dtype)

        @pl.when(i < nblk)
        def _():
            mask = _at_mask(i, nbs)
            sel = _head_lanes()
            for j in range(npair):
                cols = slice(j * LANES, (j + 1) * LANES)
                kcols = slice(D + j * LANES, D + (j + 1) * LANES)
                vcols = slice(2 * D + j * LANES, 2 * D + (j + 1) * LANES)
                q = q_ref[:, cols]
                kk = jnp.concatenate([kp_ref[:, cols], kc_ref[:, cols]], axis=0)
                vv = jnp.concatenate([vp_ref[:, cols], vc_ref[:, cols]], axis=0)
                dov = do_ref[:, cols]
                dob = dov.astype(BF16)
                oo = dov * o_ref[:, cols]
                lv = l_ref[:, cols]
                dlv = dl_ref[:, cols]
                dq_pair = jnp.zeros((BLOCK, LANES), F32)
                dk_pair = jnp.zeros((2 * BLOCK, LANES), F32)
                dv_pair = jnp.zeros((2 * BLOCK, LANES), F32)
                sel2 = [jnp.concatenate([s_, s_], axis=0) for s_ in sel]
                for e in range(2):
                    qh = jnp.where(sel[e], q, jnp.zeros_like(q))
                    s = lax.dot_general(qh, kk, (((1,), (1,)), ((), ())), preferred_element_type=F32) * scale
                    s = jnp.where(mask, s + b_ref[2 * j + e], -1e30)
                    lse_h = jnp.max(jnp.where(sel[e], lv, -jnp.inf), axis=1, keepdims=True)
                    p = jnp.exp(s - lse_h)
                    doh = jnp.where(sel[e], dob, jnp.zeros_like(dob))
                    dp = lax.dot_general(doh, vv, (((1,), (1,)), ((), ())), preferred_element_type=F32)
                    delta = jnp.sum(jnp.where(sel[e], oo, 0.0), axis=1, keepdims=True)
                    dlse_h = jnp.sum(jnp.where(sel[e], dlv, 0.0), axis=1, keepdims=True)
                    ds = p * (dp - delta + dlse_h)
                    db_ref[2 * j + e] += ds
                    dsb = (ds * scale).astype(BF16)
                    dq_pair = jnp.where(sel[e], jnp.dot(dsb, kk, preferred_element_type=F32), dq_pair)
                    dk = lax.dot_general(dsb, q, (((0,), (0,)), ((), ())), preferred_element_type=F32)
                    dk_pair = jnp.where(sel2[e], dk, dk_pair)
                    dv = lax.dot_general(p.astype(BF16), dob, (((0,), (0,)), ((), ())), preferred_element_type=F32)
                    dv_pair = jnp.where(sel2[e], dv, dv_pair)
                dqkv_ref[:, cols] = carry[:, cols].astype(dqkv_ref.dtype)
                dqkv_ref[:, kcols] = (carry[:, kcols] + dk_pair[:BLOCK]).astype(dqkv_ref.dtype)
                dqkv_ref[:, vcols] = (carry[:, vcols] + dv_pair[:BLOCK]).astype(dqkv_ref.dtype)
                carry[:, cols] = dq_pair
                carry[:, kcols] = dk_pair[BLOCK:]
                carry[:, vcols] = dv_pair[BLOCK:]

    cur = lambda i: jnp.minimum(i, nblk - 1)
    prev = lambda i: jnp.maximum(jnp.minimum(i, nblk - 1) - 1, 0)
    blk = lambda c, pv: pl.BlockSpec((BLOCK, D), (lambda i: (prev(i), cb + c)) if pv else (lambda i: (cur(i), cb + c)))
    row = pl.BlockSpec((BLOCK, D), lambda i: (cur(i), 0))
    return pl.pallas_call(
        body, name="at_bwd", grid=(nblk + 1,),
        in_specs=[blk(0, False), blk(1, False), blk(1, True), blk(2, False), blk(2, True),
                  pl.BlockSpec(bias.shape, lambda i: (0, 0, 0)), row, row, row, row],
        out_specs=[pl.BlockSpec((BLOCK, 3 * D), lambda i: (jnp.maximum(i - 1, 0), 0)),
                   pl.BlockSpec(bias.shape, lambda i: (0, 0, 0))],
        out_shape=[jax.ShapeDtypeStruct((T, 3 * D), BF16), jax.ShapeDtypeStruct(bias.shape, F32)],
        scratch_shapes=[pltpu.VMEM((BLOCK, 3 * D), F32)],
        compiler_params=_cparams(("arbitrary",)),
    )(qkv, qkv, qkv, qkv, qkv, bias, o, lse, do, dlse)


def _to_residue_major(a, d):
    if d == 1:
        return a
    T, C = a.shape
    return a.reshape(T // d, d, C).transpose(1, 0, 2).reshape(T, C)


def _from_residue_major(a, d):
    if d == 1:
        return a
    T, C = a.shape
    return a.reshape(d, T // d, C).transpose(1, 0, 2).reshape(T, C)


FFN_TM = 512


def _ffn_up(h, w1t, w3t, carry=None):
    T, Dm = h.shape
    Fw = w1t.shape[0]
    tm, tn = FFN_TM, _pick(Fw, 1408)
    grid = (Fw // tn, T // tm)
    nt = (((1,), (1,)), ((), ()))

    def body(*refs):
        (h_ref, w1_ref, w3_ref), (a_ref, b_ref, u_ref), _, begin, end = _carry_hooks(carry, refs, 3, 3, 2, grid)
        begin()
        hv = h_ref[...]
        a = lax.dot_general(hv, w1_ref[...], nt, preferred_element_type=F32)
        b = lax.dot_general(hv, w3_ref[...], nt, preferred_element_type=F32)
        a_ref[...] = a.astype(a_ref.dtype)
        b_ref[...] = b.astype(b_ref.dtype)
        u_ref[...] = (jax.nn.silu(a) * b).astype(u_ref.dtype)
        end()

    extra = carry if carry is not None else _NO_CARRY
    wspec = pl.BlockSpec((tn, Dm), lambda j, i: (j, 0))
    ospec = pl.BlockSpec((tm, tn), lambda j, i: (i, j))
    res = pl.pallas_call(
        body, name="ffn_up", grid=grid,
        in_specs=[pl.BlockSpec((tm, Dm), lambda j, i: (i, 0)), wspec, wspec] + extra.in_specs,
        out_specs=[ospec] * 3 + extra.out_specs,
        out_shape=[jax.ShapeDtypeStruct((T, Fw), BF16)] * 3 + extra.out_shape, scratch_shapes=extra.scratch,
        compiler_params=_cparams(("arbitrary",) * 2 if carry is not None else ("parallel", "parallel")),
    )(h, w1t, w3t, *extra.arrays)
    return res[:3], res[3:]


def _ffn_down_dx(do, w2, a, b, carry=None):
    T, Dm = do.shape
    Fw = w2.shape[0]
    tm, tn = FFN_TM, _pick(Fw, 1408)
    grid = (Fw // tn, T // tm)

    def body(*refs):
        (do_ref, w2_ref, a_ref, b_ref), (da_ref, db_ref), _, begin, end = _carry_hooks(carry, refs, 4, 2, 2, grid)
        begin()
        du = lax.dot_general(do_ref[...], w2_ref[...], (((1,), (1,)), ((), ())), preferred_element_type=F32)
        av = a_ref[...].astype(F32)
        bv = b_ref[...].astype(F32)
        sg = jax.nn.sigmoid(av)
        silu = av * sg
        da_ref[...] = (du * bv * (sg + silu * (1.0 - sg))).astype(da_ref.dtype)
        db_ref[...] = (du * silu).astype(db_ref.dtype)
        end()

    extra = carry if carry is not None else _NO_CARRY
    ospec = pl.BlockSpec((tm, tn), lambda j, i: (i, j))
    res = pl.pallas_call(
        body, name="ffn_down_dx", grid=grid,
        in_specs=[pl.BlockSpec((tm, Dm), lambda j, i: (i, 0)), pl.BlockSpec((tn, Dm), lambda j, i: (j, 0)), ospec, ospec]
        + extra.in_specs,
        out_specs=[ospec] * 2 + extra.out_specs,
        out_shape=[jax.ShapeDtypeStruct((T, Fw), BF16)] * 2 + extra.out_shape, scratch_shapes=extra.scratch,
        compiler_params=_cparams(("arbitrary",) * 2 if carry is not None else ("parallel", "parallel")),
    )(do, w2, a, b, *extra.arrays)
    return res[:2], res[2:]


def _ffn_fwd(x, p, carry_up=None, carry_down=None):
    h, = _rows(_f_pre, [x], [p["g_pre"]], [(D_MODEL, BF16)], name="ffn_pre")
    (a, b, u), got_up = _ffn_up(h, p["w1"], p["w3"], carry_up)
    o = _mm(u, p["w2"], name="ffn_down", carry=carry_down)
    got_down = None
    if carry_down is not None:
        o, got_down = o
    xo, = _rows(_f_post(0.5, False), [x, o], [p["g_post"]], [(D_MODEL, F32)], name="ffn_post")
    return xo, (x, h, a, b, u, o), got_up, got_down


def _ffn_bwd(saved, p, dxo, carry_a=None, carry_b=None):
    x, h, a, b, u, o = saved
    (do,), (dg_post,) = _rows_vjp(_f_post_term(0.5, False), [o], [p["g_post"]], [dxo], dtypes=[BF16], name="ffn_post_bwd")
    (da, db), got_a = _ffn_down_dx(do, p["w2"], a, b, carry_a)
    dw2 = _mm(u, do, ta=True, out_dtype=BF16, name="ffn_down_dw")
    dh = _mm((da, db), (p["w1"], p["w3"]), name="ffn_up_dx", carry=carry_b)
    got_b = None
    if carry_b is not None:
        dh, got_b = dh
    dw1 = _mm(da, h, ta=True, out_dtype=BF16, name="ffn_up_dw")
    dw3 = _mm(db, h, ta=True, out_dtype=BF16, name="ffn_up_dw")
    (dx,), (dg_pre,) = _rows_vjp(_f_pre, [x], [p["g_pre"]], [dh], dtypes=[F32], adds={0: dxo}, name="ffn_pre_bwd")
    return dx, {"w1": dw1, "w3": dw3, "w2": dw2, "g_pre": dg_pre, "g_post": dg_post}, got_a, got_b


def _expand_blocks(w, rows_first):
    w = w.reshape(S5_NB, S5_GB, S5_GROUP, S5_STATE)
    eye = jnp.eye(S5_GB, dtype=F32)
    if rows_first:
        e = w[:, :, :, None, :] * eye[None, :, None, :, None]
        return e.reshape(S5_NB, S5_GB * S5_GROUP, S5_BW)
    e = jnp.transpose(w, (0, 1, 3, 2))[:, :, :, None, :] * eye[None, :, None, :, None]
    return e.reshape(S5_NB, S5_BW, S5_GB * S5_GROUP)


def _extract_blocks(e, rows_first):
    eye = jnp.eye(S5_GB, dtype=F32)
    if rows_first:
        e = e.reshape(S5_NB, S5_GB, S5_GROUP, S5_GB, S5_STATE)
        w = jnp.sum(e * eye[None, :, None, :, None], axis=3)
    else:
        e = e.reshape(S5_NB, S5_GB, S5_STATE, S5_GB, S5_GROUP)
        w = jnp.transpose(jnp.sum(e * eye[None, :, None, :, None], axis=3), (0, 1, 3, 2))
    return w.reshape(S5_GROUPS, S5_GROUP, S5_STATE)


def _s5_prep(p):
    G, P, HG = S5_GROUPS, S5_STATE, S5_GROUP
    args = [p["a_re"].reshape(G, 1, P), p["a_im"].reshape(G, 1, P), p["log_dt"].reshape(G, 1, 1),
            jnp.transpose(p["b_re"], (0, 2, 1)), jnp.transpose(p["b_im"], (0, 2, 1))]
    abr, abi, bbr, bbi = _small(_f_s5_disc, args, [(G, 1, P)] * 2 + [(G, HG, P)] * 2, name="s5_disc")
    return args, abr.reshape(1, G * P), abi.reshape(1, G * P), bbr, bbi


def _s5_fwd(h, p):
    disc_args, abr, abi, bbr, bbi = _s5_prep(p)
    c_re, c_im = p["c_re"], p["c_im"]
    u = _mm(h, p["w_in"], name="s5_in")
    bur = _bd([u], [_expand_blocks(bbr, True)], name="s5_bu")
    bui = _bd([u], [_expand_blocks(bbi, True)], name="s5_bu")
    sr, si = _scan_fwd(bur, bui, abr, abi)
    ylin = _bd([sr, si], [_expand_blocks(c_re, False), _expand_blocks(-c_im, False)], name="s5_y")
    y, = _rows(_f_s5_gelu, [ylin, u], [p["d"]], [(D_MODEL, F32)], name="s5_gelu")
    gl = _mm(y, p["w_glu"], name="s5_glu_mm")
    z, = _rows(_f_s5_glu, [y, gl], [p["b_glu"]], [(D_MODEL, BF16)], name="s5_glu")
    m = _mm(z, p["w_out"], name="s5_out")
    return m, None, (h, disc_args, abr, abi, bbr, bbi, u, sr, si, ylin, y, gl, z)


def _s5_bwd(saved, p, dm):
    h, disc_args, abr, abi, bbr, bbi, u, sr, si, ylin, y, gl, z = saved
    c_re, c_im = p["c_re"], p["c_im"]
    dz = _mm(dm, p["w_out"], tb=True, name="s5_out_dx")
    dw_out = _mm(z, dm, ta=True, out_dtype=BF16, name="s5_out_dw")
    (dy1, dgl), (db_glu,) = _rows_vjp(_f_s5_glu, [y, gl], [p["b_glu"]], [dz], dtypes=[F32, BF16], name="s5_glu_bwd")
    dy2 = _mm(dgl, p["w_glu"], tb=True, name="s5_glu_dx")
    dw_glu = _mm(y, dgl, ta=True, out_dtype=BF16, name="s5_glu_dw")
    (dylin, du1), (dd,) = _rows_vjp(_f_s5_gelu, [ylin, u], [p["d"]], [(dy1, dy2)], dtypes=[F32, F32], name="s5_gelu_bwd")
    gr = _bd([dylin], [jnp.transpose(_expand_blocks(c_re, False), (0, 2, 1))], name="s5_y_dx")
    gi = _bd([dylin], [jnp.transpose(_expand_blocks(-c_im, False), (0, 2, 1))], name="s5_y_dx")
    dc_re = _extract_blocks(_bd_wgrad(sr, dylin, S5_BW, LANES, name="s5_y_dw"), False)
    dc_im = -_extract_blocks(_bd_wgrad(si, dylin, S5_BW, LANES, name="s5_y_dw"), False)
    lr, li, dabr, dabi = _scan_bwd(gr, gi, sr, si, abr, abi)
    du = _bd([lr, li], [jnp.transpose(_expand_blocks(bbr, True), (0, 2, 1)),
                        jnp.transpose(_expand_blocks(bbi, True), (0, 2, 1))], add=du1, out_dtype=BF16, name="s5_bu_dx")
    dbbr = _extract_blocks(_bd_wgrad(u, lr, LANES, S5_BW, name="s5_bu_dw"), True)
    dbbi = _extract_blocks(_bd_wgrad(u, li, LANES, S5_BW, name="s5_bu_dw"), True)
    G, P = S5_GROUPS, S5_STATE
    dar, dai, dldt, dbr, dbi = _small_vjp(_f_s5_disc, disc_args,
                                          [dabr.reshape(G, 1, P), dabi.reshape(G, 1, P), dbbr, dbbi], name="s5_disc_bwd")
    dh = _mm(du, p["w_in"], tb=True, name="s5_in_dx")
    dw_in = _mm(h, du, ta=True, out_dtype=BF16, name="s5_in_dw")
    grads = {"w_in": dw_in, "w_glu": dw_glu, "w_out": dw_out, "b_glu": db_glu, "d": dd,
             "a_re": dar.reshape(G, P), "a_im": dai.reshape(G, P), "log_dt": dldt.reshape(G),
             "b_re": jnp.transpose(dbr, (0, 2, 1)), "b_im": jnp.transpose(dbi, (0, 2, 1)),
             "c_re": dc_re, "c_im": dc_im}
    return dh, grads


def _cv_fwd(h, p):
    z0 = _mm(h, p["w_in"], tb=True, name="cv_in")
    zg, = _rows(_f_cv_glu, [z0], [p["b_in"]], [(D_MODEL, F32)], name="cv_glu")
    zc = _conv_fwd(zg, p["dw"], p["dw_b"])
    zl, = _rows(_f_cv_ln, [zc], [p["ln_g"], p["ln_b"]], [(D_MODEL, BF16)], name="cv_ln")
    m = _mm(zl, p["w_out"], name="cv_out")
    return m, p["b_out"], (h, z0, zg, zc, zl)


def _cv_bwd(saved, p, dm):
    h, z0, zg, zc, zl = saved
    dzl = _mm(dm, p["w_out"], tb=True, name="cv_out_dx")
    dw_out = _mm(zl, dm, ta=True, out_dtype=BF16, name="cv_out_dw")
    (dzc,), (dln_g, dln_b) = _rows_vjp(_f_cv_ln, [zc], [p["ln_g"], p["ln_b"]], [dzl], dtypes=[F32], name="cv_ln_bwd")
    dzg, ddw, ddw_b = _conv_bwd(dzc, zg, p["dw"])
    (dz0,), (db_in,) = _rows_vjp(_f_cv_glu, [z0], [p["b_in"]], [dzg], dtypes=[BF16], name="cv_glu_bwd")
    dh = _mm(dz0, p["w_in"], name="cv_in_dx")
    dw_in = _mm(dz0, h, ta=True, out_dtype=BF16, name="cv_in_dw")
    return dh, {"w_in": dw_in, "b_in": db_in, "dw": ddw, "dw_b": ddw_b, "ln_g": dln_g, "ln_b": dln_b, "w_out": dw_out}


def _gm_fwd(h, p):
    z0 = _mm(h, p["w_in"], tb=True, name="gm_in")
    u, v = _rows(_f_gm_in, [z0], [p["b_in"], p["ln_g"], p["ln_b"]], [(GM_E, F32), (GM_E, BF16)], name="gm_act")
    bs_col = p["b_s"].reshape(GM_HEADS, GM_CHUNK, 1)
    us = _gm_sg_fwd(u, v, p["w_s"], bs_col)
    m = _mm(us, p["w_out"], name="gm_out")
    return m, p["b_out"], (h, z0, u, v, us, bs_col)


def _gm_bwd(saved, p, dm):
    h, z0, u, v, us, bs_col = saved
    dus = _mm(dm, p["w_out"], tb=True, name="gm_out_dx")
    dw_out = _mm(us, dm, ta=True, out_dtype=BF16, name="gm_out_dw")
    du, dv, dw_s, db_s = _gm_sg_bwd(dus, u, v, p["w_s"], bs_col)
    (dz0,), (db_in, dln_g, dln_b) = _rows_vjp(_f_gm_in, [z0], [p["b_in"], p["ln_g"], p["ln_b"]], [du, dv],
                                              dtypes=[BF16], name="gm_act_bwd")
    dh = _mm(dz0, p["w_in"], name="gm_in_dx")
    dw_in = _mm(dz0, h, ta=True, out_dtype=BF16, name="gm_in_dw")
    return dh, {"w_in": dw_in, "b_in": db_in, "ln_g": dln_g, "ln_b": dln_b, "w_s": dw_s, "b_s": db_s[:, :, 0],
                "w_out": dw_out}


def _at_fwd_mixer(h, p):
    T = h.shape[0]
    D = D_MODEL
    qkv = _mm(h, p["w_qkv"], tb=True, out_dtype=BF16, name="at_qkv")
    res, outs, lses, biases = [], [], [], []
    for g, (window, d) in enumerate(PATTERNS):
        assert window // d == BLOCK and T % (BLOCK * d) == 0
        bias = _at_bias(p["rel_bias"], g, d)
        if d == 1:
            r, cb = qkv, 3 * g
        else:
            r, cb = _to_residue_major(qkv[:, g * 3 * D:(g + 1) * 3 * D], d), 0
        o, lse = _at_fwd(r, bias, T // d // BLOCK, cb)
        res.append((r, cb, o, lse))
        biases.append(bias)
        outs.append(_from_residue_major(o, d))
        lses.append(_from_residue_major(lse, d))
    oc, = _rows(_f_at_combine, outs + lses, [], [(D, BF16)], name="at_combine")
    m = _mm(oc, p["w_out"], name="at_out")
    return m, None, (h, res, biases, outs, lses, oc)


def _at_bwd_mixer(saved, p, dm):
    h, res, biases, outs, lses, oc = saved
    T = h.shape[0]
    doc = _mm(dm, p["w_out"], tb=True, name="at_out_dx")
    dw_out = _mm(oc, dm, ta=True, out_dtype=BF16, name="at_out_dw")
    dol, _ = _rows_vjp(_f_at_combine, outs + lses, [], [doc], dtypes=[F32] * 6, name="at_combine_bwd")
    dqkv, dtab = [], []
    for g, (window, d) in enumerate(PATTERNS):
        r, cb, o_res, lse_res = res[g]
        dq, dbias = _at_bwd(r, biases[g], o_res, lse_res, _to_residue_major(dol[g], d),
                            _to_residue_major(dol[3 + g], d), T // d // BLOCK, cb)
        dqkv.append(_from_residue_major(dq, d))
        dtab.append(_at_bias_bwd(dbias, d))
    dqkv = jnp.concatenate(dqkv, axis=1)
    dh = _mm(dqkv, p["w_qkv"], name="at_qkv_dx")
    dw_qkv = _mm(dqkv, h, ta=True, out_dtype=BF16, name="at_qkv_dw")
    return dh, {"w_qkv": dw_qkv, "w_out": dw_out, "rel_bias": jnp.concatenate(dtab, axis=1)}


_MIXERS = ((_s5_fwd, _s5_bwd), (_cv_fwd, _cv_bwd), (_gm_fwd, _gm_bwd), (_at_fwd_mixer, _at_bwd_mixer))


def _mixer_fwd(x, p, kind):
    h, = _rows(_f_pre, [x], [p["g_pre"]], [(D_MODEL, BF16)], name="mix_pre")
    m, bias, saved = _MIXERS[kind][0](h, p)
    extra = [] if bias is None else [bias]
    xo, = _rows(_f_post(1.0, bias is not None), [x, m], [p["g_post"]] + extra, [(D_MODEL, F32)], name="mix_post")
    return xo, (x, m, bias, saved)


def _mixer_bwd(saved_all, p, kind, dxo):
    x, m, bias, saved = saved_all
    extra = [] if bias is None else [bias]
    (dm,), dpars = _rows_vjp(_f_post_term(1.0, bias is not None), [m], [p["g_post"]] + extra, [dxo], dtypes=[BF16],
                             name="mix_post_bwd")
    dh, grads = _MIXERS[kind][1](saved, p, dm)
    (dx,), (dg_pre,) = _rows_vjp(_f_pre, [x], [p["g_pre"]], [dh], dtypes=[F32], adds={0: dxo}, name="mix_pre_bwd")
    grads["g_pre"] = dg_pre
    grads["g_post"] = dpars[0]
    if bias is not None:
        grads["b_out"] = dpars[1]
    return dx, grads


class _Carry:
    def __init__(self, arrays, kinds):
        self.arrays, self.kinds, self.n = list(arrays), list(kinds), len(arrays)
        hbm = pl.BlockSpec(memory_space=pl.ANY)
        self.in_specs = [hbm] * self.n
        self.out_specs = [hbm] * self.n
        self.out_shape = [jax.ShapeDtypeStruct((N_DEV,) + (a.shape[1:] if k == "a2a" else a.shape), a.dtype)
                          for a, k in zip(arrays, kinds)]
        self.scratch = [pltpu.SemaphoreType.DMA((self.n * (N_DEV - 1),)), pltpu.SemaphoreType.DMA((self.n * (N_DEV - 1),)),
                        pltpu.SemaphoreType.DMA((self.n,))]

    def _copies(self, ins, outs, sems, arrivals):
        send_sems, recv_sems, local_sems = sems
        x, y, c = lax.axis_index("x"), lax.axis_index("y"), lax.axis_index("c")
        me = 4 * x + 2 * y + c
        local, remote = [], []
        for a in range(self.n):
            a2a = self.kinds[a] == "a2a"
            if not arrivals:
                local.append(pltpu.make_async_copy(ins[a].at[me] if a2a else ins[a], outs[a].at[me], local_sems.at[a]))
            for k in range(1, N_DEV):
                px = 1 - x if k & 4 else x
                py = 1 - y if k & 2 else y
                pc = 1 - c if k & 1 else c
                peer = 4 * px + 2 * py + pc
                idx = a * (N_DEV - 1) + k - 1
                remote.append(pltpu.make_async_remote_copy(
                    src_ref=ins[a].at[peer] if a2a else ins[a], dst_ref=outs[a].at[peer if arrivals else me],
                    send_sem=send_sems.at[idx], recv_sem=recv_sems.at[idx], device_id=(px, py, pc),
                    device_id_type=pl.DeviceIdType.MESH))
        return local, remote

    def start(self, ins, outs, sems):
        local, sends = self._copies(ins, outs, sems, False)
        for cp in local + sends:
            cp.start()

    def wait(self, ins, outs, sems):
        local, sends = self._copies(ins, outs, sems, False)
        _, recvs = self._copies(ins, outs, sems, True)
        for cp in sends:
            cp.wait_send()
        for cp in recvs:
            cp.wait_recv()
        for cp in local:
            cp.wait()


class _NoCarry:
    n = 0
    arrays = in_specs = out_specs = out_shape = scratch = []


_NO_CARRY = _NoCarry()


def _carry_hooks(carry, refs, n_in, n_out, grid_rank, grid):
    nc = carry.n if carry is not None else 0
    ins, cin = refs[:n_in], refs[n_in:n_in + nc]
    outs, cout = refs[n_in + nc:n_in + nc + n_out], refs[n_in + nc + n_out:n_in + 2 * nc + n_out]
    rest = refs[n_in + 2 * nc + n_out:]
    scratch, sems = (rest[:len(rest) - 3], rest[len(rest) - 3:]) if nc else (rest, ())

    def at(step_of):
        cond = None
        for ax in range(grid_rank):
            c = pl.program_id(ax) == step_of(ax)
            cond = c if cond is None else cond & c
        return cond

    def begin():
        if nc:
            @pl.when(at(lambda ax: 0))
            def _():
                carry.start(cin, cout, sems)

    def end():
        if nc:
            @pl.when(at(lambda ax: grid[ax] - 1))
            def _():
                carry.wait(cin, cout, sems)

    return ins, outs, scratch, begin, end


def _exchange(arrays, kinds, *, name):
    carry = _Carry(arrays, kinds)

    def body(*refs):
        n = carry.n
        carry.start(refs[:n], refs[n:2 * n], refs[2 * n:])
        carry.wait(refs[:n], refs[n:2 * n], refs[2 * n:])

    return pl.pallas_call(body, name=name, in_specs=carry.in_specs, out_specs=carry.out_specs,
                          out_shape=carry.out_shape, scratch_shapes=carry.scratch)(*arrays)


def _adam(recv, w, m, v, *, name):
    R, C = w.shape
    tr = _pick_rows(R, 128)
    c1 = 1.0 - ADAM_B1 ** ADAM_STEP
    c2 = 1.0 - ADAM_B2 ** ADAM_STEP

    def body(r_ref, w_ref, m_ref, v_ref, g_ref, d_ref, nm_ref, nv_ref):
        g = r_ref[0].astype(F32)
        for q in range(1, N_DEV):
            g = g + r_ref[q].astype(F32)
        mm = ADAM_B1 * m_ref[...] + (1.0 - ADAM_B1) * g
        vv = ADAM_B2 * v_ref[...] + (1.0 - ADAM_B2) * jnp.square(g)
        m_hat = mm / c1
        v_hat = vv / c2
        g_ref[...] = g
        d_ref[...] = -ADAM_LR * (m_hat / (jnp.sqrt(v_hat) + ADAM_EPS) + ADAM_WD * w_ref[...])
        nm_ref[...] = mm
        nv_ref[...] = vv

    blk = pl.BlockSpec((tr, C), lambda i: (i, 0))
    return pl.pallas_call(
        body, name=name, grid=(R // tr,),
        in_specs=[pl.BlockSpec((N_DEV, tr, C), lambda i: (0, i, 0)), blk, blk, blk], out_specs=[blk] * 4,
        out_shape=[jax.ShapeDtypeStruct((R, C), F32)] * 4,
        compiler_params=_cparams(("parallel",)),
    )(recv, w, m, v)


PACK_COLS = 1024


def _padded(n):
    return -(-n // PACK_ALIGN) * PACK_ALIGN


def _pack_flat(pieces):
    flat = jnp.concatenate([p.reshape(-1) for p in pieces])
    n = flat.shape[0]
    return jnp.pad(flat, (0, _padded(n) - n)).reshape(-1, PACK_COLS)


def _shard_shape(shape, axis):
    s = list(shape)
    assert s[axis] % N_DEV == 0
    s[axis] //= N_DEV
    return tuple(s)


def _split_full(full, axis):
    s = full.shape
    r = full.reshape(s[:axis] + (N_DEV, s[axis] // N_DEV) + s[axis + 1:])
    return jnp.moveaxis(r, axis, 0)


def _merge_full(parts, axis):
    r = jnp.moveaxis(parts, 0, axis)
    s = r.shape
    return r.reshape(s[:axis] + (s[axis] * s[axis + 1],) + s[axis + 2:])


def _pack_full(entries, grads):
    flat = jnp.concatenate([_split_full(grads[k].reshape(shape), axis).reshape(N_DEV, -1)
                            for k, shape, axis in entries], axis=1)
    n = flat.shape[1]
    return jnp.pad(flat, ((0, 0), (0, _padded(n) - n))).reshape(N_DEV, -1, PACK_COLS)


def _unpack_gathered(entries, buf):
    flat = buf.reshape(N_DEV, -1)
    out, pos = {}, 0
    for k, shape, axis in entries:
        ss = _shard_shape(shape, axis)
        n = int(np.prod(ss))
        out[k] = _merge_full(flat[:, pos:pos + n].reshape((N_DEV,) + ss), axis)
        pos += n
    return out


def _unpack_shard(entries, buf):
    flat = buf.reshape(-1)
    out, pos = {}, 0
    for k, shape, axis in entries:
        ss = _shard_shape(shape, axis)
        n = int(np.prod(ss))
        out[k] = flat[pos:pos + n].reshape(ss)
        pos += n
    return out


def _unpack_flat(entries, buf):
    flat = buf.reshape(-1)
    out, pos = {}, 0
    for k, shape in entries:
        n = int(np.prod(shape))
        out[k] = flat[pos:pos + n].reshape(shape)
        pos += n
    return out


D, FF = D_MODEL, D_FF
_FFN_MATS = (("w1", (D, FF), 1), ("w3", (D, FF), 1), ("w2", (FF, D), 0))
_NORM_VECS = (("g_pre", (D,), 0), ("g_post", (D,), 0))
_MIX_MATS = (
    (("w_in", (D, D), 0), ("w_glu", (D, D), 0), ("w_out", (D, D), 0)),
    (("w_in", (D, 2 * D), 1), ("w_out", (D, D), 0)),
    (("w_in", (D, 2 * GM_E), 1), ("w_out", (GM_E, D), 0)),
    (("w_qkv", (D, 9 * D), 1), ("w_out", (D, D), 0)),
)
_MIX_VECS = (
    (),
    (("b_in", (2 * D,), 0), ("dw", (CONV_W, D), 1), ("dw_b", (D,), 0), ("ln_g", (D,), 0), ("ln_b", (D,), 0),
     ("b_out", (D,), 0)),
    (("b_in", (2 * GM_E,), 0), ("ln_g", (GM_E,), 0), ("ln_b", (GM_E,), 0), ("b_out", (D,), 0)),
    (),
)
_REPLICATED = (
    ("rel_bias", 3, "rel_bias", (NUM_BUCKETS, 3 * AT_HEADS)),
    ("s5_a_re", 0, "a_re", (S5_GROUPS, S5_STATE)), ("s5_a_im", 0, "a_im", (S5_GROUPS, S5_STATE)),
    ("s5_log_dt", 0, "log_dt", (S5_GROUPS,)),
    ("s5_b_re", 0, "b_re", (S5_GROUPS, S5_STATE, S5_GROUP)), ("s5_b_im", 0, "b_im", (S5_GROUPS, S5_STATE, S5_GROUP)),
    ("s5_c_re", 0, "c_re", (S5_GROUPS, S5_GROUP, S5_STATE)), ("s5_c_im", 0, "c_im", (S5_GROUPS, S5_GROUP, S5_STATE)),
    ("s5_d", 0, "d", (D,)), ("s5_b_glu", 0, "b_glu", (D,)),
    ("gm_w_s", 2, "w_s", (GM_HEADS, GM_CHUNK, GM_CHUNK)), ("gm_b_s", 2, "b_s", (GM_HEADS, GM_CHUNK)),
)
_MIX_PREFIX = ("s5_", "cv_", "gm_", "at_")
_TWIN_WEIGHTS = ('norm_pre', 'norm_post', 'ffn_w1', 'ffn_w3', 'ffn_w2', 'rel_bias', 's5_w_in', 's5_a_re', 's5_a_im',
                 's5_log_dt', 's5_b_re', 's5_b_im', 's5_c_re', 's5_c_im', 's5_d', 's5_w_glu', 's5_b_glu', 's5_w_out',
                 'cv_w_in', 'cv_b_in', 'cv_dw', 'cv_dw_b', 'cv_ln_g', 'cv_ln_b', 'cv_w_out', 'cv_b_out', 'gm_w_in',
                 'gm_b_in', 'gm_ln_g', 'gm_ln_b', 'gm_w_s', 'gm_b_s', 'gm_w_out', 'gm_b_out', 'at_w_qkv', 'at_w_out')


def _part_entries(part):
    if part[0] == "ffn":
        return _FFN_MATS, _NORM_VECS
    kind = part[1] % 4
    return _MIX_MATS[kind], _NORM_VECS + _MIX_VECS[kind]


def _part_shards(part, get):
    if part[0] == "ffn":
        _, i, j = part
        n = 0 if j == 0 else 2
        return {"w1": get("ffn_w1")[i, j], "w3": get("ffn_w3")[i, j], "w2": get("ffn_w2")[i, j],
                "g_pre": get("norm_pre")[i, n], "g_post": get("norm_post")[i, n]}
    _, i = part
    kind, j = i % 4, i // 4
    out = {"g_pre": get("norm_pre")[i, 1], "g_post": get("norm_post")[i, 1]}
    for k, _, _ in _MIX_MATS[kind] + _MIX_VECS[kind]:
        out[k] = get(_MIX_PREFIX[kind] + k)[j]
    return out


def _parts():
    parts = []
    for i in range(DEPTH):
        parts += [("ffn", i, 0), ("mix", i), ("ffn", i, 1)]
    return parts


def _as_par(v):
    return v.reshape(1, -1)


def _prepare_part(part, full, rep):
    if part[0] == "ffn":
        return {"w1": full["w1"], "w3": full["w3"], "w2": full["w2"],
                "g_pre": _as_par(full["g_pre"]), "g_post": _as_par(full["g_post"])}
    kind = part[1] % 4
    p = {"g_pre": _as_par(full["g_pre"]), "g_post": _as_par(full["g_post"])}
    for k, _, _ in _MIX_MATS[kind]:
        p[k] = full[k]
    for k, _, _ in _MIX_VECS[kind]:
        p[k] = _as_par(full[k]) if k != "dw" else jnp.pad(full[k], ((0, CONV_HALO - CONV_W), (0, 0)))
    if kind == 0:
        for k in ("a_re", "a_im", "log_dt", "b_re", "b_im"):
            p[k] = rep[k]
        p["c_re"], p["c_im"] = rep["c_re"], rep["c_im"]
        p["d"], p["b_glu"] = _as_par(rep["d"]), _as_par(rep["b_glu"])
    elif kind == 2:
        p["w_s"], p["b_s"] = rep["w_s"], rep["b_s"]
    elif kind == 3:
        p["rel_bias"] = rep["rel_bias"]
    return p


def _finish_grads(part, grads):
    out = dict(grads)
    for k in ("g_pre", "g_post", "b_in", "dw_b", "ln_g", "ln_b", "b_out", "d", "b_glu"):
        if k in out:
            out[k] = out[k].reshape(-1)
    if "dw" in out:
        out["dw"] = out["dw"][:CONV_W]
    return out


def _step(x, tgt, inputs, moments_m, moments_v):
    parts = _parts()
    rep = {}
    for name, kind, key, shape in _REPLICATED:
        rep[key] = inputs[name][0] if name != "rel_bias" else inputs[name]

    def stored(part, get):
        mats, vecs = _part_entries(part)
        sh = _part_shards(part, get)
        return [sh[k].T if axis == 1 else sh[k] for k, _, axis in mats], _pack_flat([sh[k] for k, _, _ in vecs])

    stored_w = [stored(part, lambda n: inputs[n]) for part in parts]

    def gather_of(idx):
        wmats, wv = stored_w[idx]
        return _Carry([w.astype(BF16) for w in wmats] + [wv], ["bcast"] * (len(wmats) + 1))

    def gathered(idx, bufs):
        mats, vecs = _part_entries(parts[idx])
        full = {k: b.reshape(-1, b.shape[-1]) for (k, _, _), b in zip(mats, bufs)}
        full.update(_unpack_gathered(vecs, bufs[-1]))
        return _prepare_part(parts[idx], full, rep)

    params = [None] * len(parts)
    first = gather_of(0)
    params[0] = gathered(0, _exchange(first.arrays, first.kinds, name="gather_first"))
    saved = []
    h = x
    for idx, part in enumerate(parts):
        if part[0] == "ffn":
            ahead = [i for i in (idx + 1, idx + 2) if i < len(parts) and params[i] is None]
            if part[2] == 1:
                ahead = ahead[:1]
            ahead = ahead[::-1]
            c_up = gather_of(ahead[0]) if ahead else None
            c_down = gather_of(ahead[1]) if len(ahead) > 1 else None
            h, s, got_up, got_down = _ffn_fwd(h, params[idx], c_up, c_down)
            if c_up is not None:
                params[ahead[0]] = gathered(ahead[0], got_up)
            if c_down is not None:
                params[ahead[1]] = gathered(ahead[1], got_down)
        else:
            h, s = _mixer_fwd(h, params[idx], part[1] % 4)
        saved.append(s)
    dh, loss_vec = _loss_call(h, tgt)
    loss_local = loss_vec[0, 0]

    results = {}
    rep_grads = {}

    def scatter_of(idx, grads):
        mats, vecs = _part_entries(parts[idx])
        gm = [grads[k].reshape(N_DEV, -1, grads[k].shape[-1]) for k, _, _ in mats]
        return _Carry(gm + [_pack_full(vecs, grads)], ["a2a"] * (len(gm) + 1))

    def update(idx, bufs):
        part = parts[idx]
        mats, vecs = _part_entries(part)
        wmats, wv = stored_w[idx]
        mmats, mv = stored(part, lambda n: moments_m[n])
        vmats, vv = stored(part, lambda n: moments_v[n])
        res = [dict() for _ in range(4)]
        for (k, _, axis), buf, w_, m_, v_ in zip(mats, bufs, wmats, mmats, vmats):
            for r, o in zip(res, _adam(buf, w_, m_, v_, name="adam_mat")):
                r[k] = o.T if axis == 1 else o
        for r, o in zip(res, _adam(bufs[-1], wv, mv, vv, name="adam_vecs")):
            r.update(_unpack_shard(vecs, o))
        results[part] = res

    rep_entries = [(name, shape) for name, _, _, shape in _REPLICATED]
    rg = None
    pending = []
    for idx in range(len(parts) - 1, -1, -1):
        part, p = parts[idx], params[idx]
        if part[0] == "ffn":
            riders = pending[:2]
            pending = pending[2:]
            c_a = riders[0][1] if riders else None
            c_b = riders[1][1] if len(riders) > 1 else None
            if idx == 0 and c_a is not None:
                c_a = _Carry(c_a.arrays + [_pack_flat([rep_grads[name] for name, _ in rep_entries])], c_a.kinds + ["bcast"])
            dh, grads, got_a, got_b = _ffn_bwd(saved[idx], p, dh, c_a, c_b)
            if idx == 0 and c_a is not None:
                rg, got_a = got_a[-1], got_a[:-1]
            for (ridx, _), got in zip(riders, (got_a, got_b)):
                update(ridx, got)
        else:
            dh, grads = _mixer_bwd(saved[idx], p, part[1] % 4, dh)
        grads = _finish_grads(part, grads)
        for name, kind, key, shape in _REPLICATED:
            if part[0] == "mix" and kind == part[1] % 4:
                rep_grads[name] = grads[key]
        pending.append((idx, scatter_of(idx, grads)))
    for ridx, c in pending:
        update(ridx, _exchange(c.arrays, c.kinds, name="scatter_last"))

    get_rep = lambda d: _pack_flat([(d[name][0] if name != "rel_bias" else d[name]) for name, _ in rep_entries])
    assert rg is not None
    orep = _adam(rg, get_rep(inputs), get_rep(moments_m), get_rep(moments_v), name="adam_rep")
    rep_out = [_unpack_flat(rep_entries, o) for o in orep]
    return loss_local, dh, results, rep_out


def _assemble(name, results, rep_out, which):
    for rname, _, _, _ in _REPLICATED:
        if rname == name:
            a = rep_out[which][name]
            return a if name == "rel_bias" else a[None]
    if name in ("norm_pre", "norm_post"):
        key = "g_pre" if name == "norm_pre" else "g_post"
        rows = []
        for i in range(DEPTH):
            rows.append(jnp.stack([results[("ffn", i, 0)][which][key], results[("mix", i)][which][key],
                                   results[("ffn", i, 1)][which][key]]))
        return jnp.stack(rows)
    if name.startswith("ffn_"):
        key = name[4:]
        return jnp.stack([jnp.stack([results[("ffn", i, j)][which][key] for j in range(2)]) for i in range(DEPTH)])
    kind = _MIX_PREFIX.index(name[:3])
    layers = [i for i in range(DEPTH) if i % 4 == kind]
    return jnp.stack([results[("mix", i)][which][name[3:]] for i in layers])


def kernel(x, norm_pre, norm_post, ffn_w1, ffn_w3, ffn_w2, rel_bias, s5_w_in, s5_a_re, s5_a_im, s5_log_dt, s5_b_re, s5_b_im, s5_c_re, s5_c_im, s5_d, s5_w_glu, s5_b_glu, s5_w_out, cv_w_in, cv_b_in, cv_dw, cv_dw_b, cv_ln_g, cv_ln_b, cv_w_out, cv_b_out, gm_w_in, gm_b_in, gm_ln_g, gm_ln_b, gm_w_s, gm_b_s, gm_w_out, gm_b_out, at_w_qkv, at_w_out, loss_target, m_norm_pre, m_norm_post, m_ffn_w1, m_ffn_w3, m_ffn_w2, m_rel_bias, m_s5_w_in, m_s5_a_re, m_s5_a_im, m_s5_log_dt, m_s5_b_re, m_s5_b_im, m_s5_c_re, m_s5_c_im, m_s5_d, m_s5_w_glu, m_s5_b_glu, m_s5_w_out, m_cv_w_in, m_cv_b_in, m_cv_dw, m_cv_dw_b, m_cv_ln_g, m_cv_ln_b, m_cv_w_out, m_cv_b_out, m_gm_w_in, m_gm_b_in, m_gm_ln_g, m_gm_ln_b, m_gm_w_s, m_gm_b_s, m_gm_w_out, m_gm_b_out, m_at_w_qkv, m_at_w_out, v_norm_pre, v_norm_post, v_ffn_w1, v_ffn_w3, v_ffn_w2, v_rel_bias, v_s5_w_in, v_s5_a_re, v_s5_a_im, v_s5_log_dt, v_s5_b_re, v_s5_b_im, v_s5_c_re, v_s5_c_im, v_s5_d, v_s5_w_glu, v_s5_b_glu, v_s5_w_out, v_cv_w_in, v_cv_b_in, v_cv_dw, v_cv_dw_b, v_cv_ln_g, v_cv_ln_b, v_cv_w_out, v_cv_b_out, v_gm_w_in, v_gm_b_in, v_gm_ln_g, v_gm_ln_b, v_gm_w_s, v_gm_b_s, v_gm_w_out, v_gm_b_out, v_at_w_qkv, v_at_w_out):
    args = locals()
    inputs = {n: args[n] for n in _TWIN_WEIGHTS}
    moments_m = {n: args["m_" + n] for n in _TWIN_WEIGHTS}
    moments_v = {n: args["v_" + n] for n in _TWIN_WEIGHTS}
    loss_local, dx, results, rep_out = _step(x[0], loss_target[0], inputs, moments_m, moments_v)
    loss = lax.psum(loss_local, AXES)
    out = [loss, dx[None]]
    for which in range(4):
        out += [_assemble(n, results, rep_out, which) for n in _TWIN_WEIGHTS]
    return tuple(out)
```

```python
import functools
import math

import numpy as np

import jax
import jax.numpy as jnp
from jax import lax
from jax.experimental import pallas as pl
from jax.experimental.pallas import tpu as pltpu

F32 = jnp.float32
BF16 = jnp.bfloat16

D_MODEL = 1024
DEPTH = 4
D_FF = 2816
EPS = 1e-6
S5_GROUP = 16
S5_STATE = 64
CONV_W = 31
GM_CHUNK = 128
GM_HEADS = 8
HEAD_DIM = 64
PATTERNS = ((128, 1), (512, 4), (2048, 16))
BLOCK = 128
NUM_BUCKETS = 32
MAX_DISTANCE = 2048
ADAM_LR = 0.001
ADAM_B1 = 0.9
ADAM_B2 = 0.999
ADAM_EPS = 1e-08
ADAM_WD = 0.01
ADAM_STEP = 10

N_DEV = 8
AXES = ("x", "y", "c")
LANES = 128
GM_E = 2 * D_MODEL
S5_GROUPS = D_MODEL // S5_GROUP
S5_GB = LANES // S5_GROUP
S5_NB = D_MODEL // LANES
S5_BW = S5_GB * S5_STATE
S5_NS = S5_GROUPS * S5_STATE
AT_HEADS = D_MODEL // HEAD_DIM
VMEM_LIMIT = 56 * 1024 * 1024
PACK_ALIGN = 16 * 1024


def _cparams(sem):
    return pltpu.CompilerParams(dimension_semantics=sem, vmem_limit_bytes=VMEM_LIMIT)


def _pick(n, cap):
    if n <= cap:
        return n
    best = None
    for t in range(LANES, cap + 1, LANES):
        if n % t == 0:
            best = t
    assert best is not None, (n, cap)
    return best


def _pick_rows(n, cap):
    best = None
    for t in range(16, min(n, cap) + 1, 16):
        if n % t == 0:
            best = t
    assert best is not None, (n, cap)
    return best


MM_VMEM_BUDGET = 40 * 1024 * 1024


def _mm(a, b, *, ta=False, tb=False, out_dtype=F32, name, carry=None):
    a_list = list(a) if isinstance(a, (tuple, list)) else [a]
    b_list = list(b) if isinstance(b, (tuple, list)) else [b]
    n_op = len(a_list)
    assert n_op == len(b_list)
    K, M = a_list[0].shape if ta else a_list[0].shape[::-1]
    N, K2 = b_list[0].shape if tb else b_list[0].shape[::-1]
    assert K == K2, (a_list[0].shape, b_list[0].shape, ta, tb)
    a_bytes = sum(x.dtype.itemsize for x in a_list)
    b_bytes = sum(x.dtype.itemsize for x in b_list)
    o_bytes = jnp.dtype(out_dtype).itemsize

    def vmem(tm, tn, tk, nk):
        acc = tm * tn * 4 if (nk > 1 and out_dtype != F32) else 0
        return 2 * (tm * tk * a_bytes + tk * tn * b_bytes + tm * tn * o_bytes) + acc

    if ta:
        tm, tn = _pick(M, 1408), _pick(N, 1408)
        tk = next(t for t in (2048, 1024, 512, 256) if K % t == 0 and vmem(tm, tn, t, 2) <= MM_VMEM_BUDGET)
    else:
        tm, tn = _pick(M, 512), _pick(N, 1408)
        tk = next(t for t in (K, _pick(K, 4608), _pick(K, 2816), _pick(K, 1024))
                  if vmem(tm, tn, t, K // t) <= MM_VMEM_BUDGET)
    nk = K // tk
    a_spec = pl.BlockSpec((tk, tm), lambda j, i, k: (k, i)) if ta else pl.BlockSpec((tm, tk), lambda j, i, k: (i, k))
    b_spec = pl.BlockSpec((tn, tk), lambda j, i, k: (j, k)) if tb else pl.BlockSpec((tk, tn), lambda j, i, k: (k, j))
    dims = (((0 if ta else 1,), (1 if tb else 0,)), ((), ()))
    use_scratch = nk > 1 and out_dtype != F32
    grid = (N // tn, M // tm, nk)

    def body(*refs):
        ins, (o_ref,), scratch, begin, end = _carry_hooks(carry, refs, 2 * n_op, 1, 3, grid)
        begin()
        p = None
        for a_ref, b_ref in zip(ins[:n_op], ins[n_op:]):
            d = lax.dot_general(a_ref[...].astype(BF16), b_ref[...].astype(BF16), dims, preferred_element_type=F32)
            p = d if p is None else p + d
        if nk == 1:
            o_ref[...] = p.astype(o_ref.dtype)
        else:
            acc = scratch[0] if use_scratch else o_ref
            k = pl.program_id(2)

            @pl.when(k == 0)
            def _():
                acc[...] = p

            @pl.when(k > 0)
            def _():
                acc[...] += p

            if use_scratch:
                @pl.when(k == nk - 1)
                def _():
                    o_ref[...] = acc[...].astype(o_ref.dtype)
        end()

    extra = carry if carry is not None else _NO_CARRY
    res = pl.pallas_call(
        body, name=name, grid=grid, in_specs=[a_spec] * n_op + [b_spec] * n_op + extra.in_specs,
        out_specs=[pl.BlockSpec((tm, tn), lambda j, i, k: (i, j))] + extra.out_specs,
        out_shape=[jax.ShapeDtypeStruct((M, N), out_dtype)] + extra.out_shape,
        scratch_shapes=([pltpu.VMEM((tm, tn), F32)] if use_scratch else []) + extra.scratch,
        compiler_params=_cparams(("arbitrary",) * 3 if carry is not None else ("parallel", "parallel", "arbitrary")),
    )(*a_list, *b_list, *extra.arrays)
    return res[0] if carry is None else (res[0], res[1:])


ROW_TILE_BYTES = 8 * 1024 * 1024


def _row_tile(arrays):
    row_bytes = sum(w * jnp.dtype(dt).itemsize for w, dt in arrays)
    for tile in (256, 128, 64, 32):
        if tile * row_bytes <= ROW_TILE_BYTES:
            return tile
    return 16


def _rows(fn, rows, pars, outs, *, name):
    T = rows[0].shape[0]
    tile = _row_tile([(r.shape[1], r.dtype) for r in rows] + list(outs))
    nr, npar = len(rows), len(pars)

    def body(*refs):
        r = [refs[i][...] for i in range(nr)]
        p = [refs[nr + i][...] for i in range(npar)]
        res = fn(*r, *p)
        for o_ref, o in zip(refs[nr + npar:], res):
            o_ref[...] = o.astype(o_ref.dtype)

    in_specs = [pl.BlockSpec((tile, r.shape[1]), lambda i: (i, 0)) for r in rows]
    in_specs += [pl.BlockSpec(p.shape, lambda i, nd=p.ndim: (0,) * nd) for p in pars]
    return pl.pallas_call(
        body, name=name, grid=(T // tile,), in_specs=in_specs,
        out_specs=[pl.BlockSpec((tile, w), lambda i: (i, 0)) for w, _ in outs],
        out_shape=[jax.ShapeDtypeStruct((T, w), dt) for w, dt in outs],
        compiler_params=_cparams(("parallel",)),
    )(*rows, *pars)


def _rows_vjp(fn, rows, pars, cts, *, dtypes, adds=None, name):
    adds = adds or {}
    cts = [c if isinstance(c, (tuple, list)) else (c,) for c in cts]
    flat_cts = [a for c in cts for a in c]
    add_keys = sorted(adds)
    add_arrs = [adds[k] for k in add_keys]
    want = [i for i, d in enumerate(dtypes) if d is not None]
    T = rows[0].shape[0]
    tile = _row_tile([(a.shape[1], a.dtype) for a in list(rows) + flat_cts + add_arrs]
                     + [(rows[i].shape[1], dtypes[i]) for i in want])
    nr, npar, nc, na = len(rows), len(pars), len(flat_cts), len(add_arrs)

    def body(*refs):
        r = [refs[i][...].astype(F32) for i in range(nr)]
        p = [refs[nr + i][...] for i in range(npar)]
        cvals = [refs[nr + npar + i][...].astype(F32) for i in range(nc)]
        avals = [refs[nr + npar + nc + i][...].astype(F32) for i in range(na)]
        outs = refs[nr + npar + nc + na:]
        ct, pos = [], 0
        for c in cts:
            s = cvals[pos]
            for extra in cvals[pos + 1:pos + len(c)]:
                s = s + extra
            pos += len(c)
            ct.append(s)
        _, vjp = jax.vjp(lambda *a: tuple(fn(*a)), *r, *p)
        g = vjp(tuple(ct))
        for o_ref, i in zip(outs[:len(want)], want):
            gi = g[i]
            if i in adds:
                gi = gi + avals[add_keys.index(i)]
            o_ref[...] = gi.astype(o_ref.dtype)
        first = pl.program_id(0) == 0
        for o_ref, gp in zip(outs[len(want):], g[nr:]):
            @pl.when(first)
            def _(o_ref=o_ref, gp=gp):
                o_ref[...] = gp

            @pl.when(jnp.logical_not(first))
            def _(o_ref=o_ref, gp=gp):
                o_ref[...] += gp

    row_spec = lambda a: pl.BlockSpec((tile, a.shape[1]), lambda i: (i, 0))
    par_spec = lambda a: pl.BlockSpec(a.shape, lambda i, nd=a.ndim: (0,) * nd)
    res = pl.pallas_call(
        body, name=name, grid=(T // tile,),
        in_specs=[row_spec(a) for a in rows] + [par_spec(a) for a in pars] + [row_spec(a) for a in flat_cts + add_arrs],
        out_specs=[row_spec(rows[i]) for i in want] + [par_spec(a) for a in pars],
        out_shape=[jax.ShapeDtypeStruct(rows[i].shape, dtypes[i]) for i in want]
        + [jax.ShapeDtypeStruct(a.shape, F32) for a in pars],
        compiler_params=_cparams(("arbitrary",)),
    )(*rows, *pars, *flat_cts, *add_arrs)
    return res[:len(want)], res[len(want):]


def _small(fn, args, outs, *, name):
    n = len(args)

    def body(*refs):
        res = fn(*[r[...] for r in refs[:n]])
        for o_ref, o in zip(refs[n:], res):
            o_ref[...] = o

    return pl.pallas_call(body, name=name, out_shape=[jax.ShapeDtypeStruct(s, F32) for s in outs],
                          compiler_params=pltpu.CompilerParams(vmem_limit_bytes=VMEM_LIMIT))(*args)


def _small_vjp(fn, args, cts, *, name):
    n, nc = len(args), len(cts)

    def body(*refs):
        _, vjp = jax.vjp(lambda *a: tuple(fn(*a)), *[r[...] for r in refs[:n]])
        g = vjp(tuple(r[...] for r in refs[n:n + nc]))
        for o_ref, gi in zip(refs[n + nc:], g):
            o_ref[...] = gi

    return pl.pallas_call(body, name=name, out_shape=[jax.ShapeDtypeStruct(a.shape, F32) for a in args],
                          compiler_params=pltpu.CompilerParams(vmem_limit_bytes=VMEM_LIMIT))(*args, *cts)


def _rms(x, g):
    return x * lax.rsqrt(jnp.mean(x * x, axis=-1, keepdims=True) + EPS) * g


def _layernorm(x, g, b):
    mu = jnp.mean(x, axis=-1, keepdims=True)
    var = jnp.mean(jnp.square(x - mu), axis=-1, keepdims=True)
    return (x - mu) * lax.rsqrt(var + EPS) * g + b


def _f_pre(x, g):
    return (_rms(x.astype(F32), g),)


def _f_post_term(scale, has_bias):
    def fn(o, g, *b):
        o = o.astype(F32)
        if has_bias:
            o = o + b[0]
        return (scale * _rms(o, g),)
    return fn


def _f_post(scale, has_bias):
    term = _f_post_term(scale, has_bias)

    def fn(x, o, g, *b):
        return (x + term(o, g, *b)[0],)
    return fn


def _f_s5_gelu(ylin, u, d):
    return (jax.nn.gelu(ylin.astype(F32) + d * u.astype(F32)),)


def _f_s5_glu(y, gl, b):
    return (y.astype(F32) * jax.nn.sigmoid(gl.astype(F32) + b),)


def _f_cv_glu(z0, b):
    z = z0.astype(F32) + b
    return (z[:, :D_MODEL] * jax.nn.sigmoid(z[:, D_MODEL:]),)


def _f_cv_ln(zc, g, b):
    return (jax.nn.silu(_layernorm(zc.astype(F32), g, b)),)


def _f_gm_in(z0, b, g, bl):
    z = jax.nn.gelu(z0.astype(F32) + b)
    return z[:, :GM_E], _layernorm(z[:, GM_E:], g, bl)


def _f_at_combine(o0, o1, o2, l0, l1, l2):
    m = jnp.maximum(jnp.maximum(l0, l1), l2)
    e0, e1, e2 = jnp.exp(l0 - m), jnp.exp(l1 - m), jnp.exp(l2 - m)
    return ((e0 * o0 + e1 * o1 + e2 * o2) / (e0 + e1 + e2),)


def _f_s5_disc(ar, ai, ldt, br, bi):
    dt = jnp.exp(ldt)
    mag = jnp.exp(dt * ar)
    abr = mag * jnp.cos(dt * ai)
    abi = mag * jnp.sin(dt * ai)
    den = ar * ar + ai * ai
    nr = abr - 1.0
    f_re = (nr * ar + abi * ai) / den
    f_im = (abi * ar - nr * ai) / den
    return abr, abi, f_re * br - f_im * bi, f_re * bi + f_im * br


def _loss_call(y, tgt):
    T, D = y.shape
    tile = 256

    def body(y_ref, t_ref, dy_ref, l_ref):
        err = y_ref[...] - t_ref[...]
        dy_ref[...] = err * (1.0 / D)
        part = 0.5 * jnp.sum(jnp.mean(err * err, axis=-1, keepdims=True), axis=0, keepdims=True)
        part = jnp.broadcast_to(part, (1, LANES))
        first = pl.program_id(0) == 0

        @pl.when(first)
        def _():
            l_ref[...] = part

        @pl.when(jnp.logical_not(first))
        def _():
            l_ref[...] += part

    return pl.pallas_call(
        body, name="loss", grid=(T // tile,),
        in_specs=[pl.BlockSpec((tile, D), lambda i: (i, 0))] * 2,
        out_specs=[pl.BlockSpec((tile, D), lambda i: (i, 0)), pl.BlockSpec((1, LANES), lambda i: (0, 0))],
        out_shape=[jax.ShapeDtypeStruct((T, D), F32), jax.ShapeDtypeStruct((1, LANES), F32)],
        compiler_params=_cparams(("arbitrary",)),
    )(y, tgt)


def _bd(xs, ws, *, add=None, out_dtype=F32, name):
    T = xs[0].shape[0]
    nb, kw, nw = ws[0].shape
    tm = 256
    n = len(xs)

    def body(*refs):
        o_ref = refs[-1]
        for j in range(nb):
            acc = None
            for x_ref, w_ref in zip(refs[:n], refs[n:2 * n]):
                p = jnp.dot(x_ref[:, j * kw:(j + 1) * kw].astype(BF16), w_ref[j].astype(BF16),
                            preferred_element_type=F32)
                acc = p if acc is None else acc + p
            if add is not None:
                acc = acc + refs[2 * n][:, j * nw:(j + 1) * nw].astype(F32)
            o_ref[:, j * nw:(j + 1) * nw] = acc.astype(o_ref.dtype)

    in_specs = [pl.BlockSpec((tm, nb * kw), lambda i: (i, 0)) for _ in xs]
    in_specs += [pl.BlockSpec((nb, kw, nw), lambda i: (0, 0, 0)) for _ in ws]
    args = list(xs) + list(ws)
    if add is not None:
        in_specs.append(pl.BlockSpec((tm, nb * nw), lambda i: (i, 0)))
        args.append(add)
    return pl.pallas_call(
        body, name=name, grid=(T // tm,), in_specs=in_specs,
        out_specs=pl.BlockSpec((tm, nb * nw), lambda i: (i, 0)),
        out_shape=jax.ShapeDtypeStruct((T, nb * nw), out_dtype),
        compiler_params=_cparams(("parallel",)),
    )(*args)


def _bd_wgrad(x, dy, kw, nw, *, name):
    T = x.shape[0]
    nb = x.shape[1] // kw
    tk = 512

    def body(x_ref, dy_ref, o_ref):
        first = pl.program_id(0) == 0
        for j in range(nb):
            p = lax.dot_general(x_ref[:, j * kw:(j + 1) * kw].astype(BF16), dy_ref[:, j * nw:(j + 1) * nw].astype(BF16),
                                (((0,), (0,)), ((), ())), preferred_element_type=F32)

            @pl.when(first)
            def _(j=j, p=p):
                o_ref[j] = p

            @pl.when(jnp.logical_not(first))
            def _(j=j, p=p):
                o_ref[j] += p

    return pl.pallas_call(
        body, name=name, grid=(T // tk,),
        in_specs=[pl.BlockSpec((tk, nb * kw), lambda k: (k, 0)), pl.BlockSpec((tk, nb * nw), lambda k: (k, 0))],
        out_specs=pl.BlockSpec((nb, kw, nw), lambda k: (0, 0, 0)),
        out_shape=jax.ShapeDtypeStruct((nb, kw, nw), F32),
        compiler_params=_cparams(("arbitrary",)),
    )(x, dy)


SCAN_COLS = 1024
SCAN_ROWS = 256


def _scan_fwd(bur, bui, ar, ai):
    T, NS = bur.shape
    cw, tc = SCAN_COLS, SCAN_ROWS

    def body(bur_ref, bui_ref, ar_ref, ai_ref, sr_ref, si_ref, cr, ci):
        @pl.when(pl.program_id(1) == 0)
        def _():
            cr[...] = jnp.zeros_like(cr)
            ci[...] = jnp.zeros_like(ci)

        a_r, a_i = ar_ref[...], ai_ref[...]

        def step8(t8, carry):
            sr, si = carry
            base = pl.multiple_of(t8 * 8, 8)
            for r in range(8):
                br = bur_ref[pl.ds(base + r, 1), :]
                bi = bui_ref[pl.ds(base + r, 1), :]
                sr, si = a_r * sr - a_i * si + br, a_r * si + a_i * sr + bi
                sr_ref[pl.ds(base + r, 1), :] = sr
                si_ref[pl.ds(base + r, 1), :] = si
            return sr, si

        sr, si = lax.fori_loop(0, tc // 8, step8, (cr[...], ci[...]))
        cr[...] = sr
        ci[...] = si

    blk = pl.BlockSpec((tc, cw), lambda c, t: (t, c))
    vec = pl.BlockSpec((1, cw), lambda c, t: (0, c))
    return pl.pallas_call(
        body, name="s5_scan_fwd", grid=(NS // cw, T // tc), in_specs=[blk, blk, vec, vec], out_specs=[blk, blk],
        out_shape=[jax.ShapeDtypeStruct((T, NS), F32)] * 2,
        scratch_shapes=[pltpu.VMEM((1, cw), F32)] * 2,
        compiler_params=_cparams(("parallel", "arbitrary")),
    )(bur, bui, ar, ai)


def _scan_bwd(gr, gi, sr, si, ar, ai):
    T, NS = gr.shape
    cw, tc = SCAN_COLS, SCAN_ROWS
    nt = T // tc

    def body(gr_ref, gi_ref, sr_ref, si_ref, ar_ref, ai_ref, lr_ref, li_ref, dar_ref, dai_ref, cr, ci):
        @pl.when(pl.program_id(1) == 0)
        def _():
            cr[...] = jnp.zeros_like(cr)
            ci[...] = jnp.zeros_like(ci)
            dar_ref[...] = jnp.zeros_like(dar_ref)
            dai_ref[...] = jnp.zeros_like(dai_ref)

        a_r, a_i = ar_ref[...], ai_ref[...]

        def step8(k, carry):
            lr, li, dar, dai = carry
            base = pl.multiple_of((tc // 8 - 1 - k) * 8, 8)
            for r in range(7, -1, -1):
                s_r = sr_ref[pl.ds(base + r, 1), :]
                s_i = si_ref[pl.ds(base + r, 1), :]
                dar = dar + lr * s_r + li * s_i
                dai = dai + li * s_r - lr * s_i
                g_r = gr_ref[pl.ds(base + r, 1), :]
                g_i = gi_ref[pl.ds(base + r, 1), :]
                lr, li = g_r + a_r * lr + a_i * li, g_i + a_r * li - a_i * lr
                lr_ref[pl.ds(base + r, 1), :] = lr
                li_ref[pl.ds(base + r, 1), :] = li
            return lr, li, dar, dai

        lr, li, dar, dai = lax.fori_loop(0, tc // 8, step8, (cr[...], ci[...], dar_ref[...], dai_ref[...]))
        cr[...] = lr
        ci[...] = li
        dar_ref[...] = dar
        dai_ref[...] = dai

    blk = pl.BlockSpec((tc, cw), lambda c, t: (nt - 1 - t, c))
    vec = pl.BlockSpec((1, cw), lambda c, t: (0, c))
    return pl.pallas_call(
        body, name="s5_scan_bwd", grid=(NS // cw, nt), in_specs=[blk, blk, blk, blk, vec, vec],
        out_specs=[blk, blk, vec, vec],
        out_shape=[jax.ShapeDtypeStruct((T, NS), F32)] * 2 + [jax.ShapeDtypeStruct((1, NS), F32)] * 2,
        scratch_shapes=[pltpu.VMEM((1, cw), F32)] * 2,
        compiler_params=_cparams(("parallel", "arbitrary")),
    )(gr, gi, sr, si, ar, ai)


CONV_ROWS = 256
CONV_HALO = 32
CONV_PAD = CONV_HALO - (CONV_W - 1)
CONV_SUB = 16


def _conv_shifts(ext, sh, n):
    ext[pl.ds(n, 8), :] = jnp.zeros((8, ext.shape[1]), F32)
    for s in range(8):
        sh[s] = ext[pl.ds(s, n), :]


def _conv_rows(sh, start):
    return sh[start % 8, pl.ds(start - start % 8, CONV_SUB), :]


def _conv_fwd(z, dw, dwb):
    T, D = z.shape
    tc, hl = CONV_ROWS, CONV_HALO
    per = tc // hl

    def body(z_ref, zp_ref, dw_ref, b_ref, o_ref, ext, sh):
        i = pl.program_id(0)
        ext[pl.ds(0, hl), :] = jnp.where(i > 0, zp_ref[...], 0.0)
        ext[pl.ds(hl, tc), :] = z_ref[...]
        _conv_shifts(ext, sh, tc + hl)
        for rb in range(tc // CONV_SUB):
            r0 = rb * CONV_SUB
            acc = jnp.zeros((CONV_SUB, D), F32) + b_ref[...]
            for k in range(CONV_W):
                acc = acc + dw_ref[pl.ds(k, 1), :] * _conv_rows(sh, r0 + CONV_PAD + k)
            o_ref[pl.ds(r0, CONV_SUB), :] = acc

    return pl.pallas_call(
        body, name="conv_fwd", grid=(T // tc,),
        in_specs=[pl.BlockSpec((tc, D), lambda i: (i, 0)),
                  pl.BlockSpec((hl, D), lambda i: (jnp.maximum(i * per - 1, 0), 0)),
                  pl.BlockSpec((hl, D), lambda i: (0, 0)), pl.BlockSpec((1, D), lambda i: (0, 0))],
        out_specs=pl.BlockSpec((tc, D), lambda i: (i, 0)),
        out_shape=jax.ShapeDtypeStruct((T, D), F32),
        scratch_shapes=[pltpu.VMEM((tc + hl + 8, D), F32), pltpu.VMEM((8, tc + hl, D), F32)],
        compiler_params=_cparams(("parallel",)),
    )(z, z, dw, dwb)


def _conv_bwd(dout, z, dw):
    T, D = z.shape
    tc, hl = CONV_ROWS, CONV_HALO
    per = tc // hl
    nblk = T // tc

    def body(g_ref, gn_ref, z_ref, zp_ref, dw_ref, dz_ref, ddw_ref, db_ref, gext, zext, gsh, zsh, acc8):
        i = pl.program_id(0)
        gext[pl.ds(0, tc), :] = g_ref[...]
        gext[pl.ds(tc, hl), :] = jnp.where(i < nblk - 1, gn_ref[...], 0.0)
        zext[pl.ds(0, hl), :] = jnp.where(i > 0, zp_ref[...], 0.0)
        zext[pl.ds(hl, tc), :] = z_ref[...]
        _conv_shifts(gext, gsh, tc + hl)
        _conv_shifts(zext, zsh, tc + hl)
        for rb in range(tc // CONV_SUB):
            r0 = rb * CONV_SUB
            acc = jnp.zeros((CONV_SUB, D), F32)
            for k in range(CONV_W):
                acc = acc + dw_ref[pl.ds(k, 1), :] * _conv_rows(gsh, r0 + CONV_W - 1 - k)
            dz_ref[pl.ds(r0, CONV_SUB), :] = acc

        @pl.when(i == 0)
        def _():
            acc8[...] = jnp.zeros_like(acc8)
            db_ref[...] = jnp.zeros_like(db_ref)

        db_ref[...] += jnp.sum(g_ref[...], axis=0, keepdims=True)
        for k in range(CONV_W):
            part = jnp.zeros((8, D), F32)
            for rb in range(tc // CONV_SUB):
                r0 = rb * CONV_SUB
                prod = g_ref[pl.ds(r0, CONV_SUB), :] * _conv_rows(zsh, r0 + CONV_PAD + k)
                for s in range(CONV_SUB // 8):
                    part = part + prod[s * 8:(s + 1) * 8]
            acc8[k] += part

        @pl.when(i == nblk - 1)
        def _():
            ddw_ref[...] = jnp.sum(acc8[...], axis=1)

    return pl.pallas_call(
        body, name="conv_bwd", grid=(nblk,),
        in_specs=[pl.BlockSpec((tc, D), lambda i: (i, 0)),
                  pl.BlockSpec((hl, D), lambda i: (jnp.minimum((i + 1) * per, nblk * per - 1), 0)),
                  pl.BlockSpec((tc, D), lambda i: (i, 0)),
                  pl.BlockSpec((hl, D), lambda i: (jnp.maximum(i * per - 1, 0), 0)),
                  pl.BlockSpec((hl, D), lambda i: (0, 0))],
        out_specs=[pl.BlockSpec((tc, D), lambda i: (i, 0)), pl.BlockSpec((hl, D), lambda i: (0, 0)),
                   pl.BlockSpec((1, D), lambda i: (0, 0))],
        out_shape=[jax.ShapeDtypeStruct((T, D), F32), jax.ShapeDtypeStruct((hl, D), F32),
                   jax.ShapeDtypeStruct((1, D), F32)],
        scratch_shapes=[pltpu.VMEM((tc + hl + 8, D), F32)] * 2 + [pltpu.VMEM((8, tc + hl, D), F32)] * 2
        + [pltpu.VMEM((hl, 8, D), F32)],
        compiler_params=_cparams(("arbitrary",)),
    )(dout, dout, z, z, dw)


def _gm_causal():
    r = lax.broadcasted_iota(jnp.int32, (GM_CHUNK, GM_CHUNK), 0)
    c = lax.broadcasted_iota(jnp.int32, (GM_CHUNK, GM_CHUNK), 1)
    return r >= c


def _gm_sg_fwd(u, v, ws, bs_col):
    T, E = u.shape
    hw = E // GM_HEADS

    def body(u_ref, v_ref, w_ref, b_ref, o_ref):
        causal = _gm_causal()
        for h in range(GM_HEADS):
            cols = slice(h * hw, (h + 1) * hw)
            w = jnp.where(causal, w_ref[h], 0.0).astype(BF16)
            s = jnp.dot(w, v_ref[:, cols], preferred_element_type=F32) + b_ref[h]
            o_ref[:, cols] = (u_ref[:, cols] * s).astype(o_ref.dtype)

    return pl.pallas_call(
        body, name="gm_sg_fwd", grid=(T // GM_CHUNK,),
        in_specs=[pl.BlockSpec((GM_CHUNK, E), lambda i: (i, 0)), pl.BlockSpec((GM_CHUNK, E), lambda i: (i, 0)),
                  pl.BlockSpec(ws.shape, lambda i: (0, 0, 0)), pl.BlockSpec(bs_col.shape, lambda i: (0, 0, 0))],
        out_specs=pl.BlockSpec((GM_CHUNK, E), lambda i: (i, 0)),
        out_shape=jax.ShapeDtypeStruct((T, E), BF16),
        compiler_params=_cparams(("parallel",)),
    )(u, v, ws, bs_col)


def _gm_sg_bwd(dus, u, v, ws, bs_col):
    T, E = u.shape
    hw = E // GM_HEADS

    def body(g_ref, u_ref, v_ref, w_ref, b_ref, du_ref, dv_ref, dw_ref, db_ref):
        causal = _gm_causal()

        @pl.when(pl.program_id(0) == 0)
        def _():
            dw_ref[...] = jnp.zeros_like(dw_ref)
            db_ref[...] = jnp.zeros_like(db_ref)

        for h in range(GM_HEADS):
            cols = slice(h * hw, (h + 1) * hw)
            w = jnp.where(causal, w_ref[h], 0.0).astype(BF16)
            vh = v_ref[:, cols]
            s = jnp.dot(w, vh, preferred_element_type=F32) + b_ref[h]
            g = g_ref[:, cols]
            du_ref[:, cols] = g * s
            ds = g * u_ref[:, cols]
            dsb = ds.astype(BF16)
            dv_ref[:, cols] = lax.dot_general(w, dsb, (((0,), (0,)), ((), ())), preferred_element_type=F32)
            dwh = lax.dot_general(dsb, vh, (((1,), (1,)), ((), ())), preferred_element_type=F32)
            dw_ref[h] += jnp.where(causal, dwh, 0.0)
            db_ref[h] += jnp.broadcast_to(jnp.sum(ds, axis=1, keepdims=True), (GM_CHUNK, LANES))

    blk = pl.BlockSpec((GM_CHUNK, E), lambda i: (i, 0))
    return pl.pallas_call(
        body, name="gm_sg_bwd", grid=(T // GM_CHUNK,),
        in_specs=[blk, blk, blk, pl.BlockSpec(ws.shape, lambda i: (0, 0, 0)),
                  pl.BlockSpec(bs_col.shape, lambda i: (0, 0, 0))],
        out_specs=[blk, blk, pl.BlockSpec(ws.shape, lambda i: (0, 0, 0)),
                   pl.BlockSpec((GM_HEADS, GM_CHUNK, LANES), lambda i: (0, 0, 0))],
        out_shape=[jax.ShapeDtypeStruct((T, E), F32), jax.ShapeDtypeStruct((T, E), F32),
                   jax.ShapeDtypeStruct(ws.shape, F32), jax.ShapeDtypeStruct((GM_HEADS, GM_CHUNK, LANES), F32)],
        compiler_params=_cparams(("arbitrary",)),
    )(dus, u, v, ws, bs_col)


def _t5_bucket_steps(dilation):
    max_exact = NUM_BUCKETS // 2
    delta = np.arange(BLOCK + 1)
    dist = delta * dilation
    distf = np.maximum(dist, 1).astype(np.float32)
    large = max_exact + (np.log(distf / np.float32(max_exact)) / np.float32(math.log(MAX_DISTANCE / max_exact))
                         * np.float32(NUM_BUCKETS - max_exact)).astype(np.int32)
    large = np.minimum(large, NUM_BUCKETS - 1)
    bucket = np.where(dist < max_exact, dist, large)
    steps = []
    for d in range(1, BLOCK + 1):
        inc = int(bucket[d] - bucket[d - 1])
        assert inc >= 0
        if inc:
            steps.append((d, inc))
    assert int(bucket[0]) == 0
    return steps


def _bucket_map(dilation):
    qi = lax.broadcasted_iota(jnp.int32, (BLOCK, 2 * BLOCK), 0)
    ki = lax.broadcasted_iota(jnp.int32, (BLOCK, 2 * BLOCK), 1)
    delta = qi + BLOCK - ki
    bm = jnp.zeros((BLOCK, 2 * BLOCK), jnp.int32)
    for thr, inc in _t5_bucket_steps(dilation):
        bm = bm + jnp.where(delta >= thr, inc, 0)
    return bm


def _at_bias(table, g, dilation):
    H = AT_HEADS

    def body(t_ref, o_ref):
        bm = _bucket_map(dilation)
        for h in range(H):
            acc = jnp.zeros((BLOCK, 2 * BLOCK), F32)
            for b in range(NUM_BUCKETS):
                acc = jnp.where(bm == b, t_ref[b, g * H + h], acc)
            o_ref[h] = acc

    return pl.pallas_call(body, name="at_bias", in_specs=[pl.BlockSpec(memory_space=pltpu.SMEM)],
                          out_shape=jax.ShapeDtypeStruct((H, BLOCK, 2 * BLOCK), F32))(table)


def _at_bias_bwd(dbias, dilation):
    H = AT_HEADS

    def body(d_ref, o_ref):
        bm = _bucket_map(dilation)
        for h in range(H):
            d = d_ref[h]
            for b in range(NUM_BUCKETS):
                o_ref[b, h] = jnp.sum(jnp.where(bm == b, d, 0.0))

    return pl.pallas_call(body, name="at_bias_bwd", out_specs=pl.BlockSpec(memory_space=pltpu.SMEM),
                          out_shape=jax.ShapeDtypeStruct((NUM_BUCKETS, H), F32))(dbias)


def _at_mask(i, nbs):
    qi = lax.broadcasted_iota(jnp.int32, (BLOCK, 2 * BLOCK), 0)
    ki = lax.broadcasted_iota(jnp.int32, (BLOCK, 2 * BLOCK), 1)
    no_prev = jnp.where(i % nbs == 0, 4 * BLOCK, 0)
    return ((ki < BLOCK) & (ki >= qi + no_prev)) | ((ki >= BLOCK) & (ki - BLOCK <= qi))


def _head_lanes():
    lane = lax.broadcasted_iota(jnp.int32, (BLOCK, LANES), 1)
    return [lane < HEAD_DIM, lane >= HEAD_DIM]


def _at_fwd(qkv, bias, nbs, cb):
    T = qkv.shape[0]
    D = D_MODEL
    npair = D // LANES
    scale = HEAD_DIM ** -0.5

    def body(q_ref, kc_ref, kp_ref, vc_ref, vp_ref, b_ref, o_ref, l_ref):
        i = pl.program_id(0)
        mask = _at_mask(i, nbs)
        sel = _head_lanes()
        for j in range(npair):
            cols = slice(j * LANES, (j + 1) * LANES)
            q = q_ref[:, cols]
            kk = jnp.concatenate([kp_ref[:, cols], kc_ref[:, cols]], axis=0)
            vv = jnp.concatenate([vp_ref[:, cols], vc_ref[:, cols]], axis=0)
            o_pair = jnp.zeros((BLOCK, LANES), F32)
            l_pair = jnp.zeros((BLOCK, LANES), F32)
            for e in range(2):
                qh = jnp.where(sel[e], q, jnp.zeros_like(q))
                s = lax.dot_general(qh, kk, (((1,), (1,)), ((), ())), preferred_element_type=F32) * scale
                s = jnp.where(mask, s + b_ref[2 * j + e], -1e30)
                m = jnp.max(s, axis=1, keepdims=True)
                p = jnp.exp(s - m)
                den = jnp.sum(p, axis=1, keepdims=True)
                o = jnp.dot(p.astype(BF16), vv, preferred_element_type=F32) / den
                o_pair = jnp.where(sel[e], o, o_pair)
                l_pair = jnp.where(sel[e], m + jnp.log(den), l_pair)
            o_ref[:, cols] = o_pair
            l_ref[:, cols] = l_pair

    blk = lambda c, prev: pl.BlockSpec((BLOCK, D), (lambda i: (jnp.maximum(i - 1, 0), cb + c)) if prev
                                       else (lambda i: (i, cb + c)))
    out = pl.BlockSpec((BLOCK, D), lambda i: (i, 0))
    return pl.pallas_call(
        body, name="at_fwd", grid=(T // BLOCK,),
        in_specs=[blk(0, False), blk(1, False), blk(1, True), blk(2, False), blk(2, True),
                  pl.BlockSpec(bias.shape, lambda i: (0, 0, 0))],
        out_specs=[out, out], out_shape=[jax.ShapeDtypeStruct((T, D), F32)] * 2,
        compiler_params=_cparams(("parallel",)),
    )(qkv, qkv, qkv, qkv, qkv, bias)


def _at_bwd(qkv, bias, o, lse, do, dlse, nbs, cb):
    T = qkv.shape[0]
    D = D_MODEL
    nblk = T // BLOCK
    npair = D // LANES
    scale = HEAD_DIM ** -0.5

    def body(q_ref, kc_ref, kp_ref, vc_ref, vp_ref, b_ref, o_ref, l_ref, do_ref, dl_ref, dqkv_ref, db_ref, carry):
        i = pl.program_id(0)

        @pl.when(i == 0)
        def _():
            carry[...] = jnp.zeros_like(carry)
            db_ref[...] = jnp.zeros_like(db_ref)

        @pl.when(i == nblk)
        def _():
            dqkv_ref[...] = carry[...].astype(dqkv_ref.dtype)

        @pl.when(i < nblk)
        def _():
            mask = _at_mask(i, nbs)
            sel = _head_lanes()
            for j in range(npair):
                cols = slice(j * LANES, (j + 1) * LANES)
                kcols = slice(D + j * LANES, D + (j + 1) * LANES)
                vcols = slice(2 * D + j * LANES, 2 * D + (j + 1) * LANES)
                q = q_ref[:, cols]
                kk = jnp.concatenate([kp_ref[:, cols], kc_ref[:, cols]], axis=0)
                vv = jnp.concatenate([vp_ref[:, cols], vc_ref[:, cols]], axis=0)
                dov = do_ref[:, cols]
                dob = dov.astype(BF16)
                oo = dov * o_ref[:, cols]
                lv = l_ref[:, cols]
                dlv = dl_ref[:, cols]
                dq_pair = jnp.zeros((BLOCK, LANES), F32)
                dk_pair = jnp.zeros((2 * BLOCK, LANES), F32)
                dv_pair = jnp.zeros((2 * BLOCK, LANES), F32)
                sel2 = [jnp.concatenate([s_, s_], axis=0) for s_ in sel]
                for e in range(2):
                    qh = jnp.where(sel[e], q, jnp.zeros_like(q))
                    s = lax.dot_general(qh, kk, (((1,), (1,)), ((), ())), preferred_element_type=F32) * scale
                    s = jnp.where(mask, s + b_ref[2 * j + e], -1e30)
                    lse_h = jnp.max(jnp.where(sel[e], lv, -jnp.inf), axis=1, keepdims=True)
                    p = jnp.exp(s - lse_h)
                    doh = jnp.where(sel[e], dob, jnp.zeros_like(dob))
                    dp = lax.dot_general(doh, vv, (((1,), (1,)), ((), ())), preferred_element_type=F32)
                    delta = jnp.sum(jnp.where(sel[e], oo, 0.0), axis=1, keepdims=True)
                    dlse_h = jnp.sum(jnp.where(sel[e], dlv, 0.0), axis=1, keepdims=True)
                    ds = p * (dp - delta + dlse_h)
                    db_ref[2 * j + e] += ds
                    dsb = (ds * scale).astype(BF16)
                    dq_pair = jnp.where(sel[e], jnp.dot(dsb, kk, preferred_element_type=F32), dq_pair)
                    dk = lax.dot_general(dsb, q, (((0,), (0,)), ((), ())), preferred_element_type=F32)
                    dk_pair = jnp.where(sel2[e], dk, dk_pair)
                    dv = lax.dot_general(p.astype(BF16), dob, (((0,), (0,)), ((), ())), preferred_element_type=F32)
                    dv_pair = jnp.where(sel2[e], dv, dv_pair)
                dqkv_ref[:, cols] = carry[:, cols].astype(dqkv_ref.dtype)
                dqkv_ref[:, kcols] = (carry[:, kcols] + dk_pair[:BLOCK]).astype(dqkv_ref.dtype)
                dqkv_ref[:, vcols] = (carry[:, vcols] + dv_pair[:BLOCK]).astype(dqkv_ref.dtype)
                carry[:, cols] = dq_pair
                carry[:, kcols] = dk_pair[BLOCK:]
                carry[:, vcols] = dv_pair[BLOCK:]

    cur = lambda i: jnp.minimum(i, nblk - 1)
    prev = lambda i: jnp.maximum(jnp.minimum(i, nblk - 1) - 1, 0)
    blk = lambda c, pv: pl.BlockSpec((BLOCK, D), (lambda i: (prev(i), cb + c)) if pv else (lambda i: (cur(i), cb + c)))
    row = pl.BlockSpec((BLOCK, D), lambda i: (cur(i), 0))
    return pl.pallas_call(
        body, name="at_bwd", grid=(nblk + 1,),
        in_specs=[blk(0, False), blk(1, False), blk(1, True), blk(2, False), blk(2, True),
                  pl.BlockSpec(bias.shape, lambda i: (0, 0, 0)), row, row, row, row],
        out_specs=[pl.BlockSpec((BLOCK, 3 * D), lambda i: (jnp.maximum(i - 1, 0), 0)),
                   pl.BlockSpec(bias.shape, lambda i: (0, 0, 0))],
        out_shape=[jax.ShapeDtypeStruct((T, 3 * D), BF16), jax.ShapeDtypeStruct(bias.shape, F32)],
        scratch_shapes=[pltpu.VMEM((BLOCK, 3 * D), F32)],
        compiler_params=_cparams(("arbitrary",)),
    )(qkv, qkv, qkv, qkv, qkv, bias, o, lse, do, dlse)


def _to_residue_major(a, d):
    if d == 1:
        return a
    T, C = a.shape
    return a.reshape(T // d, d, C).transpose(1, 0, 2).reshape(T, C)


def _from_residue_major(a, d):
    if d == 1:
        return a
    T, C = a.shape
    return a.reshape(d, T // d, C).transpose(1, 0, 2).reshape(T, C)


FFN_TM = 512


def _ffn_up(h, w1t, w3t, carry=None):
    T, Dm = h.shape
    Fw = w1t.shape[0]
    tm, tn = FFN_TM, _pick(Fw, 1408)
    grid = (Fw // tn, T // tm)
    nt = (((1,), (1,)), ((), ()))

    def body(*refs):
        (h_ref, w1_ref, w3_ref), (a_ref, b_ref, u_ref), _, begin, end = _carry_hooks(carry, refs, 3, 3, 2, grid)
        begin()
        hv = h_ref[...]
        a = lax.dot_general(hv, w1_ref[...], nt, preferred_element_type=F32)
        b = lax.dot_general(hv, w3_ref[...], nt, preferred_element_type=F32)
        a_ref[...] = a.astype(a_ref.dtype)
        b_ref[...] = b.astype(b_ref.dtype)
        u_ref[...] = (jax.nn.silu(a) * b).astype(u_ref.dtype)
        end()

    extra = carry if carry is not None else _NO_CARRY
    wspec = pl.BlockSpec((tn, Dm), lambda j, i: (j, 0))
    ospec = pl.BlockSpec((tm, tn), lambda j, i: (i, j))
    res = pl.pallas_call(
        body, name="ffn_up", grid=grid,
        in_specs=[pl.BlockSpec((tm, Dm), lambda j, i: (i, 0)), wspec, wspec] + extra.in_specs,
        out_specs=[ospec] * 3 + extra.out_specs,
        out_shape=[jax.ShapeDtypeStruct((T, Fw), BF16)] * 3 + extra.out_shape, scratch_shapes=extra.scratch,
        compiler_params=_cparams(("arbitrary",) * 2 if carry is not None else ("parallel", "parallel")),
    )(h, w1t, w3t, *extra.arrays)
    return res[:3], res[3:]


def _ffn_down_dx(do, w2, a, b, carry=None):
    T, Dm = do.shape
    Fw = w2.shape[0]
    tm, tn = FFN_TM, _pick(Fw, 1408)
    grid = (Fw // tn, T // tm)

    def body(*refs):
        (do_ref, w2_ref, a_ref, b_ref), (da_ref, db_ref), _, begin, end = _carry_hooks(carry, refs, 4, 2, 2, grid)
        begin()
        du = lax.dot_general(do_ref[...], w2_ref[...], (((1,), (1,)), ((), ())), preferred_element_type=F32)
        av = a_ref[...].astype(F32)
        bv = b_ref[...].astype(F32)
        sg = jax.nn.sigmoid(av)
        silu = av * sg
        da_ref[...] = (du * bv * (sg + silu * (1.0 - sg))).astype(da_ref.dtype)
        db_ref[...] = (du * silu).astype(db_ref.dtype)
        end()

    extra = carry if carry is not None else _NO_CARRY
    ospec = pl.BlockSpec((tm, tn), lambda j, i: (i, j))
    res = pl.pallas_call(
        body, name="ffn_down_dx", grid=grid,
        in_specs=[pl.BlockSpec((tm, Dm), lambda j, i: (i, 0)), pl.BlockSpec((tn, Dm), lambda j, i: (j, 0)), ospec, ospec]
        + extra.in_specs,
        out_specs=[ospec] * 2 + extra.out_specs,
        out_shape=[jax.ShapeDtypeStruct((T, Fw), BF16)] * 2 + extra.out_shape, scratch_shapes=extra.scratch,
        compiler_params=_cparams(("arbitrary",) * 2 if carry is not None else ("parallel", "parallel")),
    )(do, w2, a, b, *extra.arrays)
    return res[:2], res[2:]


def _pre_norm(x, g):
    return _rows(_f_pre, [x], [g], [(D_MODEL, BF16)], name="pre_norm")[0]


def _close_part(x, o, bias, g_post, g_pre_next, scale, name):
    extra = [] if bias is None else [bias]
    if g_pre_next is None:
        xo, = _rows(_f_post(scale, bias is not None), [x, o], [g_post] + extra, [(D_MODEL, F32)], name=name)
        return xo, None

    post = _f_post(scale, bias is not None)

    def fn(xv, ov, g, *rest):
        xo = post(xv, ov, g, *rest[:-1])[0]
        return xo, _rms(xo, rest[-1])

    return _rows(fn, [x, o], [g_post] + extra + [g_pre_next], [(D_MODEL, F32), (D_MODEL, BF16)], name=name)


def _ffn_fwd(x, h, p, carry_up=None, carry_down=None):
    (a, b, u), got_up = _ffn_up(h, p["w1"], p["w3"], carry_up)
    o = _mm(u, p["w2"], name="ffn_down", carry=carry_down)
    got_down = None
    if carry_down is not None:
        o, got_down = o
    return o, (x, h, a, b, u, o), got_up, got_down


def _ffn_bwd(saved, p, dxo, carry_a=None, carry_b=None):
    x, h, a, b, u, o = saved
    (do,), (dg_post,) = _rows_vjp(_f_post_term(0.5, False), [o], [p["g_post"]], [dxo], dtypes=[BF16], name="ffn_post_bwd")
    (da, db), got_a = _ffn_down_dx(do, p["w2"], a, b, carry_a)
    dw2 = _mm(u, do, ta=True, out_dtype=BF16, name="ffn_down_dw")
    dh = _mm((da, db), (p["w1"], p["w3"]), name="ffn_up_dx", carry=carry_b)
    got_b = None
    if carry_b is not None:
        dh, got_b = dh
    dw1 = _mm(da, h, ta=True, out_dtype=BF16, name="ffn_up_dw")
    dw3 = _mm(db, h, ta=True, out_dtype=BF16, name="ffn_up_dw")
    (dx,), (dg_pre,) = _rows_vjp(_f_pre, [x], [p["g_pre"]], [dh], dtypes=[F32], adds={0: dxo}, name="ffn_pre_bwd")
    return dx, {"w1": dw1, "w3": dw3, "w2": dw2, "g_pre": dg_pre, "g_post": dg_post}, got_a, got_b


def _expand_blocks(w, rows_first):
    w = w.reshape(S5_NB, S5_GB, S5_GROUP, S5_STATE)
    eye = jnp.eye(S5_GB, dtype=F32)
    if rows_first:
        e = w[:, :, :, None, :] * eye[None, :, None, :, None]
        return e.reshape(S5_NB, S5_GB * S5_GROUP, S5_BW)
    e = jnp.transpose(w, (0, 1, 3, 2))[:, :, :, None, :] * eye[None, :, None, :, None]
    return e.reshape(S5_NB, S5_BW, S5_GB * S5_GROUP)


def _extract_blocks(e, rows_first):
    eye = jnp.eye(S5_GB, dtype=F32)
    if rows_first:
        e = e.reshape(S5_NB, S5_GB, S5_GROUP, S5_GB, S5_STATE)
        w = jnp.sum(e * eye[None, :, None, :, None], axis=3)
    else:
        e = e.reshape(S5_NB, S5_GB, S5_STATE, S5_GB, S5_GROUP)
        w = jnp.transpose(jnp.sum(e * eye[None, :, None, :, None], axis=3), (0, 1, 3, 2))
    return w.reshape(S5_GROUPS, S5_GROUP, S5_STATE)


def _s5_prep(p):
    G, P, HG = S5_GROUPS, S5_STATE, S5_GROUP
    args = [p["a_re"].reshape(G, 1, P), p["a_im"].reshape(G, 1, P), p["log_dt"].reshape(G, 1, 1),
            jnp.transpose(p["b_re"], (0, 2, 1)), jnp.transpose(p["b_im"], (0, 2, 1))]
    abr, abi, bbr, bbi = _small(_f_s5_disc, args, [(G, 1, P)] * 2 + [(G, HG, P)] * 2, name="s5_disc")
    return args, abr.reshape(1, G * P), abi.reshape(1, G * P), bbr, bbi


def _s5_fwd(h, p):
    disc_args, abr, abi, bbr, bbi = _s5_prep(p)
    c_re, c_im = p["c_re"], p["c_im"]
    u = _mm(h, p["w_in"], name="s5_in")
    bur = _bd([u], [_expand_blocks(bbr, True)], name="s5_bu")
    bui = _bd([u], [_expand_blocks(bbi, True)], name="s5_bu")
    sr, si = _scan_fwd(bur, bui, abr, abi)
    ylin = _bd([sr, si], [_expand_blocks(c_re, False), _expand_blocks(-c_im, False)], name="s5_y")
    y, = _rows(_f_s5_gelu, [ylin, u], [p["d"]], [(D_MODEL, F32)], name="s5_gelu")
    gl = _mm(y, p["w_glu"], name="s5_glu_mm")
    z, = _rows(_f_s5_glu, [y, gl], [p["b_glu"]], [(D_MODEL, BF16)], name="s5_glu")
    m = _mm(z, p["w_out"], name="s5_out")
    return m, None, (h, disc_args, abr, abi, bbr, bbi, u, sr, si, ylin, y, gl, z)


def _s5_bwd(saved, p, dm):
    h, disc_args, abr, abi, bbr, bbi, u, sr, si, ylin, y, gl, z = saved
    c_re, c_im = p["c_re"], p["c_im"]
    dz = _mm(dm, p["w_out"], tb=True, name="s5_out_dx")
    dw_out = _mm(z, dm, ta=True, out_dtype=BF16, name="s5_out_dw")
    (dy1, dgl), (db_glu,) = _rows_vjp(_f_s5_glu, [y, gl], [p["b_glu"]], [dz], dtypes=[F32, BF16], name="s5_glu_bwd")
    dy2 = _mm(dgl, p["w_glu"], tb=True, name="s5_glu_dx")
    dw_glu = _mm(y, dgl, ta=True, out_dtype=BF16, name="s5_glu_dw")
    (dylin, du1), (dd,) = _rows_vjp(_f_s5_gelu, [ylin, u], [p["d"]], [(dy1, dy2)], dtypes=[F32, F32], name="s5_gelu_bwd")
    gr = _bd([dylin], [jnp.transpose(_expand_blocks(c_re, False), (0, 2, 1))], name="s5_y_dx")
    gi = _bd([dylin], [jnp.transpose(_expand_blocks(-c_im, False), (0, 2, 1))], name="s5_y_dx")
    dc_re = _extract_blocks(_bd_wgrad(sr, dylin, S5_BW, LANES, name="s5_y_dw"), False)
    dc_im = -_extract_blocks(_bd_wgrad(si, dylin, S5_BW, LANES, name="s5_y_dw"), False)
    lr, li, dabr, dabi = _scan_bwd(gr, gi, sr, si, abr, abi)
    du = _bd([lr, li], [jnp.transpose(_expand_blocks(bbr, True), (0, 2, 1)),
                        jnp.transpose(_expand_blocks(bbi, True), (0, 2, 1))], add=du1, out_dtype=BF16, name="s5_bu_dx")
    dbbr = _extract_blocks(_bd_wgrad(u, lr, LANES, S5_BW, name="s5_bu_dw"), True)
    dbbi = _extract_blocks(_bd_wgrad(u, li, LANES, S5_BW, name="s5_bu_dw"), True)
    G, P = S5_GROUPS, S5_STATE
    dar, dai, dldt, dbr, dbi = _small_vjp(_f_s5_disc, disc_args,
                                          [dabr.reshape(G, 1, P), dabi.reshape(G, 1, P), dbbr, dbbi], name="s5_disc_bwd")
    dh = _mm(du, p["w_in"], tb=True, name="s5_in_dx")
    dw_in = _mm(h, du, ta=True, out_dtype=BF16, name="s5_in_dw")
    grads = {"w_in": dw_in, "w_glu": dw_glu, "w_out": dw_out, "b_glu": db_glu, "d": dd,
             "a_re": dar.reshape(G, P), "a_im": dai.reshape(G, P), "log_dt": dldt.reshape(G),
             "b_re": jnp.transpose(dbr, (0, 2, 1)), "b_im": jnp.transpose(dbi, (0, 2, 1)),
             "c_re": dc_re, "c_im": dc_im}
    return dh, grads


def _cv_fwd(h, p):
    z0 = _mm(h, p["w_in"], tb=True, name="cv_in")
    zg, = _rows(_f_cv_glu, [z0], [p["b_in"]], [(D_MODEL, F32)], name="cv_glu")
    zc = _conv_fwd(zg, p["dw"], p["dw_b"])
    zl, = _rows(_f_cv_ln, [zc], [p["ln_g"], p["ln_b"]], [(D_MODEL, BF16)], name="cv_ln")
    m = _mm(zl, p["w_out"], name="cv_out")
    return m, p["b_out"], (h, z0, zg, zc, zl)


def _cv_bwd(saved, p, dm):
    h, z0, zg, zc, zl = saved
    dzl = _mm(dm, p["w_out"], tb=True, name="cv_out_dx")
    dw_out = _mm(zl, dm, ta=True, out_dtype=BF16, name="cv_out_dw")
    (dzc,), (dln_g, dln_b) = _rows_vjp(_f_cv_ln, [zc], [p["ln_g"], p["ln_b"]], [dzl], dtypes=[F32], name="cv_ln_bwd")
    dzg, ddw, ddw_b = _conv_bwd(dzc, zg, p["dw"])
    (dz0,), (db_in,) = _rows_vjp(_f_cv_glu, [z0], [p["b_in"]], [dzg], dtypes=[BF16], name="cv_glu_bwd")
    dh = _mm(dz0, p["w_in"], name="cv_in_dx")
    dw_in = _mm(dz0, h, ta=True, out_dtype=BF16, name="cv_in_dw")
    return dh, {"w_in": dw_in, "b_in": db_in, "dw": ddw, "dw_b": ddw_b, "ln_g": dln_g, "ln_b": dln_b, "w_out": dw_out}


def _gm_fwd(h, p):
    z0 = _mm(h, p["w_in"], tb=True, name="gm_in")
    u, v = _rows(_f_gm_in, [z0], [p["b_in"], p["ln_g"], p["ln_b"]], [(GM_E, F32), (GM_E, BF16)], name="gm_act")
    bs_col = p["b_s"].reshape(GM_HEADS, GM_CHUNK, 1)
    us = _gm_sg_fwd(u, v, p["w_s"], bs_col)
    m = _mm(us, p["w_out"], name="gm_out")
    return m, p["b_out"], (h, z0, u, v, us, bs_col)


def _gm_bwd(saved, p, dm):
    h, z0, u, v, us, bs_col = saved
    dus = _mm(dm, p["w_out"], tb=True, name="gm_out_dx")
    dw_out = _mm(us, dm, ta=True, out_dtype=BF16, name="gm_out_dw")
    du, dv, dw_s, db_s = _gm_sg_bwd(dus, u, v, p["w_s"], bs_col)
    (dz0,), (db_in, dln_g, dln_b) = _rows_vjp(_f_gm_in, [z0], [p["b_in"], p["ln_g"], p["ln_b"]], [du, dv],
                                              dtypes=[BF16], name="gm_act_bwd")
    dh = _mm(dz0, p["w_in"], name="gm_in_dx")
    dw_in = _mm(dz0, h, ta=True, out_dtype=BF16, name="gm_in_dw")
    return dh, {"w_in": dw_in, "b_in": db_in, "ln_g": dln_g, "ln_b": dln_b, "w_s": dw_s, "b_s": db_s[:, :, 0],
                "w_out": dw_out}


def _at_fwd_mixer(h, p):
    T = h.shape[0]
    D = D_MODEL
    qkv = _mm(h, p["w_qkv"], tb=True, out_dtype=BF16, name="at_qkv")
    res, outs, lses, biases = [], [], [], []
    for g, (window, d) in enumerate(PATTERNS):
        assert window // d == BLOCK and T % (BLOCK * d) == 0
        bias = _at_bias(p["rel_bias"], g, d)
        if d == 1:
            r, cb = qkv, 3 * g
        else:
            r, cb = _to_residue_major(qkv[:, g * 3 * D:(g + 1) * 3 * D], d), 0
        o, lse = _at_fwd(r, bias, T // d // BLOCK, cb)
        res.append((r, cb, o, lse))
        biases.append(bias)
        outs.append(_from_residue_major(o, d))
        lses.append(_from_residue_major(lse, d))
    oc, = _rows(_f_at_combine, outs + lses, [], [(D, BF16)], name="at_combine")
    m = _mm(oc, p["w_out"], name="at_out")
    return m, None, (h, res, biases, outs, lses, oc)


def _at_bwd_mixer(saved, p, dm):
    h, res, biases, outs, lses, oc = saved
    T = h.shape[0]
    doc = _mm(dm, p["w_out"], tb=True, name="at_out_dx")
    dw_out = _mm(oc, dm, ta=True, out_dtype=BF16, name="at_out_dw")
    dol, _ = _rows_vjp(_f_at_combine, outs + lses, [], [doc], dtypes=[F32] * 6, name="at_combine_bwd")
    dqkv, dtab = [], []
    for g, (window, d) in enumerate(PATTERNS):
        r, cb, o_res, lse_res = res[g]
        dq, dbias = _at_bwd(r, biases[g], o_res, lse_res, _to_residue_major(dol[g], d),
                            _to_residue_major(dol[3 + g], d), T // d // BLOCK, cb)
        dqkv.append(_from_residue_major(dq, d))
        dtab.append(_at_bias_bwd(dbias, d))
    dqkv = jnp.concatenate(dqkv, axis=1)
    dh = _mm(dqkv, p["w_qkv"], name="at_qkv_dx")
    dw_qkv = _mm(dqkv, h, ta=True, out_dtype=BF16, name="at_qkv_dw")
    return dh, {"w_qkv": dw_qkv, "w_out": dw_out, "rel_bias": jnp.concatenate(dtab, axis=1)}


_MIXERS = ((_s5_fwd, _s5_bwd), (_cv_fwd, _cv_bwd), (_gm_fwd, _gm_bwd), (_at_fwd_mixer, _at_bwd_mixer))


def _mixer_fwd(x, h, p, kind):
    m, bias, saved = _MIXERS[kind][0](h, p)
    return m, bias, (x, m, bias, saved)


def _mixer_bwd(saved_all, p, kind, dxo):
    x, m, bias, saved = saved_all
    extra = [] if bias is None else [bias]
    (dm,), dpars = _rows_vjp(_f_post_term(1.0, bias is not None), [m], [p["g_post"]] + extra, [dxo], dtypes=[BF16],
                             name="mix_post_bwd")
    dh, grads = _MIXERS[kind][1](saved, p, dm)
    (dx,), (dg_pre,) = _rows_vjp(_f_pre, [x], [p["g_pre"]], [dh], dtypes=[F32], adds={0: dxo}, name="mix_pre_bwd")
    grads["g_pre"] = dg_pre
    grads["g_post"] = dpars[0]
    if bias is not None:
        grads["b_out"] = dpars[1]
    return dx, grads


class _Carry:
    def __init__(self, arrays, kinds):
        self.arrays, self.kinds, self.n = list(arrays), list(kinds), len(arrays)
        hbm = pl.BlockSpec(memory_space=pl.ANY)
        self.in_specs = [hbm] * self.n
        self.out_specs = [hbm] * self.n
        self.out_shape = [jax.ShapeDtypeStruct((N_DEV,) + (a.shape[1:] if k == "a2a" else a.shape), a.dtype)
                          for a, k in zip(arrays, kinds)]
        self.scratch = [pltpu.SemaphoreType.DMA((self.n * (N_DEV - 1),)), pltpu.SemaphoreType.DMA((self.n * (N_DEV - 1),)),
                        pltpu.SemaphoreType.DMA((self.n,))]

    def _copies(self, ins, outs, sems, arrivals):
        send_sems, recv_sems, local_sems = sems
        x, y, c = lax.axis_index("x"), lax.axis_index("y"), lax.axis_index("c")
        me = 4 * x + 2 * y + c
        local, remote = [], []
        for a in range(self.n):
            a2a = self.kinds[a] == "a2a"
            if not arrivals:
                local.append(pltpu.make_async_copy(ins[a].at[me] if a2a else ins[a], outs[a].at[me], local_sems.at[a]))
            for k in range(1, N_DEV):
                px = 1 - x if k & 4 else x
                py = 1 - y if k & 2 else y
                pc = 1 - c if k & 1 else c
                peer = 4 * px + 2 * py + pc
                idx = a * (N_DEV - 1) + k - 1
                remote.append(pltpu.make_async_remote_copy(
                    src_ref=ins[a].at[peer] if a2a else ins[a], dst_ref=outs[a].at[peer if arrivals else me],
                    send_sem=send_sems.at[idx], recv_sem=recv_sems.at[idx], device_id=(px, py, pc),
                    device_id_type=pl.DeviceIdType.MESH))
        return local, remote

    def start(self, ins, outs, sems):
        local, sends = self._copies(ins, outs, sems, False)
        for cp in local + sends:
            cp.start()

    def wait(self, ins, outs, sems):
        local, sends = self._copies(ins, outs, sems, False)
        _, recvs = self._copies(ins, outs, sems, True)
        for cp in sends:
            cp.wait_send()
        for cp in recvs:
            cp.wait_recv()
        for cp in local:
            cp.wait()


class _NoCarry:
    n = 0
    arrays = in_specs = out_specs = out_shape = scratch = []


_NO_CARRY = _NoCarry()


def _carry_hooks(carry, refs, n_in, n_out, grid_rank, grid):
    nc = carry.n if carry is not None else 0
    ins, cin = refs[:n_in], refs[n_in:n_in + nc]
    outs, cout = refs[n_in + nc:n_in + nc + n_out], refs[n_in + nc + n_out:n_in + 2 * nc + n_out]
    rest = refs[n_in + 2 * nc + n_out:]
    scratch, sems = (rest[:len(rest) - 3], rest[len(rest) - 3:]) if nc else (rest, ())

    def at(step_of):
        cond = None
        for ax in range(grid_rank):
            c = pl.program_id(ax) == step_of(ax)
            cond = c if cond is None else cond & c
        return cond

    def begin():
        if nc:
            @pl.when(at(lambda ax: 0))
            def _():
                carry.start(cin, cout, sems)

    def end():
        if nc:
            @pl.when(at(lambda ax: grid[ax] - 1))
            def _():
                carry.wait(cin, cout, sems)

    return ins, outs, scratch, begin, end


def _exchange(arrays, kinds, *, name):
    carry = _Carry(arrays, kinds)

    def body(*refs):
        n = carry.n
        carry.start(refs[:n], refs[n:2 * n], refs[2 * n:])
        carry.wait(refs[:n], refs[n:2 * n], refs[2 * n:])

    return pl.pallas_call(body, name=name, in_specs=carry.in_specs, out_specs=carry.out_specs,
                          out_shape=carry.out_shape, scratch_shapes=carry.scratch)(*arrays)


def _adam(recv, w, m, v, *, name):
    R, C = w.shape
    tr = _pick_rows(R, 128)
    c1 = 1.0 - ADAM_B1 ** ADAM_STEP
    c2 = 1.0 - ADAM_B2 ** ADAM_STEP

    def body(r_ref, w_ref, m_ref, v_ref, g_ref, d_ref, nm_ref, nv_ref):
        g = r_ref[0].astype(F32)
        for q in range(1, N_DEV):
            g = g + r_ref[q].astype(F32)
        mm = ADAM_B1 * m_ref[...] + (1.0 - ADAM_B1) * g
        vv = ADAM_B2 * v_ref[...] + (1.0 - ADAM_B2) * jnp.square(g)
        m_hat = mm / c1
        v_hat = vv / c2
        g_ref[...] = g
        d_ref[...] = -ADAM_LR * (m_hat / (jnp.sqrt(v_hat) + ADAM_EPS) + ADAM_WD * w_ref[...])
        nm_ref[...] = mm
        nv_ref[...] = vv

    blk = pl.BlockSpec((tr, C), lambda i: (i, 0))
    return pl.pallas_call(
        body, name=name, grid=(R // tr,),
        in_specs=[pl.BlockSpec((N_DEV, tr, C), lambda i: (0, i, 0)), blk, blk, blk], out_specs=[blk] * 4,
        out_shape=[jax.ShapeDtypeStruct((R, C), F32)] * 4,
        compiler_params=_cparams(("parallel",)),
    )(recv, w, m, v)


PACK_COLS = 1024


def _padded(n):
    return -(-n // PACK_ALIGN) * PACK_ALIGN


def _pack_flat(pieces):
    flat = jnp.concatenate([p.reshape(-1) for p in pieces])
    n = flat.shape[0]
    return jnp.pad(flat, (0, _padded(n) - n)).reshape(-1, PACK_COLS)


def _shard_shape(shape, axis):
    s = list(shape)
    assert s[axis] % N_DEV == 0
    s[axis] //= N_DEV
    return tuple(s)


def _split_full(full, axis):
    s = full.shape
    r = full.reshape(s[:axis] + (N_DEV, s[axis] // N_DEV) + s[axis + 1:])
    return jnp.moveaxis(r, axis, 0)


def _merge_full(parts, axis):
    r = jnp.moveaxis(parts, 0, axis)
    s = r.shape
    return r.reshape(s[:axis] + (s[axis] * s[axis + 1],) + s[axis + 2:])


def _pack_full(entries, grads):
    flat = jnp.concatenate([_split_full(grads[k].reshape(shape), axis).reshape(N_DEV, -1)
                            for k, shape, axis in entries], axis=1)
    n = flat.shape[1]
    return jnp.pad(flat, ((0, 0), (0, _padded(n) - n))).reshape(N_DEV, -1, PACK_COLS)


def _unpack_gathered(entries, buf):
    flat = buf.reshape(N_DEV, -1)
    out, pos = {}, 0
    for k, shape, axis in entries:
        ss = _shard_shape(shape, axis)
        n = int(np.prod(ss))
        out[k] = _merge_full(flat[:, pos:pos + n].reshape((N_DEV,) + ss), axis)
        pos += n
    return out


def _unpack_shard(entries, buf):
    flat = buf.reshape(-1)
    out, pos = {}, 0
    for k, shape, axis in entries:
        ss = _shard_shape(shape, axis)
        n = int(np.prod(ss))
        out[k] = flat[pos:pos + n].reshape(ss)
        pos += n
    return out


def _unpack_flat(entries, buf):
    flat = buf.reshape(-1)
    out, pos = {}, 0
    for k, shape in entries:
        n = int(np.prod(shape))
        out[k] = flat[pos:pos + n].reshape(shape)
        pos += n
    return out


D, FF = D_MODEL, D_FF
_FFN_MATS = (("w1", (D, FF), 1), ("w3", (D, FF), 1), ("w2", (FF, D), 0))
_NORM_VECS = (("g_pre", (D,), 0), ("g_post", (D,), 0))
_MIX_MATS = (
    (("w_in", (D, D), 0), ("w_glu", (D, D), 0), ("w_out", (D, D), 0)),
    (("w_in", (D, 2 * D), 1), ("w_out", (D, D), 0)),
    (("w_in", (D, 2 * GM_E), 1), ("w_out", (GM_E, D), 0)),
    (("w_qkv", (D, 9 * D), 1), ("w_out", (D, D), 0)),
)
_MIX_VECS = (
    (),
    (("b_in", (2 * D,), 0), ("dw", (CONV_W, D), 1), ("dw_b", (D,), 0), ("ln_g", (D,), 0), ("ln_b", (D,), 0),
     ("b_out", (D,), 0)),
    (("b_in", (2 * GM_E,), 0), ("ln_g", (GM_E,), 0), ("ln_b", (GM_E,), 0), ("b_out", (D,), 0)),
    (),
)
_REPLICATED = (
    ("rel_bias", 3, "rel_bias", (NUM_BUCKETS, 3 * AT_HEADS)),
    ("s5_a_re", 0, "a_re", (S5_GROUPS, S5_STATE)), ("s5_a_im", 0, "a_im", (S5_GROUPS, S5_STATE)),
    ("s5_log_dt", 0, "log_dt", (S5_GROUPS,)),
    ("s5_b_re", 0, "b_re", (S5_GROUPS, S5_STATE, S5_GROUP)), ("s5_b_im", 0, "b_im", (S5_GROUPS, S5_STATE, S5_GROUP)),
    ("s5_c_re", 0, "c_re", (S5_GROUPS, S5_GROUP, S5_STATE)), ("s5_c_im", 0, "c_im", (S5_GROUPS, S5_GROUP, S5_STATE)),
    ("s5_d", 0, "d", (D,)), ("s5_b_glu", 0, "b_glu", (D,)),
    ("gm_w_s", 2, "w_s", (GM_HEADS, GM_CHUNK, GM_CHUNK)), ("gm_b_s", 2, "b_s", (GM_HEADS, GM_CHUNK)),
)
_MIX_PREFIX = ("s5_", "cv_", "gm_", "at_")
_TWIN_WEIGHTS = ('norm_pre', 'norm_post', 'ffn_w1', 'ffn_w3', 'ffn_w2', 'rel_bias', 's5_w_in', 's5_a_re', 's5_a_im',
                 's5_log_dt', 's5_b_re', 's5_b_im', 's5_c_re', 's5_c_im', 's5_d', 's5_w_glu', 's5_b_glu', 's5_w_out',
                 'cv_w_in', 'cv_b_in', 'cv_dw', 'cv_dw_b', 'cv_ln_g', 'cv_ln_b', 'cv_w_out', 'cv_b_out', 'gm_w_in',
                 'gm_b_in', 'gm_ln_g', 'gm_ln_b', 'gm_w_s', 'gm_b_s', 'gm_w_out', 'gm_b_out', 'at_w_qkv', 'at_w_out')


def _part_entries(part):
    if part[0] == "ffn":
        return _FFN_MATS, _NORM_VECS
    kind = part[1] % 4
    return _MIX_MATS[kind], _NORM_VECS + _MIX_VECS[kind]


def _part_shards(part, get):
    if part[0] == "ffn":
        _, i, j = part
        n = 0 if j == 0 else 2
        return {"w1": get("ffn_w1")[i, j], "w3": get("ffn_w3")[i, j], "w2": get("ffn_w2")[i, j],
                "g_pre": get("norm_pre")[i, n], "g_post": get("norm_post")[i, n]}
    _, i = part
    kind, j = i % 4, i // 4
    out = {"g_pre": get("norm_pre")[i, 1], "g_post": get("norm_post")[i, 1]}
    for k, _, _ in _MIX_MATS[kind] + _MIX_VECS[kind]:
        out[k] = get(_MIX_PREFIX[kind] + k)[j]
    return out


def _parts():
    parts = []
    for i in range(DEPTH):
        parts += [("ffn", i, 0), ("mix", i), ("ffn", i, 1)]
    return parts


def _as_par(v):
    return v.reshape(1, -1)


def _prepare_part(part, full, rep):
    if part[0] == "ffn":
        return {"w1": full["w1"], "w3": full["w3"], "w2": full["w2"],
                "g_pre": _as_par(full["g_pre"]), "g_post": _as_par(full["g_post"])}
    kind = part[1] % 4
    p = {"g_pre": _as_par(full["g_pre"]), "g_post": _as_par(full["g_post"])}
    for k, _, _ in _MIX_MATS[kind]:
        p[k] = full[k]
    for k, _, _ in _MIX_VECS[kind]:
        p[k] = _as_par(full[k]) if k != "dw" else jnp.pad(full[k], ((0, CONV_HALO - CONV_W), (0, 0)))
    if kind == 0:
        for k in ("a_re", "a_im", "log_dt", "b_re", "b_im"):
            p[k] = rep[k]
        p["c_re"], p["c_im"] = rep["c_re"], rep["c_im"]
        p["d"], p["b_glu"] = _as_par(rep["d"]), _as_par(rep["b_glu"])
    elif kind == 2:
        p["w_s"], p["b_s"] = rep["w_s"], rep["b_s"]
    elif kind == 3:
        p["rel_bias"] = rep["rel_bias"]
    return p


def _finish_grads(part, grads):
    out = dict(grads)
    for k in ("g_pre", "g_post", "b_in", "dw_b", "ln_g", "ln_b", "b_out", "d", "b_glu"):
        if k in out:
            out[k] = out[k].reshape(-1)
    if "dw" in out:
        out["dw"] = out["dw"][:CONV_W]
    return out


def _step(x, tgt, inputs, moments_m, moments_v):
    parts = _parts()
    rep = {}
    for name, kind, key, shape in _REPLICATED:
        rep[key] = inputs[name][0] if name != "rel_bias" else inputs[name]

    def stored(part, get):
        mats, vecs = _part_entries(part)
        sh = _part_shards(part, get)
        return [sh[k].T if axis == 1 else sh[k] for k, _, axis in mats], _pack_flat([sh[k] for k, _, _ in vecs])

    stored_w = [stored(part, lambda n: inputs[n]) for part in parts]

    def gather_of(idx):
        wmats, wv = stored_w[idx]
        return _Carry([w.astype(BF16) for w in wmats] + [wv], ["bcast"] * (len(wmats) + 1))

    def gathered(idx, bufs):
        mats, vecs = _part_entries(parts[idx])
        full = {k: b.reshape(-1, b.shape[-1]) for (k, _, _), b in zip(mats, bufs)}
        full.update(_unpack_gathered(vecs, bufs[-1]))
        return _prepare_part(parts[idx], full, rep)

    params = [None] * len(parts)
    first = gather_of(0)
    params[0] = gathered(0, _exchange(first.arrays, first.kinds, name="gather_first"))
    saved = []
    xs = x
    h = _pre_norm(xs, params[0]["g_pre"])
    for idx, part in enumerate(parts):
        if part[0] == "ffn":
            ahead = [i for i in (idx + 1, idx + 2) if i < len(parts) and params[i] is None]
            if part[2] == 1:
                ahead = ahead[:1]
            ahead = ahead[::-1]
            c_up = gather_of(ahead[0]) if ahead else None
            c_down = gather_of(ahead[1]) if len(ahead) > 1 else None
            o, s, got_up, got_down = _ffn_fwd(xs, h, params[idx], c_up, c_down)
            bias, scale = None, 0.5
            if c_up is not None:
                params[ahead[0]] = gathered(ahead[0], got_up)
            if c_down is not None:
                params[ahead[1]] = gathered(ahead[1], got_down)
        else:
            o, bias, s = _mixer_fwd(xs, h, params[idx], part[1] % 4)
            scale = 1.0
        saved.append(s)
        g_next = params[idx + 1]["g_pre"] if idx + 1 < len(parts) else None
        xs, h = _close_part(xs, o, bias, params[idx]["g_post"], g_next, scale, part[0] + "_post")
    dh, loss_vec = _loss_call(xs, tgt)
    loss_local = loss_vec[0, 0]

    results = {}
    rep_grads = {}

    def scatter_of(idx, grads):
        mats, vecs = _part_entries(parts[idx])
        gm = [grads[k].reshape(N_DEV, -1, grads[k].shape[-1]) for k, _, _ in mats]
        return _Carry(gm + [_pack_full(vecs, grads)], ["a2a"] * (len(gm) + 1))

    def update(idx, bufs):
        part = parts[idx]
        mats, vecs = _part_entries(part)
        wmats, wv = stored_w[idx]
        mmats, mv = stored(part, lambda n: moments_m[n])
        vmats, vv = stored(part, lambda n: moments_v[n])
        res = [dict() for _ in range(4)]
        for (k, _, axis), buf, w_, m_, v_ in zip(mats, bufs, wmats, mmats, vmats):
            for r, o in zip(res, _adam(buf, w_, m_, v_, name="adam_mat")):
                r[k] = o.T if axis == 1 else o
        for r, o in zip(res, _adam(bufs[-1], wv, mv, vv, name="adam_vecs")):
            r.update(_unpack_shard(vecs, o))
        results[part] = res

    rep_entries = [(name, shape) for name, _, _, shape in _REPLICATED]
    rg = None
    pending = []
    for idx in range(len(parts) - 1, -1, -1):
        part, p = parts[idx], params[idx]
        if part[0] == "ffn":
            riders = pending[:2]
            pending = pending[2:]
            c_a = riders[0][1] if riders else None
            c_b = riders[1][1] if len(riders) > 1 else None
            if idx == 0 and c_a is not None:
                c_a = _Carry(c_a.arrays + [_pack_flat([rep_grads[name] for name, _ in rep_entries])], c_a.kinds + ["bcast"])
            dh, grads, got_a, got_b = _ffn_bwd(saved[idx], p, dh, c_a, c_b)
            if idx == 0 and c_a is not None:
                rg, got_a = got_a[-1], got_a[:-1]
            for (ridx, _), got in zip(riders, (got_a, got_b)):
                update(ridx, got)
        else:
            dh, grads = _mixer_bwd(saved[idx], p, part[1] % 4, dh)
        grads = _finish_grads(part, grads)
        for name, kind, key, shape in _REPLICATED:
            if part[0] == "mix" and kind == part[1] % 4:
                rep_grads[name] = grads[key]
        pending.append((idx, scatter_of(idx, grads)))
    for ridx, c in pending:
        update(ridx, _exchange(c.arrays, c.kinds, name="scatter_last"))

    get_rep = lambda d: _pack_flat([(d[name][0] if name != "rel_bias" else d[name]) for name, _ in rep_entries])
    assert rg is not None
    orep = _adam(rg, get_rep(inputs), get_rep(moments_m), get_rep(moments_v), name="adam_rep")
    rep_out = [_unpack_flat(rep_entries, o) for o in orep]
    return loss_local, dh, results, rep_out


def _assemble(name, results, rep_out, which):
    for rname, _, _, _ in _REPLICATED:
        if rname == name:
            a = rep_out[which][name]
            return a if name == "rel_bias" else a[None]
    if name in ("norm_pre", "norm_post"):
        key = "g_pre" if name == "norm_pre" else "g_post"
        rows = []
        for i in range(DEPTH):
            rows.append(jnp.stack([results[("ffn", i, 0)][which][key], results[("mix", i)][which][key],
                                   results[("ffn", i, 1)][which][key]]))
        return jnp.stack(rows)
    if name.startswith("ffn_"):
        key = name[4:]
        return jnp.stack([jnp.stack([results[("ffn", i, j)][which][key] for j in range(2)]) for i in range(DEPTH)])
    kind = _MIX_PREFIX.index(name[:3])
    layers = [i for i in range(DEPTH) if i % 4 == kind]
    return jnp.stack([results[("mix", i)][which][name[3:]] for i in layers])


def kernel(x, norm_pre, norm_post, ffn_w1, ffn_w3, ffn_w2, rel_bias, s5_w_in, s5_a_re, s5_a_im, s5_log_dt, s5_b_re, s5_b_im, s5_c_re, s5_c_im, s5_d, s5_w_glu, s5_b_glu, s5_w_out, cv_w_in, cv_b_in, cv_dw, cv_dw_b, cv_ln_g, cv_ln_b, cv_w_out, cv_b_out, gm_w_in, gm_b_in, gm_ln_g, gm_ln_b, gm_w_s, gm_b_s, gm_w_out, gm_b_out, at_w_qkv, at_w_out, loss_target, m_norm_pre, m_norm_post, m_ffn_w1, m_ffn_w3, m_ffn_w2, m_rel_bias, m_s5_w_in, m_s5_a_re, m_s5_a_im, m_s5_log_dt, m_s5_b_re, m_s5_b_im, m_s5_c_re, m_s5_c_im, m_s5_d, m_s5_w_glu, m_s5_b_glu, m_s5_w_out, m_cv_w_in, m_cv_b_in, m_cv_dw, m_cv_dw_b, m_cv_ln_g, m_cv_ln_b, m_cv_w_out, m_cv_b_out, m_gm_w_in, m_gm_b_in, m_gm_ln_g, m_gm_ln_b, m_gm_w_s, m_gm_b_s, m_gm_w_out, m_gm_b_out, m_at_w_qkv, m_at_w_out, v_norm_pre, v_norm_post, v_ffn_w1, v_ffn_w3, v_ffn_w2, v_rel_bias, v_s5_w_in, v_s5_a_re, v_s5_a_im, v_s5_log_dt, v_s5_b_re, v_s5_b_im, v_s5_c_re, v_s5_c_im, v_s5_d, v_s5_w_glu, v_s5_b_glu, v_s5_w_out, v_cv_w_in, v_cv_b_in, v_cv_dw, v_cv_dw_b, v_cv_ln_g, v_cv_ln_b, v_cv_w_out, v_cv_b_out, v_gm_w_in, v_gm_b_in, v_gm_ln_g, v_gm_ln_b, v_gm_w_s, v_gm_b_s, v_gm_w_out, v_gm_b_out, v_at_w_qkv, v_at_w_out):
    args = locals()
    inputs = {n: args[n] for n in _TWIN_WEIGHTS}
    moments_m = {n: args["m_" + n] for n in _TWIN_WEIGHTS}
    moments_v = {n: args["v_" + n] for n in _TWIN_WEIGHTS}
    loss_local, dx, results, rep_out = _step(x[0], loss_target[0], inputs, moments_m, moments_v)
    loss = lax.psum(loss_local, AXES)
    out = [loss, dx[None]]
    for which in range(4):
        out += [_assemble(n, results, rep_out, which) for n in _TWIN_WEIGHTS]
    return tuple(out)
```

```python
import functools
import math

import numpy as np

import jax
import jax.numpy as jnp
from jax import lax
from jax.experimental import pallas as pl
from jax.experimental.pallas import tpu as pltpu

F32 = jnp.float32
BF16 = jnp.bfloat16

D_MODEL = 1024
DEPTH = 4
D_FF = 2816
EPS = 1e-6
S5_GROUP = 16
S5_STATE = 64
CONV_W = 31
GM_CHUNK = 128
GM_HEADS = 8
HEAD_DIM = 64
PATTERNS = ((128, 1), (512, 4), (2048, 16))
BLOCK = 128
NUM_BUCKETS = 32
MAX_DISTANCE = 2048
ADAM_LR = 0.001
ADAM_B1 = 0.9
ADAM_B2 = 0.999
ADAM_EPS = 1e-08
ADAM_WD = 0.01
ADAM_STEP = 10

N_DEV = 8
AXES = ("x", "y", "c")
LANES = 128
GM_E = 2 * D_MODEL
S5_GROUPS = D_MODEL // S5_GROUP
S5_GB = LANES // S5_GROUP
S5_NB = D_MODEL // LANES
S5_BW = S5_GB * S5_STATE
S5_NS = S5_GROUPS * S5_STATE
AT_HEADS = D_MODEL // HEAD_DIM
VMEM_LIMIT = 56 * 1024 * 1024
PACK_ALIGN = 16 * 1024


def _cparams(sem):
    return pltpu.CompilerParams(dimension_semantics=sem, vmem_limit_bytes=VMEM_LIMIT)


def _pick(n, cap):
    if n <= cap:
        return n
    best = None
    for t in range(LANES, cap + 1, LANES):
        if n % t == 0:
            best = t
    assert best is not None, (n, cap)
    return best


def _pick_rows(n, cap):
    best = None
    for t in range(16, min(n, cap) + 1, 16):
        if n % t == 0:
            best = t
    assert best is not None, (n, cap)
    return best


MM_VMEM_BUDGET = 40 * 1024 * 1024


def _mm(a, b, *, ta=False, tb=False, out_dtype=F32, name, carry=None):
    a_list = list(a) if isinstance(a, (tuple, list)) else [a]
    b_list = list(b) if isinstance(b, (tuple, list)) else [b]
    n_op = len(a_list)
    assert n_op == len(b_list)
    K, M = a_list[0].shape if ta else a_list[0].shape[::-1]
    N, K2 = b_list[0].shape if tb else b_list[0].shape[::-1]
    assert K == K2, (a_list[0].shape, b_list[0].shape, ta, tb)
    a_bytes = sum(x.dtype.itemsize for x in a_list)
    b_bytes = sum(x.dtype.itemsize for x in b_list)
    o_bytes = jnp.dtype(out_dtype).itemsize

    def vmem(tm, tn, tk, nk):
        acc = tm * tn * 4 if (nk > 1 and out_dtype != F32) else 0
        return 2 * (tm * tk * a_bytes + tk * tn * b_bytes + tm * tn * o_bytes) + acc

    if ta:
        tm, tn = _pick(M, 1408), _pick(N, 1408)
        tk = next(t for t in (2048, 1024, 512, 256) if K % t == 0 and vmem(tm, tn, t, 2) <= MM_VMEM_BUDGET)
    else:
        tm, tn = _pick(M, 512), _pick(N, 1408)
        tk = next(t for t in (K, _pick(K, 4608), _pick(K, 2816), _pick(K, 1024))
                  if vmem(tm, tn, t, K // t) <= MM_VMEM_BUDGET)
    nk = K // tk
    a_spec = pl.BlockSpec((tk, tm), lambda j, i, k: (k, i)) if ta else pl.BlockSpec((tm, tk), lambda j, i, k: (i, k))
    b_spec = pl.BlockSpec((tn, tk), lambda j, i, k: (j, k)) if tb else pl.BlockSpec((tk, tn), lambda j, i, k: (k, j))
    dims = (((0 if ta else 1,), (1 if tb else 0,)), ((), ()))
    use_scratch = nk > 1 and out_dtype != F32
    grid = (N // tn, M // tm, nk)

    def body(*refs):
        ins, (o_ref,), scratch, begin, end = _carry_hooks(carry, refs, 2 * n_op, 1, 3, grid)
        begin()
        p = None
        for a_ref, b_ref in zip(ins[:n_op], ins[n_op:]):
            d = lax.dot_general(a_ref[...].astype(BF16), b_ref[...].astype(BF16), dims, preferred_element_type=F32)
            p = d if p is None else p + d
        if nk == 1:
            o_ref[...] = p.astype(o_ref.dtype)
        else:
            acc = scratch[0] if use_scratch else o_ref
            k = pl.program_id(2)

            @pl.when(k == 0)
            def _():
                acc[...] = p

            @pl.when(k > 0)
            def _():
                acc[...] += p

            if use_scratch:
                @pl.when(k == nk - 1)
                def _():
                    o_ref[...] = acc[...].astype(o_ref.dtype)
        end()

    extra = carry if carry is not None else _NO_CARRY
    res = pl.pallas_call(
        body, name=name, grid=grid, in_specs=[a_spec] * n_op + [b_spec] * n_op + extra.in_specs,
        out_specs=[pl.BlockSpec((tm, tn), lambda j, i, k: (i, j))] + extra.out_specs,
        out_shape=[jax.ShapeDtypeStruct((M, N), out_dtype)] + extra.out_shape,
        scratch_shapes=([pltpu.VMEM((tm, tn), F32)] if use_scratch else []) + extra.scratch,
        compiler_params=_cparams(("arbitrary",) * 3 if carry is not None else ("parallel", "parallel", "arbitrary")),
    )(*a_list, *b_list, *extra.arrays)
    return res[0] if carry is None else (res[0], res[1:])


ROW_TILE_BYTES = 8 * 1024 * 1024


def _row_tile(arrays):
    row_bytes = sum(w * jnp.dtype(dt).itemsize for w, dt in arrays)
    for tile in (256, 128, 64, 32):
        if tile * row_bytes <= ROW_TILE_BYTES:
            return tile
    return 16


def _rows(fn, rows, pars, outs, *, name):
    T = rows[0].shape[0]
    tile = _row_tile([(r.shape[1], r.dtype) for r in rows] + list(outs))
    nr, npar = len(rows), len(pars)

    def body(*refs):
        r = [refs[i][...] for i in range(nr)]
        p = [refs[nr + i][...] for i in range(npar)]
        res = fn(*r, *p)
        for o_ref, o in zip(refs[nr + npar:], res):
            o_ref[...] = o.astype(o_ref.dtype)

    in_specs = [pl.BlockSpec((tile, r.shape[1]), lambda i: (i, 0)) for r in rows]
    in_specs += [pl.BlockSpec(p.shape, lambda i, nd=p.ndim: (0,) * nd) for p in pars]
    return pl.pallas_call(
        body, name=name, grid=(T // tile,), in_specs=in_specs,
        out_specs=[pl.BlockSpec((tile, w), lambda i: (i, 0)) for w, _ in outs],
        out_shape=[jax.ShapeDtypeStruct((T, w), dt) for w, dt in outs],
        compiler_params=_cparams(("parallel",)),
    )(*rows, *pars)


def _rows_vjp(fn, rows, pars, cts, *, dtypes, adds=None, name):
    adds = adds or {}
    cts = [c if isinstance(c, (tuple, list)) else (c,) for c in cts]
    flat_cts = [a for c in cts for a in c]
    add_keys = sorted(adds)
    add_arrs = [adds[k] for k in add_keys]
    want = [i for i, d in enumerate(dtypes) if d is not None]
    T = rows[0].shape[0]
    tile = _row_tile([(a.shape[1], a.dtype) for a in list(rows) + flat_cts + add_arrs]
                     + [(rows[i].shape[1], dtypes[i]) for i in want])
    nr, npar, nc, na = len(rows), len(pars), len(flat_cts), len(add_arrs)

    def body(*refs):
        r = [refs[i][...].astype(F32) for i in range(nr)]
        p = [refs[nr + i][...] for i in range(npar)]
        cvals = [refs[nr + npar + i][...].astype(F32) for i in range(nc)]
        avals = [refs[nr + npar + nc + i][...].astype(F32) for i in range(na)]
        outs = refs[nr + npar + nc + na:]
        ct, pos = [], 0
        for c in cts:
            s = cvals[pos]
            for extra in cvals[pos + 1:pos + len(c)]:
                s = s + extra
            pos += len(c)
            ct.append(s)
        _, vjp = jax.vjp(lambda *a: tuple(fn(*a)), *r, *p)
        g = vjp(tuple(ct))
        for o_ref, i in zip(outs[:len(want)], want):
            gi = g[i]
            if i in adds:
                gi = gi + avals[add_keys.index(i)]
            o_ref[...] = gi.astype(o_ref.dtype)
        first = pl.program_id(0) == 0
        for o_ref, gp in zip(outs[len(want):], g[nr:]):
            @pl.when(first)
            def _(o_ref=o_ref, gp=gp):
                o_ref[...] = gp

            @pl.when(jnp.logical_not(first))
            def _(o_ref=o_ref, gp=gp):
                o_ref[...] += gp

    row_spec = lambda a: pl.BlockSpec((tile, a.shape[1]), lambda i: (i, 0))
    par_spec = lambda a: pl.BlockSpec(a.shape, lambda i, nd=a.ndim: (0,) * nd)
    res = pl.pallas_call(
        body, name=name, grid=(T // tile,),
        in_specs=[row_spec(a) for a in rows] + [par_spec(a) for a in pars] + [row_spec(a) for a in flat_cts + add_arrs],
        out_specs=[row_spec(rows[i]) for i in want] + [par_spec(a) for a in pars],
        out_shape=[jax.ShapeDtypeStruct(rows[i].shape, dtypes[i]) for i in want]
        + [jax.ShapeDtypeStruct(a.shape, F32) for a in pars],
        compiler_params=_cparams(("arbitrary",)),
    )(*rows, *pars, *flat_cts, *add_arrs)
    return res[:len(want)], res[len(want):]


def _small(fn, args, outs, *, name):
    n = len(args)

    def body(*refs):
        res = fn(*[r[...] for r in refs[:n]])
        for o_ref, o in zip(refs[n:], res):
            o_ref[...] = o

    return pl.pallas_call(body, name=name, out_shape=[jax.ShapeDtypeStruct(s, F32) for s in outs],
                          compiler_params=pltpu.CompilerParams(vmem_limit_bytes=VMEM_LIMIT))(*args)


def _small_vjp(fn, args, cts, *, name):
    n, nc = len(args), len(cts)

    def body(*refs):
        _, vjp = jax.vjp(lambda *a: tuple(fn(*a)), *[r[...] for r in refs[:n]])
        g = vjp(tuple(r[...] for r in refs[n:n + nc]))
        for o_ref, gi in zip(refs[n + nc:], g):
            o_ref[...] = gi

    return pl.pallas_call(body, name=name, out_shape=[jax.ShapeDtypeStruct(a.shape, F32) for a in args],
                          compiler_params=pltpu.CompilerParams(vmem_limit_bytes=VMEM_LIMIT))(*args, *cts)


def _rms(x, g):
    return x * lax.rsqrt(jnp.mean(x * x, axis=-1, keepdims=True) + EPS) * g


def _layernorm(x, g, b):
    mu = jnp.mean(x, axis=-1, keepdims=True)
    var = jnp.mean(jnp.square(x - mu), axis=-1, keepdims=True)
    return (x - mu) * lax.rsqrt(var + EPS) * g + b


def _f_pre(x, g):
    return (_rms(x.astype(F32), g),)


def _f_post_term(scale, has_bias):
    def fn(o, g, *b):
        o = o.astype(F32)
        if has_bias:
            o = o + b[0]
        return (scale * _rms(o, g),)
    return fn


def _f_post(scale, has_bias):
    term = _f_post_term(scale, has_bias)

    def fn(x, o, g, *b):
        return (x + term(o, g, *b)[0],)
    return fn


def _f_s5_gelu(ylin, u, d):
    return (jax.nn.gelu(ylin.astype(F32) + d * u.astype(F32)),)


def _f_s5_glu(y, gl, b):
    return (y.astype(F32) * jax.nn.sigmoid(gl.astype(F32) + b),)


def _f_cv_glu(z0, b):
    z = z0.astype(F32) + b
    return (z[:, :D_MODEL] * jax.nn.sigmoid(z[:, D_MODEL:]),)


def _f_cv_ln(zc, g, b):
    return (jax.nn.silu(_layernorm(zc.astype(F32), g, b)),)


def _f_gm_in(z0, b, g, bl):
    z = jax.nn.gelu(z0.astype(F32) + b)
    return z[:, :GM_E], _layernorm(z[:, GM_E:], g, bl)


def _f_at_combine(o0, o1, o2, l0, l1, l2):
    m = jnp.maximum(jnp.maximum(l0, l1), l2)
    e0, e1, e2 = jnp.exp(l0 - m), jnp.exp(l1 - m), jnp.exp(l2 - m)
    return ((e0 * o0 + e1 * o1 + e2 * o2) / (e0 + e1 + e2),)


def _f_s5_disc(ar, ai, ldt, br, bi):
    dt = jnp.exp(ldt)
    mag = jnp.exp(dt * ar)
    abr = mag * jnp.cos(dt * ai)
    abi = mag * jnp.sin(dt * ai)
    den = ar * ar + ai * ai
    nr = abr - 1.0
    f_re = (nr * ar + abi * ai) / den
    f_im = (abi * ar - nr * ai) / den
    return abr, abi, f_re * br - f_im * bi, f_re * bi + f_im * br


def _loss_call(y, tgt):
    T, D = y.shape
    tile = 256

    def body(y_ref, t_ref, dy_ref, l_ref):
        err = y_ref[...] - t_ref[...]
        dy_ref[...] = err * (1.0 / D)
        part = 0.5 * jnp.sum(jnp.mean(err * err, axis=-1, keepdims=True), axis=0, keepdims=True)
        part = jnp.broadcast_to(part, (1, LANES))
        first = pl.program_id(0) == 0

        @pl.when(first)
        def _():
            l_ref[...] = part

        @pl.when(jnp.logical_not(first))
        def _():
            l_ref[...] += part

    return pl.pallas_call(
        body, name="loss", grid=(T // tile,),
        in_specs=[pl.BlockSpec((tile, D), lambda i: (i, 0))] * 2,
        out_specs=[pl.BlockSpec((tile, D), lambda i: (i, 0)), pl.BlockSpec((1, LANES), lambda i: (0, 0))],
        out_shape=[jax.ShapeDtypeStruct((T, D), F32), jax.ShapeDtypeStruct((1, LANES), F32)],
        compiler_params=_cparams(("arbitrary",)),
    )(y, tgt)


def _bd(xs, ws, *, add=None, out_dtype=F32, name):
    T = xs[0].shape[0]
    nb, kw, nw = ws[0].shape
    tm = 256
    n = len(xs)

    def body(*refs):
        o_ref = refs[-1]
        for j in range(nb):
            acc = None
            for x_ref, w_ref in zip(refs[:n], refs[n:2 * n]):
                p = jnp.dot(x_ref[:, j * kw:(j + 1) * kw].astype(BF16), w_ref[j].astype(BF16),
                            preferred_element_type=F32)
                acc = p if acc is None else acc + p
            if add is not None:
                acc = acc + refs[2 * n][:, j * nw:(j + 1) * nw].astype(F32)
            o_ref[:, j * nw:(j + 1) * nw] = acc.astype(o_ref.dtype)

    in_specs = [pl.BlockSpec((tm, nb * kw), lambda i: (i, 0)) for _ in xs]
    in_specs += [pl.BlockSpec((nb, kw, nw), lambda i: (0, 0, 0)) for _ in ws]
    args = list(xs) + list(ws)
    if add is not None:
        in_specs.append(pl.BlockSpec((tm, nb * nw), lambda i: (i, 0)))
        args.append(add)
    return pl.pallas_call(
        body, name=name, grid=(T // tm,), in_specs=in_specs,
        out_specs=pl.BlockSpec((tm, nb * nw), lambda i: (i, 0)),
        out_shape=jax.ShapeDtypeStruct((T, nb * nw), out_dtype),
        compiler_params=_cparams(("parallel",)),
    )(*args)


def _bd_wgrad(x, dy, kw, nw, *, name):
    T = x.shape[0]
    nb = x.shape[1] // kw
    tk = 512

    def body(x_ref, dy_ref, o_ref):
        first = pl.program_id(0) == 0
        for j in range(nb):
            p = lax.dot_general(x_ref[:, j * kw:(j + 1) * kw].astype(BF16), dy_ref[:, j * nw:(j + 1) * nw].astype(BF16),
                                (((0,), (0,)), ((), ())), preferred_element_type=F32)

            @pl.when(first)
            def _(j=j, p=p):
                o_ref[j] = p

            @pl.when(jnp.logical_not(first))
            def _(j=j, p=p):
                o_ref[j] += p

    return pl.pallas_call(
        body, name=name, grid=(T // tk,),
        in_specs=[pl.BlockSpec((tk, nb * kw), lambda k: (k, 0)), pl.BlockSpec((tk, nb * nw), lambda k: (k, 0))],
        out_specs=pl.BlockSpec((nb, kw, nw), lambda k: (0, 0, 0)),
        out_shape=jax.ShapeDtypeStruct((nb, kw, nw), F32),
        compiler_params=_cparams(("arbitrary",)),
    )(x, dy)


SCAN_COLS = min(S5_NS, 4096)
SCAN_ROWS = 128


def _scan_fwd(bur, bui, ar, ai):
    T, NS = bur.shape
    cw, tc = SCAN_COLS, SCAN_ROWS

    def body(bur_ref, bui_ref, ar_ref, ai_ref, sr_ref, si_ref, cr, ci):
        @pl.when(pl.program_id(1) == 0)
        def _():
            cr[...] = jnp.zeros_like(cr)
            ci[...] = jnp.zeros_like(ci)

        a_r, a_i = ar_ref[...], ai_ref[...]

        def step8(t8, carry):
            sr, si = carry
            base = pl.multiple_of(t8 * 8, 8)
            for r in range(8):
                br = bur_ref[pl.ds(base + r, 1), :]
                bi = bui_ref[pl.ds(base + r, 1), :]
                sr, si = a_r * sr - a_i * si + br, a_r * si + a_i * sr + bi
                sr_ref[pl.ds(base + r, 1), :] = sr
                si_ref[pl.ds(base + r, 1), :] = si
            return sr, si

        sr, si = lax.fori_loop(0, tc // 8, step8, (cr[...], ci[...]))
        cr[...] = sr
        ci[...] = si

    blk = pl.BlockSpec((tc, cw), lambda c, t: (t, c))
    vec = pl.BlockSpec((1, cw), lambda c, t: (0, c))
    return pl.pallas_call(
        body, name="s5_scan_fwd", grid=(NS // cw, T // tc), in_specs=[blk, blk, vec, vec], out_specs=[blk, blk],
        out_shape=[jax.ShapeDtypeStruct((T, NS), F32)] * 2,
        scratch_shapes=[pltpu.VMEM((1, cw), F32)] * 2,
        compiler_params=_cparams(("parallel", "arbitrary")),
    )(bur, bui, ar, ai)


def _scan_bwd(gr, gi, sr, si, ar, ai):
    T, NS = gr.shape
    cw, tc = SCAN_COLS, SCAN_ROWS
    nt = T // tc

    def body(gr_ref, gi_ref, sr_ref, si_ref, ar_ref, ai_ref, lr_ref, li_ref, dar_ref, dai_ref, cr, ci):
        @pl.when(pl.program_id(1) == 0)
        def _():
            cr[...] = jnp.zeros_like(cr)
            ci[...] = jnp.zeros_like(ci)
            dar_ref[...] = jnp.zeros_like(dar_ref)
            dai_ref[...] = jnp.zeros_like(dai_ref)

        a_r, a_i = ar_ref[...], ai_ref[...]

        def step8(k, carry):
            lr, li, dar, dai = carry
            base = pl.multiple_of((tc // 8 - 1 - k) * 8, 8)
            for r in range(7, -1, -1):
                s_r = sr_ref[pl.ds(base + r, 1), :]
                s_i = si_ref[pl.ds(base + r, 1), :]
                dar = dar + lr * s_r + li * s_i
                dai = dai + li * s_r - lr * s_i
                g_r = gr_ref[pl.ds(base + r, 1), :]
                g_i = gi_ref[pl.ds(base + r, 1), :]
                lr, li = g_r + a_r * lr + a_i * li, g_i + a_r * li - a_i * lr
                lr_ref[pl.ds(base + r, 1), :] = lr
                li_ref[pl.ds(base + r, 1), :] = li
            return lr, li, dar, dai

        lr, li, dar, dai = lax.fori_loop(0, tc // 8, step8, (cr[...], ci[...], dar_ref[...], dai_ref[...]))
        cr[...] = lr
        ci[...] = li
        dar_ref[...] = dar
        dai_ref[...] = dai

    blk = pl.BlockSpec((tc, cw), lambda c, t: (nt - 1 - t, c))
    vec = pl.BlockSpec((1, cw), lambda c, t: (0, c))
    return pl.pallas_call(
        body, name="s5_scan_bwd", grid=(NS // cw, nt), in_specs=[blk, blk, blk, blk, vec, vec],
        out_specs=[blk, blk, vec, vec],
        out_shape=[jax.ShapeDtypeStruct((T, NS), F32)] * 2 + [jax.ShapeDtypeStruct((1, NS), F32)] * 2,
        scratch_shapes=[pltpu.VMEM((1, cw), F32)] * 2,
        compiler_params=_cparams(("parallel", "arbitrary")),
    )(gr, gi, sr, si, ar, ai)


CONV_ROWS = 256
CONV_HALO = 32
CONV_PAD = CONV_HALO - (CONV_W - 1)
CONV_SUB = 16


def _conv_shifts(ext, sh, n):
    ext[pl.ds(n, 8), :] = jnp.zeros((8, ext.shape[1]), F32)
    for s in range(8):
        sh[s] = ext[pl.ds(s, n), :]


def _conv_rows(sh, start):
    return sh[start % 8, pl.ds(start - start % 8, CONV_SUB), :]


def _conv_fwd(z, dw, dwb):
    T, D = z.shape
    tc, hl = CONV_ROWS, CONV_HALO
    per = tc // hl

    def body(z_ref, zp_ref, dw_ref, b_ref, o_ref, ext, sh):
        i = pl.program_id(0)
        ext[pl.ds(0, hl), :] = jnp.where(i > 0, zp_ref[...], 0.0)
        ext[pl.ds(hl, tc), :] = z_ref[...]
        _conv_shifts(ext, sh, tc + hl)
        for rb in range(tc // CONV_SUB):
            r0 = rb * CONV_SUB
            acc = jnp.zeros((CONV_SUB, D), F32) + b_ref[...]
            for k in range(CONV_W):
                acc = acc + dw_ref[pl.ds(k, 1), :] * _conv_rows(sh, r0 + CONV_PAD + k)
            o_ref[pl.ds(r0, CONV_SUB), :] = acc

    return pl.pallas_call(
        body, name="conv_fwd", grid=(T // tc,),
        in_specs=[pl.BlockSpec((tc, D), lambda i: (i, 0)),
                  pl.BlockSpec((hl, D), lambda i: (jnp.maximum(i * per - 1, 0), 0)),
                  pl.BlockSpec((hl, D), lambda i: (0, 0)), pl.BlockSpec((1, D), lambda i: (0, 0))],
        out_specs=pl.BlockSpec((tc, D), lambda i: (i, 0)),
        out_shape=jax.ShapeDtypeStruct((T, D), F32),
        scratch_shapes=[pltpu.VMEM((tc + hl + 8, D), F32), pltpu.VMEM((8, tc + hl, D), F32)],
        compiler_params=_cparams(("parallel",)),
    )(z, z, dw, dwb)


def _conv_bwd(dout, z, dw):
    T, D = z.shape
    tc, hl = CONV_ROWS, CONV_HALO
    per = tc // hl
    nblk = T // tc

    def body(g_ref, gn_ref, z_ref, zp_ref, dw_ref, dz_ref, ddw_ref, db_ref, gext, zext, gsh, zsh, acc8):
        i = pl.program_id(0)
        gext[pl.ds(0, tc), :] = g_ref[...]
        gext[pl.ds(tc, hl), :] = jnp.where(i < nblk - 1, gn_ref[...], 0.0)
        zext[pl.ds(0, hl), :] = jnp.where(i > 0, zp_ref[...], 0.0)
        zext[pl.ds(hl, tc), :] = z_ref[...]
        _conv_shifts(gext, gsh, tc + hl)
        _conv_shifts(zext, zsh, tc + hl)
        for rb in range(tc // CONV_SUB):
            r0 = rb * CONV_SUB
            acc = jnp.zeros((CONV_SUB, D), F32)
            for k in range(CONV_W):
                acc = acc + dw_ref[pl.ds(k, 1), :] * _conv_rows(gsh, r0 + CONV_W - 1 - k)
            dz_ref[pl.ds(r0, CONV_SUB), :] = acc

        @pl.when(i == 0)
        def _():
            acc8[...] = jnp.zeros_like(acc8)
            db_ref[...] = jnp.zeros_like(db_ref)

        db_ref[...] += jnp.sum(g_ref[...], axis=0, keepdims=True)
        for k in range(CONV_W):
            part = jnp.zeros((8, D), F32)
            for rb in range(tc // CONV_SUB):
                r0 = rb * CONV_SUB
                prod = g_ref[pl.ds(r0, CONV_SUB), :] * _conv_rows(zsh, r0 + CONV_PAD + k)
                for s in range(CONV_SUB // 8):
                    part = part + prod[s * 8:(s + 1) * 8]
            acc8[k] += part

        @pl.when(i == nblk - 1)
        def _():
            ddw_ref[...] = jnp.sum(acc8[...], axis=1)

    return pl.pallas_call(
        body, name="conv_bwd", grid=(nblk,),
        in_specs=[pl.BlockSpec((tc, D), lambda i: (i, 0)),
                  pl.BlockSpec((hl, D), lambda i: (jnp.minimum((i + 1) * per, nblk * per - 1), 0)),
                  pl.BlockSpec((tc, D), lambda i: (i, 0)),
                  pl.BlockSpec((hl, D), lambda i: (jnp.maximum(i * per - 1, 0), 0)),
                  pl.BlockSpec((hl, D), lambda i: (0, 0))],
        out_specs=[pl.BlockSpec((tc, D), lambda i: (i, 0)), pl.BlockSpec((hl, D), lambda i: (0, 0)),
                   pl.BlockSpec((1, D), lambda i: (0, 0))],
        out_shape=[jax.ShapeDtypeStruct((T, D), F32), jax.ShapeDtypeStruct((hl, D), F32),
                   jax.ShapeDtypeStruct((1, D), F32)],
        scratch_shapes=[pltpu.VMEM((tc + hl + 8, D), F32)] * 2 + [pltpu.VMEM((8, tc + hl, D), F32)] * 2
        + [pltpu.VMEM((hl, 8, D), F32)],
        compiler_params=_cparams(("arbitrary",)),
    )(dout, dout, z, z, dw)


def _gm_causal():
    r = lax.broadcasted_iota(jnp.int32, (GM_CHUNK, GM_CHUNK), 0)
    c = lax.broadcasted_iota(jnp.int32, (GM_CHUNK, GM_CHUNK), 1)
    return r >= c


def _gm_sg_fwd(u, v, ws, bs_col):
    T, E = u.shape
    hw = E // GM_HEADS

    def body(u_ref, v_ref, w_ref, b_ref, o_ref):
        causal = _gm_causal()
        for h in range(GM_HEADS):
            cols = slice(h * hw, (h + 1) * hw)
            w = jnp.where(causal, w_ref[h], 0.0).astype(BF16)
            s = jnp.dot(w, v_ref[:, cols], preferred_element_type=F32) + b_ref[h]
            o_ref[:, cols] = (u_ref[:, cols] * s).astype(o_ref.dtype)

    return pl.pallas_call(
        body, name="gm_sg_fwd", grid=(T // GM_CHUNK,),
        in_specs=[pl.BlockSpec((GM_CHUNK, E), lambda i: (i, 0)), pl.BlockSpec((GM_CHUNK, E), lambda i: (i, 0)),
                  pl.BlockSpec(ws.shape, lambda i: (0, 0, 0)), pl.BlockSpec(bs_col.shape, lambda i: (0, 0, 0))],
        out_specs=pl.BlockSpec((GM_CHUNK, E), lambda i: (i, 0)),
        out_shape=jax.ShapeDtypeStruct((T, E), BF16),
        compiler_params=_cparams(("parallel",)),
    )(u, v, ws, bs_col)


def _gm_sg_bwd(dus, u, v, ws, bs_col):
    T, E = u.shape
    hw = E // GM_HEADS

    def body(g_ref, u_ref, v_ref, w_ref, b_ref, du_ref, dv_ref, dw_ref, db_ref):
        causal = _gm_causal()

        @pl.when(pl.program_id(0) == 0)
        def _():
            dw_ref[...] = jnp.zeros_like(dw_ref)
            db_ref[...] = jnp.zeros_like(db_ref)

        for h in range(GM_HEADS):
            cols = slice(h * hw, (h + 1) * hw)
            w = jnp.where(causal, w_ref[h], 0.0).astype(BF16)
            vh = v_ref[:, cols]
            s = jnp.dot(w, vh, preferred_element_type=F32) + b_ref[h]
            g = g_ref[:, cols]
            du_ref[:, cols] = g * s
            ds = g * u_ref[:, cols]
            dsb = ds.astype(BF16)
            dv_ref[:, cols] = lax.dot_general(w, dsb, (((0,), (0,)), ((), ())), preferred_element_type=F32)
            dwh = lax.dot_general(dsb, vh, (((1,), (1,)), ((), ())), preferred_element_type=F32)
            dw_ref[h] += jnp.where(causal, dwh, 0.0)
            db_ref[h] += jnp.broadcast_to(jnp.sum(ds, axis=1, keepdims=True), (GM_CHUNK, LANES))

    blk = pl.BlockSpec((GM_CHUNK, E), lambda i: (i, 0))
    return pl.pallas_call(
        body, name="gm_sg_bwd", grid=(T // GM_CHUNK,),
        in_specs=[blk, blk, blk, pl.BlockSpec(ws.shape, lambda i: (0, 0, 0)),
                  pl.BlockSpec(bs_col.shape, lambda i: (0, 0, 0))],
        out_specs=[blk, blk, pl.BlockSpec(ws.shape, lambda i: (0, 0, 0)),
                   pl.BlockSpec((GM_HEADS, GM_CHUNK, LANES), lambda i: (0, 0, 0))],
        out_shape=[jax.ShapeDtypeStruct((T, E), F32), jax.ShapeDtypeStruct((T, E), F32),
                   jax.ShapeDtypeStruct(ws.shape, F32), jax.ShapeDtypeStruct((GM_HEADS, GM_CHUNK, LANES), F32)],
        compiler_params=_cparams(("arbitrary",)),
    )(dus, u, v, ws, bs_col)


def _t5_bucket_steps(dilation):
    max_exact = NUM_BUCKETS // 2
    delta = np.arange(BLOCK + 1)
    dist = delta * dilation
    distf = np.maximum(dist, 1).astype(np.float32)
    large = max_exact + (np.log(distf / np.float32(max_exact)) / np.float32(math.log(MAX_DISTANCE / max_exact))
                         * np.float32(NUM_BUCKETS - max_exact)).astype(np.int32)
    large = np.minimum(large, NUM_BUCKETS - 1)
    bucket = np.where(dist < max_exact, dist, large)
    steps = []
    for d in range(1, BLOCK + 1):
        inc = int(bucket[d] - bucket[d - 1])
        assert inc >= 0
        if inc:
            steps.append((d, inc))
    assert int(bucket[0]) == 0
    return steps


def _bucket_map(dilation):
    qi = lax.broadcasted_iota(jnp.int32, (BLOCK, 2 * BLOCK), 0)
    ki = lax.broadcasted_iota(jnp.int32, (BLOCK, 2 * BLOCK), 1)
    delta = qi + BLOCK - ki
    bm = jnp.zeros((BLOCK, 2 * BLOCK), jnp.int32)
    for thr, inc in _t5_bucket_steps(dilation):
        bm = bm + jnp.where(delta >= thr, inc, 0)
    return bm


def _at_bias(table, g, dilation):
    H = AT_HEADS

    def body(t_ref, o_ref):
        bm = _bucket_map(dilation)
        for h in range(H):
            acc = jnp.zeros((BLOCK, 2 * BLOCK), F32)
            for b in range(NUM_BUCKETS):
                acc = jnp.where(bm == b, t_ref[b, g * H + h], acc)
            o_ref[h] = acc

    return pl.pallas_call(body, name="at_bias", in_specs=[pl.BlockSpec(memory_space=pltpu.SMEM)],
                          out_shape=jax.ShapeDtypeStruct((H, BLOCK, 2 * BLOCK), F32))(table)


def _at_bias_bwd(dbias, dilation):
    H = AT_HEADS

    def body(d_ref, o_ref):
        bm = _bucket_map(dilation)
        for h in range(H):
            d = d_ref[h]
            for b in range(NUM_BUCKETS):
                o_ref[b, h] = jnp.sum(jnp.where(bm == b, d, 0.0))

    return pl.pallas_call(body, name="at_bias_bwd", out_specs=pl.BlockSpec(memory_space=pltpu.SMEM),
                          out_shape=jax.ShapeDtypeStruct((NUM_BUCKETS, H), F32))(dbias)


def _at_mask(i, nbs):
    qi = lax.broadcasted_iota(jnp.int32, (BLOCK, 2 * BLOCK), 0)
    ki = lax.broadcasted_iota(jnp.int32, (BLOCK, 2 * BLOCK), 1)
    no_prev = jnp.where(i % nbs == 0, 4 * BLOCK, 0)
    return ((ki < BLOCK) & (ki >= qi + no_prev)) | ((ki >= BLOCK) & (ki - BLOCK <= qi))


def _head_lanes():
    lane = lax.broadcasted_iota(jnp.int32, (BLOCK, LANES), 1)
    return [lane < HEAD_DIM, lane >= HEAD_DIM]


def _at_fwd(qkv, bias, nbs, cb):
    T = qkv.shape[0]
    D = D_MODEL
    npair = D // LANES
    scale = HEAD_DIM ** -0.5

    def body(q_ref, kc_ref, kp_ref, vc_ref, vp_ref, b_ref, o_ref, l_ref):
        i = pl.program_id(0)
        mask = _at_mask(i, nbs)
        sel = _head_lanes()
        for j in range(npair):
            cols = slice(j * LANES, (j + 1) * LANES)
            q = q_ref[:, cols]
            kk = jnp.concatenate([kp_ref[:, cols], kc_ref[:, cols]], axis=0)
            vv = jnp.concatenate([vp_ref[:, cols], vc_ref[:, cols]], axis=0)
            o_pair = jnp.zeros((BLOCK, LANES), F32)
            l_pair = jnp.zeros((BLOCK, LANES), F32)
            for e in range(2):
                qh = jnp.where(sel[e], q, jnp.zeros_like(q))
                s = lax.dot_general(qh, kk, (((1,), (1,)), ((), ())), preferred_element_type=F32) * scale
                s = jnp.where(mask, s + b_ref[2 * j + e], -1e30)
                m = jnp.max(s, axis=1, keepdims=True)
                p = jnp.exp(s - m)
                den = jnp.sum(p, axis=1, keepdims=True)
                o = jnp.dot(p.astype(BF16), vv, preferred_element_type=F32) / den
                o_pair = jnp.where(sel[e], o, o_pair)
                l_pair = jnp.where(sel[e], m + jnp.log(den), l_pair)
            o_ref[:, cols] = o_pair
            l_ref[:, cols] = l_pair

    blk = lambda c, prev: pl.BlockSpec((BLOCK, D), (lambda i: (jnp.maximum(i - 1, 0), cb + c)) if prev
                                       else (lambda i: (i, cb + c)))
    out = pl.BlockSpec((BLOCK, D), lambda i: (i, 0))
    return pl.pallas_call(
        body, name="at_fwd", grid=(T // BLOCK,),
        in_specs=[blk(0, False), blk(1, False), blk(1, True), blk(2, False), blk(2, True),
                  pl.BlockSpec(bias.shape, lambda i: (0, 0, 0))],
        out_specs=[out, out], out_shape=[jax.ShapeDtypeStruct((T, D), F32)] * 2,
        compiler_params=_cparams(("parallel",)),
    )(qkv, qkv, qkv, qkv, qkv, bias)


def _at_bwd(qkv, bias, o, lse, do, dlse, nbs, cb):
    T = qkv.shape[0]
    D = D_MODEL
    nblk = T // BLOCK
    npair = D // LANES
    scale = HEAD_DIM ** -0.5

    def body(q_ref, kc_ref, kp_ref, vc_ref, vp_ref, b_ref, o_ref, l_ref, do_ref, dl_ref, dqkv_ref, db_ref, carry):
        i = pl.program_id(0)

        @pl.when(i == 0)
        def _():
            carry[...] = jnp.zeros_like(carry)
            db_ref[...] = jnp.zeros_like(db_ref)

        @pl.when(i == nblk)
        def _():
            dqkv_ref[...] = carry[...].astype(dqkv_ref.dtype)

        @pl.when(i < nblk)
        def _():
            mask = _at_mask(i, nbs)
            sel = _head_lanes()
            for j in range(npair):
                cols = slice(j * LANES, (j + 1) * LANES)
                kcols = slice(D + j * LANES, D + (j + 1) * LANES)
                vcols = slice(2 * D + j * LANES, 2 * D + (j + 1) * LANES)
                q = q_ref[:, cols]
                kk = jnp.concatenate([kp_ref[:, cols], kc_ref[:, cols]], axis=0)
                vv = jnp.concatenate([vp_ref[:, cols], vc_ref[:, cols]], axis=0)
                dov = do_ref[:, cols]
                dob = dov.astype(BF16)
                oo = dov * o_ref[:, cols]
                lv = l_ref[:, cols]
                dlv = dl_ref[:, cols]
                dq_pair = jnp.zeros((BLOCK, LANES), F32)
                dk_pair = jnp.zeros((2 * BLOCK, LANES), F32)
                dv_pair = jnp.zeros((2 * BLOCK, LANES), F32)
                sel2 = [jnp.concatenate([s_, s_], axis=0) for s_ in sel]
                for e in range(2):
                    qh = jnp.where(sel[e], q, jnp.zeros_like(q))
                    s = lax.dot_general(qh, kk, (((1,), (1,)), ((), ())), preferred_element_type=F32) * scale
                    s = jnp.where(mask, s + b_ref[2 * j + e], -1e30)
                    lse_h = jnp.max(jnp.where(sel[e], lv, -jnp.inf), axis=1, keepdims=True)
                    p = jnp.exp(s - lse_h)
                    doh = jnp.where(sel[e], dob, jnp.zeros_like(dob))
                    dp = lax.dot_general(doh, vv, (((1,), (1,)), ((), ())), preferred_element_type=F32)
                    delta = jnp.sum(jnp.where(sel[e], oo, 0.0), axis=1, keepdims=True)
                    dlse_h = jnp.sum(jnp.where(sel[e], dlv, 0.0), axis=1, keepdims=True)
                    ds = p * (dp - delta + dlse_h)
                    db_ref[2 * j + e] += ds
                    dsb = (ds * scale).astype(BF16)
                    dq_pair = jnp.where(sel[e], jnp.dot(dsb, kk, preferred_element_type=F32), dq_pair)
                    dk = lax.dot_general(dsb, q, (((0,), (0,)), ((), ())), preferred_element_type=F32)
                    dk_pair = jnp.where(sel2[e], dk, dk_pair)
                    dv = lax.dot_general(p.astype(BF16), dob, (((0,), (0,)), ((), ())), preferred_element_type=F32)
                    dv_pair = jnp.where(sel2[e], dv, dv_pair)
                dqkv_ref[:, cols] = carry[:, cols].astype(dqkv_ref.dtype)
                dqkv_ref[:, kcols] = (carry[:, kcols] + dk_pair[:BLOCK]).astype(dqkv_ref.dtype)
                dqkv_ref[:, vcols] = (carry[:, vcols] + dv_pair[:BLOCK]).astype(dqkv_ref.dtype)
                carry[:, cols] = dq_pair
                carry[:, kcols] = dk_pair[BLOCK:]
                carry[:, vcols] = dv_pair[BLOCK:]

    cur = lambda i: jnp.minimum(i, nblk - 1)
    prev = lambda i: jnp.maximum(jnp.minimum(i, nblk - 1) - 1, 0)
    blk = lambda c, pv: pl.BlockSpec((BLOCK, D), (lambda i: (prev(i), cb + c)) if pv else (lambda i: (cur(i), cb + c)))
    row = pl.BlockSpec((BLOCK, D), lambda i: (cur(i), 0))
    return pl.pallas_call(
        body, name="at_bwd", grid=(nblk + 1,),
        in_specs=[blk(0, False), blk(1, False), blk(1, True), blk(2, False), blk(2, True),
                  pl.BlockSpec(bias.shape, lambda i: (0, 0, 0)), row, row, row, row],
        out_specs=[pl.BlockSpec((BLOCK, 3 * D), lambda i: (jnp.maximum(i - 1, 0), 0)),
                   pl.BlockSpec(bias.shape, lambda i: (0, 0, 0))],
        out_shape=[jax.ShapeDtypeStruct((T, 3 * D), BF16), jax.ShapeDtypeStruct(bias.shape, F32)],
        scratch_shapes=[pltpu.VMEM((BLOCK, 3 * D), F32)],
        compiler_params=_cparams(("arbitrary",)),
    )(qkv, qkv, qkv, qkv, qkv, bias, o, lse, do, dlse)


def _to_residue_major(a, d):
    if d == 1:
        return a
    T, C = a.shape
    return a.reshape(T // d, d, C).transpose(1, 0, 2).reshape(T, C)


def _from_residue_major(a, d):
    if d == 1:
        return a
    T, C = a.shape
    return a.reshape(d, T // d, C).transpose(1, 0, 2).reshape(T, C)


FFN_TM = 512


def _ffn_up(h, w1t, w3t, carry=None):
    T, Dm = h.shape
    Fw = w1t.shape[0]
    tm, tn = FFN_TM, _pick(Fw, 1408)
    grid = (Fw // tn, T // tm)
    nt = (((1,), (1,)), ((), ()))

    def body(*refs):
        (h_ref, w1_ref, w3_ref), (a_ref, b_ref, u_ref), _, begin, end = _carry_hooks(carry, refs, 3, 3, 2, grid)
        begin()
        hv = h_ref[...]
        a = lax.dot_general(hv, w1_ref[...], nt, preferred_element_type=F32)
        b = lax.dot_general(hv, w3_ref[...], nt, preferred_element_type=F32)
        a_ref[...] = a.astype(a_ref.dtype)
        b_ref[...] = b.astype(b_ref.dtype)
        u_ref[...] = (jax.nn.silu(a) * b).astype(u_ref.dtype)
        end()

    extra = carry if carry is not None else _NO_CARRY
    wspec = pl.BlockSpec((tn, Dm), lambda j, i: (j, 0))
    ospec = pl.BlockSpec((tm, tn), lambda j, i: (i, j))
    res = pl.pallas_call(
        body, name="ffn_up", grid=grid,
        in_specs=[pl.BlockSpec((tm, Dm), lambda j, i: (i, 0)), wspec, wspec] + extra.in_specs,
        out_specs=[ospec] * 3 + extra.out_specs,
        out_shape=[jax.ShapeDtypeStruct((T, Fw), BF16)] * 3 + extra.out_shape, scratch_shapes=extra.scratch,
        compiler_params=_cparams(("arbitrary",) * 2 if carry is not None else ("parallel", "parallel")),
    )(h, w1t, w3t, *extra.arrays)
    return res[:3], res[3:]


def _ffn_down_dx(do, w2, a, b, carry=None):
    T, Dm = do.shape
    Fw = w2.shape[0]
    tm, tn = FFN_TM, _pick(Fw, 1408)
    grid = (Fw // tn, T // tm)

    def body(*refs):
        (do_ref, w2_ref, a_ref, b_ref), (da_ref, db_ref), _, begin, end = _carry_hooks(carry, refs, 4, 2, 2, grid)
        begin()
        du = lax.dot_general(do_ref[...], w2_ref[...], (((1,), (1,)), ((), ())), preferred_element_type=F32)
        av = a_ref[...].astype(F32)
        bv = b_ref[...].astype(F32)
        sg = jax.nn.sigmoid(av)
        silu = av * sg
        da_ref[...] = (du * bv * (sg + silu * (1.0 - sg))).astype(da_ref.dtype)
        db_ref[...] = (du * silu).astype(db_ref.dtype)
        end()

    extra = carry if carry is not None else _NO_CARRY
    ospec = pl.BlockSpec((tm, tn), lambda j, i: (i, j))
    res = pl.pallas_call(
        body, name="ffn_down_dx", grid=grid,
        in_specs=[pl.BlockSpec((tm, Dm), lambda j, i: (i, 0)), pl.BlockSpec((tn, Dm), lambda j, i: (j, 0)), ospec, ospec]
        + extra.in_specs,
        out_specs=[ospec] * 2 + extra.out_specs,
        out_shape=[jax.ShapeDtypeStruct((T, Fw), BF16)] * 2 + extra.out_shape, scratch_shapes=extra.scratch,
        compiler_params=_cparams(("arbitrary",) * 2 if carry is not None else ("parallel", "parallel")),
    )(do, w2, a, b, *extra.arrays)
    return res[:2], res[2:]


def _pre_norm(x, g):
    return _rows(_f_pre, [x], [g], [(D_MODEL, BF16)], name="pre_norm")[0]


def _close_part(x, o, bias, g_post, g_pre_next, scale, name):
    extra = [] if bias is None else [bias]
    if g_pre_next is None:
        xo, = _rows(_f_post(scale, bias is not None), [x, o], [g_post] + extra, [(D_MODEL, F32)], name=name)
        return xo, None

    post = _f_post(scale, bias is not None)

    def fn(xv, ov, g, *rest):
        xo = post(xv, ov, g, *rest[:-1])[0]
        return xo, _rms(xo, rest[-1])

    return _rows(fn, [x, o], [g_post] + extra + [g_pre_next], [(D_MODEL, F32), (D_MODEL, BF16)], name=name)


def _ffn_fwd(x, h, p, carry_up=None, carry_down=None):
    (a, b, u), got_up = _ffn_up(h, p["w1"], p["w3"], carry_up)
    o = _mm(u, p["w2"], name="ffn_down", carry=carry_down)
    got_down = None
    if carry_down is not None:
        o, got_down = o
    return o, (x, h, a, b, u, o), got_up, got_down


def _ffn_bwd(saved, p, dxo, carry_a=None, carry_b=None):
    x, h, a, b, u, o = saved
    (do,), (dg_post,) = _rows_vjp(_f_post_term(0.5, False), [o], [p["g_post"]], [dxo], dtypes=[BF16], name="ffn_post_bwd")
    (da, db), got_a = _ffn_down_dx(do, p["w2"], a, b, carry_a)
    dw2 = _mm(u, do, ta=True, out_dtype=BF16, name="ffn_down_dw")
    dh = _mm((da, db), (p["w1"], p["w3"]), name="ffn_up_dx", carry=carry_b)
    got_b = None
    if carry_b is not None:
        dh, got_b = dh
    dw1 = _mm(da, h, ta=True, out_dtype=BF16, name="ffn_up_dw")
    dw3 = _mm(db, h, ta=True, out_dtype=BF16, name="ffn_up_dw")
    (dx,), (dg_pre,) = _rows_vjp(_f_pre, [x], [p["g_pre"]], [dh], dtypes=[F32], adds={0: dxo}, name="ffn_pre_bwd")
    return dx, {"w1": dw1, "w3": dw3, "w2": dw2, "g_pre": dg_pre, "g_post": dg_post}, got_a, got_b


def _expand_blocks(w, rows_first):
    w = w.reshape(S5_NB, S5_GB, S5_GROUP, S5_STATE)
    eye = jnp.eye(S5_GB, dtype=F32)
    if rows_first:
        e = w[:, :, :, None, :] * eye[None, :, None, :, None]
        return e.reshape(S5_NB, S5_GB * S5_GROUP, S5_BW)
    e = jnp.transpose(w, (0, 1, 3, 2))[:, :, :, None, :] * eye[None, :, None, :, None]
    return e.reshape(S5_NB, S5_BW, S5_GB * S5_GROUP)


def _extract_blocks(e, rows_first):
    eye = jnp.eye(S5_GB, dtype=F32)
    if rows_first:
        e = e.reshape(S5_NB, S5_GB, S5_GROUP, S5_GB, S5_STATE)
        w = jnp.sum(e * eye[None, :, None, :, None], axis=3)
    else:
        e = e.reshape(S5_NB, S5_GB, S5_STATE, S5_GB, S5_GROUP)
        w = jnp.transpose(jnp.sum(e * eye[None, :, None, :, None], axis=3), (0, 1, 3, 2))
    return w.reshape(S5_GROUPS, S5_GROUP, S5_STATE)


def _s5_prep(p):
    G, P, HG = S5_GROUPS, S5_STATE, S5_GROUP
    args = [p["a_re"].reshape(G, 1, P), p["a_im"].reshape(G, 1, P), p["log_dt"].reshape(G, 1, 1),
            jnp.transpose(p["b_re"], (0, 2, 1)), jnp.transpose(p["b_im"], (0, 2, 1))]
    abr, abi, bbr, bbi = _small(_f_s5_disc, args, [(G, 1, P)] * 2 + [(G, HG, P)] * 2, name="s5_disc")
    return args, abr.reshape(1, G * P), abi.reshape(1, G * P), bbr, bbi


def _s5_fwd(h, p):
    disc_args, abr, abi, bbr, bbi = _s5_prep(p)
    c_re, c_im = p["c_re"], p["c_im"]
    u = _mm(h, p["w_in"], name="s5_in")
    bur = _bd([u], [_expand_blocks(bbr, True)], name="s5_bu")
    bui = _bd([u], [_expand_blocks(bbi, True)], name="s5_bu")
    sr, si = _scan_fwd(bur, bui, abr, abi)
    ylin = _bd([sr, si], [_expand_blocks(c_re, False), _expand_blocks(-c_im, False)], name="s5_y")
    y, = _rows(_f_s5_gelu, [ylin, u], [p["d"]], [(D_MODEL, F32)], name="s5_gelu")
    gl = _mm(y, p["w_glu"], name="s5_glu_mm")
    z, = _rows(_f_s5_glu, [y, gl], [p["b_glu"]], [(D_MODEL, BF16)], name="s5_glu")
    m = _mm(z, p["w_out"], name="s5_out")
    return m, None, (h, disc_args, abr, abi, bbr, bbi, u, sr, si, ylin, y, gl, z)


def _s5_bwd(saved, p, dm):
    h, disc_args, abr, abi, bbr, bbi, u, sr, si, ylin, y, gl, z = saved
    c_re, c_im = p["c_re"], p["c_im"]
    dz = _mm(dm, p["w_out"], tb=True, name="s5_out_dx")
    dw_out = _mm(z, dm, ta=True, out_dtype=BF16, name="s5_out_dw")
    (dy1, dgl), (db_glu,) = _rows_vjp(_f_s5_glu, [y, gl], [p["b_glu"]], [dz], dtypes=[F32, BF16], name="s5_glu_bwd")
    dy2 = _mm(dgl, p["w_glu"], tb=True, name="s5_glu_dx")
    dw_glu = _mm(y, dgl, ta=True, out_dtype=BF16, name="s5_glu_dw")
    (dylin, du1), (dd,) = _rows_vjp(_f_s5_gelu, [ylin, u], [p["d"]], [(dy1, dy2)], dtypes=[F32, F32], name="s5_gelu_bwd")
    gr = _bd([dylin], [jnp.transpose(_expand_blocks(c_re, False), (0, 2, 1))], name="s5_y_dx")
    gi = _bd([dylin], [jnp.transpose(_expand_blocks(-c_im, False), (0, 2, 1))], name="s5_y_dx")
    dc_re = _extract_blocks(_bd_wgrad(sr, dylin, S5_BW, LANES, name="s5_y_dw"), False)
    dc_im = -_extract_blocks(_bd_wgrad(si, dylin, S5_BW, LANES, name="s5_y_dw"), False)
    lr, li, dabr, dabi = _scan_bwd(gr, gi, sr, si, abr, abi)
    du = _bd([lr, li], [jnp.transpose(_expand_blocks(bbr, True), (0, 2, 1)),
                        jnp.transpose(_expand_blocks(bbi, True), (0, 2, 1))], add=du1, out_dtype=BF16, name="s5_bu_dx")
    dbbr = _extract_blocks(_bd_wgrad(u, lr, LANES, S5_BW, name="s5_bu_dw"), True)
    dbbi = _extract_blocks(_bd_wgrad(u, li, LANES, S5_BW, name="s5_bu_dw"), True)
    G, P = S5_GROUPS, S5_STATE
    dar, dai, dldt, dbr, dbi = _small_vjp(_f_s5_disc, disc_args,
                                          [dabr.reshape(G, 1, P), dabi.reshape(G, 1, P), dbbr, dbbi], name="s5_disc_bwd")
    dh = _mm(du, p["w_in"], tb=True, name="s5_in_dx")
    dw_in = _mm(h, du, ta=True, out_dtype=BF16, name="s5_in_dw")
    grads = {"w_in": dw_in, "w_glu": dw_glu, "w_out": dw_out, "b_glu": db_glu, "d": dd,
             "a_re": dar.reshape(G, P), "a_im": dai.reshape(G, P), "log_dt": dldt.reshape(G),
             "b_re": jnp.transpose(dbr, (0, 2, 1)), "b_im": jnp.transpose(dbi, (0, 2, 1)),
             "c_re": dc_re, "c_im": dc_im}
    return dh, grads


def _cv_fwd(h, p):
    z0 = _mm(h, p["w_in"], tb=True, name="cv_in")
    zg, = _rows(_f_cv_glu, [z0], [p["b_in"]], [(D_MODEL, F32)], name="cv_glu")
    zc = _conv_fwd(zg, p["dw"], p["dw_b"])
    zl, = _rows(_f_cv_ln, [zc], [p["ln_g"], p["ln_b"]], [(D_MODEL, BF16)], name="cv_ln")
    m = _mm(zl, p["w_out"], name="cv_out")
    return m, p["b_out"], (h, z0, zg, zc, zl)


def _cv_bwd(saved, p, dm):
    h, z0, zg, zc, zl = saved
    dzl = _mm(dm, p["w_out"], tb=True, name="cv_out_dx")
    dw_out = _mm(zl, dm, ta=True, out_dtype=BF16, name="cv_out_dw")
    (dzc,), (dln_g, dln_b) = _rows_vjp(_f_cv_ln, [zc], [p["ln_g"], p["ln_b"]], [dzl], dtypes=[F32], name="cv_ln_bwd")
    dzg, ddw, ddw_b = _conv_bwd(dzc, zg, p["dw"])
    (dz0,), (db_in,) = _rows_vjp(_f_cv_glu, [z0], [p["b_in"]], [dzg], dtypes=[BF16], name="cv_glu_bwd")
    dh = _mm(dz0, p["w_in"], name="cv_in_dx")
    dw_in = _mm(dz0, h, ta=True, out_dtype=BF16, name="cv_in_dw")
    return dh, {"w_in": dw_in, "b_in": db_in, "dw": ddw, "dw_b": ddw_b, "ln_g": dln_g, "ln_b": dln_b, "w_out": dw_out}


def _gm_fwd(h, p):
    z0 = _mm(h, p["w_in"], tb=True, name="gm_in")
    u, v = _rows(_f_gm_in, [z0], [p["b_in"], p["ln_g"], p["ln_b"]], [(GM_E, F32), (GM_E, BF16)], name="gm_act")
    bs_col = p["b_s"].reshape(GM_HEADS, GM_CHUNK, 1)
    us = _gm_sg_fwd(u, v, p["w_s"], bs_col)
    m = _mm(us, p["w_out"], name="gm_out")
    return m, p["b_out"], (h, z0, u, v, us, bs_col)


def _gm_bwd(saved, p, dm):
    h, z0, u, v, us, bs_col = saved
    dus = _mm(dm, p["w_out"], tb=True, name="gm_out_dx")
    dw_out = _mm(us, dm, ta=True, out_dtype=BF16, name="gm_out_dw")
    du, dv, dw_s, db_s = _gm_sg_bwd(dus, u, v, p["w_s"], bs_col)
    (dz0,), (db_in, dln_g, dln_b) = _rows_vjp(_f_gm_in, [z0], [p["b_in"], p["ln_g"], p["ln_b"]], [du, dv],
                                              dtypes=[BF16], name="gm_act_bwd")
    dh = _mm(dz0, p["w_in"], name="gm_in_dx")
    dw_in = _mm(dz0, h, ta=True, out_dtype=BF16, name="gm_in_dw")
    return dh, {"w_in": dw_in, "b_in": db_in, "ln_g": dln_g, "ln_b": dln_b, "w_s": dw_s, "b_s": db_s[:, :, 0],
                "w_out": dw_out}


def _at_fwd_mixer(h, p):
    T = h.shape[0]
    D = D_MODEL
    qkv = _mm(h, p["w_qkv"], tb=True, out_dtype=BF16, name="at_qkv")
    res, outs, lses, biases = [], [], [], []
    for g, (window, d) in enumerate(PATTERNS):
        assert window // d == BLOCK and T % (BLOCK * d) == 0
        bias = _at_bias(p["rel_bias"], g, d)
        if d == 1:
            r, cb = qkv, 3 * g
        else:
            r, cb = _to_residue_major(qkv[:, g * 3 * D:(g + 1) * 3 * D], d), 0
        o, lse = _at_fwd(r, bias, T // d // BLOCK, cb)
        res.append((r, cb, o, lse))
        biases.append(bias)
        outs.append(_from_residue_major(o, d))
        lses.append(_from_residue_major(lse, d))
    oc, = _rows(_f_at_combine, outs + lses, [], [(D, BF16)], name="at_combine")
    m = _mm(oc, p["w_out"], name="at_out")
    return m, None, (h, res, biases, outs, lses, oc)


def _at_bwd_mixer(saved, p, dm):
    h, res, biases, outs, lses, oc = saved
    T = h.shape[0]
    doc = _mm(dm, p["w_out"], tb=True, name="at_out_dx")
    dw_out = _mm(oc, dm, ta=True, out_dtype=BF16, name="at_out_dw")
    dol, _ = _rows_vjp(_f_at_combine, outs + lses, [], [doc], dtypes=[F32] * 6, name="at_combine_bwd")
    dqkv, dtab = [], []
    for g, (window, d) in enumerate(PATTERNS):
        r, cb, o_res, lse_res = res[g]
        dq, dbias = _at_bwd(r, biases[g], o_res, lse_res, _to_residue_major(dol[g], d),
                            _to_residue_major(dol[3 + g], d), T // d // BLOCK, cb)
        dqkv.append(_from_residue_major(dq, d))
        dtab.append(_at_bias_bwd(dbias, d))
    w_rows = [p["w_qkv"][g * 3 * D_MODEL:(g + 1) * 3 * D_MODEL] for g in range(len(PATTERNS))]
    dh = _mm(tuple(dqkv), tuple(w_rows), name="at_qkv_dx")
    dw_qkv = jnp.concatenate([_mm(dq, h, ta=True, out_dtype=BF16, name="at_qkv_dw") for dq in dqkv], axis=0)
    return dh, {"w_qkv": dw_qkv, "w_out": dw_out, "rel_bias": jnp.concatenate(dtab, axis=1)}


_MIXERS = ((_s5_fwd, _s5_bwd), (_cv_fwd, _cv_bwd), (_gm_fwd, _gm_bwd), (_at_fwd_mixer, _at_bwd_mixer))


def _mixer_fwd(x, h, p, kind):
    m, bias, saved = _MIXERS[kind][0](h, p)
    return m, bias, (x, m, bias, saved)


def _mixer_bwd(saved_all, p, kind, dxo):
    x, m, bias, saved = saved_all
    extra = [] if bias is None else [bias]
    (dm,), dpars = _rows_vjp(_f_post_term(1.0, bias is not None), [m], [p["g_post"]] + extra, [dxo], dtypes=[BF16],
                             name="mix_post_bwd")
    dh, grads = _MIXERS[kind][1](saved, p, dm)
    (dx,), (dg_pre,) = _rows_vjp(_f_pre, [x], [p["g_pre"]], [dh], dtypes=[F32], adds={0: dxo}, name="mix_pre_bwd")
    grads["g_pre"] = dg_pre
    grads["g_post"] = dpars[0]
    if bias is not None:
        grads["b_out"] = dpars[1]
    return dx, grads


class _Carry:
    def __init__(self, arrays, kinds):
        self.arrays, self.kinds, self.n = list(arrays), list(kinds), len(arrays)
        hbm = pl.BlockSpec(memory_space=pl.ANY)
        self.in_specs = [hbm] * self.n
        self.out_specs = [hbm] * self.n
        self.out_shape = [jax.ShapeDtypeStruct((N_DEV,) + (a.shape[1:] if k == "a2a" else a.shape), a.dtype)
                          for a, k in zip(arrays, kinds)]
        self.scratch = [pltpu.SemaphoreType.DMA((self.n * (N_DEV - 1),)), pltpu.SemaphoreType.DMA((self.n * (N_DEV - 1),)),
                        pltpu.SemaphoreType.DMA((self.n,))]

    def _copies(self, ins, outs, sems, arrivals):
        send_sems, recv_sems, local_sems = sems
        x, y, c = lax.axis_index("x"), lax.axis_index("y"), lax.axis_index("c")
        me = 4 * x + 2 * y + c
        local, remote = [], []
        for a in range(self.n):
            a2a = self.kinds[a] == "a2a"
            if not arrivals:
                local.append(pltpu.make_async_copy(ins[a].at[me] if a2a else ins[a], outs[a].at[me], local_sems.at[a]))
            for k in range(1, N_DEV):
                px = 1 - x if k & 4 else x
                py = 1 - y if k & 2 else y
                pc = 1 - c if k & 1 else c
                peer = 4 * px + 2 * py + pc
                idx = a * (N_DEV - 1) + k - 1
                remote.append(pltpu.make_async_remote_copy(
                    src_ref=ins[a].at[peer] if a2a else ins[a], dst_ref=outs[a].at[peer if arrivals else me],
                    send_sem=send_sems.at[idx], recv_sem=recv_sems.at[idx], device_id=(px, py, pc),
                    device_id_type=pl.DeviceIdType.MESH))
        return local, remote

    def start(self, ins, outs, sems):
        local, sends = self._copies(ins, outs, sems, False)
        for cp in local + sends:
            cp.start()

    def wait(self, ins, outs, sems):
        local, sends = self._copies(ins, outs, sems, False)
        _, recvs = self._copies(ins, outs, sems, True)
        for cp in sends:
            cp.wait_send()
        for cp in recvs:
            cp.wait_recv()
        for cp in local:
            cp.wait()


class _NoCarry:
    n = 0
    arrays = in_specs = out_specs = out_shape = scratch = []


_NO_CARRY = _NoCarry()


def _carry_hooks(carry, refs, n_in, n_out, grid_rank, grid):
    nc = carry.n if carry is not None else 0
    ins, cin = refs[:n_in], refs[n_in:n_in + nc]
    outs, cout = refs[n_in + nc:n_in + nc + n_out], refs[n_in + nc + n_out:n_in + 2 * nc + n_out]
    rest = refs[n_in + 2 * nc + n_out:]
    scratch, sems = (rest[:len(rest) - 3], rest[len(rest) - 3:]) if nc else (rest, ())

    def at(step_of):
        cond = None
        for ax in range(grid_rank):
            c = pl.program_id(ax) == step_of(ax)
            cond = c if cond is None else cond & c
        return cond

    def begin():
        if nc:
            @pl.when(at(lambda ax: 0))
            def _():
                carry.start(cin, cout, sems)

    def end():
        if nc:
            @pl.when(at(lambda ax: grid[ax] - 1))
            def _():
                carry.wait(cin, cout, sems)

    return ins, outs, scratch, begin, end


def _exchange(arrays, kinds, *, name):
    carry = _Carry(arrays, kinds)

    def body(*refs):
        n = carry.n
        carry.start(refs[:n], refs[n:2 * n], refs[2 * n:])
        carry.wait(refs[:n], refs[n:2 * n], refs[2 * n:])

    return pl.pallas_call(body, name=name, in_specs=carry.in_specs, out_specs=carry.out_specs,
                          out_shape=carry.out_shape, scratch_shapes=carry.scratch)(*arrays)


def _adam(recv, w, m, v, *, name):
    R, C = w.shape
    tr = _pick_rows(R, 128)
    c1 = 1.0 - ADAM_B1 ** ADAM_STEP
    c2 = 1.0 - ADAM_B2 ** ADAM_STEP

    def body(r_ref, w_ref, m_ref, v_ref, g_ref, d_ref, nm_ref, nv_ref):
        g = r_ref[0].astype(F32)
        for q in range(1, N_DEV):
            g = g + r_ref[q].astype(F32)
        mm = ADAM_B1 * m_ref[...] + (1.0 - ADAM_B1) * g
        vv = ADAM_B2 * v_ref[...] + (1.0 - ADAM_B2) * jnp.square(g)
        m_hat = mm / c1
        v_hat = vv / c2
        g_ref[...] = g
        d_ref[...] = -ADAM_LR * (m_hat / (jnp.sqrt(v_hat) + ADAM_EPS) + ADAM_WD * w_ref[...])
        nm_ref[...] = mm
        nv_ref[...] = vv

    blk = pl.BlockSpec((tr, C), lambda i: (i, 0))
    return pl.pallas_call(
        body, name=name, grid=(R // tr,),
        in_specs=[pl.BlockSpec((N_DEV, tr, C), lambda i: (0, i, 0)), blk, blk, blk], out_specs=[blk] * 4,
        out_shape=[jax.ShapeDtypeStruct((R, C), F32)] * 4,
        compiler_params=_cparams(("parallel",)),
    )(recv, w, m, v)


PACK_COLS = 1024


def _padded(n):
    return -(-n // PACK_ALIGN) * PACK_ALIGN


def _pack_flat(pieces):
    flat = jnp.concatenate([p.reshape(-1) for p in pieces])
    n = flat.shape[0]
    return jnp.pad(flat, (0, _padded(n) - n)).reshape(-1, PACK_COLS)


def _shard_shape(shape, axis):
    s = list(shape)
    assert s[axis] % N_DEV == 0
    s[axis] //= N_DEV
    return tuple(s)


def _split_full(full, axis):
    s = full.shape
    r = full.reshape(s[:axis] + (N_DEV, s[axis] // N_DEV) + s[axis + 1:])
    return jnp.moveaxis(r, axis, 0)


def _merge_full(parts, axis):
    r = jnp.moveaxis(parts, 0, axis)
    s = r.shape
    return r.reshape(s[:axis] + (s[axis] * s[axis + 1],) + s[axis + 2:])


def _pack_full(entries, grads):
    flat = jnp.concatenate([_split_full(grads[k].reshape(shape), axis).reshape(N_DEV, -1)
                            for k, shape, axis in entries], axis=1)
    n = flat.shape[1]
    return jnp.pad(flat, ((0, 0), (0, _padded(n) - n))).reshape(N_DEV, -1, PACK_COLS)


def _unpack_gathered(entries, buf):
    flat = buf.reshape(N_DEV, -1)
    out, pos = {}, 0
    for k, shape, axis in entries:
        ss = _shard_shape(shape, axis)
        n = int(np.prod(ss))
        out[k] = _merge_full(flat[:, pos:pos + n].reshape((N_DEV,) + ss), axis)
        pos += n
    return out


def _unpack_shard(entries, buf):
    flat = buf.reshape(-1)
    out, pos = {}, 0
    for k, shape, axis in entries:
        ss = _shard_shape(shape, axis)
        n = int(np.prod(ss))
        out[k] = flat[pos:pos + n].reshape(ss)
        pos += n
    return out


def _unpack_flat(entries, buf):
    flat = buf.reshape(-1)
    out, pos = {}, 0
    for k, shape in entries:
        n = int(np.prod(shape))
        out[k] = flat[pos:pos + n].reshape(shape)
        pos += n
    return out


D, FF = D_MODEL, D_FF
_FFN_MATS = (("w1", (D, FF), 1), ("w3", (D, FF), 1), ("w2", (FF, D), 0))
_NORM_VECS = (("g_pre", (D,), 0), ("g_post", (D,), 0))
_MIX_MATS = (
    (("w_in", (D, D), 0), ("w_glu", (D, D), 0), ("w_out", (D, D), 0)),
    (("w_in", (D, 2 * D), 1), ("w_out", (D, D), 0)),
    (("w_in", (D, 2 * GM_E), 1), ("w_out", (GM_E, D), 0)),
    (("w_qkv", (D, 9 * D), 1), ("w_out", (D, D), 0)),
)
_MIX_VECS = (
    (),
    (("b_in", (2 * D,), 0), ("dw", (CONV_W, D), 1), ("dw_b", (D,), 0), ("ln_g", (D,), 0), ("ln_b", (D,), 0),
     ("b_out", (D,), 0)),
    (("b_in", (2 * GM_E,), 0), ("ln_g", (GM_E,), 0), ("ln_b", (GM_E,), 0), ("b_out", (D,), 0)),
    (),
)
_REPLICATED = (
    ("rel_bias", 3, "rel_bias", (NUM_BUCKETS, 3 * AT_HEADS)),
    ("s5_a_re", 0, "a_re", (S5_GROUPS, S5_STATE)), ("s5_a_im", 0, "a_im", (S5_GROUPS, S5_STATE)),
    ("s5_log_dt", 0, "log_dt", (S5_GROUPS,)),
    ("s5_b_re", 0, "b_re", (S5_GROUPS, S5_STATE, S5_GROUP)), ("s5_b_im", 0, "b_im", (S5_GROUPS, S5_STATE, S5_GROUP)),
    ("s5_c_re", 0, "c_re", (S5_GROUPS, S5_GROUP, S5_STATE)), ("s5_c_im", 0, "c_im", (S5_GROUPS, S5_GROUP, S5_STATE)),
    ("s5_d", 0, "d", (D,)), ("s5_b_glu", 0, "b_glu", (D,)),
    ("gm_w_s", 2, "w_s", (GM_HEADS, GM_CHUNK, GM_CHUNK)), ("gm_b_s", 2, "b_s", (GM_HEADS, GM_CHUNK)),
)
_MIX_PREFIX = ("s5_", "cv_", "gm_", "at_")
_TWIN_WEIGHTS = ('norm_pre', 'norm_post', 'ffn_w1', 'ffn_w3', 'ffn_w2', 'rel_bias', 's5_w_in', 's5_a_re', 's5_a_im',
                 's5_log_dt', 's5_b_re', 's5_b_im', 's5_c_re', 's5_c_im', 's5_d', 's5_w_glu', 's5_b_glu', 's5_w_out',
                 'cv_w_in', 'cv_b_in', 'cv_dw', 'cv_dw_b', 'cv_ln_g', 'cv_ln_b', 'cv_w_out', 'cv_b_out', 'gm_w_in',
                 'gm_b_in', 'gm_ln_g', 'gm_ln_b', 'gm_w_s', 'gm_b_s', 'gm_w_out', 'gm_b_out', 'at_w_qkv', 'at_w_out')


def _part_entries(part):
    if part[0] == "ffn":
        return _FFN_MATS, _NORM_VECS
    kind = part[1] % 4
    return _MIX_MATS[kind], _NORM_VECS + _MIX_VECS[kind]


def _part_shards(part, get):
    if part[0] == "ffn":
        _, i, j = part
        n = 0 if j == 0 else 2
        return {"w1": get("ffn_w1")[i, j], "w3": get("ffn_w3")[i, j], "w2": get("ffn_w2")[i, j],
                "g_pre": get("norm_pre")[i, n], "g_post": get("norm_post")[i, n]}
    _, i = part
    kind, j = i % 4, i // 4
    out = {"g_pre": get("norm_pre")[i, 1], "g_post": get("norm_post")[i, 1]}
    for k, _, _ in _MIX_MATS[kind] + _MIX_VECS[kind]:
        out[k] = get(_MIX_PREFIX[kind] + k)[j]
    return out


def _parts():
    parts = []
    for i in range(DEPTH):
        parts += [("ffn", i, 0), ("mix", i), ("ffn", i, 1)]
    return parts


def _as_par(v):
    return v.reshape(1, -1)


def _prepare_part(part, full, rep):
    if part[0] == "ffn":
        return {"w1": full["w1"], "w3": full["w3"], "w2": full["w2"],
                "g_pre": _as_par(full["g_pre"]), "g_post": _as_par(full["g_post"])}
    kind = part[1] % 4
    p = {"g_pre": _as_par(full["g_pre"]), "g_post": _as_par(full["g_post"])}
    for k, _, _ in _MIX_MATS[kind]:
        p[k] = full[k]
    for k, _, _ in _MIX_VECS[kind]:
        p[k] = _as_par(full[k]) if k != "dw" else jnp.pad(full[k], ((0, CONV_HALO - CONV_W), (0, 0)))
    if kind == 0:
        for k in ("a_re", "a_im", "log_dt", "b_re", "b_im"):
            p[k] = rep[k]
        p["c_re"], p["c_im"] = rep["c_re"], rep["c_im"]
        p["d"], p["b_glu"] = _as_par(rep["d"]), _as_par(rep["b_glu"])
    elif kind == 2:
        p["w_s"], p["b_s"] = rep["w_s"], rep["b_s"]
    elif kind == 3:
        p["rel_bias"] = rep["rel_bias"]
    return p


def _finish_grads(part, grads):
    out = dict(grads)
    for k in ("g_pre", "g_post", "b_in", "dw_b", "ln_g", "ln_b", "b_out", "d", "b_glu"):
        if k in out:
            out[k] = out[k].reshape(-1)
    if "dw" in out:
        out["dw"] = out["dw"][:CONV_W]
    return out


def _step(x, tgt, inputs, moments_m, moments_v):
    parts = _parts()
    rep = {}
    for name, kind, key, shape in _REPLICATED:
        rep[key] = inputs[name][0] if name != "rel_bias" else inputs[name]

    def stored(part, get):
        mats, vecs = _part_entries(part)
        sh = _part_shards(part, get)
        return [sh[k].T if axis == 1 else sh[k] for k, _, axis in mats], _pack_flat([sh[k] for k, _, _ in vecs])

    stored_w = [stored(part, lambda n: inputs[n]) for part in parts]

    def gather_of(idx):
        wmats, wv = stored_w[idx]
        return _Carry([w.astype(BF16) for w in wmats] + [wv], ["bcast"] * (len(wmats) + 1))

    def gathered(idx, bufs):
        mats, vecs = _part_entries(parts[idx])
        full = {k: b.reshape(-1, b.shape[-1]) for (k, _, _), b in zip(mats, bufs)}
        full.update(_unpack_gathered(vecs, bufs[-1]))
        return _prepare_part(parts[idx], full, rep)

    params = [None] * len(parts)
    first = gather_of(0)
    params[0] = gathered(0, _exchange(first.arrays, first.kinds, name="gather_first"))
    saved = []
    xs = x
    h = _pre_norm(xs, params[0]["g_pre"])
    for idx, part in enumerate(parts):
        if part[0] == "ffn":
            ahead = [i for i in (idx + 1, idx + 2) if i < len(parts) and params[i] is None]
            if part[2] == 1:
                ahead = ahead[:1]
            ahead = ahead[::-1]
            c_up = gather_of(ahead[0]) if ahead else None
            c_down = gather_of(ahead[1]) if len(ahead) > 1 else None
            o, s, got_up, got_down = _ffn_fwd(xs, h, params[idx], c_up, c_down)
            bias, scale = None, 0.5
            if c_up is not None:
                params[ahead[0]] = gathered(ahead[0], got_up)
            if c_down is not None:
                params[ahead[1]] = gathered(ahead[1], got_down)
        else:
            o, bias, s = _mixer_fwd(xs, h, params[idx], part[1] % 4)
            scale = 1.0
        saved.append(s)
        g_next = params[idx + 1]["g_pre"] if idx + 1 < len(parts) else None
        xs, h = _close_part(xs, o, bias, params[idx]["g_post"], g_next, scale, part[0] + "_post")
    dh, loss_vec = _loss_call(xs, tgt)
    loss_local = loss_vec[0, 0]

    results = {}
    rep_grads = {}

    def scatter_of(idx, grads):
        mats, vecs = _part_entries(parts[idx])
        gm = [grads[k].reshape(N_DEV, -1, grads[k].shape[-1]) for k, _, _ in mats]
        return _Carry(gm + [_pack_full(vecs, grads)], ["a2a"] * (len(gm) + 1))

    def update(idx, bufs):
        part = parts[idx]
        mats, vecs = _part_entries(part)
        wmats, wv = stored_w[idx]
        mmats, mv = stored(part, lambda n: moments_m[n])
        vmats, vv = stored(part, lambda n: moments_v[n])
        res = [dict() for _ in range(4)]
        for (k, _, axis), buf, w_, m_, v_ in zip(mats, bufs, wmats, mmats, vmats):
            for r, o in zip(res, _adam(buf, w_, m_, v_, name="adam_mat")):
                r[k] = o.T if axis == 1 else o
        for r, o in zip(res, _adam(bufs[-1], wv, mv, vv, name="adam_vecs")):
            r.update(_unpack_shard(vecs, o))
        results[part] = res

    rep_entries = [(name, shape) for name, _, _, shape in _REPLICATED]
    rg = None
    pending = []
    for idx in range(len(parts) - 1, -1, -1):
        part, p = parts[idx], params[idx]
        if part[0] == "ffn":
            riders = pending[:2]
            pending = pending[2:]
            c_a = riders[0][1] if riders else None
            c_b = riders[1][1] if len(riders) > 1 else None
            if idx == 0 and c_a is not None:
                c_a = _Carry(c_a.arrays + [_pack_flat([rep_grads[name] for name, _ in rep_entries])], c_a.kinds + ["bcast"])
            dh, grads, got_a, got_b = _ffn_bwd(saved[idx], p, dh, c_a, c_b)
            if idx == 0 and c_a is not None:
                rg, got_a = got_a[-1], got_a[:-1]
            for (ridx, _), got in zip(riders, (got_a, got_b)):
                update(ridx, got)
        else:
            dh, grads = _mixer_bwd(saved[idx], p, part[1] % 4, dh)
        grads = _finish_grads(part, grads)
        for name, kind, key, shape in _REPLICATED:
            if part[0] == "mix" and kind == part[1] % 4:
                rep_grads[name] = grads[key]
        pending.append((idx, scatter_of(idx, grads)))
    for ridx, c in pending:
        update(ridx, _exchange(c.arrays, c.kinds, name="scatter_last"))

    get_rep = lambda d: _pack_flat([(d[name][0] if name != "rel_bias" else d[name]) for name, _ in rep_entries])
    assert rg is not None
    orep = _adam(rg, get_rep(inputs), get_rep(moments_m), get_rep(moments_v), name="adam_rep")
    rep_out = [_unpack_flat(rep_entries, o) for o in orep]
    return loss_local, dh, results, rep_out


def _assemble(name, results, rep_out, which):
    for rname, _, _, _ in _REPLICATED:
        if rname == name:
            a = rep_out[which][name]
            return a if name == "rel_bias" else a[None]
    if name in ("norm_pre", "norm_post"):
        key = "g_pre" if name == "norm_pre" else "g_post"
        rows = []
        for i in range(DEPTH):
            rows.append(jnp.stack([results[("ffn", i, 0)][which][key], results[("mix", i)][which][key],
                                   results[("ffn", i, 1)][which][key]]))
        return jnp.stack(rows)
    if name.startswith("ffn_"):
        key = name[4:]
        return jnp.stack([jnp.stack([results[("ffn", i, j)][which][key] for j in range(2)]) for i in range(DEPTH)])
    kind = _MIX_PREFIX.index(name[:3])
    layers = [i for i in range(DEPTH) if i % 4 == kind]
    return jnp.stack([results[("mix", i)][which][name[3:]] for i in layers])


def kernel(x, norm_pre, norm_post, ffn_w1, ffn_w3, ffn_w2, rel_bias, s5_w_in, s5_a_re, s5_a_im, s5_log_dt, s5_b_re, s5_b_im, s5_c_re, s5_c_im, s5_d, s5_w_glu, s5_b_glu, s5_w_out, cv_w_in, cv_b_in, cv_dw, cv_dw_b, cv_ln_g, cv_ln_b, cv_w_out, cv_b_out, gm_w_in, gm_b_in, gm_ln_g, gm_ln_b, gm_w_s, gm_b_s, gm_w_out, gm_b_out, at_w_qkv, at_w_out, loss_target, m_norm_pre, m_norm_post, m_ffn_w1, m_ffn_w3, m_ffn_w2, m_rel_bias, m_s5_w_in, m_s5_a_re, m_s5_a_im, m_s5_log_dt, m_s5_b_re, m_s5_b_im, m_s5_c_re, m_s5_c_im, m_s5_d, m_s5_w_glu, m_s5_b_glu, m_s5_w_out, m_cv_w_in, m_cv_b_in, m_cv_dw, m_cv_dw_b, m_cv_ln_g, m_cv_ln_b, m_cv_w_out, m_cv_b_out, m_gm_w_in, m_gm_b_in, m_gm_ln_g, m_gm_ln_b, m_gm_w_s, m_gm_b_s, m_gm_w_out, m_gm_b_out, m_at_w_qkv, m_at_w_out, v_norm_pre, v_norm_post, v_ffn_w1, v_ffn_w3, v_ffn_w2, v_rel_bias, v_s5_w_in, v_s5_a_re, v_s5_a_im, v_s5_log_dt, v_s5_b_re, v_s5_b_im, v_s5_c_re, v_s5_c_im, v_s5_d, v_s5_w_glu, v_s5_b_glu, v_s5_w_out, v_cv_w_in, v_cv_b_in, v_cv_dw, v_cv_dw_b, v_cv_ln_g, v_cv_ln_b, v_cv_w_out, v_cv_b_out, v_gm_w_in, v_gm_b_in, v_gm_ln_g, v_gm_ln_b, v_gm_w_s, v_gm_b_s, v_gm_w_out, v_gm_b_out, v_at_w_qkv, v_at_w_out):
    args = locals()
    inputs = {n: args[n] for n in _TWIN_WEIGHTS}
    moments_m = {n: args["m_" + n] for n in _TWIN_WEIGHTS}
    moments_v = {n: args["v_" + n] for n in _TWIN_WEIGHTS}
    loss_local, dx, results, rep_out = _step(x[0], loss_target[0], inputs, moments_m, moments_v)
    loss = lax.psum(loss_local, AXES)
    out = [loss, dx[None]]
    for which in range(4):
        out += [_assemble(n, results, rep_out, which) for n in _TWIN_WEIGHTS]
    return tuple(out)
```

```python
import functools
import math

import numpy as np

import jax
import jax.numpy as jnp
from jax import lax
from jax.experimental import pallas as pl
from jax.experimental.pallas import tpu as pltpu

F32 = jnp.float32
BF16 = jnp.bfloat16

D_MODEL = 1024
DEPTH = 4
D_FF = 2816
EPS = 1e-6
S5_GROUP = 16
S5_STATE = 64
CONV_W = 31
GM_CHUNK = 128
GM_HEADS = 8
HEAD_DIM = 64
PATTERNS = ((128, 1), (512, 4), (2048, 16))
BLOCK = 128
NUM_BUCKETS = 32
MAX_DISTANCE = 2048
ADAM_LR = 0.001
ADAM_B1 = 0.9
ADAM_B2 = 0.999
ADAM_EPS = 1e-08
ADAM_WD = 0.01
ADAM_STEP = 10

N_DEV = 8
AXES = ("x", "y", "c")
LANES = 128
GM_E = 2 * D_MODEL
S5_GROUPS = D_MODEL // S5_GROUP
S5_GB = LANES // S5_GROUP
S5_NB = D_MODEL // LANES
S5_BW = S5_GB * S5_STATE
S5_NS = S5_GROUPS * S5_STATE
AT_HEADS = D_MODEL // HEAD_DIM
VMEM_LIMIT = 56 * 1024 * 1024
PACK_ALIGN = 16 * 1024


def _cparams(sem):
    return pltpu.CompilerParams(dimension_semantics=sem, vmem_limit_bytes=VMEM_LIMIT)


def _pick(n, cap):
    if n <= cap:
        return n
    best = None
    for t in range(LANES, cap + 1, LANES):
        if n % t == 0:
            best = t
    assert best is not None, (n, cap)
    return best


def _pick_rows(n, cap):
    best = None
    for t in range(16, min(n, cap) + 1, 16):
        if n % t == 0:
            best = t
    assert best is not None, (n, cap)
    return best


MM_VMEM_BUDGET = 40 * 1024 * 1024


def _mm(a, b, *, ta=False, tb=False, out_dtype=F32, name, carry=None):
    a_list = list(a) if isinstance(a, (tuple, list)) else [a]
    b_list = list(b) if isinstance(b, (tuple, list)) else [b]
    n_op = len(a_list)
    assert n_op == len(b_list)
    K, M = a_list[0].shape if ta else a_list[0].shape[::-1]
    N, K2 = b_list[0].shape if tb else b_list[0].shape[::-1]
    assert K == K2, (a_list[0].shape, b_list[0].shape, ta, tb)
    a_bytes = sum(x.dtype.itemsize for x in a_list)
    b_bytes = sum(x.dtype.itemsize for x in b_list)
    o_bytes = jnp.dtype(out_dtype).itemsize

    def vmem(tm, tn, tk, nk):
        acc = tm * tn * 4 if (nk > 1 and out_dtype != F32) else 0
        return 2 * (tm * tk * a_bytes + tk * tn * b_bytes + tm * tn * o_bytes) + acc

    if ta:
        tm, tn = _pick(M, 1408), _pick(N, 1408)
        tk = next(t for t in (2048, 1024, 512, 256) if K % t == 0 and vmem(tm, tn, t, 2) <= MM_VMEM_BUDGET)
    else:
        tm, tn = _pick(M, 512), _pick(N, 1408)
        tk = next(t for t in (K, _pick(K, 4608), _pick(K, 2816), _pick(K, 1024))
                  if vmem(tm, tn, t, K // t) <= MM_VMEM_BUDGET)
    nk = K // tk
    a_spec = pl.BlockSpec((tk, tm), lambda j, i, k: (k, i)) if ta else pl.BlockSpec((tm, tk), lambda j, i, k: (i, k))
    b_spec = pl.BlockSpec((tn, tk), lambda j, i, k: (j, k)) if tb else pl.BlockSpec((tk, tn), lambda j, i, k: (k, j))
    dims = (((0 if ta else 1,), (1 if tb else 0,)), ((), ()))
    use_scratch = nk > 1 and out_dtype != F32
    grid = (N // tn, M // tm, nk)

    def body(*refs):
        ins, (o_ref,), scratch, begin, end = _carry_hooks(carry, refs, 2 * n_op, 1, 3, grid)
        begin()
        p = None
        for a_ref, b_ref in zip(ins[:n_op], ins[n_op:]):
            d = lax.dot_general(a_ref[...].astype(BF16), b_ref[...].astype(BF16), dims, preferred_element_type=F32)
            p = d if p is None else p + d
        if nk == 1:
            o_ref[...] = p.astype(o_ref.dtype)
        else:
            acc = scratch[0] if use_scratch else o_ref
            k = pl.program_id(2)

            @pl.when(k == 0)
            def _():
                acc[...] = p

            @pl.when(k > 0)
            def _():
                acc[...] += p

            if use_scratch:
                @pl.when(k == nk - 1)
                def _():
                    o_ref[...] = acc[...].astype(o_ref.dtype)
        end()

    extra = carry if carry is not None else _NO_CARRY
    res = pl.pallas_call(
        body, name=name, grid=grid, in_specs=[a_spec] * n_op + [b_spec] * n_op + extra.in_specs,
        out_specs=[pl.BlockSpec((tm, tn), lambda j, i, k: (i, j))] + extra.out_specs,
        out_shape=[jax.ShapeDtypeStruct((M, N), out_dtype)] + extra.out_shape,
        scratch_shapes=([pltpu.VMEM((tm, tn), F32)] if use_scratch else []) + extra.scratch,
        compiler_params=_cparams(("arbitrary",) * 3 if carry is not None else ("parallel", "parallel", "arbitrary")),
    )(*a_list, *b_list, *extra.arrays)
    return res[0] if carry is None else (res[0], res[1:])


ROW_TILE_BYTES = 8 * 1024 * 1024


def _row_tile(arrays):
    row_bytes = sum(w * jnp.dtype(dt).itemsize for w, dt in arrays)
    for tile in (256, 128, 64, 32):
        if tile * row_bytes <= ROW_TILE_BYTES:
            return tile
    return 16


def _rows(fn, rows, pars, outs, *, name):
    T = rows[0].shape[0]
    tile = _row_tile([(r.shape[1], r.dtype) for r in rows] + list(outs))
    nr, npar = len(rows), len(pars)

    def body(*refs):
        r = [refs[i][...] for i in range(nr)]
        p = [refs[nr + i][...] for i in range(npar)]
        res = fn(*r, *p)
        for o_ref, o in zip(refs[nr + npar:], res):
            o_ref[...] = o.astype(o_ref.dtype)

    in_specs = [pl.BlockSpec((tile, r.shape[1]), lambda i: (i, 0)) for r in rows]
    in_specs += [pl.BlockSpec(p.shape, lambda i, nd=p.ndim: (0,) * nd) for p in pars]
    return pl.pallas_call(
        body, name=name, grid=(T // tile,), in_specs=in_specs,
        out_specs=[pl.BlockSpec((tile, w), lambda i: (i, 0)) for w, _ in outs],
        out_shape=[jax.ShapeDtypeStruct((T, w), dt) for w, dt in outs],
        compiler_params=_cparams(("parallel",)),
    )(*rows, *pars)


def _rows_vjp(fn, rows, pars, cts, *, dtypes, adds=None, name):
    adds = adds or {}
    cts = [c if isinstance(c, (tuple, list)) else (c,) for c in cts]
    flat_cts = [a for c in cts for a in c]
    add_keys = sorted(adds)
    add_arrs = [adds[k] for k in add_keys]
    want = [i for i, d in enumerate(dtypes) if d is not None]
    T = rows[0].shape[0]
    tile = _row_tile([(a.shape[1], a.dtype) for a in list(rows) + flat_cts + add_arrs]
                     + [(rows[i].shape[1], dtypes[i]) for i in want])
    nr, npar, nc, na = len(rows), len(pars), len(flat_cts), len(add_arrs)

    def body(*refs):
        r = [refs[i][...].astype(F32) for i in range(nr)]
        p = [refs[nr + i][...] for i in range(npar)]
        cvals = [refs[nr + npar + i][...].astype(F32) for i in range(nc)]
        avals = [refs[nr + npar + nc + i][...].astype(F32) for i in range(na)]
        outs = refs[nr + npar + nc + na:]
        ct, pos = [], 0
        for c in cts:
            s = cvals[pos]
            for extra in cvals[pos + 1:pos + len(c)]:
                s = s + extra
            pos += len(c)
            ct.append(s)
        _, vjp = jax.vjp(lambda *a: tuple(fn(*a)), *r, *p)
        g = vjp(tuple(ct))
        for o_ref, i in zip(outs[:len(want)], want):
            gi = g[i]
            if i in adds:
                gi = gi + avals[add_keys.index(i)]
            o_ref[...] = gi.astype(o_ref.dtype)
        first = pl.program_id(0) == 0
        for o_ref, gp in zip(outs[len(want):], g[nr:]):
            @pl.when(first)
            def _(o_ref=o_ref, gp=gp):
                o_ref[...] = gp

            @pl.when(jnp.logical_not(first))
            def _(o_ref=o_ref, gp=gp):
                o_ref[...] += gp

    row_spec = lambda a: pl.BlockSpec((tile, a.shape[1]), lambda i: (i, 0))
    par_spec = lambda a: pl.BlockSpec(a.shape, lambda i, nd=a.ndim: (0,) * nd)
    res = pl.pallas_call(
        body, name=name, grid=(T // tile,),
        in_specs=[row_spec(a) for a in rows] + [par_spec(a) for a in pars] + [row_spec(a) for a in flat_cts + add_arrs],
        out_specs=[row_spec(rows[i]) for i in want] + [par_spec(a) for a in pars],
        out_shape=[jax.ShapeDtypeStruct(rows[i].shape, dtypes[i]) for i in want]
        + [jax.ShapeDtypeStruct(a.shape, F32) for a in pars],
        compiler_params=_cparams(("arbitrary",)),
    )(*rows, *pars, *flat_cts, *add_arrs)
    return res[:len(want)], res[len(want):]


def _small(fn, args, outs, *, name):
    n = len(args)

    def body(*refs):
        res = fn(*[r[...] for r in refs[:n]])
        for o_ref, o in zip(refs[n:], res):
            o_ref[...] = o

    return pl.pallas_call(body, name=name, out_shape=[jax.ShapeDtypeStruct(s, F32) for s in outs],
                          compiler_params=pltpu.CompilerParams(vmem_limit_bytes=VMEM_LIMIT))(*args)


def _small_vjp(fn, args, cts, *, name):
    n, nc = len(args), len(cts)

    def body(*refs):
        _, vjp = jax.vjp(lambda *a: tuple(fn(*a)), *[r[...] for r in refs[:n]])
        g = vjp(tuple(r[...] for r in refs[n:n + nc]))
        for o_ref, gi in zip(refs[n + nc:], g):
            o_ref[...] = gi

    return pl.pallas_call(body, name=name, out_shape=[jax.ShapeDtypeStruct(a.shape, F32) for a in args],
                          compiler_params=pltpu.CompilerParams(vmem_limit_bytes=VMEM_LIMIT))(*args, *cts)


def _rms(x, g):
    return x * lax.rsqrt(jnp.mean(x * x, axis=-1, keepdims=True) + EPS) * g


def _layernorm(x, g, b):
    mu = jnp.mean(x, axis=-1, keepdims=True)
    var = jnp.mean(jnp.square(x - mu), axis=-1, keepdims=True)
    return (x - mu) * lax.rsqrt(var + EPS) * g + b


def _f_pre(x, g):
    return (_rms(x.astype(F32), g),)


def _f_post_term(scale, has_bias):
    def fn(o, g, *b):
        o = o.astype(F32)
        if has_bias:
            o = o + b[0]
        return (scale * _rms(o, g),)
    return fn


def _f_post(scale, has_bias):
    term = _f_post_term(scale, has_bias)

    def fn(x, o, g, *b):
        return (x + term(o, g, *b)[0],)
    return fn


def _f_s5_gelu(ylin, u, d):
    return (jax.nn.gelu(ylin.astype(F32) + d * u.astype(F32)),)


def _f_s5_glu(y, gl, b):
    return (y.astype(F32) * jax.nn.sigmoid(gl.astype(F32) + b),)


def _f_cv_glu(z0, b):
    z = z0.astype(F32) + b
    return (z[:, :D_MODEL] * jax.nn.sigmoid(z[:, D_MODEL:]),)


def _f_cv_ln(zc, g, b):
    return (jax.nn.silu(_layernorm(zc.astype(F32), g, b)),)


def _f_gm_in(z0, b, g, bl):
    z = jax.nn.gelu(z0.astype(F32) + b)
    return z[:, :GM_E], _layernorm(z[:, GM_E:], g, bl)


def _f_at_combine(o0, o1, o2, l0, l1, l2):
    m = jnp.maximum(jnp.maximum(l0, l1), l2)
    e0, e1, e2 = jnp.exp(l0 - m), jnp.exp(l1 - m), jnp.exp(l2 - m)
    return ((e0 * o0 + e1 * o1 + e2 * o2) / (e0 + e1 + e2),)


def _f_s5_disc(ar, ai, ldt, br, bi):
    dt = jnp.exp(ldt)
    mag = jnp.exp(dt * ar)
    abr = mag * jnp.cos(dt * ai)
    abi = mag * jnp.sin(dt * ai)
    den = ar * ar + ai * ai
    nr = abr - 1.0
    f_re = (nr * ar + abi * ai) / den
    f_im = (abi * ar - nr * ai) / den
    return abr, abi, f_re * br - f_im * bi, f_re * bi + f_im * br


def _loss_call(y, tgt):
    T, D = y.shape
    tile = 256

    def body(y_ref, t_ref, dy_ref, l_ref):
        err = y_ref[...] - t_ref[...]
        dy_ref[...] = err * (1.0 / D)
        part = 0.5 * jnp.sum(jnp.mean(err * err, axis=-1, keepdims=True), axis=0, keepdims=True)
        part = jnp.broadcast_to(part, (1, LANES))
        first = pl.program_id(0) == 0

        @pl.when(first)
        def _():
            l_ref[...] = part

        @pl.when(jnp.logical_not(first))
        def _():
            l_ref[...] += part

    return pl.pallas_call(
        body, name="loss", grid=(T // tile,),
        in_specs=[pl.BlockSpec((tile, D), lambda i: (i, 0))] * 2,
        out_specs=[pl.BlockSpec((tile, D), lambda i: (i, 0)), pl.BlockSpec((1, LANES), lambda i: (0, 0))],
        out_shape=[jax.ShapeDtypeStruct((T, D), F32), jax.ShapeDtypeStruct((1, LANES), F32)],
        compiler_params=_cparams(("arbitrary",)),
    )(y, tgt)


def _bd(xs, ws, *, add=None, out_dtype=F32, name):
    T = xs[0].shape[0]
    nb, kw, nw = ws[0].shape
    tm = 256
    n = len(xs)

    def body(*refs):
        o_ref = refs[-1]
        for j in range(nb):
            acc = None
            for x_ref, w_ref in zip(refs[:n], refs[n:2 * n]):
                p = jnp.dot(x_ref[:, j * kw:(j + 1) * kw].astype(BF16), w_ref[j].astype(BF16),
                            preferred_element_type=F32)
                acc = p if acc is None else acc + p
            if add is not None:
                acc = acc + refs[2 * n][:, j * nw:(j + 1) * nw].astype(F32)
            o_ref[:, j * nw:(j + 1) * nw] = acc.astype(o_ref.dtype)

    in_specs = [pl.BlockSpec((tm, nb * kw), lambda i: (i, 0)) for _ in xs]
    in_specs += [pl.BlockSpec((nb, kw, nw), lambda i: (0, 0, 0)) for _ in ws]
    args = list(xs) + list(ws)
    if add is not None:
        in_specs.append(pl.BlockSpec((tm, nb * nw), lambda i: (i, 0)))
        args.append(add)
    return pl.pallas_call(
        body, name=name, grid=(T // tm,), in_specs=in_specs,
        out_specs=pl.BlockSpec((tm, nb * nw), lambda i: (i, 0)),
        out_shape=jax.ShapeDtypeStruct((T, nb * nw), out_dtype),
        compiler_params=_cparams(("parallel",)),
    )(*args)


def _bd_wgrad(x, dy, kw, nw, *, name):
    T = x.shape[0]
    nb = x.shape[1] // kw
    tk = 512

    def body(x_ref, dy_ref, o_ref):
        first = pl.program_id(0) == 0
        for j in range(nb):
            p = lax.dot_general(x_ref[:, j * kw:(j + 1) * kw].astype(BF16), dy_ref[:, j * nw:(j + 1) * nw].astype(BF16),
                                (((0,), (0,)), ((), ())), preferred_element_type=F32)

            @pl.when(first)
            def _(j=j, p=p):
                o_ref[j] = p

            @pl.when(jnp.logical_not(first))
            def _(j=j, p=p):
                o_ref[j] += p

    return pl.pallas_call(
        body, name=name, grid=(T // tk,),
        in_specs=[pl.BlockSpec((tk, nb * kw), lambda k: (k, 0)), pl.BlockSpec((tk, nb * nw), lambda k: (k, 0))],
        out_specs=pl.BlockSpec((nb, kw, nw), lambda k: (0, 0, 0)),
        out_shape=jax.ShapeDtypeStruct((nb, kw, nw), F32),
        compiler_params=_cparams(("arbitrary",)),
    )(x, dy)


SCAN_COLS = min(S5_NS, 4096)
SCAN_ROWS = 128


def _scan_fwd(bur, bui, ar, ai):
    T, NS = bur.shape
    cw, tc = SCAN_COLS, SCAN_ROWS

    def body(bur_ref, bui_ref, ar_ref, ai_ref, sr_ref, si_ref, cr, ci):
        @pl.when(pl.program_id(1) == 0)
        def _():
            cr[...] = jnp.zeros_like(cr)
            ci[...] = jnp.zeros_like(ci)

        a_r, a_i = ar_ref[...], ai_ref[...]

        def step8(t8, carry):
            sr, si = carry
            base = pl.multiple_of(t8 * 8, 8)
            for r in range(8):
                br = bur_ref[pl.ds(base + r, 1), :]
                bi = bui_ref[pl.ds(base + r, 1), :]
                sr, si = a_r * sr - a_i * si + br, a_r * si + a_i * sr + bi
                sr_ref[pl.ds(base + r, 1), :] = sr
                si_ref[pl.ds(base + r, 1), :] = si
            return sr, si

        sr, si = lax.fori_loop(0, tc // 8, step8, (cr[...], ci[...]))
        cr[...] = sr
        ci[...] = si

    blk = pl.BlockSpec((tc, cw), lambda c, t: (t, c))
    vec = pl.BlockSpec((1, cw), lambda c, t: (0, c))
    return pl.pallas_call(
        body, name="s5_scan_fwd", grid=(NS // cw, T // tc), in_specs=[blk, blk, vec, vec], out_specs=[blk, blk],
        out_shape=[jax.ShapeDtypeStruct((T, NS), F32)] * 2,
        scratch_shapes=[pltpu.VMEM((1, cw), F32)] * 2,
        compiler_params=_cparams(("parallel", "arbitrary")),
    )(bur, bui, ar, ai)


def _scan_bwd(gr, gi, sr, si, ar, ai):
    T, NS = gr.shape
    cw, tc = SCAN_COLS, SCAN_ROWS
    nt = T // tc

    def body(gr_ref, gi_ref, sr_ref, si_ref, ar_ref, ai_ref, lr_ref, li_ref, dar_ref, dai_ref, cr, ci):
        @pl.when(pl.program_id(1) == 0)
        def _():
            cr[...] = jnp.zeros_like(cr)
            ci[...] = jnp.zeros_like(ci)
            dar_ref[...] = jnp.zeros_like(dar_ref)
            dai_ref[...] = jnp.zeros_like(dai_ref)

        a_r, a_i = ar_ref[...], ai_ref[...]

        def step8(k, carry):
            lr, li, dar, dai = carry
            base = pl.multiple_of((tc // 8 - 1 - k) * 8, 8)
            for r in range(7, -1, -1):
                s_r = sr_ref[pl.ds(base + r, 1), :]
                s_i = si_ref[pl.ds(base + r, 1), :]
                dar = dar + lr * s_r + li * s_i
                dai = dai + li * s_r - lr * s_i
                g_r = gr_ref[pl.ds(base + r, 1), :]
                g_i = gi_ref[pl.ds(base + r, 1), :]
                lr, li = g_r + a_r * lr + a_i * li, g_i + a_r * li - a_i * lr
                lr_ref[pl.ds(base + r, 1), :] = lr
                li_ref[pl.ds(base + r, 1), :] = li
            return lr, li, dar, dai

        lr, li, dar, dai = lax.fori_loop(0, tc // 8, step8, (cr[...], ci[...], dar_ref[...], dai_ref[...]))
        cr[...] = lr
        ci[...] = li
        dar_ref[...] = dar
        dai_ref[...] = dai

    blk = pl.BlockSpec((tc, cw), lambda c, t: (nt - 1 - t, c))
    vec = pl.BlockSpec((1, cw), lambda c, t: (0, c))
    return pl.pallas_call(
        body, name="s5_scan_bwd", grid=(NS // cw, nt), in_specs=[blk, blk, blk, blk, vec, vec],
        out_specs=[blk, blk, vec, vec],
        out_shape=[jax.ShapeDtypeStruct((T, NS), F32)] * 2 + [jax.ShapeDtypeStruct((1, NS), F32)] * 2,
        scratch_shapes=[pltpu.VMEM((1, cw), F32)] * 2,
        compiler_params=_cparams(("parallel", "arbitrary")),
    )(gr, gi, sr, si, ar, ai)


CONV_ROWS = 256
CONV_HALO = 32
CONV_PAD = CONV_HALO - (CONV_W - 1)
CONV_SUB = 16


def _conv_shifts(ext, sh, n):
    ext[pl.ds(n, 8), :] = jnp.zeros((8, ext.shape[1]), F32)
    for s in range(8):
        sh[s] = ext[pl.ds(s, n), :]


def _conv_rows(sh, start):
    return sh[start % 8, pl.ds(start - start % 8, CONV_SUB), :]


def _conv_fwd(z, dw, dwb):
    T, D = z.shape
    tc, hl = CONV_ROWS, CONV_HALO
    per = tc // hl

    def body(z_ref, zp_ref, dw_ref, b_ref, o_ref, ext, sh):
        i = pl.program_id(0)
        ext[pl.ds(0, hl), :] = jnp.where(i > 0, zp_ref[...], 0.0)
        ext[pl.ds(hl, tc), :] = z_ref[...]
        _conv_shifts(ext, sh, tc + hl)
        for rb in range(tc // CONV_SUB):
            r0 = rb * CONV_SUB
            acc = jnp.zeros((CONV_SUB, D), F32) + b_ref[...]
            for k in range(CONV_W):
                acc = acc + dw_ref[pl.ds(k, 1), :] * _conv_rows(sh, r0 + CONV_PAD + k)
            o_ref[pl.ds(r0, CONV_SUB), :] = acc

    return pl.pallas_call(
        body, name="conv_fwd", grid=(T // tc,),
        in_specs=[pl.BlockSpec((tc, D), lambda i: (i, 0)),
                  pl.BlockSpec((hl, D), lambda i: (jnp.maximum(i * per - 1, 0), 0)),
                  pl.BlockSpec((hl, D), lambda i: (0, 0)), pl.BlockSpec((1, D), lambda i: (0, 0))],
        out_specs=pl.BlockSpec((tc, D), lambda i: (i, 0)),
        out_shape=jax.ShapeDtypeStruct((T, D), F32),
        scratch_shapes=[pltpu.VMEM((tc + hl + 8, D), F32), pltpu.VMEM((8, tc + hl, D), F32)],
        compiler_params=_cparams(("parallel",)),
    )(z, z, dw, dwb)


def _conv_bwd(dout, z, dw):
    T, D = z.shape
    tc, hl = CONV_ROWS, CONV_HALO
    per = tc // hl
    nblk = T // tc

    def body(g_ref, gn_ref, z_ref, zp_ref, dw_ref, dz_ref, ddw_ref, db_ref, gext, zext, gsh, zsh, acc8):
        i = pl.program_id(0)
        gext[pl.ds(0, tc), :] = g_ref[...]
        gext[pl.ds(tc, hl), :] = jnp.where(i < nblk - 1, gn_ref[...], 0.0)
        zext[pl.ds(0, hl), :] = jnp.where(i > 0, zp_ref[...], 0.0)
        zext[pl.ds(hl, tc), :] = z_ref[...]
        _conv_shifts(gext, gsh, tc + hl)
        _conv_shifts(zext, zsh, tc + hl)
        for rb in range(tc // CONV_SUB):
            r0 = rb * CONV_SUB
            acc = jnp.zeros((CONV_SUB, D), F32)
            for k in range(CONV_W):
                acc = acc + dw_ref[pl.ds(k, 1), :] * _conv_rows(gsh, r0 + CONV_W - 1 - k)
            dz_ref[pl.ds(r0, CONV_SUB), :] = acc

        @pl.when(i == 0)
        def _():
            acc8[...] = jnp.zeros_like(acc8)
            db_ref[...] = jnp.zeros_like(db_ref)

        db_ref[...] += jnp.sum(g_ref[...], axis=0, keepdims=True)
        for k in range(CONV_W):
            part = jnp.zeros((8, D), F32)
            for rb in range(tc // CONV_SUB):
                r0 = rb * CONV_SUB
                prod = g_ref[pl.ds(r0, CONV_SUB), :] * _conv_rows(zsh, r0 + CONV_PAD + k)
                for s in range(CONV_SUB // 8):
                    part = part + prod[s * 8:(s + 1) * 8]
            acc8[k] += part

        @pl.when(i == nblk - 1)
        def _():
            ddw_ref[...] = jnp.sum(acc8[...], axis=1)

    return pl.pallas_call(
        body, name="conv_bwd", grid=(nblk,),
        in_specs=[pl.BlockSpec((tc, D), lambda i: (i, 0)),
                  pl.BlockSpec((hl, D), lambda i: (jnp.minimum((i + 1) * per, nblk * per - 1), 0)),
                  pl.BlockSpec((tc, D), lambda i: (i, 0)),
                  pl.BlockSpec((hl, D), lambda i: (jnp.maximum(i * per - 1, 0), 0)),
                  pl.BlockSpec((hl, D), lambda i: (0, 0))],
        out_specs=[pl.BlockSpec((tc, D), lambda i: (i, 0)), pl.BlockSpec((hl, D), lambda i: (0, 0)),
                   pl.BlockSpec((1, D), lambda i: (0, 0))],
        out_shape=[jax.ShapeDtypeStruct((T, D), F32), jax.ShapeDtypeStruct((hl, D), F32),
                   jax.ShapeDtypeStruct((1, D), F32)],
        scratch_shapes=[pltpu.VMEM((tc + hl + 8, D), F32)] * 2 + [pltpu.VMEM((8, tc + hl, D), F32)] * 2
        + [pltpu.VMEM((hl, 8, D), F32)],
        compiler_params=_cparams(("arbitrary",)),
    )(dout, dout, z, z, dw)


def _gm_causal():
    r = lax.broadcasted_iota(jnp.int32, (GM_CHUNK, GM_CHUNK), 0)
    c = lax.broadcasted_iota(jnp.int32, (GM_CHUNK, GM_CHUNK), 1)
    return r >= c


def _gm_sg_fwd(u, v, ws, bs_col):
    T, E = u.shape
    hw = E // GM_HEADS

    def body(u_ref, v_ref, w_ref, b_ref, o_ref):
        causal = _gm_causal()
        for h in range(GM_HEADS):
            cols = slice(h * hw, (h + 1) * hw)
            w = jnp.where(causal, w_ref[h], 0.0).astype(BF16)
            s = jnp.dot(w, v_ref[:, cols], preferred_element_type=F32) + b_ref[h]
            o_ref[:, cols] = (u_ref[:, cols] * s).astype(o_ref.dtype)

    return pl.pallas_call(
        body, name="gm_sg_fwd", grid=(T // GM_CHUNK,),
        in_specs=[pl.BlockSpec((GM_CHUNK, E), lambda i: (i, 0)), pl.BlockSpec((GM_CHUNK, E), lambda i: (i, 0)),
                  pl.BlockSpec(ws.shape, lambda i: (0, 0, 0)), pl.BlockSpec(bs_col.shape, lambda i: (0, 0, 0))],
        out_specs=pl.BlockSpec((GM_CHUNK, E), lambda i: (i, 0)),
        out_shape=jax.ShapeDtypeStruct((T, E), BF16),
        compiler_params=_cparams(("parallel",)),
    )(u, v, ws, bs_col)


def _gm_sg_bwd(dus, u, v, ws, bs_col):
    T, E = u.shape
    hw = E // GM_HEADS

    def body(g_ref, u_ref, v_ref, w_ref, b_ref, du_ref, dv_ref, dw_ref, db_ref):
        causal = _gm_causal()

        @pl.when(pl.program_id(0) == 0)
        def _():
            dw_ref[...] = jnp.zeros_like(dw_ref)
            db_ref[...] = jnp.zeros_like(db_ref)

        for h in range(GM_HEADS):
            cols = slice(h * hw, (h + 1) * hw)
            w = jnp.where(causal, w_ref[h], 0.0).astype(BF16)
            vh = v_ref[:, cols]
            s = jnp.dot(w, vh, preferred_element_type=F32) + b_ref[h]
            g = g_ref[:, cols]
            du_ref[:, cols] = g * s
            ds = g * u_ref[:, cols]
            dsb = ds.astype(BF16)
            dv_ref[:, cols] = lax.dot_general(w, dsb, (((0,), (0,)), ((), ())), preferred_element_type=F32)
            dwh = lax.dot_general(dsb, vh, (((1,), (1,)), ((), ())), preferred_element_type=F32)
            dw_ref[h] += jnp.where(causal, dwh, 0.0)
            db_ref[h] += jnp.broadcast_to(jnp.sum(ds, axis=1, keepdims=True), (GM_CHUNK, LANES))

    blk = pl.BlockSpec((GM_CHUNK, E), lambda i: (i, 0))
    return pl.pallas_call(
        body, name="gm_sg_bwd", grid=(T // GM_CHUNK,),
        in_specs=[blk, blk, blk, pl.BlockSpec(ws.shape, lambda i: (0, 0, 0)),
                  pl.BlockSpec(bs_col.shape, lambda i: (0, 0, 0))],
        out_specs=[blk, blk, pl.BlockSpec(ws.shape, lambda i: (0, 0, 0)),
                   pl.BlockSpec((GM_HEADS, GM_CHUNK, LANES), lambda i: (0, 0, 0))],
        out_shape=[jax.ShapeDtypeStruct((T, E), F32), jax.ShapeDtypeStruct((T, E), F32),
                   jax.ShapeDtypeStruct(ws.shape, F32), jax.ShapeDtypeStruct((GM_HEADS, GM_CHUNK, LANES), F32)],
        compiler_params=_cparams(("arbitrary",)),
    )(dus, u, v, ws, bs_col)


def _t5_bucket_steps(dilation):
    max_exact = NUM_BUCKETS // 2
    delta = np.arange(BLOCK + 1)
    dist = delta * dilation
    distf = np.maximum(dist, 1).astype(np.float32)
    large = max_exact + (np.log(distf / np.float32(max_exact)) / np.float32(math.log(MAX_DISTANCE / max_exact))
                         * np.float32(NUM_BUCKETS - max_exact)).astype(np.int32)
    large = np.minimum(large, NUM_BUCKETS - 1)
    bucket = np.where(dist < max_exact, dist, large)
    steps = []
    for d in range(1, BLOCK + 1):
        inc = int(bucket[d] - bucket[d - 1])
        assert inc >= 0
        if inc:
            steps.append((d, inc))
    assert int(bucket[0]) == 0
    return steps


def _bucket_map(dilation):
    qi = lax.broadcasted_iota(jnp.int32, (BLOCK, 2 * BLOCK), 0)
    ki = lax.broadcasted_iota(jnp.int32, (BLOCK, 2 * BLOCK), 1)
    delta = qi + BLOCK - ki
    bm = jnp.zeros((BLOCK, 2 * BLOCK), jnp.int32)
    for thr, inc in _t5_bucket_steps(dilation):
        bm = bm + jnp.where(delta >= thr, inc, 0)
    return bm


def _at_bias(table, g, dilation):
    H = AT_HEADS

    def body(t_ref, o_ref):
        bm = _bucket_map(dilation)
        for h in range(H):
            acc = jnp.zeros((BLOCK, 2 * BLOCK), F32)
            for b in range(NUM_BUCKETS):
                acc = jnp.where(bm == b, t_ref[b, g * H + h], acc)
            o_ref[h] = acc

    return pl.pallas_call(body, name="at_bias", in_specs=[pl.BlockSpec(memory_space=pltpu.SMEM)],
                          out_shape=jax.ShapeDtypeStruct((H, BLOCK, 2 * BLOCK), F32))(table)


def _at_bias_bwd(dbias, dilation):
    H = AT_HEADS

    def body(d_ref, o_ref):
        bm = _bucket_map(dilation)
        for h in range(H):
            d = d_ref[h]
            for b in range(NUM_BUCKETS):
                o_ref[b, h] = jnp.sum(jnp.where(bm == b, d, 0.0))

    return pl.pallas_call(body, name="at_bias_bwd", out_specs=pl.BlockSpec(memory_space=pltpu.SMEM),
                          out_shape=jax.ShapeDtypeStruct((NUM_BUCKETS, H), F32))(dbias)


def _at_mask(i, nbs):
    qi = lax.broadcasted_iota(jnp.int32, (BLOCK, 2 * BLOCK), 0)
    ki = lax.broadcasted_iota(jnp.int32, (BLOCK, 2 * BLOCK), 1)
    no_prev = jnp.where(i % nbs == 0, 4 * BLOCK, 0)
    return ((ki < BLOCK) & (ki >= qi + no_prev)) | ((ki >= BLOCK) & (ki - BLOCK <= qi))


def _head_lanes():
    lane = lax.broadcasted_iota(jnp.int32, (BLOCK, LANES), 1)
    return [lane < HEAD_DIM, lane >= HEAD_DIM]


def _at_fwd(qkv, bias, nbs, cb):
    T = qkv.shape[0]
    D = D_MODEL
    npair = D // LANES
    scale = HEAD_DIM ** -0.5

    def body(q_ref, kc_ref, kp_ref, vc_ref, vp_ref, b_ref, o_ref, l_ref):
        i = pl.program_id(0)
        mask = _at_mask(i, nbs)
        sel = _head_lanes()
        for j in range(npair):
            cols = slice(j * LANES, (j + 1) * LANES)
            q = q_ref[:, cols]
            kk = jnp.concatenate([kp_ref[:, cols], kc_ref[:, cols]], axis=0)
            vv = jnp.concatenate([vp_ref[:, cols], vc_ref[:, cols]], axis=0)
            sel2 = [jnp.concatenate([s_, s_], axis=0) for s_ in sel]
            ps, vvs, dens, lses = [], [], [], []
            for e in range(2):
                qh = jnp.where(sel[e], q, jnp.zeros_like(q))
                s = lax.dot_general(qh, kk, (((1,), (1,)), ((), ())), preferred_element_type=F32) * scale
                s = jnp.where(mask, s + b_ref[2 * j + e], -1e30)
                m = jnp.max(s, axis=1, keepdims=True)
                p = jnp.exp(s - m)
                den = jnp.sum(p, axis=1, keepdims=True)
                ps.append(p.astype(BF16))
                vvs.append(jnp.where(sel2[e], vv, jnp.zeros_like(vv)))
                dens.append(den)
                lses.append(m + jnp.log(den))
            o_pair = jnp.dot(jnp.concatenate(ps, axis=1), jnp.concatenate(vvs, axis=0), preferred_element_type=F32)
            o_ref[:, cols] = o_pair / jnp.where(sel[0], dens[0], dens[1])
            l_ref[:, cols] = jnp.where(sel[0], lses[0], lses[1])

    blk = lambda c, prev: pl.BlockSpec((BLOCK, D), (lambda i: (jnp.maximum(i - 1, 0), cb + c)) if prev
                                       else (lambda i: (i, cb + c)))
    out = pl.BlockSpec((BLOCK, D), lambda i: (i, 0))
    return pl.pallas_call(
        body, name="at_fwd", grid=(T // BLOCK,),
        in_specs=[blk(0, False), blk(1, False), blk(1, True), blk(2, False), blk(2, True),
                  pl.BlockSpec(bias.shape, lambda i: (0, 0, 0))],
        out_specs=[out, out], out_shape=[jax.ShapeDtypeStruct((T, D), F32)] * 2,
        compiler_params=_cparams(("parallel",)),
    )(qkv, qkv, qkv, qkv, qkv, bias)


def _at_bwd(qkv, bias, o, lse, do, dlse, nbs, cb):
    T = qkv.shape[0]
    D = D_MODEL
    nblk = T // BLOCK
    npair = D // LANES
    scale = HEAD_DIM ** -0.5

    def body(q_ref, kc_ref, kp_ref, vc_ref, vp_ref, b_ref, o_ref, l_ref, do_ref, dl_ref, dqkv_ref, db_ref, carry):
        i = pl.program_id(0)

        @pl.when(i == 0)
        def _():
            carry[...] = jnp.zeros_like(carry)
            db_ref[...] = jnp.zeros_like(db_ref)

        @pl.when(i == nblk)
        def _():
            dqkv_ref[...] = carry[...].astype(dqkv_ref.dtype)

        @pl.when(i < nblk)
        def _():
            mask = _at_mask(i, nbs)
            sel = _head_lanes()
            for j in range(npair):
                cols = slice(j * LANES, (j + 1) * LANES)
                kcols = slice(D + j * LANES, D + (j + 1) * LANES)
                vcols = slice(2 * D + j * LANES, 2 * D + (j + 1) * LANES)
                q = q_ref[:, cols]
                kk = jnp.concatenate([kp_ref[:, cols], kc_ref[:, cols]], axis=0)
                vv = jnp.concatenate([vp_ref[:, cols], vc_ref[:, cols]], axis=0)
                dov = do_ref[:, cols]
                dob = dov.astype(BF16)
                oo = dov * o_ref[:, cols]
                lv = l_ref[:, cols]
                dlv = dl_ref[:, cols]
                sel2 = [jnp.concatenate([s_, s_], axis=0) for s_ in sel]
                qhs, dohs, kkhs, dsbs, pbs = [], [], [], [], []
                for e in range(2):
                    qh = jnp.where(sel[e], q, jnp.zeros_like(q))
                    s = lax.dot_general(qh, kk, (((1,), (1,)), ((), ())), preferred_element_type=F32) * scale
                    s = jnp.where(mask, s + b_ref[2 * j + e], -1e30)
                    lse_h = jnp.max(jnp.where(sel[e], lv, -jnp.inf), axis=1, keepdims=True)
                    p = jnp.exp(s - lse_h)
                    doh = jnp.where(sel[e], dob, jnp.zeros_like(dob))
                    dp = lax.dot_general(doh, vv, (((1,), (1,)), ((), ())), preferred_element_type=F32)
                    delta = jnp.sum(jnp.where(sel[e], oo, 0.0), axis=1, keepdims=True)
                    dlse_h = jnp.sum(jnp.where(sel[e], dlv, 0.0), axis=1, keepdims=True)
                    ds = p * (dp - delta + dlse_h)
                    db_ref[2 * j + e] += ds
                    qhs.append(qh)
                    dohs.append(doh)
                    kkhs.append(jnp.where(sel2[e], kk, jnp.zeros_like(kk)))
                    dsbs.append((ds * scale).astype(BF16))
                    pbs.append(p.astype(BF16))
                tn = (((0,), (0,)), ((), ()))
                dq_pair = jnp.dot(jnp.concatenate(dsbs, axis=1), jnp.concatenate(kkhs, axis=0), preferred_element_type=F32)
                dk_pair = lax.dot_general(jnp.concatenate(dsbs, axis=0), jnp.concatenate(qhs, axis=0), tn,
                                          preferred_element_type=F32)
                dv_pair = lax.dot_general(jnp.concatenate(pbs, axis=0), jnp.concatenate(dohs, axis=0), tn,
                                          preferred_element_type=F32)
                dqkv_ref[:, cols] = carry[:, cols].astype(dqkv_ref.dtype)
                dqkv_ref[:, kcols] = (carry[:, kcols] + dk_pair[:BLOCK]).astype(dqkv_ref.dtype)
                dqkv_ref[:, vcols] = (carry[:, vcols] + dv_pair[:BLOCK]).astype(dqkv_ref.dtype)
                carry[:, cols] = dq_pair
                carry[:, kcols] = dk_pair[BLOCK:]
                carry[:, vcols] = dv_pair[BLOCK:]

    cur = lambda i: jnp.minimum(i, nblk - 1)
    prev = lambda i: jnp.maximum(jnp.minimum(i, nblk - 1) - 1, 0)
    blk = lambda c, pv: pl.BlockSpec((BLOCK, D), (lambda i: (prev(i), cb + c)) if pv else (lambda i: (cur(i), cb + c)))
    row = pl.BlockSpec((BLOCK, D), lambda i: (cur(i), 0))
    return pl.pallas_call(
        body, name="at_bwd", grid=(nblk + 1,),
        in_specs=[blk(0, False), blk(1, False), blk(1, True), blk(2, False), blk(2, True),
                  pl.BlockSpec(bias.shape, lambda i: (0, 0, 0)), row, row, row, row],
        out_specs=[pl.BlockSpec((BLOCK, 3 * D), lambda i: (jnp.maximum(i - 1, 0), 0)),
                   pl.BlockSpec(bias.shape, lambda i: (0, 0, 0))],
        out_shape=[jax.ShapeDtypeStruct((T, 3 * D), BF16), jax.ShapeDtypeStruct(bias.shape, F32)],
        scratch_shapes=[pltpu.VMEM((BLOCK, 3 * D), F32)],
        compiler_params=_cparams(("arbitrary",)),
    )(qkv, qkv, qkv, qkv, qkv, bias, o, lse, do, dlse)


def _to_residue_major(a, d):
    if d == 1:
        return a
    T, C = a.shape
    return a.reshape(T // d, d, C).transpose(1, 0, 2).reshape(T, C)


def _from_residue_major(a, d):
    if d == 1:
        return a
    T, C = a.shape
    return a.reshape(d, T // d, C).transpose(1, 0, 2).reshape(T, C)


FFN_TM = 512


def _ffn_up(h, w1t, w3t, carry=None):
    T, Dm = h.shape
    Fw = w1t.shape[0]
    tm, tn = FFN_TM, _pick(Fw, 1408)
    grid = (Fw // tn, T // tm)
    nt = (((1,), (1,)), ((), ()))

    def body(*refs):
        (h_ref, w1_ref, w3_ref), (a_ref, b_ref, u_ref), _, begin, end = _carry_hooks(carry, refs, 3, 3, 2, grid)
        begin()
        hv = h_ref[...]
        a = lax.dot_general(hv, w1_ref[...], nt, preferred_element_type=F32)
        b = lax.dot_general(hv, w3_ref[...], nt, preferred_element_type=F32)
        a_ref[...] = a.astype(a_ref.dtype)
        b_ref[...] = b.astype(b_ref.dtype)
        u_ref[...] = (jax.nn.silu(a) * b).astype(u_ref.dtype)
        end()

    extra = carry if carry is not None else _NO_CARRY
    wspec = pl.BlockSpec((tn, Dm), lambda j, i: (j, 0))
    ospec = pl.BlockSpec((tm, tn), lambda j, i: (i, j))
    res = pl.pallas_call(
        body, name="ffn_up", grid=grid,
        in_specs=[pl.BlockSpec((tm, Dm), lambda j, i: (i, 0)), wspec, wspec] + extra.in_specs,
        out_specs=[ospec] * 3 + extra.out_specs,
        out_shape=[jax.ShapeDtypeStruct((T, Fw), BF16)] * 3 + extra.out_shape, scratch_shapes=extra.scratch,
        compiler_params=_cparams(("arbitrary",) * 2 if carry is not None else ("parallel", "parallel")),
    )(h, w1t, w3t, *extra.arrays)
    return res[:3], res[3:]


def _ffn_down_dx(do, w2, a, b, carry=None):
    T, Dm = do.shape
    Fw = w2.shape[0]
    tm, tn = FFN_TM, _pick(Fw, 1408)
    grid = (Fw // tn, T // tm)

    def body(*refs):
        (do_ref, w2_ref, a_ref, b_ref), (da_ref, db_ref), _, begin, end = _carry_hooks(carry, refs, 4, 2, 2, grid)
        begin()
        du = lax.dot_general(do_ref[...], w2_ref[...], (((1,), (1,)), ((), ())), preferred_element_type=F32)
        av = a_ref[...].astype(F32)
        bv = b_ref[...].astype(F32)
        sg = jax.nn.sigmoid(av)
        silu = av * sg
        da_ref[...] = (du * bv * (sg + silu * (1.0 - sg))).astype(da_ref.dtype)
        db_ref[...] = (du * silu).astype(db_ref.dtype)
        end()

    extra = carry if carry is not None else _NO_CARRY
    ospec = pl.BlockSpec((tm, tn), lambda j, i: (i, j))
    res = pl.pallas_call(
        body, name="ffn_down_dx", grid=grid,
        in_specs=[pl.BlockSpec((tm, Dm), lambda j, i: (i, 0)), pl.BlockSpec((tn, Dm), lambda j, i: (j, 0)), ospec, ospec]
        + extra.in_specs,
        out_specs=[ospec] * 2 + extra.out_specs,
        out_shape=[jax.ShapeDtypeStruct((T, Fw), BF16)] * 2 + extra.out_shape, scratch_shapes=extra.scratch,
        compiler_params=_cparams(("arbitrary",) * 2 if carry is not None else ("parallel", "parallel")),
    )(do, w2, a, b, *extra.arrays)
    return res[:2], res[2:]


def _pre_norm(x, g):
    return _rows(_f_pre, [x], [g], [(D_MODEL, BF16)], name="pre_norm")[0]


def _close_part(x, o, bias, g_post, g_pre_next, scale, name):
    extra = [] if bias is None else [bias]
    if g_pre_next is None:
        xo, = _rows(_f_post(scale, bias is not None), [x, o], [g_post] + extra, [(D_MODEL, F32)], name=name)
        return xo, None

    post = _f_post(scale, bias is not None)

    def fn(xv, ov, g, *rest):
        xo = post(xv, ov, g, *rest[:-1])[0]
        return xo, _rms(xo, rest[-1])

    return _rows(fn, [x, o], [g_post] + extra + [g_pre_next], [(D_MODEL, F32), (D_MODEL, BF16)], name=name)


def _ffn_fwd(x, h, p, carry_up=None, carry_down=None):
    (a, b, u), got_up = _ffn_up(h, p["w1"], p["w3"], carry_up)
    o = _mm(u, p["w2"], name="ffn_down", carry=carry_down)
    got_down = None
    if carry_down is not None:
        o, got_down = o
    return o, (x, h, a, b, u, o), got_up, got_down


def _ffn_bwd(saved, p, dxo, carry_a=None, carry_b=None):
    x, h, a, b, u, o = saved
    (do,), (dg_post,) = _rows_vjp(_f_post_term(0.5, False), [o], [p["g_post"]], [dxo], dtypes=[BF16], name="ffn_post_bwd")
    (da, db), got_a = _ffn_down_dx(do, p["w2"], a, b, carry_a)
    dw2 = _mm(u, do, ta=True, out_dtype=BF16, name="ffn_down_dw")
    dh = _mm((da, db), (p["w1"], p["w3"]), name="ffn_up_dx", carry=carry_b)
    got_b = None
    if carry_b is not None:
        dh, got_b = dh
    dw1 = _mm(da, h, ta=True, out_dtype=BF16, name="ffn_up_dw")
    dw3 = _mm(db, h, ta=True, out_dtype=BF16, name="ffn_up_dw")
    (dx,), (dg_pre,) = _rows_vjp(_f_pre, [x], [p["g_pre"]], [dh], dtypes=[F32], adds={0: dxo}, name="ffn_pre_bwd")
    return dx, {"w1": dw1, "w3": dw3, "w2": dw2, "g_pre": dg_pre, "g_post": dg_post}, got_a, got_b


def _expand_blocks(w, rows_first):
    w = w.reshape(S5_NB, S5_GB, S5_GROUP, S5_STATE)
    eye = jnp.eye(S5_GB, dtype=F32)
    if rows_first:
        e = w[:, :, :, None, :] * eye[None, :, None, :, None]
        return e.reshape(S5_NB, S5_GB * S5_GROUP, S5_BW)
    e = jnp.transpose(w, (0, 1, 3, 2))[:, :, :, None, :] * eye[None, :, None, :, None]
    return e.reshape(S5_NB, S5_BW, S5_GB * S5_GROUP)


def _extract_blocks(e, rows_first):
    eye = jnp.eye(S5_GB, dtype=F32)
    if rows_first:
        e = e.reshape(S5_NB, S5_GB, S5_GROUP, S5_GB, S5_STATE)
        w = jnp.sum(e * eye[None, :, None, :, None], axis=3)
    else:
        e = e.reshape(S5_NB, S5_GB, S5_STATE, S5_GB, S5_GROUP)
        w = jnp.transpose(jnp.sum(e * eye[None, :, None, :, None], axis=3), (0, 1, 3, 2))
    return w.reshape(S5_GROUPS, S5_GROUP, S5_STATE)


def _s5_prep(p):
    G, P, HG = S5_GROUPS, S5_STATE, S5_GROUP
    args = [p["a_re"].reshape(G, 1, P), p["a_im"].reshape(G, 1, P), p["log_dt"].reshape(G, 1, 1),
            jnp.transpose(p["b_re"], (0, 2, 1)), jnp.transpose(p["b_im"], (0, 2, 1))]
    abr, abi, bbr, bbi = _small(_f_s5_disc, args, [(G, 1, P)] * 2 + [(G, HG, P)] * 2, name="s5_disc")
    return args, abr.reshape(1, G * P), abi.reshape(1, G * P), bbr, bbi


def _s5_fwd(h, p):
    disc_args, abr, abi, bbr, bbi = _s5_prep(p)
    c_re, c_im = p["c_re"], p["c_im"]
    u = _mm(h, p["w_in"], name="s5_in")
    bur = _bd([u], [_expand_blocks(bbr, True)], name="s5_bu")
    bui = _bd([u], [_expand_blocks(bbi, True)], name="s5_bu")
    sr, si = _scan_fwd(bur, bui, abr, abi)
    ylin = _bd([sr, si], [_expand_blocks(c_re, False), _expand_blocks(-c_im, False)], name="s5_y")
    y, = _rows(_f_s5_gelu, [ylin, u], [p["d"]], [(D_MODEL, F32)], name="s5_gelu")
    gl = _mm(y, p["w_glu"], name="s5_glu_mm")
    z, = _rows(_f_s5_glu, [y, gl], [p["b_glu"]], [(D_MODEL, BF16)], name="s5_glu")
    m = _mm(z, p["w_out"], name="s5_out")
    return m, None, (h, disc_args, abr, abi, bbr, bbi, u, sr, si, ylin, y, gl, z)


def _s5_bwd(saved, p, dm):
    h, disc_args, abr, abi, bbr, bbi, u, sr, si, ylin, y, gl, z = saved
    c_re, c_im = p["c_re"], p["c_im"]
    dz = _mm(dm, p["w_out"], tb=True, name="s5_out_dx")
    dw_out = _mm(z, dm, ta=True, out_dtype=BF16, name="s5_out_dw")
    (dy1, dgl), (db_glu,) = _rows_vjp(_f_s5_glu, [y, gl], [p["b_glu"]], [dz], dtypes=[F32, BF16], name="s5_glu_bwd")
    dy2 = _mm(dgl, p["w_glu"], tb=True, name="s5_glu_dx")
    dw_glu = _mm(y, dgl, ta=True, out_dtype=BF16, name="s5_glu_dw")
    (dylin, du1), (dd,) = _rows_vjp(_f_s5_gelu, [ylin, u], [p["d"]], [(dy1, dy2)], dtypes=[F32, F32], name="s5_gelu_bwd")
    gr = _bd([dylin], [jnp.transpose(_expand_blocks(c_re, False), (0, 2, 1))], name="s5_y_dx")
    gi = _bd([dylin], [jnp.transpose(_expand_blocks(-c_im, False), (0, 2, 1))], name="s5_y_dx")
    dc_re = _extract_blocks(_bd_wgrad(sr, dylin, S5_BW, LANES, name="s5_y_dw"), False)
    dc_im = -_extract_blocks(_bd_wgrad(si, dylin, S5_BW, LANES, name="s5_y_dw"), False)
    lr, li, dabr, dabi = _scan_bwd(gr, gi, sr, si, abr, abi)
    du = _bd([lr, li], [jnp.transpose(_expand_blocks(bbr, True), (0, 2, 1)),
                        jnp.transpose(_expand_blocks(bbi, True), (0, 2, 1))], add=du1, out_dtype=BF16, name="s5_bu_dx")
    dbbr = _extract_blocks(_bd_wgrad(u, lr, LANES, S5_BW, name="s5_bu_dw"), True)
    dbbi = _extract_blocks(_bd_wgrad(u, li, LANES, S5_BW, name="s5_bu_dw"), True)
    G, P = S5_GROUPS, S5_STATE
    dar, dai, dldt, dbr, dbi = _small_vjp(_f_s5_disc, disc_args,
                                          [dabr.reshape(G, 1, P), dabi.reshape(G, 1, P), dbbr, dbbi], name="s5_disc_bwd")
    dh = _mm(du, p["w_in"], tb=True, name="s5_in_dx")
    dw_in = _mm(h, du, ta=True, out_dtype=BF16, name="s5_in_dw")
    grads = {"w_in": dw_in, "w_glu": dw_glu, "w_out": dw_out, "b_glu": db_glu, "d": dd,
             "a_re": dar.reshape(G, P), "a_im": dai.reshape(G, P), "log_dt": dldt.reshape(G),
             "b_re": jnp.transpose(dbr, (0, 2, 1)), "b_im": jnp.transpose(dbi, (0, 2, 1)),
             "c_re": dc_re, "c_im": dc_im}
    return dh, grads


def _cv_fwd(h, p):
    z0 = _mm(h, p["w_in"], tb=True, name="cv_in")
    zg, = _rows(_f_cv_glu, [z0], [p["b_in"]], [(D_MODEL, F32)], name="cv_glu")
    zc = _conv_fwd(zg, p["dw"], p["dw_b"])
    zl, = _rows(_f_cv_ln, [zc], [p["ln_g"], p["ln_b"]], [(D_MODEL, BF16)], name="cv_ln")
    m = _mm(zl, p["w_out"], name="cv_out")
    return m, p["b_out"], (h, z0, zg, zc, zl)


def _cv_bwd(saved, p, dm):
    h, z0, zg, zc, zl = saved
    dzl = _mm(dm, p["w_out"], tb=True, name="cv_out_dx")
    dw_out = _mm(zl, dm, ta=True, out_dtype=BF16, name="cv_out_dw")
    (dzc,), (dln_g, dln_b) = _rows_vjp(_f_cv_ln, [zc], [p["ln_g"], p["ln_b"]], [dzl], dtypes=[F32], name="cv_ln_bwd")
    dzg, ddw, ddw_b = _conv_bwd(dzc, zg, p["dw"])
    (dz0,), (db_in,) = _rows_vjp(_f_cv_glu, [z0], [p["b_in"]], [dzg], dtypes=[BF16], name="cv_glu_bwd")
    dh = _mm(dz0, p["w_in"], name="cv_in_dx")
    dw_in = _mm(dz0, h, ta=True, out_dtype=BF16, name="cv_in_dw")
    return dh, {"w_in": dw_in, "b_in": db_in, "dw": ddw, "dw_b": ddw_b, "ln_g": dln_g, "ln_b": dln_b, "w_out": dw_out}


def _gm_fwd(h, p):
    z0 = _mm(h, p["w_in"], tb=True, name="gm_in")
    u, v = _rows(_f_gm_in, [z0], [p["b_in"], p["ln_g"], p["ln_b"]], [(GM_E, F32), (GM_E, BF16)], name="gm_act")
    bs_col = p["b_s"].reshape(GM_HEADS, GM_CHUNK, 1)
    us = _gm_sg_fwd(u, v, p["w_s"], bs_col)
    m = _mm(us, p["w_out"], name="gm_out")
    return m, p["b_out"], (h, z0, u, v, us, bs_col)


def _gm_bwd(saved, p, dm):
    h, z0, u, v, us, bs_col = saved
    dus = _mm(dm, p["w_out"], tb=True, name="gm_out_dx")
    dw_out = _mm(us, dm, ta=True, out_dtype=BF16, name="gm_out_dw")
    du, dv, dw_s, db_s = _gm_sg_bwd(dus, u, v, p["w_s"], bs_col)
    (dz0,), (db_in, dln_g, dln_b) = _rows_vjp(_f_gm_in, [z0], [p["b_in"], p["ln_g"], p["ln_b"]], [du, dv],
                                              dtypes=[BF16], name="gm_act_bwd")
    dh = _mm(dz0, p["w_in"], name="gm_in_dx")
    dw_in = _mm(dz0, h, ta=True, out_dtype=BF16, name="gm_in_dw")
    return dh, {"w_in": dw_in, "b_in": db_in, "ln_g": dln_g, "ln_b": dln_b, "w_s": dw_s, "b_s": db_s[:, :, 0],
                "w_out": dw_out}


def _at_fwd_mixer(h, p):
    T = h.shape[0]
    D = D_MODEL
    qkv = _mm(h, p["w_qkv"], tb=True, out_dtype=BF16, name="at_qkv")
    res, outs, lses, biases = [], [], [], []
    for g, (window, d) in enumerate(PATTERNS):
        assert window // d == BLOCK and T % (BLOCK * d) == 0
        bias = _at_bias(p["rel_bias"], g, d)
        if d == 1:
            r, cb = qkv, 3 * g
        else:
            r, cb = _to_residue_major(qkv[:, g * 3 * D:(g + 1) * 3 * D], d), 0
        o, lse = _at_fwd(r, bias, T // d // BLOCK, cb)
        res.append((r, cb, o, lse))
        biases.append(bias)
        outs.append(_from_residue_major(o, d))
        lses.append(_from_residue_major(lse, d))
    oc, = _rows(_f_at_combine, outs + lses, [], [(D, BF16)], name="at_combine")
    m = _mm(oc, p["w_out"], name="at_out")
    return m, None, (h, res, biases, outs, lses, oc)


def _at_bwd_mixer(saved, p, dm):
    h, res, biases, outs, lses, oc = saved
    T = h.shape[0]
    doc = _mm(dm, p["w_out"], tb=True, name="at_out_dx")
    dw_out = _mm(oc, dm, ta=True, out_dtype=BF16, name="at_out_dw")
    dol, _ = _rows_vjp(_f_at_combine, outs + lses, [], [doc], dtypes=[F32] * 6, name="at_combine_bwd")
    dqkv, dtab = [], []
    for g, (window, d) in enumerate(PATTERNS):
        r, cb, o_res, lse_res = res[g]
        dq, dbias = _at_bwd(r, biases[g], o_res, lse_res, _to_residue_major(dol[g], d),
                            _to_residue_major(dol[3 + g], d), T // d // BLOCK, cb)
        dqkv.append(_from_residue_major(dq, d))
        dtab.append(_at_bias_bwd(dbias, d))
    w_rows = [p["w_qkv"][g * 3 * D_MODEL:(g + 1) * 3 * D_MODEL] for g in range(len(PATTERNS))]
    dh = _mm(tuple(dqkv), tuple(w_rows), name="at_qkv_dx")
    dw_qkv = jnp.concatenate([_mm(dq, h, ta=True, out_dtype=BF16, name="at_qkv_dw") for dq in dqkv], axis=0)
    return dh, {"w_qkv": dw_qkv, "w_out": dw_out, "rel_bias": jnp.concatenate(dtab, axis=1)}


_MIXERS = ((_s5_fwd, _s5_bwd), (_cv_fwd, _cv_bwd), (_gm_fwd, _gm_bwd), (_at_fwd_mixer, _at_bwd_mixer))


def _mixer_fwd(x, h, p, kind):
    m, bias, saved = _MIXERS[kind][0](h, p)
    return m, bias, (x, m, bias, saved)


def _mixer_bwd(saved_all, p, kind, dxo):
    x, m, bias, saved = saved_all
    extra = [] if bias is None else [bias]
    (dm,), dpars = _rows_vjp(_f_post_term(1.0, bias is not None), [m], [p["g_post"]] + extra, [dxo], dtypes=[BF16],
                             name="mix_post_bwd")
    dh, grads = _MIXERS[kind][1](saved, p, dm)
    (dx,), (dg_pre,) = _rows_vjp(_f_pre, [x], [p["g_pre"]], [dh], dtypes=[F32], adds={0: dxo}, name="mix_pre_bwd")
    grads["g_pre"] = dg_pre
    grads["g_post"] = dpars[0]
    if bias is not None:
        grads["b_out"] = dpars[1]
    return dx, grads


class _Carry:
    def __init__(self, arrays, kinds):
        self.arrays, self.kinds, self.n = list(arrays), list(kinds), len(arrays)
        hbm = pl.BlockSpec(memory_space=pl.ANY)
        self.in_specs = [hbm] * self.n
        self.out_specs = [hbm] * self.n
        self.out_shape = [jax.ShapeDtypeStruct((N_DEV,) + (a.shape[1:] if k == "a2a" else a.shape), a.dtype)
                          for a, k in zip(arrays, kinds)]
        self.scratch = [pltpu.SemaphoreType.DMA((self.n * (N_DEV - 1),)), pltpu.SemaphoreType.DMA((self.n * (N_DEV - 1),)),
                        pltpu.SemaphoreType.DMA((self.n,))]

    def _copies(self, ins, outs, sems, arrivals):
        send_sems, recv_sems, local_sems = sems
        x, y, c = lax.axis_index("x"), lax.axis_index("y"), lax.axis_index("c")
        me = 4 * x + 2 * y + c
        local, remote = [], []
        for a in range(self.n):
            a2a = self.kinds[a] == "a2a"
            if not arrivals:
                local.append(pltpu.make_async_copy(ins[a].at[me] if a2a else ins[a], outs[a].at[me], local_sems.at[a]))
            for k in range(1, N_DEV):
                px = 1 - x if k & 4 else x
                py = 1 - y if k & 2 else y
                pc = 1 - c if k & 1 else c
                peer = 4 * px + 2 * py + pc
                idx = a * (N_DEV - 1) + k - 1
                remote.append(pltpu.make_async_remote_copy(
                    src_ref=ins[a].at[peer] if a2a else ins[a], dst_ref=outs[a].at[peer if arrivals else me],
                    send_sem=send_sems.at[idx], recv_sem=recv_sems.at[idx], device_id=(px, py, pc),
                    device_id_type=pl.DeviceIdType.MESH))
        return local, remote

    def start(self, ins, outs, sems):
        local, sends = self._copies(ins, outs, sems, False)
        for cp in local + sends:
            cp.start()

    def wait(self, ins, outs, sems):
        local, sends = self._copies(ins, outs, sems, False)
        _, recvs = self._copies(ins, outs, sems, True)
        for cp in sends:
            cp.wait_send()
        for cp in recvs:
            cp.wait_recv()
        for cp in local:
            cp.wait()


class _NoCarry:
    n = 0
    arrays = in_specs = out_specs = out_shape = scratch = []


_NO_CARRY = _NoCarry()


def _carry_hooks(carry, refs, n_in, n_out, grid_rank, grid):
    nc = carry.n if carry is not None else 0
    ins, cin = refs[:n_in], refs[n_in:n_in + nc]
    outs, cout = refs[n_in + nc:n_in + nc + n_out], refs[n_in + nc + n_out:n_in + 2 * nc + n_out]
    rest = refs[n_in + 2 * nc + n_out:]
    scratch, sems = (rest[:len(rest) - 3], rest[len(rest) - 3:]) if nc else (rest, ())

    def at(step_of):
        cond = None
        for ax in range(grid_rank):
            c = pl.program_id(ax) == step_of(ax)
            cond = c if cond is None else cond & c
        return cond

    def begin():
        if nc:
            @pl.when(at(lambda ax: 0))
            def _():
                carry.start(cin, cout, sems)

    def end():
        if nc:
            @pl.when(at(lambda ax: grid[ax] - 1))
            def _():
                carry.wait(cin, cout, sems)

    return ins, outs, scratch, begin, end


def _exchange(arrays, kinds, *, name):
    carry = _Carry(arrays, kinds)

    def body(*refs):
        n = carry.n
        carry.start(refs[:n], refs[n:2 * n], refs[2 * n:])
        carry.wait(refs[:n], refs[n:2 * n], refs[2 * n:])

    return pl.pallas_call(body, name=name, in_specs=carry.in_specs, out_specs=carry.out_specs,
                          out_shape=carry.out_shape, scratch_shapes=carry.scratch)(*arrays)


def _adam(recv, w, m, v, *, name):
    R, C = w.shape
    tr = _pick_rows(R, 128)
    c1 = 1.0 - ADAM_B1 ** ADAM_STEP
    c2 = 1.0 - ADAM_B2 ** ADAM_STEP

    def body(r_ref, w_ref, m_ref, v_ref, g_ref, d_ref, nm_ref, nv_ref):
        g = r_ref[0].astype(F32)
        for q in range(1, N_DEV):
            g = g + r_ref[q].astype(F32)
        mm = ADAM_B1 * m_ref[...] + (1.0 - ADAM_B1) * g
        vv = ADAM_B2 * v_ref[...] + (1.0 - ADAM_B2) * jnp.square(g)
        m_hat = mm / c1
        v_hat = vv / c2
        g_ref[...] = g
        d_ref[...] = -ADAM_LR * (m_hat / (jnp.sqrt(v_hat) + ADAM_EPS) + ADAM_WD * w_ref[...])
        nm_ref[...] = mm
        nv_ref[...] = vv

    blk = pl.BlockSpec((tr, C), lambda i: (i, 0))
    return pl.pallas_call(
        body, name=name, grid=(R // tr,),
        in_specs=[pl.BlockSpec((N_DEV, tr, C), lambda i: (0, i, 0)), blk, blk, blk], out_specs=[blk] * 4,
        out_shape=[jax.ShapeDtypeStruct((R, C), F32)] * 4,
        compiler_params=_cparams(("parallel",)),
    )(recv, w, m, v)


PACK_COLS = 1024


def _padded(n):
    return -(-n // PACK_ALIGN) * PACK_ALIGN


def _pack_flat(pieces):
    flat = jnp.concatenate([p.reshape(-1) for p in pieces])
    n = flat.shape[0]
    return jnp.pad(flat, (0, _padded(n) - n)).reshape(-1, PACK_COLS)


def _shard_shape(shape, axis):
    s = list(shape)
    assert s[axis] % N_DEV == 0
    s[axis] //= N_DEV
    return tuple(s)


def _split_full(full, axis):
    s = full.shape
    r = full.reshape(s[:axis] + (N_DEV, s[axis] // N_DEV) + s[axis + 1:])
    return jnp.moveaxis(r, axis, 0)


def _merge_full(parts, axis):
    r = jnp.moveaxis(parts, 0, axis)
    s = r.shape
    return r.reshape(s[:axis] + (s[axis] * s[axis + 1],) + s[axis + 2:])


def _pack_full(entries, grads):
    flat = jnp.concatenate([_split_full(grads[k].reshape(shape), axis).reshape(N_DEV, -1)
                            for k, shape, axis in entries], axis=1)
    n = flat.shape[1]
    return jnp.pad(flat, ((0, 0), (0, _padded(n) - n))).reshape(N_DEV, -1, PACK_COLS)


def _unpack_gathered(entries, buf):
    flat = buf.reshape(N_DEV, -1)
    out, pos = {}, 0
    for k, shape, axis in entries:
        ss = _shard_shape(shape, axis)
        n = int(np.prod(ss))
        out[k] = _merge_full(flat[:, pos:pos + n].reshape((N_DEV,) + ss), axis)
        pos += n
    return out


def _unpack_shard(entries, buf):
    flat = buf.reshape(-1)
    out, pos = {}, 0
    for k, shape, axis in entries:
        ss = _shard_shape(shape, axis)
        n = int(np.prod(ss))
        out[k] = flat[pos:pos + n].reshape(ss)
        pos += n
    return out


def _unpack_flat(entries, buf):
    flat = buf.reshape(-1)
    out, pos = {}, 0
    for k, shape in entries:
        n = int(np.prod(shape))
        out[k] = flat[pos:pos + n].reshape(shape)
        pos += n
    return out


D, FF = D_MODEL, D_FF
_FFN_MATS = (("w1", (D, FF), 1), ("w3", (D, FF), 1), ("w2", (FF, D), 0))
_NORM_VECS = (("g_pre", (D,), 0), ("g_post", (D,), 0))
_MIX_MATS = (
    (("w_in", (D, D), 0), ("w_glu", (D, D), 0), ("w_out", (D, D), 0)),
    (("w_in", (D, 2 * D), 1), ("w_out", (D, D), 0)),
    (("w_in", (D, 2 * GM_E), 1), ("w_out", (GM_E, D), 0)),
    (("w_qkv", (D, 9 * D), 1), ("w_out", (D, D), 0)),
)
_MIX_VECS = (
    (),
    (("b_in", (2 * D,), 0), ("dw", (CONV_W, D), 1), ("dw_b", (D,), 0), ("ln_g", (D,), 0), ("ln_b", (D,), 0),
     ("b_out", (D,), 0)),
    (("b_in", (2 * GM_E,), 0), ("ln_g", (GM_E,), 0), ("ln_b", (GM_E,), 0), ("b_out", (D,), 0)),
    (),
)
_REPLICATED = (
    ("rel_bias", 3, "rel_bias", (NUM_BUCKETS, 3 * AT_HEADS)),
    ("s5_a_re", 0, "a_re", (S5_GROUPS, S5_STATE)), ("s5_a_im", 0, "a_im", (S5_GROUPS, S5_STATE)),
    ("s5_log_dt", 0, "log_dt", (S5_GROUPS,)),
    ("s5_b_re", 0, "b_re", (S5_GROUPS, S5_STATE, S5_GROUP)), ("s5_b_im", 0, "b_im", (S5_GROUPS, S5_STATE, S5_GROUP)),
    ("s5_c_re", 0, "c_re", (S5_GROUPS, S5_GROUP, S5_STATE)), ("s5_c_im", 0, "c_im", (S5_GROUPS, S5_GROUP, S5_STATE)),
    ("s5_d", 0, "d", (D,)), ("s5_b_glu", 0, "b_glu", (D,)),
    ("gm_w_s", 2, "w_s", (GM_HEADS, GM_CHUNK, GM_CHUNK)), ("gm_b_s", 2, "b_s", (GM_HEADS, GM_CHUNK)),
)
_MIX_PREFIX = ("s5_", "cv_", "gm_", "at_")
_TWIN_WEIGHTS = ('norm_pre', 'norm_post', 'ffn_w1', 'ffn_w3', 'ffn_w2', 'rel_bias', 's5_w_in', 's5_a_re', 's5_a_im',
                 's5_log_dt', 's5_b_re', 's5_b_im', 's5_c_re', 's5_c_im', 's5_d', 's5_w_glu', 's5_b_glu', 's5_w_out',
                 'cv_w_in', 'cv_b_in', 'cv_dw', 'cv_dw_b', 'cv_ln_g', 'cv_ln_b', 'cv_w_out', 'cv_b_out', 'gm_w_in',
                 'gm_b_in', 'gm_ln_g', 'gm_ln_b', 'gm_w_s', 'gm_b_s', 'gm_w_out', 'gm_b_out', 'at_w_qkv', 'at_w_out')


def _part_entries(part):
    if part[0] == "ffn":
        return _FFN_MATS, _NORM_VECS
    kind = part[1] % 4
    return _MIX_MATS[kind], _NORM_VECS + _MIX_VECS[kind]


def _part_shards(part, get):
    if part[0] == "ffn":
        _, i, j = part
        n = 0 if j == 0 else 2
        return {"w1": get("ffn_w1")[i, j], "w3": get("ffn_w3")[i, j], "w2": get("ffn_w2")[i, j],
                "g_pre": get("norm_pre")[i, n], "g_post": get("norm_post")[i, n]}
    _, i = part
    kind, j = i % 4, i // 4
    out = {"g_pre": get("norm_pre")[i, 1], "g_post": get("norm_post")[i, 1]}
    for k, _, _ in _MIX_MATS[kind] + _MIX_VECS[kind]:
        out[k] = get(_MIX_PREFIX[kind] + k)[j]
    return out


def _parts():
    parts = []
    for i in range(DEPTH):
        parts += [("ffn", i, 0), ("mix", i), ("ffn", i, 1)]
    return parts


def _as_par(v):
    return v.reshape(1, -1)


def _prepare_part(part, full, rep):
    if part[0] == "ffn":
        return {"w1": full["w1"], "w3": full["w3"], "w2": full["w2"],
                "g_pre": _as_par(full["g_pre"]), "g_post": _as_par(full["g_post"])}
    kind = part[1] % 4
    p = {"g_pre": _as_par(full["g_pre"]), "g_post": _as_par(full["g_post"])}
    for k, _, _ in _MIX_MATS[kind]:
        p[k] = full[k]
    for k, _, _ in _MIX_VECS[kind]:
        p[k] = _as_par(full[k]) if k != "dw" else jnp.pad(full[k], ((0, CONV_HALO - CONV_W), (0, 0)))
    if kind == 0:
        for k in ("a_re", "a_im", "log_dt", "b_re", "b_im"):
            p[k] = rep[k]
        p["c_re"], p["c_im"] = rep["c_re"], rep["c_im"]
        p["d"], p["b_glu"] = _as_par(rep["d"]), _as_par(rep["b_glu"])
    elif kind == 2:
        p["w_s"], p["b_s"] = rep["w_s"], rep["b_s"]
    elif kind == 3:
        p["rel_bias"] = rep["rel_bias"]
    return p


def _finish_grads(part, grads):
    out = dict(grads)
    for k in ("g_pre", "g_post", "b_in", "dw_b", "ln_g", "ln_b", "b_out", "d", "b_glu"):
        if k in out:
            out[k] = out[k].reshape(-1)
    if "dw" in out:
        out["dw"] = out["dw"][:CONV_W]
    return out


def _step(x, tgt, inputs, moments_m, moments_v):
    parts = _parts()
    rep = {}
    for name, kind, key, shape in _REPLICATED:
        rep[key] = inputs[name][0] if name != "rel_bias" else inputs[name]

    def stored(part, get):
        mats, vecs = _part_entries(part)
        sh = _part_shards(part, get)
        return [sh[k].T if axis == 1 else sh[k] for k, _, axis in mats], _pack_flat([sh[k] for k, _, _ in vecs])

    stored_w = [stored(part, lambda n: inputs[n]) for part in parts]

    def gather_of(idx):
        wmats, wv = stored_w[idx]
        return _Carry([w.astype(BF16) for w in wmats] + [wv], ["bcast"] * (len(wmats) + 1))

    def gathered(idx, bufs):
        mats, vecs = _part_entries(parts[idx])
        full = {k: b.reshape(-1, b.shape[-1]) for (k, _, _), b in zip(mats, bufs)}
        full.update(_unpack_gathered(vecs, bufs[-1]))
        return _prepare_part(parts[idx], full, rep)

    params = [None] * len(parts)
    first = gather_of(0)
    params[0] = gathered(0, _exchange(first.arrays, first.kinds, name="gather_first"))
    saved = []
    xs = x
    h = _pre_norm(xs, params[0]["g_pre"])
    for idx, part in enumerate(parts):
        if part[0] == "ffn":
            ahead = [i for i in (idx + 1, idx + 2) if i < len(parts) and params[i] is None]
            if part[2] == 1:
                ahead = ahead[:1]
            ahead = ahead[::-1]
            c_up = gather_of(ahead[0]) if ahead else None
            c_down = gather_of(ahead[1]) if len(ahead) > 1 else None
            o, s, got_up, got_down = _ffn_fwd(xs, h, params[idx], c_up, c_down)
            bias, scale = None, 0.5
            if c_up is not None:
                params[ahead[0]] = gathered(ahead[0], got_up)
            if c_down is not None:
                params[ahead[1]] = gathered(ahead[1], got_down)
        else:
            o, bias, s = _mixer_fwd(xs, h, params[idx], part[1] % 4)
            scale = 1.0
        saved.append(s)
        g_next = params[idx + 1]["g_pre"] if idx + 1 < len(parts) else None
        xs, h = _close_part(xs, o, bias, params[idx]["g_post"], g_next, scale, part[0] + "_post")
    dh, loss_vec = _loss_call(xs, tgt)
    loss_local = loss_vec[0, 0]

    results = {}
    rep_grads = {}

    def scatter_of(idx, grads):
        mats, vecs = _part_entries(parts[idx])
        gm = [grads[k].reshape(N_DEV, -1, grads[k].shape[-1]) for k, _, _ in mats]
        return _Carry(gm + [_pack_full(vecs, grads)], ["a2a"] * (len(gm) + 1))

    def update(idx, bufs):
        part = parts[idx]
        mats, vecs = _part_entries(part)
        wmats, wv = stored_w[idx]
        mmats, mv = stored(part, lambda n: moments_m[n])
        vmats, vv = stored(part, lambda n: moments_v[n])
        res = [dict() for _ in range(4)]
        for (k, _, axis), buf, w_, m_, v_ in zip(mats, bufs, wmats, mmats, vmats):
            for r, o in zip(res, _adam(buf, w_, m_, v_, name="adam_mat")):
                r[k] = o.T if axis == 1 else o
        for r, o in zip(res, _adam(bufs[-1], wv, mv, vv, name="adam_vecs")):
            r.update(_unpack_shard(vecs, o))
        results[part] = res

    rep_entries = [(name, shape) for name, _, _, shape in _REPLICATED]
    rg = None
    pending = []
    for idx in range(len(parts) - 1, -1, -1):
        part, p = parts[idx], params[idx]
        if part[0] == "ffn":
            riders = pending[:2]
            pending = pending[2:]
            c_a = riders[0][1] if riders else None
            c_b = riders[1][1] if len(riders) > 1 else None
            if idx == 0 and c_a is not None:
                c_a = _Carry(c_a.arrays + [_pack_flat([rep_grads[name] for name, _ in rep_entries])], c_a.kinds + ["bcast"])
            dh, grads, got_a, got_b = _ffn_bwd(saved[idx], p, dh, c_a, c_b)
            if idx == 0 and c_a is not None:
                rg, got_a = got_a[-1], got_a[:-1]
            for (ridx, _), got in zip(riders, (got_a, got_b)):
                update(ridx, got)
        else:
            dh, grads = _mixer_bwd(saved[idx], p, part[1] % 4, dh)
        grads = _finish_grads(part, grads)
        for name, kind, key, shape in _REPLICATED:
            if part[0] == "mix" and kind == part[1] % 4:
                rep_grads[name] = grads[key]
        pending.append((idx, scatter_of(idx, grads)))
    for ridx, c in pending:
        update(ridx, _exchange(c.arrays, c.kinds, name="scatter_last"))

    get_rep = lambda d: _pack_flat([(d[name][0] if name != "rel_bias" else d[name]) for name, _ in rep_entries])
    assert rg is not None
    orep = _adam(rg, get_rep(inputs), get_rep(moments_m), get_rep(moments_v), name="adam_rep")
    rep_out = [_unpack_flat(rep_entries, o) for o in orep]
    return loss_local, dh, results, rep_out


def _assemble(name, results, rep_out, which):
    for rname, _, _, _ in _REPLICATED:
        if rname == name:
            a = rep_out[which][name]
            return a if name == "rel_bias" else a[None]
    if name in ("norm_pre", "norm_post"):
        key = "g_pre" if name == "norm_pre" else "g_post"
        rows = []
        for i in range(DEPTH):
            rows.append(jnp.stack([results[("ffn", i, 0)][which][key], results[("mix", i)][which][key],
                                   results[("ffn", i, 1)][which][key]]))
        return jnp.stack(rows)
    if name.startswith("ffn_"):
        key = name[4:]
        return jnp.stack([jnp.stack([results[("ffn", i, j)][which][key] for j in range(2)]) for i in range(DEPTH)])
    kind = _MIX_PREFIX.index(name[:3])
    layers = [i for i in range(DEPTH) if i % 4 == kind]
    return jnp.stack([results[("mix", i)][which][name[3:]] for i in layers])


def kernel(x, norm_pre, norm_post, ffn_w1, ffn_w3, ffn_w2, rel_bias, s5_w_in, s5_a_re, s5_a_im, s5_log_dt, s5_b_re, s5_b_im, s5_c_re, s5_c_im, s5_d, s5_w_glu, s5_b_glu, s5_w_out, cv_w_in, cv_b_in, cv_dw, cv_dw_b, cv_ln_g, cv_ln_b, cv_w_out, cv_b_out, gm_w_in, gm_b_in, gm_ln_g, gm_ln_b, gm_w_s, gm_b_s, gm_w_out, gm_b_out, at_w_qkv, at_w_out, loss_target, m_norm_pre, m_norm_post, m_ffn_w1, m_ffn_w3, m_ffn_w2, m_rel_bias, m_s5_w_in, m_s5_a_re, m_s5_a_im, m_s5_log_dt, m_s5_b_re, m_s5_b_im, m_s5_c_re, m_s5_c_im, m_s5_d, m_s5_w_glu, m_s5_b_glu, m_s5_w_out, m_cv_w_in, m_cv_b_in, m_cv_dw, m_cv_dw_b, m_cv_ln_g, m_cv_ln_b, m_cv_w_out, m_cv_b_out, m_gm_w_in, m_gm_b_in, m_gm_ln_g, m_gm_ln_b, m_gm_w_s, m_gm_b_s, m_gm_w_out, m_gm_b_out, m_at_w_qkv, m_at_w_out, v_norm_pre, v_norm_post, v_ffn_w1, v_ffn_w3, v_ffn_w2, v_rel_bias, v_s5_w_in, v_s5_a_re, v_s5_a_im, v_s5_log_dt, v_s5_b_re, v_s5_b_im, v_s5_c_re, v_s5_c_im, v_s5_d, v_s5_w_glu, v_s5_b_glu, v_s5_w_out, v_cv_w_in, v_cv_b_in, v_cv_dw, v_cv_dw_b, v_cv_ln_g, v_cv_ln_b, v_cv_w_out, v_cv_b_out, v_gm_w_in, v_gm_b_in, v_gm_ln_g, v_gm_ln_b, v_gm_w_s, v_gm_b_s, v_gm_w_out, v_gm_b_out, v_at_w_qkv, v_at_w_out):
    args = locals()
    inputs = {n: args[n] for n in _TWIN_WEIGHTS}
    moments_m = {n: args["m_" + n] for n in _TWIN_WEIGHTS}
    moments_v = {n: args["v_" + n] for n in _TWIN_WEIGHTS}
    loss_local, dx, results, rep_out = _step(x[0], loss_target[0], inputs, moments_m, moments_v)
    loss = lax.psum(loss_local, AXES)
    out = [loss, dx[None]]
    for which in range(4):
        out += [_assemble(n, results, rep_out, which) for n in _TWIN_WEIGHTS]
    return tuple(out)
```

```python
import functools
import math

import numpy as np

import jax
import jax.numpy as jnp
from jax import lax
from jax.experimental import pallas as pl
from jax.experimental.pallas import tpu as pltpu

F32 = jnp.float32
BF16 = jnp.bfloat16

D_MODEL = 1024
DEPTH = 4
D_FF = 2816
EPS = 1e-6
S5_GROUP = 16
S5_STATE = 64
CONV_W = 31
GM_CHUNK = 128
GM_HEADS = 8
HEAD_DIM = 64
PATTERNS = ((128, 1), (512, 4), (2048, 16))
BLOCK = 128
NUM_BUCKETS = 32
MAX_DISTANCE = 2048
ADAM_LR = 0.001
ADAM_B1 = 0.9
ADAM_B2 = 0.999
ADAM_EPS = 1e-08
ADAM_WD = 0.01
ADAM_STEP = 10

N_DEV = 8
AXES = ("x", "y", "c")
LANES = 128
GM_E = 2 * D_MODEL
S5_GROUPS = D_MODEL // S5_GROUP
S5_GB = LANES // S5_GROUP
S5_NB = D_MODEL // LANES
S5_BW = S5_GB * S5_STATE
S5_NS = S5_GROUPS * S5_STATE
AT_HEADS = D_MODEL // HEAD_DIM
VMEM_LIMIT = 56 * 1024 * 1024
PACK_ALIGN = 16 * 1024


def _cparams(sem):
    return pltpu.CompilerParams(dimension_semantics=sem, vmem_limit_bytes=VMEM_LIMIT)


def _pick(n, cap):
    if n <= cap:
        return n
    best = None
    for t in range(LANES, cap + 1, LANES):
        if n % t == 0:
            best = t
    assert best is not None, (n, cap)
    return best


def _pick_rows(n, cap):
    best = None
    for t in range(16, min(n, cap) + 1, 16):
        if n % t == 0:
            best = t
    assert best is not None, (n, cap)
    return best


MM_VMEM_BUDGET = 40 * 1024 * 1024


def _mm(a, b, *, ta=False, tb=False, out_dtype=F32, name, carry=None):
    a_list = list(a) if isinstance(a, (tuple, list)) else [a]
    b_list = list(b) if isinstance(b, (tuple, list)) else [b]
    n_op = len(a_list)
    assert n_op == len(b_list)
    K, M = a_list[0].shape if ta else a_list[0].shape[::-1]
    N, K2 = b_list[0].shape if tb else b_list[0].shape[::-1]
    assert K == K2, (a_list[0].shape, b_list[0].shape, ta, tb)
    a_bytes = sum(x.dtype.itemsize for x in a_list)
    b_bytes = sum(x.dtype.itemsize for x in b_list)
    o_bytes = jnp.dtype(out_dtype).itemsize

    def vmem(tm, tn, tk, nk):
        acc = tm * tn * 4 if (nk > 1 and out_dtype != F32) else 0
        return 2 * (tm * tk * a_bytes + tk * tn * b_bytes + tm * tn * o_bytes) + acc

    if ta:
        tm, tn = _pick(M, 1408), _pick(N, 1408)
        tk = next(t for t in (2048, 1024, 512, 256) if K % t == 0 and vmem(tm, tn, t, 2) <= MM_VMEM_BUDGET)
    else:
        tm, tn = _pick(M, 512), _pick(N, 1408)
        tk = next(t for t in (K, _pick(K, 4608), _pick(K, 2816), _pick(K, 1024))
                  if vmem(tm, tn, t, K // t) <= MM_VMEM_BUDGET)
    nk = K // tk
    a_spec = pl.BlockSpec((tk, tm), lambda j, i, k: (k, i)) if ta else pl.BlockSpec((tm, tk), lambda j, i, k: (i, k))
    b_spec = pl.BlockSpec((tn, tk), lambda j, i, k: (j, k)) if tb else pl.BlockSpec((tk, tn), lambda j, i, k: (k, j))
    dims = (((0 if ta else 1,), (1 if tb else 0,)), ((), ()))
    use_scratch = nk > 1 and out_dtype != F32
    grid = (N // tn, M // tm, nk)

    def body(*refs):
        ins, (o_ref,), scratch, begin, end = _carry_hooks(carry, refs, 2 * n_op, 1, 3, grid)
        begin()
        p = None
        for a_ref, b_ref in zip(ins[:n_op], ins[n_op:]):
            d = lax.dot_general(a_ref[...].astype(BF16), b_ref[...].astype(BF16), dims, preferred_element_type=F32)
            p = d if p is None else p + d
        if nk == 1:
            o_ref[...] = p.astype(o_ref.dtype)
        else:
            acc = scratch[0] if use_scratch else o_ref
            k = pl.program_id(2)

            @pl.when(k == 0)
            def _():
                acc[...] = p

            @pl.when(k > 0)
            def _():
                acc[...] += p

            if use_scratch:
                @pl.when(k == nk - 1)
                def _():
                    o_ref[...] = acc[...].astype(o_ref.dtype)
        end()

    extra = carry if carry is not None else _NO_CARRY
    res = pl.pallas_call(
        body, name=name, grid=grid, in_specs=[a_spec] * n_op + [b_spec] * n_op + extra.in_specs,
        out_specs=[pl.BlockSpec((tm, tn), lambda j, i, k: (i, j))] + extra.out_specs,
        out_shape=[jax.ShapeDtypeStruct((M, N), out_dtype)] + extra.out_shape,
        scratch_shapes=([pltpu.VMEM((tm, tn), F32)] if use_scratch else []) + extra.scratch,
        compiler_params=_cparams(("arbitrary",) * 3 if carry is not None else ("parallel", "parallel", "arbitrary")),
    )(*a_list, *b_list, *extra.arrays)
    return res[0] if carry is None else (res[0], res[1:])


ROW_TILE_BYTES = 8 * 1024 * 1024


def _row_tile(arrays):
    row_bytes = sum(w * jnp.dtype(dt).itemsize for w, dt in arrays)
    for tile in (256, 128, 64, 32):
        if tile * row_bytes <= ROW_TILE_BYTES:
            return tile
    return 16


def _rows(fn, rows, pars, outs, *, name):
    T = rows[0].shape[0]
    tile = _row_tile([(r.shape[1], r.dtype) for r in rows] + list(outs))
    nr, npar = len(rows), len(pars)

    def body(*refs):
        r = [refs[i][...] for i in range(nr)]
        p = [refs[nr + i][...] for i in range(npar)]
        res = fn(*r, *p)
        for o_ref, o in zip(refs[nr + npar:], res):
            o_ref[...] = o.astype(o_ref.dtype)

    in_specs = [pl.BlockSpec((tile, r.shape[1]), lambda i: (i, 0)) for r in rows]
    in_specs += [pl.BlockSpec(p.shape, lambda i, nd=p.ndim: (0,) * nd) for p in pars]
    return pl.pallas_call(
        body, name=name, grid=(T // tile,), in_specs=in_specs,
        out_specs=[pl.BlockSpec((tile, w), lambda i: (i, 0)) for w, _ in outs],
        out_shape=[jax.ShapeDtypeStruct((T, w), dt) for w, dt in outs],
        compiler_params=_cparams(("parallel",)),
    )(*rows, *pars)


def _rows_vjp(fn, rows, pars, cts, *, dtypes, adds=None, name):
    adds = adds or {}
    cts = [c if isinstance(c, (tuple, list)) else (c,) for c in cts]
    flat_cts = [a for c in cts for a in c]
    add_keys = sorted(adds)
    add_arrs = [adds[k] for k in add_keys]
    want = [i for i, d in enumerate(dtypes) if d is not None]
    T = rows[0].shape[0]
    tile = _row_tile([(a.shape[1], a.dtype) for a in list(rows) + flat_cts + add_arrs]
                     + [(rows[i].shape[1], dtypes[i]) for i in want])
    nr, npar, nc, na = len(rows), len(pars), len(flat_cts), len(add_arrs)

    def body(*refs):
        r = [refs[i][...].astype(F32) for i in range(nr)]
        p = [refs[nr + i][...] for i in range(npar)]
        cvals = [refs[nr + npar + i][...].astype(F32) for i in range(nc)]
        avals = [refs[nr + npar + nc + i][...].astype(F32) for i in range(na)]
        outs = refs[nr + npar + nc + na:]
        ct, pos = [], 0
        for c in cts:
            s = cvals[pos]
            for extra in cvals[pos + 1:pos + len(c)]:
                s = s + extra
            pos += len(c)
            ct.append(s)
        _, vjp = jax.vjp(lambda *a: tuple(fn(*a)), *r, *p)
        g = vjp(tuple(ct))
        for o_ref, i in zip(outs[:len(want)], want):
            gi = g[i]
            if i in adds:
                gi = gi + avals[add_keys.index(i)]
            o_ref[...] = gi.astype(o_ref.dtype)
        first = pl.program_id(0) == 0
        for o_ref, gp in zip(outs[len(want):], g[nr:]):
            @pl.when(first)
            def _(o_ref=o_ref, gp=gp):
                o_ref[...] = gp

            @pl.when(jnp.logical_not(first))
            def _(o_ref=o_ref, gp=gp):
                o_ref[...] += gp

    row_spec = lambda a: pl.BlockSpec((tile, a.shape[1]), lambda i: (i, 0))
    par_spec = lambda a: pl.BlockSpec(a.shape, lambda i, nd=a.ndim: (0,) * nd)
    res = pl.pallas_call(
        body, name=name, grid=(T // tile,),
        in_specs=[row_spec(a) for a in rows] + [par_spec(a) for a in pars] + [row_spec(a) for a in flat_cts + add_arrs],
        out_specs=[row_spec(rows[i]) for i in want] + [par_spec(a) for a in pars],
        out_shape=[jax.ShapeDtypeStruct(rows[i].shape, dtypes[i]) for i in want]
        + [jax.ShapeDtypeStruct(a.shape, F32) for a in pars],
        compiler_params=_cparams(("arbitrary",)),
    )(*rows, *pars, *flat_cts, *add_arrs)
    return res[:len(want)], res[len(want):]


def _small(fn, args, outs, *, name):
    n = len(args)

    def body(*refs):
        res = fn(*[r[...] for r in refs[:n]])
        for o_ref, o in zip(refs[n:], res):
            o_ref[...] = o

    return pl.pallas_call(body, name=name, out_shape=[jax.ShapeDtypeStruct(s, F32) for s in outs],
                          compiler_params=pltpu.CompilerParams(vmem_limit_bytes=VMEM_LIMIT))(*args)


def _small_vjp(fn, args, cts, *, name):
    n, nc = len(args), len(cts)

    def body(*refs):
        _, vjp = jax.vjp(lambda *a: tuple(fn(*a)), *[r[...] for r in refs[:n]])
        g = vjp(tuple(r[...] for r in refs[n:n + nc]))
        for o_ref, gi in zip(refs[n + nc:], g):
            o_ref[...] = gi

    return pl.pallas_call(body, name=name, out_shape=[jax.ShapeDtypeStruct(a.shape, F32) for a in args],
                          compiler_params=pltpu.CompilerParams(vmem_limit_bytes=VMEM_LIMIT))(*args, *cts)


def _rms(x, g):
    return x * lax.rsqrt(jnp.mean(x * x, axis=-1, keepdims=True) + EPS) * g


def _layernorm(x, g, b):
    mu = jnp.mean(x, axis=-1, keepdims=True)
    var = jnp.mean(jnp.square(x - mu), axis=-1, keepdims=True)
    return (x - mu) * lax.rsqrt(var + EPS) * g + b


def _f_pre(x, g):
    return (_rms(x.astype(F32), g),)


def _f_post_term(scale, has_bias):
    def fn(o, g, *b):
        o = o.astype(F32)
        if has_bias:
            o = o + b[0]
        return (scale * _rms(o, g),)
    return fn


def _f_post(scale, has_bias):
    term = _f_post_term(scale, has_bias)

    def fn(x, o, g, *b):
        return (x + term(o, g, *b)[0],)
    return fn


def _f_s5_gelu(ylin, u, d):
    return (jax.nn.gelu(ylin.astype(F32) + d * u.astype(F32)),)


def _f_s5_glu(y, gl, b):
    return (y.astype(F32) * jax.nn.sigmoid(gl.astype(F32) + b),)


def _f_cv_glu(z0, b):
    z = z0.astype(F32) + b
    return (z[:, :D_MODEL] * jax.nn.sigmoid(z[:, D_MODEL:]),)


def _f_cv_ln(zc, g, b):
    return (jax.nn.silu(_layernorm(zc.astype(F32), g, b)),)


def _f_gm_in(z0, b, g, bl):
    z = jax.nn.gelu(z0.astype(F32) + b)
    return z[:, :GM_E], _layernorm(z[:, GM_E:], g, bl)


def _f_at_combine(o0, o1, o2, l0, l1, l2):
    m = jnp.maximum(jnp.maximum(l0, l1), l2)
    e0, e1, e2 = jnp.exp(l0 - m), jnp.exp(l1 - m), jnp.exp(l2 - m)
    return ((e0 * o0 + e1 * o1 + e2 * o2) / (e0 + e1 + e2),)


def _f_s5_disc(ar, ai, ldt, br, bi):
    dt = jnp.exp(ldt)
    mag = jnp.exp(dt * ar)
    abr = mag * jnp.cos(dt * ai)
    abi = mag * jnp.sin(dt * ai)
    den = ar * ar + ai * ai
    nr = abr - 1.0
    f_re = (nr * ar + abi * ai) / den
    f_im = (abi * ar - nr * ai) / den
    return abr, abi, f_re * br - f_im * bi, f_re * bi + f_im * br


def _loss_call(y, tgt):
    T, D = y.shape
    tile = 256

    def body(y_ref, t_ref, dy_ref, l_ref):
        err = y_ref[...] - t_ref[...]
        dy_ref[...] = err * (1.0 / D)
        part = 0.5 * jnp.sum(jnp.mean(err * err, axis=-1, keepdims=True), axis=0, keepdims=True)
        part = jnp.broadcast_to(part, (1, LANES))
        first = pl.program_id(0) == 0

        @pl.when(first)
        def _():
            l_ref[...] = part

        @pl.when(jnp.logical_not(first))
        def _():
            l_ref[...] += part

    return pl.pallas_call(
        body, name="loss", grid=(T // tile,),
        in_specs=[pl.BlockSpec((tile, D), lambda i: (i, 0))] * 2,
        out_specs=[pl.BlockSpec((tile, D), lambda i: (i, 0)), pl.BlockSpec((1, LANES), lambda i: (0, 0))],
        out_shape=[jax.ShapeDtypeStruct((T, D), F32), jax.ShapeDtypeStruct((1, LANES), F32)],
        compiler_params=_cparams(("arbitrary",)),
    )(y, tgt)


def _bd(xs, ws, *, add=None, out_dtype=F32, name):
    T = xs[0].shape[0]
    nb, kw, nw = ws[0].shape
    tm = 256
    n = len(xs)

    def body(*refs):
        o_ref = refs[-1]
        for j in range(nb):
            acc = None
            for x_ref, w_ref in zip(refs[:n], refs[n:2 * n]):
                p = jnp.dot(x_ref[:, j * kw:(j + 1) * kw].astype(BF16), w_ref[j].astype(BF16),
                            preferred_element_type=F32)
                acc = p if acc is None else acc + p
            if add is not None:
                acc = acc + refs[2 * n][:, j * nw:(j + 1) * nw].astype(F32)
            o_ref[:, j * nw:(j + 1) * nw] = acc.astype(o_ref.dtype)

    in_specs = [pl.BlockSpec((tm, nb * kw), lambda i: (i, 0)) for _ in xs]
    in_specs += [pl.BlockSpec((nb, kw, nw), lambda i: (0, 0, 0)) for _ in ws]
    args = list(xs) + list(ws)
    if add is not None:
        in_specs.append(pl.BlockSpec((tm, nb * nw), lambda i: (i, 0)))
        args.append(add)
    return pl.pallas_call(
        body, name=name, grid=(T // tm,), in_specs=in_specs,
        out_specs=pl.BlockSpec((tm, nb * nw), lambda i: (i, 0)),
        out_shape=jax.ShapeDtypeStruct((T, nb * nw), out_dtype),
        compiler_params=_cparams(("parallel",)),
    )(*args)


def _bd_wgrad(x, dy, kw, nw, *, name):
    T = x.shape[0]
    nb = x.shape[1] // kw
    tk = 512

    def body(x_ref, dy_ref, o_ref):
        first = pl.program_id(0) == 0
        for j in range(nb):
            p = lax.dot_general(x_ref[:, j * kw:(j + 1) * kw].astype(BF16), dy_ref[:, j * nw:(j + 1) * nw].astype(BF16),
                                (((0,), (0,)), ((), ())), preferred_element_type=F32)

            @pl.when(first)
            def _(j=j, p=p):
                o_ref[j] = p

            @pl.when(jnp.logical_not(first))
            def _(j=j, p=p):
                o_ref[j] += p

    return pl.pallas_call(
        body, name=name, grid=(T // tk,),
        in_specs=[pl.BlockSpec((tk, nb * kw), lambda k: (k, 0)), pl.BlockSpec((tk, nb * nw), lambda k: (k, 0))],
        out_specs=pl.BlockSpec((nb, kw, nw), lambda k: (0, 0, 0)),
        out_shape=jax.ShapeDtypeStruct((nb, kw, nw), F32),
        compiler_params=_cparams(("arbitrary",)),
    )(x, dy)


SCAN_COLS = min(S5_NS, 4096)
SCAN_ROWS = 128


def _scan_fwd(bur, bui, ar, ai):
    T, NS = bur.shape
    cw, tc = SCAN_COLS, SCAN_ROWS

    def body(bur_ref, bui_ref, ar_ref, ai_ref, sr_ref, si_ref, cr, ci):
        @pl.when(pl.program_id(1) == 0)
        def _():
            cr[...] = jnp.zeros_like(cr)
            ci[...] = jnp.zeros_like(ci)

        a_r, a_i = ar_ref[...], ai_ref[...]

        def step8(t8, carry):
            sr, si = carry
            base = pl.multiple_of(t8 * 8, 8)
            for r in range(8):
                br = bur_ref[pl.ds(base + r, 1), :]
                bi = bui_ref[pl.ds(base + r, 1), :]
                sr, si = a_r * sr - a_i * si + br, a_r * si + a_i * sr + bi
                sr_ref[pl.ds(base + r, 1), :] = sr
                si_ref[pl.ds(base + r, 1), :] = si
            return sr, si

        sr, si = lax.fori_loop(0, tc // 8, step8, (cr[...], ci[...]))
        cr[...] = sr
        ci[...] = si

    blk = pl.BlockSpec((tc, cw), lambda c, t: (t, c))
    vec = pl.BlockSpec((1, cw), lambda c, t: (0, c))
    return pl.pallas_call(
        body, name="s5_scan_fwd", grid=(NS // cw, T // tc), in_specs=[blk, blk, vec, vec], out_specs=[blk, blk],
        out_shape=[jax.ShapeDtypeStruct((T, NS), F32)] * 2,
        scratch_shapes=[pltpu.VMEM((1, cw), F32)] * 2,
        compiler_params=_cparams(("parallel", "arbitrary")),
    )(bur, bui, ar, ai)


def _scan_bwd(gr, gi, sr, si, ar, ai):
    T, NS = gr.shape
    cw, tc = SCAN_COLS, SCAN_ROWS
    nt = T // tc

    def body(gr_ref, gi_ref, sr_ref, si_ref, ar_ref, ai_ref, lr_ref, li_ref, dar_ref, dai_ref, cr, ci):
        @pl.when(pl.program_id(1) == 0)
        def _():
            cr[...] = jnp.zeros_like(cr)
            ci[...] = jnp.zeros_like(ci)
            dar_ref[...] = jnp.zeros_like(dar_ref)
            dai_ref[...] = jnp.zeros_like(dai_ref)

        a_r, a_i = ar_ref[...], ai_ref[...]

        def step8(k, carry):
            lr, li, dar, dai = carry
            base = pl.multiple_of((tc // 8 - 1 - k) * 8, 8)
            for r in range(7, -1, -1):
                s_r = sr_ref[pl.ds(base + r, 1), :]
                s_i = si_ref[pl.ds(base + r, 1), :]
                dar = dar + lr * s_r + li * s_i
                dai = dai + li * s_r - lr * s_i
                g_r = gr_ref[pl.ds(base + r, 1), :]
                g_i = gi_ref[pl.ds(base + r, 1), :]
                lr, li = g_r + a_r * lr + a_i * li, g_i + a_r * li - a_i * lr
                lr_ref[pl.ds(base + r, 1), :] = lr
                li_ref[pl.ds(base + r, 1), :] = li
            return lr, li, dar, dai

        lr, li, dar, dai = lax.fori_loop(0, tc // 8, step8, (cr[...], ci[...], dar_ref[...], dai_ref[...]))
        cr[...] = lr
        ci[...] = li
        dar_ref[...] = dar
        dai_ref[...] = dai

    blk = pl.BlockSpec((tc, cw), lambda c, t: (nt - 1 - t, c))
    vec = pl.BlockSpec((1, cw), lambda c, t: (0, c))
    return pl.pallas_call(
        body, name="s5_scan_bwd", grid=(NS // cw, nt), in_specs=[blk, blk, blk, blk, vec, vec],
        out_specs=[blk, blk, vec, vec],
        out_shape=[jax.ShapeDtypeStruct((T, NS), F32)] * 2 + [jax.ShapeDtypeStruct((1, NS), F32)] * 2,
        scratch_shapes=[pltpu.VMEM((1, cw), F32)] * 2,
        compiler_params=_cparams(("parallel", "arbitrary")),
    )(gr, gi, sr, si, ar, ai)


CONV_ROWS = 256
CONV_HALO = 32
CONV_PAD = CONV_HALO - (CONV_W - 1)
CONV_SUB = 16


def _conv_shifts(ext, sh, n):
    ext[pl.ds(n, 8), :] = jnp.zeros((8, ext.shape[1]), F32)
    for s in range(8):
        sh[s] = ext[pl.ds(s, n), :]


def _conv_rows(sh, start):
    return sh[start % 8, pl.ds(start - start % 8, CONV_SUB), :]


def _conv_fwd(z, dw, dwb):
    T, D = z.shape
    tc, hl = CONV_ROWS, CONV_HALO
    per = tc // hl

    def body(z_ref, zp_ref, dw_ref, b_ref, o_ref, ext, sh):
        i = pl.program_id(0)
        ext[pl.ds(0, hl), :] = jnp.where(i > 0, zp_ref[...], 0.0)
        ext[pl.ds(hl, tc), :] = z_ref[...]
        _conv_shifts(ext, sh, tc + hl)
        for rb in range(tc // CONV_SUB):
            r0 = rb * CONV_SUB
            acc = jnp.zeros((CONV_SUB, D), F32) + b_ref[...]
            for k in range(CONV_W):
                acc = acc + dw_ref[pl.ds(k, 1), :] * _conv_rows(sh, r0 + CONV_PAD + k)
            o_ref[pl.ds(r0, CONV_SUB), :] = acc

    return pl.pallas_call(
        body, name="conv_fwd", grid=(T // tc,),
        in_specs=[pl.BlockSpec((tc, D), lambda i: (i, 0)),
                  pl.BlockSpec((hl, D), lambda i: (jnp.maximum(i * per - 1, 0), 0)),
                  pl.BlockSpec((hl, D), lambda i: (0, 0)), pl.BlockSpec((1, D), lambda i: (0, 0))],
        out_specs=pl.BlockSpec((tc, D), lambda i: (i, 0)),
        out_shape=jax.ShapeDtypeStruct((T, D), F32),
        scratch_shapes=[pltpu.VMEM((tc + hl + 8, D), F32), pltpu.VMEM((8, tc + hl, D), F32)],
        compiler_params=_cparams(("parallel",)),
    )(z, z, dw, dwb)


def _conv_bwd(dout, z, dw):
    T, D = z.shape
    tc, hl = CONV_ROWS, CONV_HALO
    per = tc // hl
    nblk = T // tc

    def body(g_ref, gn_ref, z_ref, zp_ref, dw_ref, dz_ref, ddw_ref, db_ref, gext, zext, gsh, zsh, acc8):
        i = pl.program_id(0)
        gext[pl.ds(0, tc), :] = g_ref[...]
        gext[pl.ds(tc, hl), :] = jnp.where(i < nblk - 1, gn_ref[...], 0.0)
        zext[pl.ds(0, hl), :] = jnp.where(i > 0, zp_ref[...], 0.0)
        zext[pl.ds(hl, tc), :] = z_ref[...]
        _conv_shifts(gext, gsh, tc + hl)
        _conv_shifts(zext, zsh, tc + hl)
        for rb in range(tc // CONV_SUB):
            r0 = rb * CONV_SUB
            acc = jnp.zeros((CONV_SUB, D), F32)
            for k in range(CONV_W):
                acc = acc + dw_ref[pl.ds(k, 1), :] * _conv_rows(gsh, r0 + CONV_W - 1 - k)
            dz_ref[pl.ds(r0, CONV_SUB), :] = acc

        @pl.when(i == 0)
        def _():
            acc8[...] = jnp.zeros_like(acc8)
            db_ref[...] = jnp.zeros_like(db_ref)

        db_ref[...] += jnp.sum(g_ref[...], axis=0, keepdims=True)
        for k in range(CONV_W):
            part = jnp.zeros((8, D), F32)
            for rb in range(tc // CONV_SUB):
                r0 = rb * CONV_SUB
                prod = g_ref[pl.ds(r0, CONV_SUB), :] * _conv_rows(zsh, r0 + CONV_PAD + k)
                for s in range(CONV_SUB // 8):
                    part = part + prod[s * 8:(s + 1) * 8]
            acc8[k] += part

        @pl.when(i == nblk - 1)
        def _():
            ddw_ref[...] = jnp.sum(acc8[...], axis=1)

    return pl.pallas_call(
        body, name="conv_bwd", grid=(nblk,),
        in_specs=[pl.BlockSpec((tc, D), lambda i: (i, 0)),
                  pl.BlockSpec((hl, D), lambda i: (jnp.minimum((i + 1) * per, nblk * per - 1), 0)),
                  pl.BlockSpec((tc, D), lambda i: (i, 0)),
                  pl.BlockSpec((hl, D), lambda i: (jnp.maximum(i * per - 1, 0), 0)),
                  pl.BlockSpec((hl, D), lambda i: (0, 0))],
        out_specs=[pl.BlockSpec((tc, D), lambda i: (i, 0)), pl.BlockSpec((hl, D), lambda i: (0, 0)),
                   pl.BlockSpec((1, D), lambda i: (0, 0))],
        out_shape=[jax.ShapeDtypeStruct((T, D), F32), jax.ShapeDtypeStruct((hl, D), F32),
                   jax.ShapeDtypeStruct((1, D), F32)],
        scratch_shapes=[pltpu.VMEM((tc + hl + 8, D), F32)] * 2 + [pltpu.VMEM((8, tc + hl, D), F32)] * 2
        + [pltpu.VMEM((hl, 8, D), F32)],
        compiler_params=_cparams(("arbitrary",)),
    )(dout, dout, z, z, dw)


def _gm_causal():
    r = lax.broadcasted_iota(jnp.int32, (GM_CHUNK, GM_CHUNK), 0)
    c = lax.broadcasted_iota(jnp.int32, (GM_CHUNK, GM_CHUNK), 1)
    return r >= c


def _gm_sg_fwd(u, v, ws, bs_col):
    T, E = u.shape
    hw = E // GM_HEADS

    def body(u_ref, v_ref, w_ref, b_ref, o_ref):
        causal = _gm_causal()
        for h in range(GM_HEADS):
            cols = slice(h * hw, (h + 1) * hw)
            w = jnp.where(causal, w_ref[h], 0.0).astype(BF16)
            s = jnp.dot(w, v_ref[:, cols], preferred_element_type=F32) + b_ref[h]
            o_ref[:, cols] = (u_ref[:, cols] * s).astype(o_ref.dtype)

    return pl.pallas_call(
        body, name="gm_sg_fwd", grid=(T // GM_CHUNK,),
        in_specs=[pl.BlockSpec((GM_CHUNK, E), lambda i: (i, 0)), pl.BlockSpec((GM_CHUNK, E), lambda i: (i, 0)),
                  pl.BlockSpec(ws.shape, lambda i: (0, 0, 0)), pl.BlockSpec(bs_col.shape, lambda i: (0, 0, 0))],
        out_specs=pl.BlockSpec((GM_CHUNK, E), lambda i: (i, 0)),
        out_shape=jax.ShapeDtypeStruct((T, E), BF16),
        compiler_params=_cparams(("parallel",)),
    )(u, v, ws, bs_col)


def _gm_sg_bwd(dus, u, v, ws, bs_col):
    T, E = u.shape
    hw = E // GM_HEADS

    def body(g_ref, u_ref, v_ref, w_ref, b_ref, du_ref, dv_ref, dw_ref, db_ref):
        causal = _gm_causal()

        @pl.when(pl.program_id(0) == 0)
        def _():
            dw_ref[...] = jnp.zeros_like(dw_ref)
            db_ref[...] = jnp.zeros_like(db_ref)

        for h in range(GM_HEADS):
            cols = slice(h * hw, (h + 1) * hw)
            w = jnp.where(causal, w_ref[h], 0.0).astype(BF16)
            vh = v_ref[:, cols]
            s = jnp.dot(w, vh, preferred_element_type=F32) + b_ref[h]
            g = g_ref[:, cols]
            du_ref[:, cols] = g * s
            ds = g * u_ref[:, cols]
            dsb = ds.astype(BF16)
            dv_ref[:, cols] = lax.dot_general(w, dsb, (((0,), (0,)), ((), ())), preferred_element_type=F32)
            dwh = lax.dot_general(dsb, vh, (((1,), (1,)), ((), ())), preferred_element_type=F32)
            dw_ref[h] += jnp.where(causal, dwh, 0.0)
            db_ref[h] += jnp.broadcast_to(jnp.sum(ds, axis=1, keepdims=True), (GM_CHUNK, LANES))

    blk = pl.BlockSpec((GM_CHUNK, E), lambda i: (i, 0))
    return pl.pallas_call(
        body, name="gm_sg_bwd", grid=(T // GM_CHUNK,),
        in_specs=[blk, blk, blk, pl.BlockSpec(ws.shape, lambda i: (0, 0, 0)),
                  pl.BlockSpec(bs_col.shape, lambda i: (0, 0, 0))],
        out_specs=[blk, blk, pl.BlockSpec(ws.shape, lambda i: (0, 0, 0)),
                   pl.BlockSpec((GM_HEADS, GM_CHUNK, LANES), lambda i: (0, 0, 0))],
        out_shape=[jax.ShapeDtypeStruct((T, E), F32), jax.ShapeDtypeStruct((T, E), F32),
                   jax.ShapeDtypeStruct(ws.shape, F32), jax.ShapeDtypeStruct((GM_HEADS, GM_CHUNK, LANES), F32)],
        compiler_params=_cparams(("arbitrary",)),
    )(dus, u, v, ws, bs_col)


def _t5_bucket_steps(dilation):
    max_exact = NUM_BUCKETS // 2
    delta = np.arange(BLOCK + 1)
    dist = delta * dilation
    distf = np.maximum(dist, 1).astype(np.float32)
    large = max_exact + (np.log(distf / np.float32(max_exact)) / np.float32(math.log(MAX_DISTANCE / max_exact))
                         * np.float32(NUM_BUCKETS - max_exact)).astype(np.int32)
    large = np.minimum(large, NUM_BUCKETS - 1)
    bucket = np.where(dist < max_exact, dist, large)
    steps = []
    for d in range(1, BLOCK + 1):
        inc = int(bucket[d] - bucket[d - 1])
        assert inc >= 0
        if inc:
            steps.append((d, inc))
    assert int(bucket[0]) == 0
    return steps


def _bucket_map(dilation):
    qi = lax.broadcasted_iota(jnp.int32, (BLOCK, 2 * BLOCK), 0)
    ki = lax.broadcasted_iota(jnp.int32, (BLOCK, 2 * BLOCK), 1)
    delta = qi + BLOCK - ki
    bm = jnp.zeros((BLOCK, 2 * BLOCK), jnp.int32)
    for thr, inc in _t5_bucket_steps(dilation):
        bm = bm + jnp.where(delta >= thr, inc, 0)
    return bm


def _at_bias(table, g, dilation):
    H = AT_HEADS

    def body(t_ref, o_ref):
        bm = _bucket_map(dilation)
        for h in range(H):
            acc = jnp.zeros((BLOCK, 2 * BLOCK), F32)
            for b in range(NUM_BUCKETS):
                acc = jnp.where(bm == b, t_ref[b, g * H + h], acc)
            o_ref[h] = acc

    return pl.pallas_call(body, name="at_bias", in_specs=[pl.BlockSpec(memory_space=pltpu.SMEM)],
                          out_shape=jax.ShapeDtypeStruct((H, BLOCK, 2 * BLOCK), F32))(table)


def _at_bias_bwd(dbias, dilation):
    H = AT_HEADS

    def body(d_ref, o_ref):
        bm = _bucket_map(dilation)
        for h in range(H):
            d = d_ref[h]
            for b in range(NUM_BUCKETS):
                o_ref[b, h] = jnp.sum(jnp.where(bm == b, d, 0.0))

    return pl.pallas_call(body, name="at_bias_bwd", out_specs=pl.BlockSpec(memory_space=pltpu.SMEM),
                          out_shape=jax.ShapeDtypeStruct((NUM_BUCKETS, H), F32))(dbias)


def _at_mask(i, nbs):
    qi = lax.broadcasted_iota(jnp.int32, (BLOCK, 2 * BLOCK), 0)
    ki = lax.broadcasted_iota(jnp.int32, (BLOCK, 2 * BLOCK), 1)
    no_prev = jnp.where(i % nbs == 0, 4 * BLOCK, 0)
    return ((ki < BLOCK) & (ki >= qi + no_prev)) | ((ki >= BLOCK) & (ki - BLOCK <= qi))


def _head_lanes():
    lane = lax.broadcasted_iota(jnp.int32, (BLOCK, LANES), 1)
    return [lane < HEAD_DIM, lane >= HEAD_DIM]


def _at_fwd(qkv, bias, nbs, cb):
    T = qkv.shape[0]
    D = D_MODEL
    npair = D // LANES
    scale = HEAD_DIM ** -0.5

    def body(q_ref, kc_ref, kp_ref, vc_ref, vp_ref, b_ref, o_ref, l_ref):
        i = pl.program_id(0)
        mask = _at_mask(i, nbs)
        sel = _head_lanes()
        for j in range(npair):
            cols = slice(j * LANES, (j + 1) * LANES)
            q = q_ref[:, cols]
            kk = jnp.concatenate([kp_ref[:, cols], kc_ref[:, cols]], axis=0)
            vv = jnp.concatenate([vp_ref[:, cols], vc_ref[:, cols]], axis=0)
            o_pair = jnp.zeros((BLOCK, LANES), F32)
            l_pair = jnp.zeros((BLOCK, LANES), F32)
            for e in range(2):
                qh = jnp.where(sel[e], q, jnp.zeros_like(q))
                s = lax.dot_general(qh, kk, (((1,), (1,)), ((), ())), preferred_element_type=F32) * scale
                s = jnp.where(mask, s + b_ref[2 * j + e], -1e30)
                m = jnp.max(s, axis=1, keepdims=True)
                p = jnp.exp(s - m)
                den = jnp.sum(p, axis=1, keepdims=True)
                o = jnp.dot(p.astype(BF16), vv, preferred_element_type=F32) / den
                o_pair = jnp.where(sel[e], o, o_pair)
                l_pair = jnp.where(sel[e], m + jnp.log(den), l_pair)
            o_ref[:, cols] = o_pair
            l_ref[:, cols] = l_pair

    blk = lambda c, prev: pl.BlockSpec((BLOCK, D), (lambda i: (jnp.maximum(i - 1, 0), cb + c)) if prev
                                       else (lambda i: (i, cb + c)))
    out = pl.BlockSpec((BLOCK, D), lambda i: (i, 0))
    return pl.pallas_call(
        body, name="at_fwd", grid=(T // BLOCK,),
        in_specs=[blk(0, False), blk(1, False), blk(1, True), blk(2, False), blk(2, True),
                  pl.BlockSpec(bias.shape, lambda i: (0, 0, 0))],
        out_specs=[out, out], out_shape=[jax.ShapeDtypeStruct((T, D), F32)] * 2,
        compiler_params=_cparams(("parallel",)),
    )(qkv, qkv, qkv, qkv, qkv, bias)


def _at_bwd(qkv, bias, o, lse, do, dlse, nbs, cb):
    T = qkv.shape[0]
    D = D_MODEL
    nblk = T // BLOCK
    npair = D // LANES
    scale = HEAD_DIM ** -0.5

    def body(q_ref, kc_ref, kp_ref, vc_ref, vp_ref, b_ref, o_ref, l_ref, do_ref, dl_ref, dqkv_ref, db_ref, carry):
        i = pl.program_id(0)

        @pl.when(i == 0)
        def _():
            carry[...] = jnp.zeros_like(carry)
            db_ref[...] = jnp.zeros_like(db_ref)

        @pl.when(i == nblk)
        def _():
            dqkv_ref[...] = carry[...].astype(dqkv_ref.dtype)

        @pl.when(i < nblk)
        def _():
            mask = _at_mask(i, nbs)
            sel = _head_lanes()
            for j in range(npair):
                cols = slice(j * LANES, (j + 1) * LANES)
                kcols = slice(D + j * LANES, D + (j + 1) * LANES)
                vcols = slice(2 * D + j * LANES, 2 * D + (j + 1) * LANES)
                q = q_ref[:, cols]
                kk = jnp.concatenate([kp_ref[:, cols], kc_ref[:, cols]], axis=0)
                vv = jnp.concatenate([vp_ref[:, cols], vc_ref[:, cols]], axis=0)
                dov = do_ref[:, cols]
                dob = dov.astype(BF16)
                oo = dov * o_ref[:, cols]
                lv = l_ref[:, cols]
                dlv = dl_ref[:, cols]
                sel2 = [jnp.concatenate([s_, s_], axis=0) for s_ in sel]
                qhs, dohs, kkhs, dsbs, pbs = [], [], [], [], []
                for e in range(2):
                    qh = jnp.where(sel[e], q, jnp.zeros_like(q))
                    s = lax.dot_general(qh, kk, (((1,), (1,)), ((), ())), preferred_element_type=F32) * scale
                    s = jnp.where(mask, s + b_ref[2 * j + e], -1e30)
                    lse_h = jnp.max(jnp.where(sel[e], lv, -jnp.inf), axis=1, keepdims=True)
                    p = jnp.exp(s - lse_h)
                    doh = jnp.where(sel[e], dob, jnp.zeros_like(dob))
                    dp = lax.dot_general(doh, vv, (((1,), (1,)), ((), ())), preferred_element_type=F32)
                    delta = jnp.sum(jnp.where(sel[e], oo, 0.0), axis=1, keepdims=True)
                    dlse_h = jnp.sum(jnp.where(sel[e], dlv, 0.0), axis=1, keepdims=True)
                    ds = p * (dp - delta + dlse_h)
                    db_ref[2 * j + e] += ds
                    qhs.append(qh)
                    dohs.append(doh)
                    kkhs.append(jnp.where(sel2[e], kk, jnp.zeros_like(kk)))
                    dsbs.append((ds * scale).astype(BF16))
                    pbs.append(p.astype(BF16))
                tn = (((0,), (0,)), ((), ()))
                dq_pair = jnp.dot(jnp.concatenate(dsbs, axis=1), jnp.concatenate(kkhs, axis=0), preferred_element_type=F32)
                dk_pair = lax.dot_general(jnp.concatenate(dsbs, axis=0), jnp.concatenate(qhs, axis=0), tn,
                                          preferred_element_type=F32)
                dv_pair = lax.dot_general(jnp.concatenate(pbs, axis=0), jnp.concatenate(dohs, axis=0), tn,
                                          preferred_element_type=F32)
                dqkv_ref[:, cols] = carry[:, cols].astype(dqkv_ref.dtype)
                dqkv_ref[:, kcols] = (carry[:, kcols] + dk_pair[:BLOCK]).astype(dqkv_ref.dtype)
                dqkv_ref[:, vcols] = (carry[:, vcols] + dv_pair[:BLOCK]).astype(dqkv_ref.dtype)
                carry[:, cols] = dq_pair
                carry[:, kcols] = dk_pair[BLOCK:]
                carry[:, vcols] = dv_pair[BLOCK:]

    cur = lambda i: jnp.minimum(i, nblk - 1)
    prev = lambda i: jnp.maximum(jnp.minimum(i, nblk - 1) - 1, 0)
    blk = lambda c, pv: pl.BlockSpec((BLOCK, D), (lambda i: (prev(i), cb + c)) if pv else (lambda i: (cur(i), cb + c)))
    row = pl.BlockSpec((BLOCK, D), lambda i: (cur(i), 0))
    return pl.pallas_call(
        body, name="at_bwd", grid=(nblk + 1,),
        in_specs=[blk(0, False), blk(1, False), blk(1, True), blk(2, False), blk(2, True),
                  pl.BlockSpec(bias.shape, lambda i: (0, 0, 0)), row, row, row, row],
        out_specs=[pl.BlockSpec((BLOCK, 3 * D), lambda i: (jnp.maximum(i - 1, 0), 0)),
                   pl.BlockSpec(bias.shape, lambda i: (0, 0, 0))],
        out_shape=[jax.ShapeDtypeStruct((T, 3 * D), BF16), jax.ShapeDtypeStruct(bias.shape, F32)],
        scratch_shapes=[pltpu.VMEM((BLOCK, 3 * D), F32)],
        compiler_params=_cparams(("arbitrary",)),
    )(qkv, qkv, qkv, qkv, qkv, bias, o, lse, do, dlse)


def _to_residue_major(a, d):
    if d == 1:
        return a
    T, C = a.shape
    return a.reshape(T // d, d, C).transpose(1, 0, 2).reshape(T, C)


def _from_residue_major(a, d):
    if d == 1:
        return a
    T, C = a.shape
    return a.reshape(d, T // d, C).transpose(1, 0, 2).reshape(T, C)


FFN_TM = 512


def _ffn_up(h, w1t, w3t, carry=None):
    T, Dm = h.shape
    Fw = w1t.shape[0]
    tm, tn = FFN_TM, _pick(Fw, 1408)
    grid = (Fw // tn, T // tm)
    nt = (((1,), (1,)), ((), ()))

    def body(*refs):
        (h_ref, w1_ref, w3_ref), (a_ref, b_ref, u_ref), _, begin, end = _carry_hooks(carry, refs, 3, 3, 2, grid)
        begin()
        hv = h_ref[...]
        a = lax.dot_general(hv, w1_ref[...], nt, preferred_element_type=F32)
        b = lax.dot_general(hv, w3_ref[...], nt, preferred_element_type=F32)
        a_ref[...] = a.astype(a_ref.dtype)
        b_ref[...] = b.astype(b_ref.dtype)
        u_ref[...] = (jax.nn.silu(a) * b).astype(u_ref.dtype)
        end()

    extra = carry if carry is not None else _NO_CARRY
    wspec = pl.BlockSpec((tn, Dm), lambda j, i: (j, 0))
    ospec = pl.BlockSpec((tm, tn), lambda j, i: (i, j))
    res = pl.pallas_call(
        body, name="ffn_up", grid=grid,
        in_specs=[pl.BlockSpec((tm, Dm), lambda j, i: (i, 0)), wspec, wspec] + extra.in_specs,
        out_specs=[ospec] * 3 + extra.out_specs,
        out_shape=[jax.ShapeDtypeStruct((T, Fw), BF16)] * 3 + extra.out_shape, scratch_shapes=extra.scratch,
        compiler_params=_cparams(("arbitrary",) * 2 if carry is not None else ("parallel", "parallel")),
    )(h, w1t, w3t, *extra.arrays)
    return res[:3], res[3:]


def _ffn_down_dx(do, w2, a, b, carry=None):
    T, Dm = do.shape
    Fw = w2.shape[0]
    tm, tn = FFN_TM, _pick(Fw, 1408)
    grid = (Fw // tn, T // tm)

    def body(*refs):
        (do_ref, w2_ref, a_ref, b_ref), (da_ref, db_ref), _, begin, end = _carry_hooks(carry, refs, 4, 2, 2, grid)
        begin()
        du = lax.dot_general(do_ref[...], w2_ref[...], (((1,), (1,)), ((), ())), preferred_element_type=F32)
        av = a_ref[...].astype(F32)
        bv = b_ref[...].astype(F32)
        sg = jax.nn.sigmoid(av)
        silu = av * sg
        da_ref[...] = (du * bv * (sg + silu * (1.0 - sg))).astype(da_ref.dtype)
        db_ref[...] = (du * silu).astype(db_ref.dtype)
        end()

    extra = carry if carry is not None else _NO_CARRY
    ospec = pl.BlockSpec((tm, tn), lambda j, i: (i, j))
    res = pl.pallas_call(
        body, name="ffn_down_dx", grid=grid,
        in_specs=[pl.BlockSpec((tm, Dm), lambda j, i: (i, 0)), pl.BlockSpec((tn, Dm), lambda j, i: (j, 0)), ospec, ospec]
        + extra.in_specs,
        out_specs=[ospec] * 2 + extra.out_specs,
        out_shape=[jax.ShapeDtypeStruct((T, Fw), BF16)] * 2 + extra.out_shape, scratch_shapes=extra.scratch,
        compiler_params=_cparams(("arbitrary",) * 2 if carry is not None else ("parallel", "parallel")),
    )(do, w2, a, b, *extra.arrays)
    return res[:2], res[2:]


def _pre_norm(x, g):
    return _rows(_f_pre, [x], [g], [(D_MODEL, BF16)], name="pre_norm")[0]


def _close_part(x, o, bias, g_post, g_pre_next, scale, name):
    extra = [] if bias is None else [bias]
    if g_pre_next is None:
        xo, = _rows(_f_post(scale, bias is not None), [x, o], [g_post] + extra, [(D_MODEL, F32)], name=name)
        return xo, None

    post = _f_post(scale, bias is not None)

    def fn(xv, ov, g, *rest):
        xo = post(xv, ov, g, *rest[:-1])[0]
        return xo, _rms(xo, rest[-1])

    return _rows(fn, [x, o], [g_post] + extra + [g_pre_next], [(D_MODEL, F32), (D_MODEL, BF16)], name=name)


def _ffn_fwd(x, h, p, carry_up=None, carry_down=None):
    (a, b, u), got_up = _ffn_up(h, p["w1"], p["w3"], carry_up)
    o = _mm(u, p["w2"], name="ffn_down", carry=carry_down)
    got_down = None
    if carry_down is not None:
        o, got_down = o
    return o, (x, h, a, b, u, o), got_up, got_down


def _ffn_bwd(saved, p, dxo, carry_a=None, carry_b=None):
    x, h, a, b, u, o = saved
    (do,), (dg_post,) = _rows_vjp(_f_post_term(0.5, False), [o], [p["g_post"]], [dxo], dtypes=[BF16], name="ffn_post_bwd")
    (da, db), got_a = _ffn_down_dx(do, p["w2"], a, b, carry_a)
    dw2 = _mm(u, do, ta=True, out_dtype=BF16, name="ffn_down_dw")
    dh = _mm((da, db), (p["w1"], p["w3"]), name="ffn_up_dx", carry=carry_b)
    got_b = None
    if carry_b is not None:
        dh, got_b = dh
    dw1 = _mm(da, h, ta=True, out_dtype=BF16, name="ffn_up_dw")
    dw3 = _mm(db, h, ta=True, out_dtype=BF16, name="ffn_up_dw")
    (dx,), (dg_pre,) = _rows_vjp(_f_pre, [x], [p["g_pre"]], [dh], dtypes=[F32], adds={0: dxo}, name="ffn_pre_bwd")
    return dx, {"w1": dw1, "w3": dw3, "w2": dw2, "g_pre": dg_pre, "g_post": dg_post}, got_a, got_b


def _expand_blocks(w, rows_first):
    w = w.reshape(S5_NB, S5_GB, S5_GROUP, S5_STATE)
    eye = jnp.eye(S5_GB, dtype=F32)
    if rows_first:
        e = w[:, :, :, None, :] * eye[None, :, None, :, None]
        return e.reshape(S5_NB, S5_GB * S5_GROUP, S5_BW)
    e = jnp.transpose(w, (0, 1, 3, 2))[:, :, :, None, :] * eye[None, :, None, :, None]
    return e.reshape(S5_NB, S5_BW, S5_GB * S5_GROUP)


def _extract_blocks(e, rows_first):
    eye = jnp.eye(S5_GB, dtype=F32)
    if rows_first:
        e = e.reshape(S5_NB, S5_GB, S5_GROUP, S5_GB, S5_STATE)
        w = jnp.sum(e * eye[None, :, None, :, None], axis=3)
    else:
        e = e.reshape(S5_NB, S5_GB, S5_STATE, S5_GB, S5_GROUP)
        w = jnp.transpose(jnp.sum(e * eye[None, :, None, :, None], axis=3), (0, 1, 3, 2))
    return w.reshape(S5_GROUPS, S5_GROUP, S5_STATE)


def _s5_prep(p):
    G, P, HG = S5_GROUPS, S5_STATE, S5_GROUP
    args = [p["a_re"].reshape(G, 1, P), p["a_im"].reshape(G, 1, P), p["log_dt"].reshape(G, 1, 1),
            jnp.transpose(p["b_re"], (0, 2, 1)), jnp.transpose(p["b_im"], (0, 2, 1))]
    abr, abi, bbr, bbi = _small(_f_s5_disc, args, [(G, 1, P)] * 2 + [(G, HG, P)] * 2, name="s5_disc")
    return args, abr.reshape(1, G * P), abi.reshape(1, G * P), bbr, bbi


def _s5_fwd(h, p):
    disc_args, abr, abi, bbr, bbi = _s5_prep(p)
    c_re, c_im = p["c_re"], p["c_im"]
    u = _mm(h, p["w_in"], name="s5_in")
    bur = _bd([u], [_expand_blocks(bbr, True)], name="s5_bu")
    bui = _bd([u], [_expand_blocks(bbi, True)], name="s5_bu")
    sr, si = _scan_fwd(bur, bui, abr, abi)
    ylin = _bd([sr, si], [_expand_blocks(c_re, False), _expand_blocks(-c_im, False)], name="s5_y")
    y, = _rows(_f_s5_gelu, [ylin, u], [p["d"]], [(D_MODEL, F32)], name="s5_gelu")
    gl = _mm(y, p["w_glu"], name="s5_glu_mm")
    z, = _rows(_f_s5_glu, [y, gl], [p["b_glu"]], [(D_MODEL, BF16)], name="s5_glu")
    m = _mm(z, p["w_out"], name="s5_out")
    return m, None, (h, disc_args, abr, abi, bbr, bbi, u, sr, si, ylin, y, gl, z)


def _s5_bwd(saved, p, dm):
    h, disc_args, abr, abi, bbr, bbi, u, sr, si, ylin, y, gl, z = saved
    c_re, c_im = p["c_re"], p["c_im"]
    dz = _mm(dm, p["w_out"], tb=True, name="s5_out_dx")
    dw_out = _mm(z, dm, ta=True, out_dtype=BF16, name="s5_out_dw")
    (dy1, dgl), (db_glu,) = _rows_vjp(_f_s5_glu, [y, gl], [p["b_glu"]], [dz], dtypes=[F32, BF16], name="s5_glu_bwd")
    dy2 = _mm(dgl, p["w_glu"], tb=True, name="s5_glu_dx")
    dw_glu = _mm(y, dgl, ta=True, out_dtype=BF16, name="s5_glu_dw")
    (dylin, du1), (dd,) = _rows_vjp(_f_s5_gelu, [ylin, u], [p["d"]], [(dy1, dy2)], dtypes=[F32, F32], name="s5_gelu_bwd")
    gr = _bd([dylin], [jnp.transpose(_expand_blocks(c_re, False), (0, 2, 1))], name="s5_y_dx")
    gi = _bd([dylin], [jnp.transpose(_expand_blocks(-c_im, False), (0, 2, 1))], name="s5_y_dx")
    dc_re = _extract_blocks(_bd_wgrad(sr, dylin, S5_BW, LANES, name="s5_y_dw"), False)
    dc_im = -_extract_blocks(_bd_wgrad(si, dylin, S5_BW, LANES, name="s5_y_dw"), False)
    lr, li, dabr, dabi = _scan_bwd(gr, gi, sr, si, abr, abi)
    du = _bd([lr, li], [jnp.transpose(_expand_blocks(bbr, True), (0, 2, 1)),
                        jnp.transpose(_expand_blocks(bbi, True), (0, 2, 1))], add=du1, out_dtype=BF16, name="s5_bu_dx")
    dbbr = _extract_blocks(_bd_wgrad(u, lr, LANES, S5_BW, name="s5_bu_dw"), True)
    dbbi = _extract_blocks(_bd_wgrad(u, li, LANES, S5_BW, name="s5_bu_dw"), True)
    G, P = S5_GROUPS, S5_STATE
    dar, dai, dldt, dbr, dbi = _small_vjp(_f_s5_disc, disc_args,
                                          [dabr.reshape(G, 1, P), dabi.reshape(G, 1, P), dbbr, dbbi], name="s5_disc_bwd")
    dh = _mm(du, p["w_in"], tb=True, name="s5_in_dx")
    dw_in = _mm(h, du, ta=True, out_dtype=BF16, name="s5_in_dw")
    grads = {"w_in": dw_in, "w_glu": dw_glu, "w_out": dw_out, "b_glu": db_glu, "d": dd,
             "a_re": dar.reshape(G, P), "a_im": dai.reshape(G, P), "log_dt": dldt.reshape(G),
             "b_re": jnp.transpose(dbr, (0, 2, 1)), "b_im": jnp.transpose(dbi, (0, 2, 1)),
             "c_re": dc_re, "c_im": dc_im}
    return dh, grads


def _cv_fwd(h, p):
    z0 = _mm(h, p["w_in"], tb=True, name="cv_in")
    zg, = _rows(_f_cv_glu, [z0], [p["b_in"]], [(D_MODEL, F32)], name="cv_glu")
    zc = _conv_fwd(zg, p["dw"], p["dw_b"])
    zl, = _rows(_f_cv_ln, [zc], [p["ln_g"], p["ln_b"]], [(D_MODEL, BF16)], name="cv_ln")
    m = _mm(zl, p["w_out"], name="cv_out")
    return m, p["b_out"], (h, z0, zg, zc, zl)


def _cv_bwd(saved, p, dm):
    h, z0, zg, zc, zl = saved
    dzl = _mm(dm, p["w_out"], tb=True, name="cv_out_dx")
    dw_out = _mm(zl, dm, ta=True, out_dtype=BF16, name="cv_out_dw")
    (dzc,), (dln_g, dln_b) = _rows_vjp(_f_cv_ln, [zc], [p["ln_g"], p["ln_b"]], [dzl], dtypes=[F32], name="cv_ln_bwd")
    dzg, ddw, ddw_b = _conv_bwd(dzc, zg, p["dw"])
    (dz0,), (db_in,) = _rows_vjp(_f_cv_glu, [z0], [p["b_in"]], [dzg], dtypes=[BF16], name="cv_glu_bwd")
    dh = _mm(dz0, p["w_in"], name="cv_in_dx")
    dw_in = _mm(dz0, h, ta=True, out_dtype=BF16, name="cv_in_dw")
    return dh, {"w_in": dw_in, "b_in": db_in, "dw": ddw, "dw_b": ddw_b, "ln_g": dln_g, "ln_b": dln_b, "w_out": dw_out}


def _gm_fwd(h, p):
    z0 = _mm(h, p["w_in"], tb=True, name="gm_in")
    u, v = _rows(_f_gm_in, [z0], [p["b_in"], p["ln_g"], p["ln_b"]], [(GM_E, F32), (GM_E, BF16)], name="gm_act")
    bs_col = p["b_s"].reshape(GM_HEADS, GM_CHUNK, 1)
    us = _gm_sg_fwd(u, v, p["w_s"], bs_col)
    m = _mm(us, p["w_out"], name="gm_out")
    return m, p["b_out"], (h, z0, u, v, us, bs_col)


def _gm_bwd(saved, p, dm):
    h, z0, u, v, us, bs_col = saved
    dus = _mm(dm, p["w_out"], tb=True, name="gm_out_dx")
    dw_out = _mm(us, dm, ta=True, out_dtype=BF16, name="gm_out_dw")
    du, dv, dw_s, db_s = _gm_sg_bwd(dus, u, v, p["w_s"], bs_col)
    (dz0,), (db_in, dln_g, dln_b) = _rows_vjp(_f_gm_in, [z0], [p["b_in"], p["ln_g"], p["ln_b"]], [du, dv],
                                              dtypes=[BF16], name="gm_act_bwd")
    dh = _mm(dz0, p["w_in"], name="gm_in_dx")
    dw_in = _mm(dz0, h, ta=True, out_dtype=BF16, name="gm_in_dw")
    return dh, {"w_in": dw_in, "b_in": db_in, "ln_g": dln_g, "ln_b": dln_b, "w_s": dw_s, "b_s": db_s[:, :, 0],
                "w_out": dw_out}


def _at_fwd_mixer(h, p):
    T = h.shape[0]
    D = D_MODEL
    qkv = _mm(h, p["w_qkv"], tb=True, out_dtype=BF16, name="at_qkv")
    res, outs, lses, biases = [], [], [], []
    for g, (window, d) in enumerate(PATTERNS):
        assert window // d == BLOCK and T % (BLOCK * d) == 0
        bias = _at_bias(p["rel_bias"], g, d)
        if d == 1:
            r, cb = qkv, 3 * g
        else:
            r, cb = _to_residue_major(qkv[:, g * 3 * D:(g + 1) * 3 * D], d), 0
        o, lse = _at_fwd(r, bias, T // d // BLOCK, cb)
        res.append((r, cb, o, lse))
        biases.append(bias)
        outs.append(_from_residue_major(o, d))
        lses.append(_from_residue_major(lse, d))
    oc, = _rows(_f_at_combine, outs + lses, [], [(D, BF16)], name="at_combine")
    m = _mm(oc, p["w_out"], name="at_out")
    return m, None, (h, res, biases, outs, lses, oc)


def _at_bwd_mixer(saved, p, dm):
    h, res, biases, outs, lses, oc = saved
    T = h.shape[0]
    doc = _mm(dm, p["w_out"], tb=True, name="at_out_dx")
    dw_out = _mm(oc, dm, ta=True, out_dtype=BF16, name="at_out_dw")
    dol, _ = _rows_vjp(_f_at_combine, outs + lses, [], [doc], dtypes=[F32] * 6, name="at_combine_bwd")
    dqkv, dtab = [], []
    for g, (window, d) in enumerate(PATTERNS):
        r, cb, o_res, lse_res = res[g]
        dq, dbias = _at_bwd(r, biases[g], o_res, lse_res, _to_residue_major(dol[g], d),
                            _to_residue_major(dol[3 + g], d), T // d // BLOCK, cb)
        dqkv.append(_from_residue_major(dq, d))
        dtab.append(_at_bias_bwd(dbias, d))
    w_rows = [p["w_qkv"][g * 3 * D_MODEL:(g + 1) * 3 * D_MODEL] for g in range(len(PATTERNS))]
    dh = _mm(tuple(dqkv), tuple(w_rows), name="at_qkv_dx")
    dw_qkv = jnp.concatenate([_mm(dq, h, ta=True, out_dtype=BF16, name="at_qkv_dw") for dq in dqkv], axis=0)
    return dh, {"w_qkv": dw_qkv, "w_out": dw_out, "rel_bias": jnp.concatenate(dtab, axis=1)}


_MIXERS = ((_s5_fwd, _s5_bwd), (_cv_fwd, _cv_bwd), (_gm_fwd, _gm_bwd), (_at_fwd_mixer, _at_bwd_mixer))


def _mixer_fwd(x, h, p, kind):
    m, bias, saved = _MIXERS[kind][0](h, p)
    return m, bias, (x, m, bias, saved)


def _mixer_bwd(saved_all, p, kind, dxo):
    x, m, bias, saved = saved_all
    extra = [] if bias is None else [bias]
    (dm,), dpars = _rows_vjp(_f_post_term(1.0, bias is not None), [m], [p["g_post"]] + extra, [dxo], dtypes=[BF16],
                             name="mix_post_bwd")
    dh, grads = _MIXERS[kind][1](saved, p, dm)
    (dx,), (dg_pre,) = _rows_vjp(_f_pre, [x], [p["g_pre"]], [dh], dtypes=[F32], adds={0: dxo}, name="mix_pre_bwd")
    grads["g_pre"] = dg_pre
    grads["g_post"] = dpars[0]
    if bias is not None:
        grads["b_out"] = dpars[1]
    return dx, grads


class _Carry:
    def __init__(self, arrays, kinds):
        self.arrays, self.kinds, self.n = list(arrays), list(kinds), len(arrays)
        hbm = pl.BlockSpec(memory_space=pl.ANY)
        self.in_specs = [hbm] * self.n
        self.out_specs = [hbm] * self.n
        self.out_shape = [jax.ShapeDtypeStruct((N_DEV,) + (a.shape[1:] if k == "a2a" else a.shape), a.dtype)
                          for a, k in zip(arrays, kinds)]
        self.scratch = [pltpu.SemaphoreType.DMA((self.n * (N_DEV - 1),)), pltpu.SemaphoreType.DMA((self.n * (N_DEV - 1),)),
                        pltpu.SemaphoreType.DMA((self.n,))]

    def _copies(self, ins, outs, sems, arrivals):
        send_sems, recv_sems, local_sems = sems
        x, y, c = lax.axis_index("x"), lax.axis_index("y"), lax.axis_index("c")
        me = 4 * x + 2 * y + c
        local, remote = [], []
        for a in range(self.n):
            a2a = self.kinds[a] == "a2a"
            if not arrivals:
                local.append(pltpu.make_async_copy(ins[a].at[me] if a2a else ins[a], outs[a].at[me], local_sems.at[a]))
            for k in range(1, N_DEV):
                px = 1 - x if k & 4 else x
                py = 1 - y if k & 2 else y
                pc = 1 - c if k & 1 else c
                peer = 4 * px + 2 * py + pc
                idx = a * (N_DEV - 1) + k - 1
                remote.append(pltpu.make_async_remote_copy(
                    src_ref=ins[a].at[peer] if a2a else ins[a], dst_ref=outs[a].at[peer if arrivals else me],
                    send_sem=send_sems.at[idx], recv_sem=recv_sems.at[idx], device_id=(px, py, pc),
                    device_id_type=pl.DeviceIdType.MESH))
        return local, remote

    def start(self, ins, outs, sems):
        local, sends = self._copies(ins, outs, sems, False)
        for cp in local + sends:
            cp.start()

    def wait(self, ins, outs, sems):
        local, sends = self._copies(ins, outs, sems, False)
        _, recvs = self._copies(ins, outs, sems, True)
        for cp in sends:
            cp.wait_send()
        for cp in recvs:
            cp.wait_recv()
        for cp in local:
            cp.wait()


class _NoCarry:
    n = 0
    arrays = in_specs = out_specs = out_shape = scratch = []


_NO_CARRY = _NoCarry()


def _carry_hooks(carry, refs, n_in, n_out, grid_rank, grid):
    nc = carry.n if carry is not None else 0
    ins, cin = refs[:n_in], refs[n_in:n_in + nc]
    outs, cout = refs[n_in + nc:n_in + nc + n_out], refs[n_in + nc + n_out:n_in + 2 * nc + n_out]
    rest = refs[n_in + 2 * nc + n_out:]
    scratch, sems = (rest[:len(rest) - 3], rest[len(rest) - 3:]) if nc else (rest, ())

    def at(step_of):
        cond = None
        for ax in range(grid_rank):
            c = pl.program_id(ax) == step_of(ax)
            cond = c if cond is None else cond & c
        return cond

    def begin():
        if nc:
            @pl.when(at(lambda ax: 0))
            def _():
                carry.start(cin, cout, sems)

    def end():
        if nc:
            @pl.when(at(lambda ax: grid[ax] - 1))
            def _():
                carry.wait(cin, cout, sems)

    return ins, outs, scratch, begin, end


def _exchange(arrays, kinds, *, name):
    carry = _Carry(arrays, kinds)

    def body(*refs):
        n = carry.n
        carry.start(refs[:n], refs[n:2 * n], refs[2 * n:])
        carry.wait(refs[:n], refs[n:2 * n], refs[2 * n:])

    return pl.pallas_call(body, name=name, in_specs=carry.in_specs, out_specs=carry.out_specs,
                          out_shape=carry.out_shape, scratch_shapes=carry.scratch)(*arrays)


def _adam(recv, w, m, v, *, name):
    R, C = w.shape
    tr = _pick_rows(R, 128)
    c1 = 1.0 - ADAM_B1 ** ADAM_STEP
    c2 = 1.0 - ADAM_B2 ** ADAM_STEP

    def body(r_ref, w_ref, m_ref, v_ref, g_ref, d_ref, nm_ref, nv_ref):
        g = r_ref[0].astype(F32)
        for q in range(1, N_DEV):
            g = g + r_ref[q].astype(F32)
        mm = ADAM_B1 * m_ref[...] + (1.0 - ADAM_B1) * g
        vv = ADAM_B2 * v_ref[...] + (1.0 - ADAM_B2) * jnp.square(g)
        m_hat = mm / c1
        v_hat = vv / c2
        g_ref[...] = g
        d_ref[...] = -ADAM_LR * (m_hat / (jnp.sqrt(v_hat) + ADAM_EPS) + ADAM_WD * w_ref[...])
        nm_ref[...] = mm
        nv_ref[...] = vv

    blk = pl.BlockSpec((tr, C), lambda i: (i, 0))
    return pl.pallas_call(
        body, name=name, grid=(R // tr,),
        in_specs=[pl.BlockSpec((N_DEV, tr, C), lambda i: (0, i, 0)), blk, blk, blk], out_specs=[blk] * 4,
        out_shape=[jax.ShapeDtypeStruct((R, C), F32)] * 4,
        compiler_params=_cparams(("parallel",)),
    )(recv, w, m, v)


PACK_COLS = 1024


def _padded(n):
    return -(-n // PACK_ALIGN) * PACK_ALIGN


def _pack_flat(pieces):
    flat = jnp.concatenate([p.reshape(-1) for p in pieces])
    n = flat.shape[0]
    return jnp.pad(flat, (0, _padded(n) - n)).reshape(-1, PACK_COLS)


def _shard_shape(shape, axis):
    s = list(shape)
    assert s[axis] % N_DEV == 0
    s[axis] //= N_DEV
    return tuple(s)


def _split_full(full, axis):
    s = full.shape
    r = full.reshape(s[:axis] + (N_DEV, s[axis] // N_DEV) + s[axis + 1:])
    return jnp.moveaxis(r, axis, 0)


def _merge_full(parts, axis):
    r = jnp.moveaxis(parts, 0, axis)
    s = r.shape
    return r.reshape(s[:axis] + (s[axis] * s[axis + 1],) + s[axis + 2:])


def _pack_full(entries, grads):
    flat = jnp.concatenate([_split_full(grads[k].reshape(shape), axis).reshape(N_DEV, -1)
                            for k, shape, axis in entries], axis=1)
    n = flat.shape[1]
    return jnp.pad(flat, ((0, 0), (0, _padded(n) - n))).reshape(N_DEV, -1, PACK_COLS)


def _unpack_gathered(entries, buf):
    flat = buf.reshape(N_DEV, -1)
    out, pos = {}, 0
    for k, shape, axis in entries:
        ss = _shard_shape(shape, axis)
        n = int(np.prod(ss))
        out[k] = _merge_full(flat[:, pos:pos + n].reshape((N_DEV,) + ss), axis)
        pos += n
    return out


def _unpack_shard(entries, buf):
    flat = buf.reshape(-1)
    out, pos = {}, 0
    for k, shape, axis in entries:
        ss = _shard_shape(shape, axis)
        n = int(np.prod(ss))
        out[k] = flat[pos:pos + n].reshape(ss)
        pos += n
    return out


def _unpack_flat(entries, buf):
    flat = buf.reshape(-1)
    out, pos = {}, 0
    for k, shape in entries:
        n = int(np.prod(shape))
        out[k] = flat[pos:pos + n].reshape(shape)
        pos += n
    return out


D, FF = D_MODEL, D_FF
_FFN_MATS = (("w1", (D, FF), 1), ("w3", (D, FF), 1), ("w2", (FF, D), 0))
_NORM_VECS = (("g_pre", (D,), 0), ("g_post", (D,), 0))
_MIX_MATS = (
    (("w_in", (D, D), 0), ("w_glu", (D, D), 0), ("w_out", (D, D), 0)),
    (("w_in", (D, 2 * D), 1), ("w_out", (D, D), 0)),
    (("w_in", (D, 2 * GM_E), 1), ("w_out", (GM_E, D), 0)),
    (("w_qkv", (D, 9 * D), 1), ("w_out", (D, D), 0)),
)
_MIX_VECS = (
    (),
    (("b_in", (2 * D,), 0), ("dw", (CONV_W, D), 1), ("dw_b", (D,), 0), ("ln_g", (D,), 0), ("ln_b", (D,), 0),
     ("b_out", (D,), 0)),
    (("b_in", (2 * GM_E,), 0), ("ln_g", (GM_E,), 0), ("ln_b", (GM_E,), 0), ("b_out", (D,), 0)),
    (),
)
_REPLICATED = (
    ("rel_bias", 3, "rel_bias", (NUM_BUCKETS, 3 * AT_HEADS)),
    ("s5_a_re", 0, "a_re", (S5_GROUPS, S5_STATE)), ("s5_a_im", 0, "a_im", (S5_GROUPS, S5_STATE)),
    ("s5_log_dt", 0, "log_dt", (S5_GROUPS,)),
    ("s5_b_re", 0, "b_re", (S5_GROUPS, S5_STATE, S5_GROUP)), ("s5_b_im", 0, "b_im", (S5_GROUPS, S5_STATE, S5_GROUP)),
    ("s5_c_re", 0, "c_re", (S5_GROUPS, S5_GROUP, S5_STATE)), ("s5_c_im", 0, "c_im", (S5_GROUPS, S5_GROUP, S5_STATE)),
    ("s5_d", 0, "d", (D,)), ("s5_b_glu", 0, "b_glu", (D,)),
    ("gm_w_s", 2, "w_s", (GM_HEADS, GM_CHUNK, GM_CHUNK)), ("gm_b_s", 2, "b_s", (GM_HEADS, GM_CHUNK)),
)
_MIX_PREFIX = ("s5_", "cv_", "gm_", "at_")
_TWIN_WEIGHTS = ('norm_pre', 'norm_post', 'ffn_w1', 'ffn_w3', 'ffn_w2', 'rel_bias', 's5_w_in', 's5_a_re', 's5_a_im',
                 's5_log_dt', 's5_b_re', 's5_b_im', 's5_c_re', 's5_c_im', 's5_d', 's5_w_glu', 's5_b_glu', 's5_w_out',
                 'cv_w_in', 'cv_b_in', 'cv_dw', 'cv_dw_b', 'cv_ln_g', 'cv_ln_b', 'cv_w_out', 'cv_b_out', 'gm_w_in',
                 'gm_b_in', 'gm_ln_g', 'gm_ln_b', 'gm_w_s', 'gm_b_s', 'gm_w_out', 'gm_b_out', 'at_w_qkv', 'at_w_out')


def _part_entries(part):
    if part[0] == "ffn":
        return _FFN_MATS, _NORM_VECS
    kind = part[1] % 4
    return _MIX_MATS[kind], _NORM_VECS + _MIX_VECS[kind]


def _part_shards(part, get):
    if part[0] == "ffn":
        _, i, j = part
        n = 0 if j == 0 else 2
        return {"w1": get("ffn_w1")[i, j], "w3": get("ffn_w3")[i, j], "w2": get("ffn_w2")[i, j],
                "g_pre": get("norm_pre")[i, n], "g_post": get("norm_post")[i, n]}
    _, i = part
    kind, j = i % 4, i // 4
    out = {"g_pre": get("norm_pre")[i, 1], "g_post": get("norm_post")[i, 1]}
    for k, _, _ in _MIX_MATS[kind] + _MIX_VECS[kind]:
        out[k] = get(_MIX_PREFIX[kind] + k)[j]
    return out


def _parts():
    parts = []
    for i in range(DEPTH):
        parts += [("ffn", i, 0), ("mix", i), ("ffn", i, 1)]
    return parts


def _as_par(v):
    return v.reshape(1, -1)


def _prepare_part(part, full, rep):
    if part[0] == "ffn":
        return {"w1": full["w1"], "w3": full["w3"], "w2": full["w2"],
                "g_pre": _as_par(full["g_pre"]), "g_post": _as_par(full["g_post"])}
    kind = part[1] % 4
    p = {"g_pre": _as_par(full["g_pre"]), "g_post": _as_par(full["g_post"])}
    for k, _, _ in _MIX_MATS[kind]:
        p[k] = full[k]
    for k, _, _ in _MIX_VECS[kind]:
        p[k] = _as_par(full[k]) if k != "dw" else jnp.pad(full[k], ((0, CONV_HALO - CONV_W), (0, 0)))
    if kind == 0:
        for k in ("a_re", "a_im", "log_dt", "b_re", "b_im"):
            p[k] = rep[k]
        p["c_re"], p["c_im"] = rep["c_re"], rep["c_im"]
        p["d"], p["b_glu"] = _as_par(rep["d"]), _as_par(rep["b_glu"])
    elif kind == 2:
        p["w_s"], p["b_s"] = rep["w_s"], rep["b_s"]
    elif kind == 3:
        p["rel_bias"] = rep["rel_bias"]
    return p


def _finish_grads(part, grads):
    out = dict(grads)
    for k in ("g_pre", "g_post", "b_in", "dw_b", "ln_g", "ln_b", "b_out", "d", "b_glu"):
        if k in out:
            out[k] = out[k].reshape(-1)
    if "dw" in out:
        out["dw"] = out["dw"][:CONV_W]
    return out


def _step(x, tgt, inputs, moments_m, moments_v):
    parts = _parts()
    rep = {}
    for name, kind, key, shape in _REPLICATED:
        rep[key] = inputs[name][0] if name != "rel_bias" else inputs[name]

    def stored(part, get):
        mats, vecs = _part_entries(part)
        sh = _part_shards(part, get)
        return [sh[k].T if axis == 1 else sh[k] for k, _, axis in mats], _pack_flat([sh[k] for k, _, _ in vecs])

    stored_w = [stored(part, lambda n: inputs[n]) for part in parts]

    def gather_of(idx):
        wmats, wv = stored_w[idx]
        return _Carry([w.astype(BF16) for w in wmats] + [wv], ["bcast"] * (len(wmats) + 1))

    def gathered(idx, bufs):
        mats, vecs = _part_entries(parts[idx])
        full = {k: b.reshape(-1, b.shape[-1]) for (k, _, _), b in zip(mats, bufs)}
        full.update(_unpack_gathered(vecs, bufs[-1]))
        return _prepare_part(parts[idx], full, rep)

    params = [None] * len(parts)
    first = gather_of(0)
    params[0] = gathered(0, _exchange(first.arrays, first.kinds, name="gather_first"))
    saved = []
    xs = x
    h = _pre_norm(xs, params[0]["g_pre"])
    for idx, part in enumerate(parts):
        if part[0] == "ffn":
            ahead = [i for i in (idx + 1, idx + 2) if i < len(parts) and params[i] is None]
            if part[2] == 1:
                ahead = ahead[:1]
            ahead = ahead[::-1]
            c_up = gather_of(ahead[0]) if ahead else None
            c_down = gather_of(ahead[1]) if len(ahead) > 1 else None
            o, s, got_up, got_down = _ffn_fwd(xs, h, params[idx], c_up, c_down)
            bias, scale = None, 0.5
            if c_up is not None:
                params[ahead[0]] = gathered(ahead[0], got_up)
            if c_down is not None:
                params[ahead[1]] = gathered(ahead[1], got_down)
        else:
            o, bias, s = _mixer_fwd(xs, h, params[idx], part[1] % 4)
            scale = 1.0
        saved.append(s)
        g_next = params[idx + 1]["g_pre"] if idx + 1 < len(parts) else None
        xs, h = _close_part(xs, o, bias, params[idx]["g_post"], g_next, scale, part[0] + "_post")
    dh, loss_vec = _loss_call(xs, tgt)
    loss_local = loss_vec[0, 0]

    results = {}
    rep_grads = {}

    def scatter_of(idx, grads):
        mats, vecs = _part_entries(parts[idx])
        gm = [grads[k].reshape(N_DEV, -1, grads[k].shape[-1]) for k, _, _ in mats]
        return _Carry(gm + [_pack_full(vecs, grads)], ["a2a"] * (len(gm) + 1))

    def update(idx, bufs):
        part = parts[idx]
        mats, vecs = _part_entries(part)
        wmats, wv = stored_w[idx]
        mmats, mv = stored(part, lambda n: moments_m[n])
        vmats, vv = stored(part, lambda n: moments_v[n])
        res = [dict() for _ in range(4)]
        for (k, _, axis), buf, w_, m_, v_ in zip(mats, bufs, wmats, mmats, vmats):
            for r, o in zip(res, _adam(buf, w_, m_, v_, name="adam_mat")):
                r[k] = o.T if axis == 1 else o
        for r, o in zip(res, _adam(bufs[-1], wv, mv, vv, name="adam_vecs")):
            r.update(_unpack_shard(vecs, o))
        results[part] = res

    rep_entries = [(name, shape) for name, _, _, shape in _REPLICATED]
    rg = None
    pending = []
    for idx in range(len(parts) - 1, -1, -1):
        part, p = parts[idx], params[idx]
        if part[0] == "ffn":
            riders = pending[:2]
            pending = pending[2:]
            c_a = riders[0][1] if riders else None
            c_b = riders[1][1] if len(riders) > 1 else None
            if idx == 0 and c_a is not None:
                c_a = _Carry(c_a.arrays + [_pack_flat([rep_grads[name] for name, _ in rep_entries])], c_a.kinds + ["bcast"])
            dh, grads, got_a, got_b = _ffn_bwd(saved[idx], p, dh, c_a, c_b)
            if idx == 0 and c_a is not None:
                rg, got_a = got_a[-1], got_a[:-1]
            for (ridx, _), got in zip(riders, (got_a, got_b)):
                update(ridx, got)
        else:
            dh, grads = _mixer_bwd(saved[idx], p, part[1] % 4, dh)
        grads = _finish_grads(part, grads)
        for name, kind, key, shape in _REPLICATED:
            if part[0] == "mix" and kind == part[1] % 4:
                rep_grads[name] = grads[key]
        pending.append((idx, scatter_of(idx, grads)))
    for ridx, c in pending:
        update(ridx, _exchange(c.arrays, c.kinds, name="scatter_last"))

    get_rep = lambda d: _pack_flat([(d[name][0] if name != "rel_bias" else d[name]) for name, _ in rep_entries])
    assert rg is not None
    orep = _adam(rg, get_rep(inputs), get_rep(moments_m), get_rep(moments_v), name="adam_rep")
    rep_out = [_unpack_flat(rep_entries, o) for o in orep]
    return loss_local, dh, results, rep_out


def _assemble(name, results, rep_out, which):
    for rname, _, _, _ in _REPLICATED:
        if rname == name:
            a = rep_out[which][name]
            return a if name == "rel_bias" else a[None]
    if name in ("norm_pre", "norm_post"):
        key = "g_pre" if name == "norm_pre" else "g_post"
        rows = []
        for i in range(DEPTH):
            rows.append(jnp.stack([results[("ffn", i, 0)][which][key], results[("mix", i)][which][key],
                                   results[("ffn", i, 1)][which][key]]))
        return jnp.stack(rows)
    if name.startswith("ffn_"):
        key = name[4:]
        return jnp.stack([jnp.stack([results[("ffn", i, j)][which][key] for j in range(2)]) for i in range(DEPTH)])
    kind = _MIX_PREFIX.index(name[:3])
    layers = [i for i in range(DEPTH) if i % 4 == kind]
    return jnp.stack([results[("mix", i)][which][name[3:]] for i in layers])


def kernel(x, norm_pre, norm_post, ffn_w1, ffn_w3, ffn_w2, rel_bias, s5_w_in, s5_a_re, s5_a_im, s5_log_dt, s5_b_re, s5_b_im, s5_c_re, s5_c_im, s5_d, s5_w_glu, s5_b_glu, s5_w_out, cv_w_in, cv_b_in, cv_dw, cv_dw_b, cv_ln_g, cv_ln_b, cv_w_out, cv_b_out, gm_w_in, gm_b_in, gm_ln_g, gm_ln_b, gm_w_s, gm_b_s, gm_w_out, gm_b_out, at_w_qkv, at_w_out, loss_target, m_norm_pre, m_norm_post, m_ffn_w1, m_ffn_w3, m_ffn_w2, m_rel_bias, m_s5_w_in, m_s5_a_re, m_s5_a_im, m_s5_log_dt, m_s5_b_re, m_s5_b_im, m_s5_c_re, m_s5_c_im, m_s5_d, m_s5_w_glu, m_s5_b_glu, m_s5_w_out, m_cv_w_in, m_cv_b_in, m_cv_dw, m_cv_dw_b, m_cv_ln_g, m_cv_ln_b, m_cv_w_out, m_cv_b_out, m_gm_w_in, m_gm_b_in, m_gm_ln_g, m_gm_ln_b, m_gm_w_s, m_gm_b_s, m_gm_w_out, m_gm_b_out, m_at_w_qkv, m_at_w_out, v_norm_pre, v_norm_post, v_ffn_w1, v_ffn_w3, v_ffn_w2, v_rel_bias, v_s5_w_in, v_s5_a_re, v_s5_a_im, v_s5_log_dt, v_s5_b_re, v_s5_b_im, v_s5_c_re, v_s5_c_im, v_s5_d, v_s5_w_glu, v_s5_b_glu, v_s5_w_out, v_cv_w_in, v_cv_b_in, v_cv_dw, v_cv_dw_b, v_cv_ln_g, v_cv_ln_b, v_cv_w_out, v_cv_b_out, v_gm_w_in, v_gm_b_in, v_gm_ln_g, v_gm_ln_b, v_gm_w_s, v_gm_b_s, v_gm_w_out, v_gm_b_out, v_at_w_qkv, v_at_w_out):
    args = locals()
    inputs = {n: args[n] for n in _TWIN_WEIGHTS}
    moments_m = {n: args["m_" + n] for n in _TWIN_WEIGHTS}
    moments_v = {n: args["v_" + n] for n in _TWIN_WEIGHTS}
    loss_local, dx, results, rep_out = _step(x[0], loss_target[0], inputs, moments_m, moments_v)
    loss = lax.psum(loss_local, AXES)
    out = [loss, dx[None]]
    for which in range(4):
        out += [_assemble(n, results, rep_out, which) for n in _TWIN_WEIGHTS]
    return tuple(out)
```
